```python
import math
import jax
import jax.numpy as jnp
from jax import lax
import numpy as np

D_MODEL = 2048
BATCH = 4
SEQ = 2048
DEPTH = 2
DEC_BATCH = 32
DEC_SEQ = 8
PAST_LEN = 8192
PAGE_SIZE = 128

HEAD_DIM = 128
A_HEADS = D_MODEL // (2 * HEAD_DIM)
A_KV = 2
A_GROUP = A_HEADS // A_KV
CMP_STRIDE = 16
CMP_BLOCK = 2 * CMP_STRIDE
CMP_HIDDEN = 2 * HEAD_DIM
SEL_BLOCK = 64
N_SEL_BLOCKS = 16
WINDOW = 512
N_GATES = 3
B_HEADS = D_MODEL // (2 * HEAD_DIM)
B_KV = 2
B_GROUP = B_HEADS // B_KV
IDX_HEADS = 4
IDX_DIM = 64
DSA_TOPK = 256
C_HEADS = D_MODEL // (2 * HEAD_DIM)
C_KV = 4
C_GROUP = C_HEADS // C_KV
C_VDIM = 2 * HEAD_DIM
LAMBDA_INIT = 0.8 - 0.6 * math.exp(-0.3 * 1)
NUM_BUCKETS = 32
MAX_DISTANCE = 128
N_BIAS_COLS = A_HEADS + B_HEADS
D_FF = 4 * D_MODEL
Q_BLOCK = 128
RMS_EPS = 1e-6
NEG = -1e30

L0_SIZES = (A_HEADS * HEAD_DIM,) + (A_KV * HEAD_DIM,) * 6 + (
    N_GATES * A_HEADS, B_HEADS * HEAD_DIM, B_KV * HEAD_DIM, B_KV * HEAD_DIM,
    IDX_HEADS * IDX_DIM, IDX_DIM, IDX_HEADS)
L0_IN = sum(L0_SIZES)
L0_OUT = A_HEADS * HEAD_DIM + B_HEADS * HEAD_DIM
L1_SIZES = (C_HEADS * 2 * HEAD_DIM, C_KV * 2 * HEAD_DIM, C_KV * C_VDIM)
L1_IN = sum(L1_SIZES)
L1_OUT = C_HEADS * C_VDIM

kernel_name = 'hybrid_nsa_dsa_diffattn_decode_step'


def rmsnorm(x, g):
    xf = x.astype(jnp.float32)
    y = xf * lax.rsqrt(jnp.mean(xf * xf, axis=-1, keepdims=True) + RMS_EPS)
    return (y * g.astype(jnp.float32)).astype(x.dtype)


def sq_relu_mlp(h, w1, w2):
    return jnp.square(jax.nn.relu(h @ w1)) @ w2


def rel_bucket(dist):
    n = jnp.maximum(dist, 0)
    max_exact = NUM_BUCKETS // 2
    nf = jnp.maximum(n, 1).astype(jnp.float32)
    large = max_exact + (jnp.log(nf / max_exact) / math.log(MAX_DISTANCE / max_exact)
                         * (NUM_BUCKETS - max_exact)).astype(jnp.int32)
    large = jnp.minimum(large, NUM_BUCKETS - 1)
    return jnp.where(n < max_exact, n, large)


def masked_softmax(logits, mask, dtype):
    z = jnp.where(mask, logits.astype(jnp.float32), NEG)
    p = jax.nn.softmax(z, axis=-1) * mask
    return p.astype(dtype)


def split_cols(z, sizes):
    return jnp.split(z, np.cumsum(sizes)[:-1].tolist(), axis=-1)


def gather_pages(pool, page_table):
    g = pool[page_table]
    return g.reshape((page_table.shape[0], page_table.shape[1] * pool.shape[1]) + pool.shape[2:])


def nsa_compress(k, pe, w1, w2):
    n, length, g = k.shape[0], k.shape[1], k.shape[2]
    n_cmp = (length - CMP_BLOCK) // CMP_STRIDE + 1
    n_chunk = -(-length // CMP_STRIDE)
    kp = jnp.pad(k, ((0, 0), (0, n_chunk * CMP_STRIDE - length), (0, 0), (0, 0)))
    chunks = kp.reshape(n, n_chunk, CMP_STRIDE, g, HEAD_DIM)
    w = w1.reshape(2, CMP_STRIDE, HEAD_DIM, CMP_HIDDEN)
    first = jnp.einsum('ncjgd,jdh->ncgh', chunks, w[0])
    second = jnp.einsum('ncjgd,jdh->ncgh', chunks, w[1])
    pos = jnp.einsum('jd,jdh->h', pe, w1)
    hid = jax.nn.gelu(first[:, :n_cmp] + second[:, 1:n_cmp + 1] + pos)
    return hid @ w2


def to_sel_blocks(k):
    n, length = k.shape[0], k.shape[1]
    n_slc = -(-length // SEL_BLOCK)
    kp = jnp.pad(k, ((0, 0), (0, n_slc * SEL_BLOCK - length), (0, 0), (0, 0)))
    return kp.reshape((n, n_slc, SEL_BLOCK) + k.shape[2:])


def nsa_core(q, gates, k_cmp, v_cmp, ks_blk, vs_blk, kw, vw, q_pos, w_pos, tbl):
    n, tq, g_ = q.shape[0], q.shape[1], q.shape[2]
    dt = q.dtype
    scale = HEAD_DIM ** -0.5
    n_cmp = k_cmp.shape[1]
    cmp_end = jnp.arange(n_cmp, dtype=jnp.int32) * CMP_STRIDE + (CMP_BLOCK - 1)
    dist_c = q_pos[:, None] - cmp_end[None, :]
    bias_c = jnp.transpose(tbl[rel_bucket(dist_c)], (0, 2, 3, 1))
    lc = jnp.einsum('nqgrd,ncgd->nqgrc', q, k_cmp) * scale + bias_c
    p_cmp = masked_softmax(lc, (dist_c >= 0)[:, None, None, :], jnp.float32)
    o_cmp = jnp.einsum('nqgrc,ncgd->nqgrd', p_cmp.astype(dt), v_cmp)
    n_slc = ks_blk.shape[1]
    n_sel = min(N_SEL_BLOCKS, n_slc)
    c_start = jnp.arange(n_cmp, dtype=jnp.int32)[:, None] * CMP_STRIDE
    b_start = jnp.arange(n_slc, dtype=jnp.int32)[None, :] * SEL_BLOCK
    overlap = ((c_start < b_start + SEL_BLOCK) & (c_start + CMP_BLOCK > b_start)).astype(jnp.float32)
    imp = jnp.einsum('nqgrc,cj->nqgj', p_cmp, overlap)
    blk = jnp.arange(n_slc, dtype=jnp.int32)
    cur = (q_pos // SEL_BLOCK)[:, None]
    forced = (blk[None, :] == 0) | (blk[None, :] == cur) | (blk[None, :] == cur - 1)
    future = blk[None, :] * SEL_BLOCK > q_pos[:, None]
    score = jnp.where(future[None, :, None, :], -1.0,
                      jnp.where(forced[None, :, None, :], 1e3, imp))
    _, sel = lax.top_k(score, n_sel)
    n_ix = jnp.arange(n)[:, None, None, None]
    g_ix = jnp.arange(g_)[None, None, :, None]
    ks = ks_blk[n_ix, sel, :, g_ix].reshape(n, tq, g_, n_sel * SEL_BLOCK, HEAD_DIM)
    vs = vs_blk[n_ix, sel, :, g_ix].reshape(n, tq, g_, n_sel * SEL_BLOCK, HEAD_DIM)
    tok = (sel[..., None] * SEL_BLOCK + jnp.arange(SEL_BLOCK, dtype=jnp.int32)).reshape(n, tq, g_, -1)
    dist_s = q_pos[None, :, None, None] - tok
    tbl_t = jnp.transpose(tbl, (1, 0, 2))
    bias_s = jnp.moveaxis(tbl_t[g_ix, rel_bucket(dist_s)], -1, 3)
    ls = jnp.einsum('nqgrd,nqgkd->nqgrk', q, ks) * scale + bias_s
    p_s = masked_softmax(ls, (dist_s >= 0)[:, :, :, None, :], dt)
    o_slc = jnp.einsum('nqgrk,nqgkd->nqgrd', p_s, vs)
    dist_w = q_pos[:, None] - w_pos[None, :]
    valid_w = (dist_w >= 0) & (dist_w < WINDOW) & (w_pos[None, :] >= 0)
    bias_w = jnp.transpose(tbl[rel_bucket(dist_w)], (0, 2, 3, 1))
    lw = jnp.einsum('nqgrd,nkgd->nqgrk', q, kw) * scale + bias_w
    p_w = masked_softmax(lw, valid_w[:, None, None, :], dt)
    o_win = jnp.einsum('nqgrk,nkgd->nqgrd', p_w, vw)
    return gates[..., 0:1] * o_cmp + gates[..., 1:2] * o_slc + gates[..., 2:3] * o_win


def dsa_core(q, qi, wi, k, v, ki, q_pos, tbl, topk):
    n = q.shape[0]
    dt = q.dtype
    length = k.shape[1]
    key_pos = jnp.arange(length, dtype=jnp.int32)
    rel = jax.nn.relu(jnp.einsum('nqhe,nse->nqhs', qi, ki).astype(jnp.float32))
    idx_score = jnp.einsum('nqhs,nqh->nqs', rel, wi.astype(jnp.float32)) * (IDX_DIM ** -0.5 * IDX_HEADS ** -0.5)
    idx_score = jnp.where(key_pos[None, None, :] <= q_pos[None, :, None], idx_score, NEG)
    _, sel = lax.top_k(idx_score, topk)
    n_ix = jnp.arange(n)[:, None, None]
    ks = k[n_ix, sel]
    vs = v[n_ix, sel]
    dist = q_pos[None, :, None] - sel
    bias = jnp.moveaxis(tbl[rel_bucket(dist)], 2, -1)
    logits = jnp.einsum('nqgrd,nqkgd->nqgrk', q, ks) * (HEAD_DIM ** -0.5) + bias
    p = masked_softmax(logits, (dist >= 0)[:, :, None, None, :], dt)
    return jnp.einsum('nqgrk,nqkgd->nqgrd', p, vs)


def diff_core(q, k, v, q_pos, tbl, lam):
    dt = q.dtype
    length = k.shape[1]
    dist = q_pos[:, None] - jnp.arange(length, dtype=jnp.int32)[None, :]
    bias = jnp.transpose(tbl[rel_bucket(dist)], (3, 4, 2, 0, 1))
    logits = jnp.einsum('nqgrmd,nsgmd->ngrmqs', q, k) * (HEAD_DIM ** -0.5) + bias
    p = masked_softmax(logits, dist >= 0, jnp.float32)
    a = p[:, :, :, 0] - lam * p[:, :, :, 1]
    return jnp.einsum('ngrqs,nsgd->nqgrd', a.astype(dt), v)


def proj_l0(h, w_in):
    n, t = h.shape[0], h.shape[1]
    qa, kc, vc, ks, vs, kw, vw, ga, qb, kb, vb, qi, ki, wi = split_cols(h @ w_in, L0_SIZES)
    kva = lambda z: z.reshape(n, t, A_KV, HEAD_DIM)
    kvb = lambda z: z.reshape(n, t, B_KV, HEAD_DIM)
    qa = qa.reshape(n, t, A_KV, A_GROUP, HEAD_DIM)
    ga = jax.nn.sigmoid(ga.astype(jnp.float32)).astype(h.dtype).reshape(n, t, A_KV, A_GROUP, N_GATES)
    qb = qb.reshape(n, t, B_KV, B_GROUP, HEAD_DIM)
    qi = qi.reshape(n, t, IDX_HEADS, IDX_DIM)
    return (qa, kva(kc), kva(vc), kva(ks), kva(vs), kva(kw), kva(vw), ga,
            qb, kvb(kb), kvb(vb), qi, ki, wi)


def out_l0(o_a, o_b, w_out):
    n, t = o_a.shape[0], o_a.shape[1]
    return jnp.concatenate([o_a.reshape(n, t, -1), o_b.reshape(n, t, -1)], axis=-1) @ w_out


def bias_tables_l0(rel_bias):
    tbl_a = rel_bias[:, :A_HEADS].reshape(NUM_BUCKETS, A_KV, A_GROUP)
    tbl_b = rel_bias[:, A_HEADS:].reshape(NUM_BUCKETS, B_KV, B_GROUP)
    return tbl_a, tbl_b


def unblock(z, n, t):
    return jnp.moveaxis(z, 0, 1).reshape((n, t) + z.shape[3:])


def mixer0_prompt(h, w_in, w_out, pe_k, w1_k, w2_k, pe_v, w1_v, w2_v, rel_bias):
    n, t = h.shape[0], h.shape[1]
    qa, kc, vc, ks, vs, kw, vw, ga, qb, kb, vb, qi, ki, wi = proj_l0(h, w_in)
    tbl_a, tbl_b = bias_tables_l0(rel_bias)
    k_cmp = nsa_compress(kc, pe_k, w1_k, w2_k)
    v_cmp = nsa_compress(vc, pe_v, w1_v, w2_v)
    ks_blk, vs_blk = to_sel_blocks(ks), to_sel_blocks(vs)
    kw_pad = jnp.pad(kw, ((0, 0), (WINDOW, 0), (0, 0), (0, 0)))
    vw_pad = jnp.pad(vw, ((0, 0), (WINDOW, 0), (0, 0), (0, 0)))
    topk = min(DSA_TOPK, t // 4)

    def block(b):
        qs = b * Q_BLOCK
        q_pos = qs + jnp.arange(Q_BLOCK, dtype=jnp.int32)
        sl = lambda z: lax.dynamic_slice_in_dim(z, qs, Q_BLOCK, axis=1)
        w_pos = qs - WINDOW + jnp.arange(WINDOW + Q_BLOCK, dtype=jnp.int32)
        kw_b = lax.dynamic_slice_in_dim(kw_pad, qs, WINDOW + Q_BLOCK, axis=1)
        vw_b = lax.dynamic_slice_in_dim(vw_pad, qs, WINDOW + Q_BLOCK, axis=1)
        o_a = nsa_core(sl(qa), sl(ga), k_cmp, v_cmp, ks_blk, vs_blk, kw_b, vw_b, q_pos, w_pos, tbl_a)
        o_b = dsa_core(sl(qb), sl(qi), sl(wi), kb, vb, ki, q_pos, tbl_b, topk)
        return o_a, o_b

    o_a, o_b = lax.map(block, jnp.arange(t // Q_BLOCK, dtype=jnp.int32))
    y = out_l0(unblock(o_a, n, t), unblock(o_b, n, t), w_out)
    win = min(WINDOW, t)
    return y, (kc, vc, ks, vs, kw[:, t - win:], vw[:, t - win:], kb, vb, ki)


def mixer0_sample(h, c_cmp_k, c_cmp_v, c_slc_k, c_slc_v, s_win_k, s_win_v, c_dsa_k, c_dsa_v, c_idx_k,
                  page_table, w_in, w_out, pe_k, w1_k, w2_k, pe_v, w1_v, w2_v, rel_bias):
    t = h.shape[1]
    past = page_table.shape[1] * PAGE_SIZE
    qa, kc, vc, ks, vs, kw, vw, ga, qb, kb, vb, qi, ki, wi = proj_l0(h, w_in)
    tbl_a, tbl_b = bias_tables_l0(rel_bias)
    full = lambda pool, new: jnp.concatenate([gather_pages(pool, page_table), new], axis=1)
    k_cmp = nsa_compress(full(c_cmp_k, kc), pe_k, w1_k, w2_k)
    v_cmp = nsa_compress(full(c_cmp_v, vc), pe_v, w1_v, w2_v)
    ks_blk = to_sel_blocks(full(c_slc_k, ks))
    vs_blk = to_sel_blocks(full(c_slc_v, vs))
    wb = s_win_k.shape[1]
    kw_all = jnp.concatenate([s_win_k, kw], axis=1)
    vw_all = jnp.concatenate([s_win_v, vw], axis=1)
    w_pos = past - wb + jnp.arange(wb + t, dtype=jnp.int32)
    q_pos = past + jnp.arange(t, dtype=jnp.int32)
    topk = min(DSA_TOPK, (past + t) // 4)
    o_a = nsa_core(qa, ga, k_cmp, v_cmp, ks_blk, vs_blk, kw_all, vw_all, q_pos, w_pos, tbl_a)
    o_b = dsa_core(qb, qi, wi, full(c_dsa_k, kb), full(c_dsa_v, vb), full(c_idx_k, ki), q_pos, tbl_b, topk)
    y = out_l0(o_a, o_b, w_out)
    return y, (kc, vc, ks, vs, kw_all[:, -wb:], vw_all[:, -wb:], kb, vb, ki)


def proj_l1(h, w_in):
    n, t = h.shape[0], h.shape[1]
    q, k, v = split_cols(h @ w_in, L1_SIZES)
    return (q.reshape(n, t, C_KV, C_GROUP, 2, HEAD_DIM), k.reshape(n, t, C_KV, 2, HEAD_DIM),
            v.reshape(n, t, C_KV, C_VDIM))


def diff_lambda(lq1, lk1, lq2, lk2):
    f = lambda a, b: jnp.exp(jnp.sum(a.astype(jnp.float32) * b.astype(jnp.float32)))
    return f(lq1, lk1) - f(lq2, lk2) + LAMBDA_INIT


def diff_finish(o, head_norm, w_out):
    n, t = o.shape[0], o.shape[1]
    o = rmsnorm(o, head_norm) * (1.0 - LAMBDA_INIT)
    return o.reshape(n, t, -1) @ w_out


def mixer1_prompt(h, w_in, w_out, lq1, lk1, lq2, lk2, head_norm, rel_bias):
    n, t = h.shape[0], h.shape[1]
    q, k, v = proj_l1(h, w_in)
    lam = diff_lambda(lq1, lk1, lq2, lk2)
    tbl_c = rel_bias.reshape(NUM_BUCKETS, 2, C_KV, C_GROUP)

    def block(b):
        qs = b * Q_BLOCK
        q_pos = qs + jnp.arange(Q_BLOCK, dtype=jnp.int32)
        return diff_core(lax.dynamic_slice_in_dim(q, qs, Q_BLOCK, axis=1), k, v, q_pos, tbl_c, lam)

    o = unblock(lax.map(block, jnp.arange(t // Q_BLOCK, dtype=jnp.int32)), n, t)
    return diff_finish(o, head_norm, w_out), (k, v)


def mixer1_sample(h, c_k, c_v, page_table, w_in, w_out, lq1, lk1, lq2, lk2, head_norm, rel_bias):
    t = h.shape[1]
    past = page_table.shape[1] * PAGE_SIZE
    q, k, v = proj_l1(h, w_in)
    lam = diff_lambda(lq1, lk1, lq2, lk2)
    tbl_c = rel_bias.reshape(NUM_BUCKETS, 2, C_KV, C_GROUP)
    k_all = jnp.concatenate([gather_pages(c_k, page_table), k], axis=1)
    v_all = jnp.concatenate([gather_pages(c_v, page_table), v], axis=1)
    q_pos = past + jnp.arange(t, dtype=jnp.int32)
    o = diff_core(q, k_all, v_all, q_pos, tbl_c, lam)
    return diff_finish(o, head_norm, w_out), (k, v)


def setup_inputs(seed: int = 0) -> dict:
    key = jax.random.key(seed)
    k = jax.random.split(key, 40)
    f32 = jnp.float32
    nrm = lambda i, shape, s: jax.random.normal(k[i], shape, f32) * s
    n_pages = PAST_LEN // PAGE_SIZE
    n_used = DEC_BATCH * n_pages
    n_pool = n_used + max(1, n_used // 4)
    win_buf = min(WINDOW, PAST_LEN)
    page_table = jax.random.permutation(k[13], n_pool)[:n_used].reshape(DEC_BATCH, n_pages).astype(jnp.int32)
    return {
        'x_prompt': nrm(0, (BATCH, SEQ, D_MODEL), 1.0),
        'x_sample': nrm(1, (DEC_BATCH, DEC_SEQ, D_MODEL), 1.0),
        'cache_l0_nsa_cmp_k': nrm(2, (n_pool, PAGE_SIZE, A_KV, HEAD_DIM), 1.0),
        'cache_l0_nsa_cmp_v': nrm(3, (n_pool, PAGE_SIZE, A_KV, HEAD_DIM), 1.0),
        'cache_l0_nsa_slc_k': nrm(4, (n_pool, PAGE_SIZE, A_KV, HEAD_DIM), 1.0),
        'cache_l0_nsa_slc_v': nrm(5, (n_pool, PAGE_SIZE, A_KV, HEAD_DIM), 1.0),
        'state_l0_nsa_win_k': nrm(6, (DEC_BATCH, win_buf, A_KV, HEAD_DIM), 1.0),
        'state_l0_nsa_win_v': nrm(7, (DEC_BATCH, win_buf, A_KV, HEAD_DIM), 1.0),
        'cache_l0_dsa_k': nrm(8, (n_pool, PAGE_SIZE, B_KV, HEAD_DIM), 1.0),
        'cache_l0_dsa_v': nrm(9, (n_pool, PAGE_SIZE, B_KV, HEAD_DIM), 1.0),
        'cache_l0_dsa_idx_k': nrm(10, (n_pool, PAGE_SIZE, IDX_DIM), 1.0),
        'cache_l1_diff_k': nrm(11, (n_pool, PAGE_SIZE, C_KV, 2, HEAD_DIM), 1.0),
        'cache_l1_diff_v': nrm(12, (n_pool, PAGE_SIZE, C_KV, C_VDIM), 1.0),
        'page_table': page_table,
        'rel_bias': nrm(14, (NUM_BUCKETS, N_BIAS_COLS), 0.2),
        'attn_norm': 1.0 + nrm(15, (DEPTH, D_MODEL), 0.02),
        'mlp_norm': 1.0 + nrm(16, (DEPTH, D_MODEL), 0.02),
        'mlp_w1': nrm(17, (DEPTH, D_MODEL, D_FF), D_MODEL ** -0.5),
        'mlp_w2': nrm(18, (DEPTH, D_FF, D_MODEL), D_FF ** -0.5),
        'l0_w_in': nrm(19, (D_MODEL, L0_IN), D_MODEL ** -0.5),
        'l0_w_out': nrm(20, (L0_OUT, D_MODEL), L0_OUT ** -0.5),
        'l0_cmp_pe_k': nrm(21, (CMP_BLOCK, HEAD_DIM), 0.5),
        'l0_cmp_w1_k': nrm(22, (CMP_BLOCK, HEAD_DIM, CMP_HIDDEN), (CMP_BLOCK * HEAD_DIM) ** -0.5),
        'l0_cmp_w2_k': nrm(23, (CMP_HIDDEN, HEAD_DIM), CMP_HIDDEN ** -0.5),
        'l0_cmp_pe_v': nrm(24, (CMP_BLOCK, HEAD_DIM), 0.5),
        'l0_cmp_w1_v': nrm(25, (CMP_BLOCK, HEAD_DIM, CMP_HIDDEN), (CMP_BLOCK * HEAD_DIM) ** -0.5),
        'l0_cmp_w2_v': nrm(26, (CMP_HIDDEN, HEAD_DIM), CMP_HIDDEN ** -0.5),
        'l1_w_in': nrm(27, (D_MODEL, L1_IN), D_MODEL ** -0.5),
        'l1_w_out': nrm(28, (L1_OUT, D_MODEL), L1_OUT ** -0.5),
        'l1_lambda_q1': nrm(29, (HEAD_DIM,), 0.1),
        'l1_lambda_k1': nrm(30, (HEAD_DIM,), 0.1),
        'l1_lambda_q2': nrm(31, (HEAD_DIM,), 0.1),
        'l1_lambda_k2': nrm(32, (HEAD_DIM,), 0.1),
        'l1_head_norm': 1.0 + nrm(33, (C_VDIM,), 0.02),
        'final_norm': 1.0 + nrm(34, (D_MODEL,), 0.02),
    }


def reference(x_prompt, x_sample, cache_l0_nsa_cmp_k, cache_l0_nsa_cmp_v, cache_l0_nsa_slc_k, cache_l0_nsa_slc_v,
              state_l0_nsa_win_k, state_l0_nsa_win_v, cache_l0_dsa_k, cache_l0_dsa_v, cache_l0_dsa_idx_k,
              cache_l1_diff_k, cache_l1_diff_v, page_table, rel_bias, attn_norm, mlp_norm, mlp_w1, mlp_w2,
              l0_w_in, l0_w_out, l0_cmp_pe_k, l0_cmp_w1_k, l0_cmp_w2_k, l0_cmp_pe_v, l0_cmp_w1_v, l0_cmp_w2_v,
              l1_w_in, l1_w_out, l1_lambda_q1, l1_lambda_k1, l1_lambda_q2, l1_lambda_k2, l1_head_norm,
              final_norm):
    xp, xs = x_prompt, x_sample
    for layer in range(DEPTH):
        hp = rmsnorm(xp, attn_norm[layer])
        hs = rmsnorm(xs, attn_norm[layer])
        if layer % 2 == 0:
            yp, (p_cmp_k, p_cmp_v, p_slc_k, p_slc_v, p_win_k, p_win_v, p_dsa_k, p_dsa_v, p_idx_k) = mixer0_prompt(
                hp, l0_w_in, l0_w_out, l0_cmp_pe_k, l0_cmp_w1_k, l0_cmp_w2_k,
                l0_cmp_pe_v, l0_cmp_w1_v, l0_cmp_w2_v, rel_bias)
            ys, (s_cmp_k, s_cmp_v, s_slc_k, s_slc_v, s_win_k, s_win_v, s_dsa_k, s_dsa_v, s_idx_k) = mixer0_sample(
                hs, cache_l0_nsa_cmp_k, cache_l0_nsa_cmp_v, cache_l0_nsa_slc_k, cache_l0_nsa_slc_v,
                state_l0_nsa_win_k, state_l0_nsa_win_v, cache_l0_dsa_k, cache_l0_dsa_v, cache_l0_dsa_idx_k,
                page_table, l0_w_in, l0_w_out, l0_cmp_pe_k, l0_cmp_w1_k, l0_cmp_w2_k,
                l0_cmp_pe_v, l0_cmp_w1_v, l0_cmp_w2_v, rel_bias)
        else:
            yp, (p_diff_k, p_diff_v) = mixer1_prompt(
                hp, l1_w_in, l1_w_out, l1_lambda_q1, l1_lambda_k1, l1_lambda_q2, l1_lambda_k2,
                l1_head_norm, rel_bias)
            ys, (s_diff_k, s_diff_v) = mixer1_sample(
                hs, cache_l1_diff_k, cache_l1_diff_v, page_table, l1_w_in, l1_w_out,
                l1_lambda_q1, l1_lambda_k1, l1_lambda_q2, l1_lambda_k2, l1_head_norm, rel_bias)
        xp = xp + yp
        xs = xs + ys
        xp = xp + sq_relu_mlp(rmsnorm(xp, mlp_norm[layer]), mlp_w1[layer], mlp_w2[layer])
        xs = xs + sq_relu_mlp(rmsnorm(xs, mlp_norm[layer]), mlp_w1[layer], mlp_w2[layer])
    y_prompt = rmsnorm(xp, final_norm)
    y_sample = rmsnorm(xs, final_norm)
    return (y_prompt, y_sample,
            p_cmp_k, s_cmp_k, p_cmp_v, s_cmp_v, p_slc_k, s_slc_k, p_slc_v, s_slc_v,
            p_win_k, s_win_k, p_win_v, s_win_v, p_dsa_k, s_dsa_k, p_dsa_v, s_dsa_v,
            p_idx_k, s_idx_k, p_diff_k, s_diff_k, p_diff_v, s_diff_v)
```

```python
import functools
import math

import jax
import jax.numpy as jnp
from jax import lax
from jax.experimental import pallas as pl
from jax.experimental.pallas import tpu as pltpu

F32 = jnp.float32
I32 = jnp.int32
MXU_DT = jnp.bfloat16

HEAD_DIM = 128
A_HEADS, A_KV, A_GROUP = 8, 2, 4
B_HEADS, B_KV, B_GROUP = 8, 2, 4
C_HEADS, C_KV, C_GROUP, C_VDIM = 8, 4, 2, 256
CMP_STRIDE, CMP_BLOCK, CMP_HIDDEN = 16, 32, 256
SEL_BLOCK, N_SEL_BLOCKS, WINDOW, N_GATES = 64, 16, 512, 3
IDX_HEADS, IDX_DIM, DSA_TOPK = 4, 64, 256
NUM_BUCKETS, MAX_DISTANCE = 32, 128
LAMBDA_INIT = 0.8 - 0.6 * math.exp(-0.3 * 1)
RMS_EPS = 1e-6
NEG = -1e30
SCALE = HEAD_DIM ** -0.5
QB = 128
PAGE = 128

C_QA, C_QB, C_KC, C_VC, C_KS, C_VS, C_KW, C_VW, C_KB, C_VB, C_QI, C_TAIL = (
    0, 1024, 2048, 2304, 2560, 2816, 3072, 3328, 3584, 3840, 4096, 4352)
T_KI, T_GA, T_WI = 0, 64, 88
L0_COLS = 4608
VMEM_LIMIT = 56 * 1024 * 1024


def _cparams(*sem):
    return pltpu.CompilerParams(dimension_semantics=sem, vmem_limit_bytes=VMEM_LIMIT)


def _nt(a, b):
    return lax.dot_general(a, b, (((1,), (1,)), ((), ())), preferred_element_type=F32)


def _mm(a, b):
    return jnp.dot(a, b, preferred_element_type=F32)


def _rms(x, g):
    return x * lax.rsqrt(jnp.mean(x * x, axis=-1, keepdims=True) + RMS_EPS) * g


def _norm_proj_kernel(x_ref, g_ref, w_ref, o_ref, xn_ref):
    @pl.when(pl.program_id(1) == 0)
    def _():
        xn_ref[...] = _rms(x_ref[...], g_ref[...]).astype(xn_ref.dtype)

    o_ref[...] = _mm(xn_ref[...], w_ref[...])


def norm_proj(x, gain, w, tm, tn):
    rows, d = x.shape
    n = w.shape[1]
    return pl.pallas_call(
        _norm_proj_kernel,
        grid=(rows // tm, n // tn),
        in_specs=[pl.BlockSpec((tm, d), lambda i, j: (i, 0)),
                  pl.BlockSpec((1, d), lambda i, j: (0, 0)),
                  pl.BlockSpec((d, tn), lambda i, j: (0, j))],
        out_specs=pl.BlockSpec((tm, tn), lambda i, j: (i, j)),
        out_shape=jax.ShapeDtypeStruct((rows, n), F32),
        scratch_shapes=[pltpu.VMEM((tm, d), MXU_DT)],
        compiler_params=_cparams("parallel", "arbitrary"),
        name="norm_proj",
    )(x, gain.reshape(1, d), w)


def _out_proj_kernel(*refs, n_in):
    x_ref, o_refs, w_refs, y_ref = refs[0], refs[1:1 + n_in], refs[1 + n_in:1 + 2 * n_in], refs[-1]
    acc = x_ref[...]
    for o_ref, w_ref in zip(o_refs, w_refs):
        acc = acc + _mm(o_ref[...].astype(MXU_DT), w_ref[...])
    y_ref[...] = acc


def out_proj(x, outs, w, tm, tn):
    rows, d = x.shape
    o_specs, w_specs, row0 = [], [], 0
    for o in outs:
        k = o.shape[1]
        o_specs.append(pl.BlockSpec((tm, k), lambda i, j: (i, 0)))
        w_specs.append(pl.BlockSpec((k, tn), lambda i, j, rb=row0 // k: (rb, j)))
        row0 += k
    return pl.pallas_call(
        functools.partial(_out_proj_kernel, n_in=len(outs)),
        grid=(rows // tm, d // tn),
        in_specs=[pl.BlockSpec((tm, tn), lambda i, j: (i, j))] + o_specs + w_specs,
        out_specs=pl.BlockSpec((tm, tn), lambda i, j: (i, j)),
        out_shape=jax.ShapeDtypeStruct((rows, d), F32),
        compiler_params=_cparams("parallel", "arbitrary"),
        name="out_proj",
    )(x, *outs, *([w] * len(outs)))


def _mlp_kernel(x_ref, g_ref, w1_ref, w2_ref, gf_ref, y_ref, xn_ref, *, final_norm):
    j = pl.program_id(1)

    @pl.when(j == 0)
    def _():
        x = x_ref[...]
        xn_ref[...] = _rms(x, g_ref[...]).astype(xn_ref.dtype)
        y_ref[...] = x

    h = jnp.square(jnp.maximum(_mm(xn_ref[...], w1_ref[...]), 0.0))
    y_ref[...] += _mm(h.astype(w2_ref.dtype), w2_ref[...])

    if final_norm:
        @pl.when(j == pl.num_programs(1) - 1)
        def _():
            y_ref[...] = _rms(y_ref[...], gf_ref[...])


def mlp(x, gain, w1, w2, final_gain, tm, tf, final_norm):
    rows, d = x.shape
    ff = w1.shape[1]
    return pl.pallas_call(
        functools.partial(_mlp_kernel, final_norm=final_norm),
        grid=(rows // tm, ff // tf),
        in_specs=[pl.BlockSpec((tm, d), lambda i, j: (i, 0)),
                  pl.BlockSpec((1, d), lambda i, j: (0, 0)),
                  pl.BlockSpec((d, tf), lambda i, j: (0, j)),
                  pl.BlockSpec((tf, d), lambda i, j: (j, 0)),
                  pl.BlockSpec((1, d), lambda i, j: (0, 0))],
        out_specs=pl.BlockSpec((tm, d), lambda i, j: (i, 0)),
        out_shape=jax.ShapeDtypeStruct((rows, d), F32),
        scratch_shapes=[pltpu.VMEM((tm, d), MXU_DT)],
        compiler_params=_cparams("parallel", "arbitrary"),
        name="mlp",
    )(x, gain.reshape(1, d), w1, w2, final_gain.reshape(1, d))


def _bucket(dist):
    n = jnp.maximum(dist, 0)
    max_exact = NUM_BUCKETS // 2
    nf = jnp.maximum(n, 1).astype(F32)
    large = max_exact + (jnp.log(nf / max_exact) / math.log(MAX_DISTANCE / max_exact)
                         * (NUM_BUCKETS - max_exact)).astype(I32)
    large = jnp.minimum(large, NUM_BUCKETS - 1)
    return jnp.where(n < max_exact, n, large)


def _lookup(tbl_ref, col, buckets):
    def body(b, accs):
        v = tbl_ref[b, col]
        return tuple(jnp.where(bk == b, v, acc) for bk, acc in zip(buckets, accs))
    return lax.fori_loop(0, NUM_BUCKETS, body, tuple(jnp.zeros(bk.shape, F32) for bk in buckets))


def _bias_tiles_kernel(tbl_ref, tp_ref, bs_ref, *, q0):
    h = pl.program_id(0)
    t = lax.broadcasted_iota(I32, (QB, QB), 0)
    k = lax.broadcasted_iota(I32, (QB, QB), 1)
    ts = lax.broadcasted_iota(I32, bs_ref.shape[1:], 0)
    ks = lax.broadcasted_iota(I32, bs_ref.shape[1:], 1)
    d0, d1, ds = _lookup(tbl_ref, h, (_bucket(t - k), _bucket(QB + t - k), _bucket(q0 + ts - ks)))
    tp_ref[0, 0] = d0
    tp_ref[0, 1] = d1
    bs_ref[0] = ds


def bias_tiles(rel_bias, q0, n_q, lp):
    nh = rel_bias.shape[1]
    return pl.pallas_call(
        functools.partial(_bias_tiles_kernel, q0=q0),
        grid=(nh,),
        in_specs=[pl.BlockSpec(memory_space=pltpu.SMEM)],
        out_specs=[pl.BlockSpec((1, 2, QB, QB), lambda h: (h, 0, 0, 0)),
                   pl.BlockSpec((1, n_q, lp), lambda h: (h, 0, 0))],
        out_shape=[jax.ShapeDtypeStruct((nh, 2, QB, QB), F32),
                   jax.ShapeDtypeStruct((nh, n_q, lp), F32)],
        compiler_params=_cparams("arbitrary"),
        name="bias_tiles",
    )(rel_bias)


def _bias_cmp_kernel(tbl_ref, bp_ref, bs_ref, *, q0):
    h = pl.program_id(0)
    tp = lax.broadcasted_iota(I32, bp_ref.shape[1:], 0)
    cp = lax.broadcasted_iota(I32, bp_ref.shape[1:], 1)
    ts = lax.broadcasted_iota(I32, bs_ref.shape[1:], 0)
    cs = lax.broadcasted_iota(I32, bs_ref.shape[1:], 1)
    end = CMP_BLOCK - 1
    bp, bs = _lookup(tbl_ref, h, (_bucket(tp - (cp * CMP_STRIDE + end)),
                                  _bucket(q0 + ts - (cs * CMP_STRIDE + end))))
    bp_ref[0] = bp
    bs_ref[0] = bs


def bias_cmp(rel_bias, t_len, mc_p, q0, n_q, mc_s):
    return pl.pallas_call(
        functools.partial(_bias_cmp_kernel, q0=q0),
        grid=(A_HEADS,),
        in_specs=[pl.BlockSpec(memory_space=pltpu.SMEM)],
        out_specs=[pl.BlockSpec((1, t_len, mc_p), lambda h: (h, 0, 0)),
                   pl.BlockSpec((1, n_q, mc_s), lambda h: (h, 0, 0))],
        out_shape=[jax.ShapeDtypeStruct((A_HEADS, t_len, mc_p), F32),
                   jax.ShapeDtypeStruct((A_HEADS, n_q, mc_s), F32)],
        compiler_params=_cparams("arbitrary"),
        name="bias_cmp",
    )(rel_bias)


def _softmax_rows(z, mask):
    z = jnp.where(mask, z, NEG)
    m = jnp.max(z, axis=-1, keepdims=True)
    e = jnp.where(mask, jnp.exp(z - m), 0.0)
    l = jnp.sum(e, axis=-1, keepdims=True)
    return e / jnp.where(l > 0.0, l, 1.0)


def _gelu_tanh(x):
    return 0.5 * x * (1.0 + jnp.tanh(math.sqrt(2.0 / math.pi) * (x + 0.044715 * (x * x * x))))


def _compress(x_ref, w1a_ref, w1b_ref, w2_ref, pe_ref, o_ref):
    m = x_ref.shape[0]
    pe = pe_ref[...].astype(MXU_DT)
    pos = _mm(pe, w1a_ref[...])[0:1] + _mm(pe, w1b_ref[...])[1:2]
    last = lax.broadcasted_iota(I32, (m, 1), 0) == m - 1
    for g in range(A_KV):
        xg = jnp.concatenate(
            [x_ref[:, (2 * j + g) * HEAD_DIM:(2 * j + g + 1) * HEAD_DIM].astype(MXU_DT)
             for j in range(CMP_STRIDE)], axis=1)
        first = _mm(xg, w1a_ref[...])
        second = pltpu.roll(_mm(xg, w1b_ref[...]), m - 1, 0)
        hid = _gelu_tanh(first + second + pos)
        out = _mm(hid.astype(MXU_DT), w2_ref[...])
        o_ref[:, g * HEAD_DIM:(g + 1) * HEAD_DIM] = jnp.where(last, 0.0, out)


def _overlap(mc, jn, n_cmp, n_slc):
    c = lax.broadcasted_iota(I32, (mc, jn), 0)
    j = lax.broadcasted_iota(I32, (mc, jn), 1)
    ov = ((c * CMP_STRIDE < j * SEL_BLOCK + SEL_BLOCK) & (c * CMP_STRIDE + CMP_BLOCK > j * SEL_BLOCK)
          & (c < n_cmp) & (j < n_slc))
    return jnp.where(ov, 1.0, 0.0)


def _select_blocks(imp, pos, n_slc):
    jn = imp.shape[1]
    jidx = lax.broadcasted_iota(I32, (1, jn), 1)
    cur = pos // SEL_BLOCK
    forced = (jidx == 0) | (jidx == cur) | (jidx == cur - 1)
    future = jidx * SEL_BLOCK > pos
    score = jnp.where(future, -1.0, jnp.where(forced, 1e3, imp))
    score = jnp.where(jidx < n_slc, score, -2.0)

    def body(i, rank):
        col = jnp.sum(jnp.where(jidx == i, score, 0.0), axis=-1, keepdims=True)
        beats = jnp.where(col > score, 1.0, jnp.where(col == score, jnp.where(i < jidx, 1.0, 0.0), 0.0))
        return rank + beats

    rank = lax.fori_loop(0, n_slc, body, jnp.zeros(score.shape, F32))
    n_sel = min(N_SEL_BLOCKS, n_slc)
    return jnp.where((rank < n_sel) & (jidx < n_slc), 1.0, 0.0)


def _topk_mask(s, key_ref, k, nbits):
    s = jnp.where(s == 0.0, 0.0, s)
    bits = lax.bitcast_convert_type(s, I32)
    key_ref[...] = jnp.where(bits < 0, bits ^ jnp.int32(0x7FFFFFFF), bits)
    kf = jnp.float32(k)

    def count_ge(c):
        return jnp.sum(jnp.where(key_ref[...] >= c, 1.0, 0.0), axis=-1, keepdims=True)

    int_min = jnp.int32(-2 ** 31)
    thr0 = jnp.where(count_ge(jnp.int32(0)) >= kf, jnp.int32(0), int_min)

    def vbody(i, thr):
        cand = thr | lax.shift_left(jnp.int32(1), 30 - i)
        return jnp.where(count_ge(cand) >= kf, cand, thr)

    thr = lax.fori_loop(0, 31, vbody, thr0)
    key = key_ref[...]
    gt = key > thr
    eq = key == thr
    need = kf - jnp.sum(jnp.where(gt, 1.0, 0.0), axis=-1, keepdims=True)
    idx = lax.broadcasted_iota(I32, s.shape, 1)

    def ibody(i, c):
        cand = c | lax.shift_left(jnp.int32(1), nbits - 1 - i)
        cnt = jnp.sum(jnp.where(eq & (idx < cand), 1.0, 0.0), axis=-1, keepdims=True)
        return jnp.where(cnt <= need, cand, c)

    cut = lax.fori_loop(0, nbits, ibody, jnp.zeros((s.shape[0], 1), I32))
    return gt | (eq & (idx < cut))


def _flash(q, k_ref, v_ref, kcol, vcol, dv, lo, hi, tile_fn):
    m_rows = q.shape[0]

    def body(kt, carry):
        m, l, acc = carry
        r0 = pl.multiple_of(kt * QB, QB)
        k = k_ref[pl.ds(r0, QB), kcol:kcol + HEAD_DIM].astype(MXU_DT)
        bias, mask = tile_fn(kt)
        s = jnp.where(mask, _nt(q, k) * SCALE + bias, NEG)
        mn = jnp.maximum(m, jnp.max(s, axis=-1, keepdims=True))
        p = jnp.where(mask, jnp.exp(s - mn), 0.0)
        a = jnp.exp(m - mn)
        v = v_ref[pl.ds(r0, QB), vcol:vcol + dv].astype(MXU_DT)
        return mn, a * l + jnp.sum(p, axis=-1, keepdims=True), a * acc + _mm(p.astype(MXU_DT), v)

    init = (jnp.full((m_rows, 1), NEG, F32), jnp.zeros((m_rows, 1), F32), jnp.zeros((m_rows, dv), F32))
    _, l, acc = lax.fori_loop(lo, hi, body, init)
    return acc / jnp.where(l > 0.0, l, 1.0)


def _prompt_bias_tile(tbl_ref, tp_ref, cols, kt, qb):
    tiles = [jnp.where(kt == qb, tp_ref[c, 0],
                       jnp.where(kt == qb - 1, tp_ref[c, 1], tbl_ref[NUM_BUCKETS - 1, c])) for c in cols]
    return jnp.concatenate(tiles, axis=0)


def _stack_heads(ref, col0, n):
    return jnp.concatenate([ref[:, col0 + r * HEAD_DIM:col0 + (r + 1) * HEAD_DIM] for r in range(n)], axis=0)


def _compress_prompt_kernel(xk_ref, xv_ref, w1ak, w1bk, w2k, pek, w1av, w1bv, w2v, pev, ok_ref, ov_ref):
    _compress(xk_ref.at[0], w1ak, w1bk, w2k, pek, ok_ref.at[0])
    _compress(xv_ref.at[0], w1av, w1bv, w2v, pev, ov_ref.at[0])


def _cmp_weight_specs():
    full = lambda shape: pl.BlockSpec(shape, lambda *_: (0,) * len(shape))
    half = CMP_STRIDE * HEAD_DIM
    one = [full((half, CMP_HIDDEN)), full((half, CMP_HIDDEN)), full((CMP_HIDDEN, HEAD_DIM)), full((16, half))]
    return one + one


def compress_prompt(xk, xv, cw):
    n, m, w = xk.shape
    spec = pl.BlockSpec((1, m, w), lambda i: (i, 0, 0))
    ospec = pl.BlockSpec((1, m, A_KV * HEAD_DIM), lambda i: (i, 0, 0))
    osh = jax.ShapeDtypeStruct((n, m, A_KV * HEAD_DIM), F32)
    return pl.pallas_call(
        _compress_prompt_kernel,
        grid=(n,),
        in_specs=[spec, spec] + _cmp_weight_specs(),
        out_specs=[ospec, ospec],
        out_shape=[osh, osh],
        compiler_params=_cparams("parallel"),
        name="compress_prompt",
    )(xk, xv, *cw)


def _nsa_prompt_kernel(tbl_ref, q_ref, tail_ref, kc_ref, vc_ref, ks_ref, vs_ref, kw_ref, vw_ref,
                       tp_ref, bc_ref, o_ref, *, t_len):
    qb = pl.program_id(1)
    mc = kc_ref.shape[1]
    n_cmp = (t_len - CMP_BLOCK) // CMP_STRIDE + 1
    n_slc = -(-t_len // SEL_BLOCK)
    pos = qb * QB + lax.broadcasted_iota(I32, (QB, 1), 0)
    pos4 = jnp.concatenate([pos] * A_GROUP, axis=0)
    klane = lax.broadcasted_iota(I32, (1, QB), 1)
    jrow = lax.broadcasted_iota(I32, (QB, 1), 0)
    cidx = lax.broadcasted_iota(I32, (1, mc), 1)
    gates = jax.nn.sigmoid(tail_ref[...])
    overlap = _overlap(mc, QB, n_cmp, n_slc)

    for g in range(A_KV):
        cols = [g * A_GROUP + r for r in range(A_GROUP)]
        q = _stack_heads(q_ref, g * A_GROUP * HEAD_DIM, A_GROUP).astype(MXU_DT)
        kc = kc_ref[0, :, g * HEAD_DIM:(g + 1) * HEAD_DIM].astype(MXU_DT)
        vc = vc_ref[0, :, g * HEAD_DIM:(g + 1) * HEAD_DIM].astype(MXU_DT)
        lc = _nt(q, kc) * SCALE + jnp.concatenate([bc_ref[c] for c in cols], axis=0)
        p_cmp = _softmax_rows(lc, (pos4 >= cidx * CMP_STRIDE + (CMP_BLOCK - 1)) & (cidx < n_cmp))
        o_cmp = _mm(p_cmp.astype(MXU_DT), vc)
        p_sum = sum(p_cmp[r * QB:(r + 1) * QB] for r in range(A_GROUP))
        imp = jnp.dot(p_sum, overlap, preferred_element_type=F32, precision=lax.Precision.HIGHEST)
        sel = _select_blocks(imp, pos, n_slc).astype(MXU_DT)

        def slc_tile(kt):
            onehot = jnp.where(jrow == 2 * kt + klane // SEL_BLOCK, 1.0, 0.0).astype(MXU_DT)
            chosen = _mm(sel, onehot) > 0.5
            mask = jnp.concatenate([chosen] * A_GROUP, axis=0) & (kt * QB + klane <= pos4)
            return _prompt_bias_tile(tbl_ref, tp_ref, cols, kt, qb), mask

        o_slc = _flash(q, ks_ref, vs_ref, g * HEAD_DIM, g * HEAD_DIM, HEAD_DIM, 0, qb + 1, slc_tile)

        def win_tile(kt):
            dist = pos4 - (kt * QB + klane)
            return _prompt_bias_tile(tbl_ref, tp_ref, cols, kt, qb), (dist >= 0) & (dist < WINDOW)

        o_win = _flash(q, kw_ref, vw_ref, g * HEAD_DIM, g * HEAD_DIM, HEAD_DIM,
                       jnp.maximum(qb - WINDOW // QB, 0), qb + 1, win_tile)

        for r in range(A_GROUP):
            h = g * A_GROUP + r
            gl = T_GA + h * N_GATES
            rows = slice(r * QB, (r + 1) * QB)
            o = (gates[:, gl:gl + 1] * o_cmp[rows] + gates[:, gl + 1:gl + 2] * o_slc[rows]
                 + gates[:, gl + 2:gl + 3] * o_win[rows])
            o_ref[:, h * HEAD_DIM:(h + 1) * HEAD_DIM] = o.astype(o_ref.dtype)


def nsa_prompt(z, k_cmp, v_cmp, tp, bc, rel_bias, n, t_len):
    nb = t_len // QB
    mc = k_cmp.shape[1]
    kv = lambda c: pl.BlockSpec((t_len, 256), lambda i, j: (i, c // 256))
    cmp_spec = pl.BlockSpec((1, mc, 256), lambda i, j: (i, 0, 0))
    return pl.pallas_call(
        functools.partial(_nsa_prompt_kernel, t_len=t_len),
        grid=(n, nb),
        in_specs=[pl.BlockSpec(memory_space=pltpu.SMEM),
                  pl.BlockSpec((QB, 1024), lambda i, j: (i * nb + j, C_QA // 1024)),
                  pl.BlockSpec((QB, 128), lambda i, j: (i * nb + j, C_TAIL // 128)),
                  cmp_spec, cmp_spec, kv(C_KS), kv(C_VS), kv(C_KW), kv(C_VW),
                  pl.BlockSpec(tp.shape, lambda i, j: (0, 0, 0, 0)),
                  pl.BlockSpec((A_HEADS, QB, mc), lambda i, j: (0, j, 0))],
        out_specs=pl.BlockSpec((QB, 1024), lambda i, j: (i * nb + j, 0)),
        out_shape=jax.ShapeDtypeStruct((n * t_len, 1024), MXU_DT),
        compiler_params=_cparams("parallel", "arbitrary"),
        name="nsa_prompt",
    )(rel_bias, z, z, k_cmp, v_cmp, z, z, z, z, tp, bc)


def _dsa_prompt_kernel(tbl_ref, q_ref, qi_ref, tailq_ref, tailk_ref, kb_ref, vb_ref, tp_ref, o_ref,
                       key_ref, sel_ref, *, topk, nbits):
    qb = pl.program_id(1)
    t_len = tailk_ref.shape[0]
    pos = qb * QB + lax.broadcasted_iota(I32, (QB, 1), 0)
    kpos = lax.broadcasted_iota(I32, (1, t_len), 1)
    ki = tailk_ref[:, T_KI:T_KI + IDX_DIM].astype(MXU_DT)
    wi = tailq_ref[:, T_WI:T_WI + IDX_HEADS]
    score = jnp.zeros((QB, t_len), F32)
    for h in range(IDX_HEADS):
        qi = qi_ref[:, h * IDX_DIM:(h + 1) * IDX_DIM].astype(MXU_DT)
        score = score + jnp.maximum(_nt(qi, ki), 0.0) * wi[:, h:h + 1]
    score = score * (IDX_DIM ** -0.5 * IDX_HEADS ** -0.5)
    causal = kpos <= pos
    sel = _topk_mask(jnp.where(causal, score, NEG), key_ref, topk, nbits) & causal
    sel_ref[...] = jnp.where(sel, 1.0, 0.0)

    for g in range(B_KV):
        cols = [A_HEADS + g * B_GROUP + r for r in range(B_GROUP)]
        q = _stack_heads(q_ref, g * B_GROUP * HEAD_DIM, B_GROUP).astype(MXU_DT)

        def tile(kt):
            chosen = sel_ref[:, pl.ds(pl.multiple_of(kt * QB, QB), QB)] > 0.5
            return (_prompt_bias_tile(tbl_ref, tp_ref, cols, kt, qb),
                    jnp.concatenate([chosen] * B_GROUP, axis=0))

        o = _flash(q, kb_ref, vb_ref, g * HEAD_DIM, g * HEAD_DIM, HEAD_DIM, 0, qb + 1, tile)
        for r in range(B_GROUP):
            h = g * B_GROUP + r
            o_ref[:, h * HEAD_DIM:(h + 1) * HEAD_DIM] = o[r * QB:(r + 1) * QB].astype(o_ref.dtype)


def dsa_prompt(z, tp, rel_bias, n, t_len):
    nb = t_len // QB
    topk = min(DSA_TOPK, t_len // 4)
    nbits = int(t_len).bit_length()
    return pl.pallas_call(
        functools.partial(_dsa_prompt_kernel, topk=topk, nbits=nbits),
        grid=(n, nb),
        in_specs=[pl.BlockSpec(memory_space=pltpu.SMEM),
                  pl.BlockSpec((QB, 1024), lambda i, j: (i * nb + j, C_QB // 1024)),
                  pl.BlockSpec((QB, 256), lambda i, j: (i * nb + j, C_QI // 256)),
                  pl.BlockSpec((QB, 128), lambda i, j: (i * nb + j, C_TAIL // 128)),
                  pl.BlockSpec((t_len, 128), lambda i, j: (i, C_TAIL // 128)),
                  pl.BlockSpec((t_len, 256), lambda i, j: (i, C_KB // 256)),
                  pl.BlockSpec((t_len, 256), lambda i, j: (i, C_VB // 256)),
                  pl.BlockSpec(tp.shape, lambda i, j: (0, 0, 0, 0))],
        out_specs=pl.BlockSpec((QB, 1024), lambda i, j: (i * nb + j, 0)),
        out_shape=jax.ShapeDtypeStruct((n * t_len, 1024), MXU_DT),
        scratch_shapes=[pltpu.VMEM((QB, t_len), I32), pltpu.VMEM((QB, t_len), F32)],
        compiler_params=_cparams("parallel", "arbitrary"),
        name="dsa_prompt",
    )(rel_bias, z, z, z, z, z, z, tp)


def _diff_lambda(lam_ref):
    v = lam_ref[...]
    e1 = jnp.exp(jnp.sum(v[0:1] * v[1:2], axis=-1, keepdims=True))
    e2 = jnp.exp(jnp.sum(v[2:3] * v[3:4], axis=-1, keepdims=True))
    return e1 - e2 + LAMBDA_INIT


def _diff_finish(o, hn_ref):
    return _rms(o, hn_ref[...]) * (1.0 - LAMBDA_INIT)


def _diff_prompt_kernel(tbl_ref, q_ref, k_ref, v_ref, tp_ref, lam_ref, hn_ref, o_ref):
    g = pl.program_id(1)
    qb = pl.program_id(2)
    pos2 = jnp.concatenate([qb * QB + lax.broadcasted_iota(I32, (QB, 1), 0)] * C_GROUP, axis=0)
    klane = lax.broadcasted_iota(I32, (1, QB), 1)
    outs = []
    for m in range(2):
        q = jnp.concatenate([q_ref[:, (r * 2 + m) * HEAD_DIM:(r * 2 + m + 1) * HEAD_DIM]
                             for r in range(C_GROUP)], axis=0).astype(MXU_DT)

        def tile(kt, m=m):
            tiles = []
            for r in range(C_GROUP):
                c = m * C_HEADS + g * C_GROUP + r
                tiles.append(jnp.where(kt == qb, tp_ref[c, 0],
                                       jnp.where(kt == qb - 1, tp_ref[c, 1], tbl_ref[NUM_BUCKETS - 1, c])))
            return jnp.concatenate(tiles, axis=0), kt * QB + klane <= pos2

        outs.append(_flash(q, k_ref, v_ref, m * HEAD_DIM, 0, C_VDIM, 0, qb + 1, tile))
    o = _diff_finish(outs[0] - _diff_lambda(lam_ref) * outs[1], hn_ref)
    for r in range(C_GROUP):
        o_ref[:, r * C_VDIM:(r + 1) * C_VDIM] = o[r * QB:(r + 1) * QB].astype(o_ref.dtype)


def diff_prompt(z1, tp, rel_bias, lam_vecs, head_norm, n, t_len):
    nb = t_len // QB
    return pl.pallas_call(
        _diff_prompt_kernel,
        grid=(n, C_KV, nb),
        in_specs=[pl.BlockSpec(memory_space=pltpu.SMEM),
                  pl.BlockSpec((QB, 512), lambda i, g, j: (i * nb + j, g)),
                  pl.BlockSpec((t_len, 256), lambda i, g, j: (i, 2048 // 256 + g)),
                  pl.BlockSpec((t_len, 256), lambda i, g, j: (i, 3072 // 256 + g)),
                  pl.BlockSpec(tp.shape, lambda i, g, j: (0, 0, 0, 0)),
                  pl.BlockSpec((4, HEAD_DIM), lambda i, g, j: (0, 0)),
                  pl.BlockSpec((1, C_VDIM), lambda i, g, j: (0, 0))],
        out_specs=pl.BlockSpec((QB, 512), lambda i, g, j: (i * nb + j, g)),
        out_shape=jax.ShapeDtypeStruct((n * t_len, C_HEADS * C_VDIM), MXU_DT),
        compiler_params=_cparams("parallel", "parallel", "arbitrary"),
        name="diff_prompt",
    )(rel_bias, z1, z1, z1, tp, lam_vecs, head_norm.reshape(1, C_VDIM))


def _gather_start(src_fn, pt_ref, b, n_pages, buf_ref, rows, sem):
    def body(p, _):
        dst = buf_ref.at[pl.ds(pl.multiple_of(p * rows, rows), rows)]
        pltpu.make_async_copy(src_fn(pt_ref[b, p]), dst, sem).start()
        return 0
    lax.fori_loop(0, n_pages, body, 0)


def _gather_wait(src_fn, n_pages, buf_ref, rows, sem):
    def body(p, _):
        pltpu.make_async_copy(src_fn(0), buf_ref.at[pl.ds(0, rows)], sem).wait()
        return 0
    lax.fori_loop(0, n_pages, body, 0)


def _pad_rows(x, rows):
    return jnp.concatenate([x, jnp.zeros((rows - x.shape[0], x.shape[1]), x.dtype)], axis=0)


def _sample_scores(q, kbuf, kcol, knew, bias_fn, mask_fn, s_ref, past, ch, scale=SCALE):
    dk = q.shape[1]

    def body(c, _):
        c0 = pl.multiple_of(c * ch, ch)
        k = kbuf[pl.ds(c0, ch), kcol:kcol + dk].astype(MXU_DT)
        s = _nt(q, k) * scale + bias_fn(c0, ch)
        s_ref[:, pl.ds(c0, ch)] = jnp.where(mask_fn(c0, ch, False), s, NEG)
        return 0

    lax.fori_loop(0, past // ch, body, 0)
    s = _nt(q, _pad_rows(knew, 128).astype(MXU_DT)) * scale + bias_fn(past, 128)
    s_ref[:, past:past + 128] = jnp.where(mask_fn(past, 128, True), s, NEG)


def _sample_softmax(s_ref):
    z = s_ref[...]
    m = jnp.max(z, axis=-1, keepdims=True)
    e = jnp.where(z > 0.5 * NEG, jnp.exp(z - m), 0.0)
    l = jnp.sum(e, axis=-1, keepdims=True)
    return e / jnp.where(l > 0.0, l, 1.0)


def _sample_pv(p_ref, vbuf, vcol, dv, vnew, past, ch):
    def body(c, acc):
        c0 = pl.multiple_of(c * ch, ch)
        v = vbuf[pl.ds(c0, ch), vcol:vcol + dv].astype(MXU_DT)
        return acc + _mm(p_ref[:, pl.ds(c0, ch)].astype(MXU_DT), v)

    acc = lax.fori_loop(0, past // ch, body, jnp.zeros((p_ref.shape[0], dv), F32))
    return acc + _mm(p_ref[:, past:past + 128].astype(MXU_DT), _pad_rows(vnew, 128).astype(MXU_DT))


def _new_key_mask(nq, rep):
    t = lax.broadcasted_iota(I32, (nq, 128), 0)
    j = lax.broadcasted_iota(I32, (nq, 128), 1)
    return jnp.concatenate([(j <= t) & (j < nq)] * rep, axis=0)


def _compress_sample_kernel(pt_ref, pk_ref, pv_ref, w1ak, w1bk, w2k, pek, w1av, w1bv, w2v, pev,
                            ok_ref, ov_ref, bk_ref, bv_ref, sem, *, n_pages):
    b = pl.program_id(0)
    rows = PAGE // CMP_STRIDE
    _gather_start(lambda pg: pk_ref.at[pg], pt_ref, b, n_pages, bk_ref, rows, sem.at[0])
    _gather_start(lambda pg: pv_ref.at[pg], pt_ref, b, n_pages, bv_ref, rows, sem.at[1])
    _gather_wait(lambda pg: pk_ref.at[pg], n_pages, bk_ref, rows, sem.at[0])
    _compress(bk_ref, w1ak, w1bk, w2k, pek, ok_ref.at[0])
    _gather_wait(lambda pg: pv_ref.at[pg], n_pages, bv_ref, rows, sem.at[1])
    _compress(bv_ref, w1av, w1bv, w2v, pev, ov_ref.at[0])


def compress_sample(pool_k, pool_v, page_table, cw):
    bd, n_pages = page_table.shape
    rows = PAGE // CMP_STRIDE
    m = n_pages * rows
    w = pool_k.shape[2]
    ospec = pl.BlockSpec((1, m, A_KV * HEAD_DIM), lambda i, pt: (i, 0, 0))
    osh = jax.ShapeDtypeStruct((bd, m, A_KV * HEAD_DIM), F32)
    return pl.pallas_call(
        functools.partial(_compress_sample_kernel, n_pages=n_pages),
        grid_spec=pltpu.PrefetchScalarGridSpec(
            num_scalar_prefetch=1, grid=(bd,),
            in_specs=[pl.BlockSpec(memory_space=pl.ANY), pl.BlockSpec(memory_space=pl.ANY)] + _cmp_weight_specs(),
            out_specs=[ospec, ospec],
            scratch_shapes=[pltpu.VMEM((m, w), F32), pltpu.VMEM((m, w), F32), pltpu.SemaphoreType.DMA((2,))]),
        out_shape=[osh, osh],
        compiler_params=_cparams("arbitrary"),
        name="compress_sample",
    )(page_table, pool_k, pool_v, *cw)


def _nsa_sample_kernel(pt_ref, z_ref, kc_ref, vc_ref, pks_ref, pvs_ref, wk_ref, wv_ref, bs_ref, bc_ref,
                       o_ref, kbuf, vbuf, s_ref, sw_ref, sem, *, n_pages, ch):
    b = pl.program_id(0)
    past = n_pages * PAGE
    nq = z_ref.shape[0]
    mc = kc_ref.shape[1]
    t_len = past + nq
    n_cmp = (t_len - CMP_BLOCK) // CMP_STRIDE + 1
    n_slc = -(-t_len // SEL_BLOCK)
    jn = 128 * (-(-n_slc // 128))
    wb = wk_ref.shape[1]
    _gather_start(lambda pg: pks_ref.at[pg], pt_ref, b, n_pages, kbuf, PAGE, sem.at[0])
    _gather_start(lambda pg: pvs_ref.at[pg], pt_ref, b, n_pages, vbuf, PAGE, sem.at[1])

    pos = past + lax.broadcasted_iota(I32, (nq, 1), 0)
    pos4 = jnp.concatenate([pos] * A_GROUP, axis=0)
    cidx = lax.broadcasted_iota(I32, (1, mc), 1)
    gates = jax.nn.sigmoid(z_ref[:, C_TAIL:C_TAIL + 128])
    overlap = _overlap(mc, jn, n_cmp, n_slc)
    new_mask = _new_key_mask(nq, A_GROUP)
    waited = False

    for g in range(A_KV):
        cols = [g * A_GROUP + r for r in range(A_GROUP)]
        q = _stack_heads(z_ref, C_QA + g * A_GROUP * HEAD_DIM, A_GROUP).astype(MXU_DT)
        gl = slice(g * HEAD_DIM, (g + 1) * HEAD_DIM)
        lc = (_nt(q, kc_ref[0, :, gl].astype(MXU_DT)) * SCALE
              + jnp.concatenate([bc_ref[c] for c in cols], axis=0))
        p_cmp = _softmax_rows(lc, (pos4 >= cidx * CMP_STRIDE + (CMP_BLOCK - 1)) & (cidx < n_cmp))
        o_cmp = _mm(p_cmp.astype(MXU_DT), vc_ref[0, :, gl].astype(MXU_DT))
        p_sum = sum(p_cmp[r * nq:(r + 1) * nq] for r in range(A_GROUP))
        imp = jnp.dot(p_sum, overlap, preferred_element_type=F32, precision=lax.Precision.HIGHEST)
        sel = _select_blocks(imp, pos, n_slc).astype(MXU_DT)
        def win_bias(c0, w, cols=cols):
            return jnp.concatenate([bs_ref[c, :, pl.ds(past - wb + c0, w)] for c in cols], axis=0)

        def win_mask(c0, w, is_new):
            dist = pos4 - (past - wb + c0 + lax.broadcasted_iota(I32, (1, w), 1))
            valid = (dist >= 0) & (dist < WINDOW)
            return valid & new_mask if is_new else valid

        _sample_scores(q, wk_ref.at[0], g * HEAD_DIM, z_ref[:, C_KW + g * HEAD_DIM:C_KW + (g + 1) * HEAD_DIM],
                       win_bias, win_mask, sw_ref, wb, wb)
        sw_ref[...] = _sample_softmax(sw_ref)
        o_win = _sample_pv(sw_ref, wv_ref.at[0], g * HEAD_DIM, HEAD_DIM,
                           z_ref[:, C_VW + g * HEAD_DIM:C_VW + (g + 1) * HEAD_DIM], wb, wb)
        if not waited:
            _gather_wait(lambda pg: pks_ref.at[pg], n_pages, kbuf, PAGE, sem.at[0])
            _gather_wait(lambda pg: pvs_ref.at[pg], n_pages, vbuf, PAGE, sem.at[1])
            waited = True

        def slc_bias(c0, w, cols=cols):
            return jnp.concatenate([bs_ref[c, :, pl.ds(c0, w)] for c in cols], axis=0)

        def slc_mask(c0, w, is_new, sel=sel):
            blk = (c0 + lax.broadcasted_iota(I32, (jn, w), 1)) // SEL_BLOCK
            onehot = jnp.where(lax.broadcasted_iota(I32, (jn, w), 0) == blk, 1.0, 0.0).astype(MXU_DT)
            chosen = jnp.concatenate([_mm(sel, onehot) > 0.5] * A_GROUP, axis=0)
            return chosen & new_mask if is_new else chosen

        _sample_scores(q, kbuf, g * HEAD_DIM, z_ref[:, C_KS + g * HEAD_DIM:C_KS + (g + 1) * HEAD_DIM],
                       slc_bias, slc_mask, s_ref, past, ch)
        s_ref[...] = _sample_softmax(s_ref)
        o_slc = _sample_pv(s_ref, vbuf, g * HEAD_DIM, HEAD_DIM,
                           z_ref[:, C_VS + g * HEAD_DIM:C_VS + (g + 1) * HEAD_DIM], past, ch)
        for r in range(A_GROUP):
            h = g * A_GROUP + r
            c = T_GA + h * N_GATES
            rows = slice(r * nq, (r + 1) * nq)
            o_ref[:, h * HEAD_DIM:(h + 1) * HEAD_DIM] = (
                gates[:, c:c + 1] * o_cmp[rows] + gates[:, c + 1:c + 2] * o_slc[rows]
                + gates[:, c + 2:c + 3] * o_win[rows])


def nsa_sample(zs, k_cmp, v_cmp, pool_ks, pool_vs, win_k, win_v, bs, bc, page_table, ch=1024):
    bd, n_pages = page_table.shape
    nq = zs.shape[0] // bd
    past = n_pages * PAGE
    mc = k_cmp.shape[1]
    wb = win_k.shape[1]
    im3 = lambda i, pt: (i, 0, 0)
    return pl.pallas_call(
        functools.partial(_nsa_sample_kernel, n_pages=n_pages, ch=ch),
        grid_spec=pltpu.PrefetchScalarGridSpec(
            num_scalar_prefetch=1, grid=(bd,),
            in_specs=[pl.BlockSpec((nq, zs.shape[1]), lambda i, pt: (i, 0)),
                      pl.BlockSpec((1, mc, 256), im3), pl.BlockSpec((1, mc, 256), im3),
                      pl.BlockSpec(memory_space=pl.ANY), pl.BlockSpec(memory_space=pl.ANY),
                      pl.BlockSpec((1, wb, 256), im3), pl.BlockSpec((1, wb, 256), im3),
                      pl.BlockSpec((A_HEADS,) + bs.shape[1:], lambda i, pt: (0, 0, 0)),
                      pl.BlockSpec(bc.shape, lambda i, pt: (0, 0, 0))],
            out_specs=pl.BlockSpec((nq, 1024), lambda i, pt: (i, 0)),
            scratch_shapes=[pltpu.VMEM((past, 256), F32), pltpu.VMEM((past, 256), F32),
                            pltpu.VMEM((A_GROUP * nq, past + 128), F32),
                            pltpu.VMEM((A_GROUP * nq, wb + 128), F32),
                            pltpu.SemaphoreType.DMA((2,))]),
        out_shape=jax.ShapeDtypeStruct((bd * nq, 1024), F32),
        compiler_params=_cparams("arbitrary"),
        name="nsa_sample",
    )(page_table, zs, k_cmp, v_cmp, pool_ks, pool_vs, win_k, win_v, bs, bc)


def _dsa_sample_kernel(pt_ref, z_ref, pk_ref, pv_ref, pi_ref, bs_ref, o_ref,
                       kbuf, vbuf, ibuf, s_ref, sc_ref, key_ref, sel_ref, sem, *, n_pages, ch, topk, nbits):
    b = pl.program_id(0)
    past = n_pages * PAGE
    nq = z_ref.shape[0]
    _gather_start(lambda pg: pi_ref.at[pg], pt_ref, b, n_pages, ibuf, PAGE, sem.at[2])
    _gather_start(lambda pg: pk_ref.at[pg], pt_ref, b, n_pages, kbuf, PAGE, sem.at[0])
    _gather_start(lambda pg: pv_ref.at[pg], pt_ref, b, n_pages, vbuf, PAGE, sem.at[1])
    qi = jnp.concatenate([z_ref[:, C_QI + h * IDX_DIM:C_QI + (h + 1) * IDX_DIM] for h in range(IDX_HEADS)],
                         axis=0).astype(MXU_DT)
    wi = z_ref[:, C_TAIL + T_WI:C_TAIL + T_WI + IDX_HEADS]
    _gather_wait(lambda pg: pi_ref.at[pg], n_pages, ibuf, PAGE, sem.at[2])
    zero = lambda c0, w: jnp.zeros((IDX_HEADS * nq, w), F32)
    true = lambda c0, w, is_new: jnp.full((IDX_HEADS * nq, w), True)
    _sample_scores(qi, ibuf, 0, z_ref[:, C_TAIL + T_KI:C_TAIL + T_KI + IDX_DIM], zero, true, s_ref,
                   past, ch, scale=1.0)
    rel = jnp.maximum(s_ref[...], 0.0)
    score = sum(rel[h * nq:(h + 1) * nq] * wi[:, h:h + 1] for h in range(IDX_HEADS))
    score = score * (IDX_DIM ** -0.5 * IDX_HEADS ** -0.5)
    new_j = lax.broadcasted_iota(I32, score.shape, 1) - past
    causal = (new_j < 0) | ((new_j <= lax.broadcasted_iota(I32, score.shape, 0)) & (new_j < nq))
    sel = _topk_mask(jnp.where(causal, score, NEG), key_ref, topk, nbits) & causal
    sel_ref[...] = jnp.where(sel, 1.0, 0.0)

    _gather_wait(lambda pg: pk_ref.at[pg], n_pages, kbuf, PAGE, sem.at[0])
    _gather_wait(lambda pg: pv_ref.at[pg], n_pages, vbuf, PAGE, sem.at[1])
    for g in range(B_KV):
        cols = [g * B_GROUP + r for r in range(B_GROUP)]
        q = _stack_heads(z_ref, C_QB + g * B_GROUP * HEAD_DIM, B_GROUP).astype(MXU_DT)

        def bias(c0, w, cols=cols):
            return jnp.concatenate([bs_ref[c, :, pl.ds(c0, w)] for c in cols], axis=0)

        def mask(c0, w, is_new):
            return jnp.concatenate([sel_ref[:, pl.ds(c0, w)] > 0.5] * B_GROUP, axis=0)

        _sample_scores(q, kbuf, g * HEAD_DIM, z_ref[:, C_KB + g * HEAD_DIM:C_KB + (g + 1) * HEAD_DIM],
                       bias, mask, sc_ref, past, ch)
        sc_ref[...] = _sample_softmax(sc_ref)
        o = _sample_pv(sc_ref, vbuf, g * HEAD_DIM, HEAD_DIM,
                       z_ref[:, C_VB + g * HEAD_DIM:C_VB + (g + 1) * HEAD_DIM], past, ch)
        for r in range(B_GROUP):
            h = g * B_GROUP + r
            o_ref[:, h * HEAD_DIM:(h + 1) * HEAD_DIM] = o[r * nq:(r + 1) * nq]


def dsa_sample(zs, pool_k, pool_v, pool_i, bs, page_table, ch=1024):
    bd, n_pages = page_table.shape
    nq = zs.shape[0] // bd
    past = n_pages * PAGE
    lp = past + 128
    topk = min(DSA_TOPK, (past + nq) // 4)
    return pl.pallas_call(
        functools.partial(_dsa_sample_kernel, n_pages=n_pages, ch=ch, topk=topk, nbits=int(lp).bit_length()),
        grid_spec=pltpu.PrefetchScalarGridSpec(
            num_scalar_prefetch=1, grid=(bd,),
            in_specs=[pl.BlockSpec((nq, zs.shape[1]), lambda i, pt: (i, 0)),
                      pl.BlockSpec(memory_space=pl.ANY), pl.BlockSpec(memory_space=pl.ANY),
                      pl.BlockSpec(memory_space=pl.ANY),
                      pl.BlockSpec((B_HEADS,) + bs.shape[1:], lambda i, pt: (1, 0, 0))],
            out_specs=pl.BlockSpec((nq, 1024), lambda i, pt: (i, 0)),
            scratch_shapes=[pltpu.VMEM((past, 256), F32), pltpu.VMEM((past, 256), F32),
                            pltpu.VMEM((past, IDX_DIM), F32),
                            pltpu.VMEM((IDX_HEADS * nq, lp), F32), pltpu.VMEM((B_GROUP * nq, lp), F32),
                            pltpu.VMEM((nq, lp), I32), pltpu.VMEM((nq, lp), F32),
                            pltpu.SemaphoreType.DMA((3,))]),
        out_shape=jax.ShapeDtypeStruct((bd * nq, 1024), F32),
        compiler_params=_cparams("arbitrary"),
        name="dsa_sample",
    )(page_table, zs, pool_k, pool_v, pool_i, bs)


def _diff_sample_kernel(pt_ref, q_ref, kn_ref, vn_ref, pk_ref, pv_ref, bs_ref, lam_ref, hn_ref, o_ref,
                        kbuf, vbuf, s0_ref, s1_ref, sem, *, n_pages, ch):
    b = pl.program_id(0)
    g = pl.program_id(1)
    past = n_pages * PAGE
    nq = q_ref.shape[0]
    gcol = pl.multiple_of(g * 256, 256)
    ksrc = lambda pg: pk_ref.at[pg, :, pl.ds(gcol, 256)]
    vsrc = lambda pg: pv_ref.at[pg, :, pl.ds(gcol, 256)]
    _gather_start(ksrc, pt_ref, b, n_pages, kbuf, PAGE, sem.at[0])
    _gather_start(vsrc, pt_ref, b, n_pages, vbuf, PAGE, sem.at[1])
    new_mask = _new_key_mask(nq, C_GROUP)
    _gather_wait(ksrc, n_pages, kbuf, PAGE, sem.at[0])
    for m, s_ref in ((0, s0_ref), (1, s1_ref)):
        q = jnp.concatenate([q_ref[:, (r * 2 + m) * HEAD_DIM:(r * 2 + m + 1) * HEAD_DIM]
                             for r in range(C_GROUP)], axis=0).astype(MXU_DT)

        def bias(c0, w, m=m):
            return jnp.concatenate([bs_ref[m * C_HEADS + g * C_GROUP + r, :, pl.ds(c0, w)]
                                    for r in range(C_GROUP)], axis=0)

        def mask(c0, w, is_new):
            return new_mask if is_new else jnp.full((C_GROUP * nq, w), True)

        _sample_scores(q, kbuf, m * HEAD_DIM, kn_ref[:, m * HEAD_DIM:(m + 1) * HEAD_DIM], bias, mask,
                       s_ref, past, ch)
    s0_ref[...] = _sample_softmax(s0_ref) - _diff_lambda(lam_ref) * _sample_softmax(s1_ref)
    _gather_wait(vsrc, n_pages, vbuf, PAGE, sem.at[1])
    o = _diff_finish(_sample_pv(s0_ref, vbuf, 0, C_VDIM, vn_ref[...], past, ch), hn_ref)
    for r in range(C_GROUP):
        o_ref[:, r * C_VDIM:(r + 1) * C_VDIM] = o[r * nq:(r + 1) * nq]


def diff_sample(z1s, pool_k, pool_v, bs, lam_vecs, head_norm, page_table, ch=1024):
    bd, n_pages = page_table.shape
    nq = z1s.shape[0] // bd
    past = n_pages * PAGE
    lp = past + 128
    return pl.pallas_call(
        functools.partial(_diff_sample_kernel, n_pages=n_pages, ch=ch),
        grid_spec=pltpu.PrefetchScalarGridSpec(
            num_scalar_prefetch=1, grid=(bd, C_KV),
            in_specs=[pl.BlockSpec((nq, 512), lambda i, g, pt: (i, g)),
                      pl.BlockSpec((nq, 256), lambda i, g, pt: (i, 2048 // 256 + g)),
                      pl.BlockSpec((nq, 256), lambda i, g, pt: (i, 3072 // 256 + g)),
                      pl.BlockSpec(memory_space=pl.ANY), pl.BlockSpec(memory_space=pl.ANY),
                      pl.BlockSpec(bs.shape, lambda i, g, pt: (0, 0, 0)),
                      pl.BlockSpec((4, HEAD_DIM), lambda i, g, pt: (0, 0)),
                      pl.BlockSpec((1, C_VDIM), lambda i, g, pt: (0, 0))],
            out_specs=pl.BlockSpec((nq, 512), lambda i, g, pt: (i, g)),
            scratch_shapes=[pltpu.VMEM((past, 256), F32), pltpu.VMEM((past, 256), F32),
                            pltpu.VMEM((C_GROUP * nq, lp), F32), pltpu.VMEM((C_GROUP * nq, lp), F32),
                            pltpu.SemaphoreType.DMA((2,))]),
        out_shape=jax.ShapeDtypeStruct((bd * nq, C_HEADS * C_VDIM), F32),
        compiler_params=_cparams("arbitrary", "arbitrary"),
        name="diff_sample",
    )(page_table, z1s, z1s, z1s, pool_k, pool_v, bs, lam_vecs, head_norm.reshape(1, C_VDIM))


def _row_tile(rows, cap=1024):
    tm = min(rows, cap)
    assert rows % tm == 0
    return tm


def _reorder_l0_weight(w):
    sizes = (A_HEADS * HEAD_DIM,) + (A_KV * HEAD_DIM,) * 6 + (
        N_GATES * A_HEADS, B_HEADS * HEAD_DIM, B_KV * HEAD_DIM, B_KV * HEAD_DIM,
        IDX_HEADS * IDX_DIM, IDX_DIM, IDX_HEADS)
    offs = [0]
    for s in sizes:
        offs.append(offs[-1] + s)
    piece = lambda i, j=None: w[:, offs[i]:offs[(i if j is None else j) + 1]]
    qa, six, ga, qb, kvb, qi, ki, wi = piece(0), piece(1, 6), piece(7), piece(8), piece(9, 10), piece(11), \
        piece(12), piece(13)
    pad = jnp.zeros((w.shape[0], L0_COLS - offs[-1]), w.dtype)
    return jnp.concatenate([qa, qb, six, kvb, qi, ki, ga, wi, pad], axis=1).astype(MXU_DT)


def _compress_weights(pe, w1, w2):
    half = CMP_STRIDE * HEAD_DIM
    w1 = w1.reshape(2, half, CMP_HIDDEN).astype(MXU_DT)
    pe_rows = jnp.zeros((16, half), F32).at[0:2].set(pe.reshape(2, half))
    return w1[0], w1[1], w2.astype(MXU_DT), pe_rows


def kernel(x_prompt, x_sample, cache_l0_nsa_cmp_k, cache_l0_nsa_cmp_v, cache_l0_nsa_slc_k, cache_l0_nsa_slc_v, state_l0_nsa_win_k, state_l0_nsa_win_v, cache_l0_dsa_k, cache_l0_dsa_v, cache_l0_dsa_idx_k, cache_l1_diff_k, cache_l1_diff_v, page_table, rel_bias, attn_norm, mlp_norm, mlp_w1, mlp_w2, l0_w_in, l0_w_out, l0_cmp_pe_k, l0_cmp_w1_k, l0_cmp_w2_k, l0_cmp_pe_v, l0_cmp_w1_v, l0_cmp_w2_v, l1_w_in, l1_w_out, l1_lambda_q1, l1_lambda_k1, l1_lambda_q2, l1_lambda_k2, l1_head_norm, final_norm):
    n, t_len, d = x_prompt.shape
    bd, nq, _ = x_sample.shape
    n_pool = cache_l0_nsa_cmp_k.shape[0]
    n_pages = page_table.shape[1]
    past = n_pages * PAGE
    lp = past + 128
    kv_w = A_KV * HEAD_DIM
    assert t_len % QB == 0 and nq <= 8 and state_l0_nsa_win_k.shape[1] == min(WINDOW, past)

    xp = x_prompt.reshape(n * t_len, d)
    xs = x_sample.reshape(bd * nq, d)
    tmp, tms = _row_tile(xp.shape[0]), _row_tile(xs.shape[0])
    w0 = _reorder_l0_weight(l0_w_in)
    cw = (_compress_weights(l0_cmp_pe_k, l0_cmp_w1_k, l0_cmp_w2_k)
          + _compress_weights(l0_cmp_pe_v, l0_cmp_w1_v, l0_cmp_w2_v))
    lam_vecs = jnp.stack([l1_lambda_q1, l1_lambda_k1, l1_lambda_q2, l1_lambda_k2])
    bf = lambda a: a.astype(MXU_DT)

    tp, bs = bias_tiles(rel_bias, past, nq, lp)
    bc_p, bc_s = bias_cmp(rel_bias, t_len, t_len // CMP_STRIDE, past, nq, past // CMP_STRIDE)

    zp = norm_proj(xp, attn_norm[0], w0, tmp, 768)
    zs = norm_proj(xs, attn_norm[0], w0, tms, 768)
    cut = lambda z, c, w: z[:, c:c + w]
    p_rows = {name: cut(zp, c, kv_w) for name, c in
              (("kc", C_KC), ("vc", C_VC), ("ks", C_KS), ("vs", C_VS), ("kw", C_KW), ("vw", C_VW),
               ("kb", C_KB), ("vb", C_VB))}
    s_rows = {name: cut(zs, c, kv_w) for name, c in
              (("kc", C_KC), ("vc", C_VC), ("ks", C_KS), ("vs", C_VS), ("kw", C_KW), ("vw", C_VW),
               ("kb", C_KB), ("vb", C_VB))}
    chunk_w = CMP_STRIDE * kv_w
    kc_p, vc_p = compress_prompt(p_rows["kc"].reshape(n, t_len // CMP_STRIDE, chunk_w),
                                 p_rows["vc"].reshape(n, t_len // CMP_STRIDE, chunk_w), cw)
    kc_s, vc_s = compress_sample(cache_l0_nsa_cmp_k.reshape(n_pool, PAGE // CMP_STRIDE, chunk_w),
                                 cache_l0_nsa_cmp_v.reshape(n_pool, PAGE // CMP_STRIDE, chunk_w),
                                 page_table, cw)
    oa_p = nsa_prompt(zp, kc_p, vc_p, tp, bc_p, rel_bias, n, t_len)
    ob_p = dsa_prompt(zp, tp, rel_bias, n, t_len)
    wb = state_l0_nsa_win_k.shape[1]
    oa_s = nsa_sample(zs, kc_s, vc_s,
                      cache_l0_nsa_slc_k.reshape(n_pool, PAGE, kv_w), cache_l0_nsa_slc_v.reshape(n_pool, PAGE, kv_w),
                      state_l0_nsa_win_k.reshape(bd, wb, kv_w), state_l0_nsa_win_v.reshape(bd, wb, kv_w),
                      bs, bc_s, page_table)
    ob_s = dsa_sample(zs, cache_l0_dsa_k.reshape(n_pool, PAGE, kv_w), cache_l0_dsa_v.reshape(n_pool, PAGE, kv_w),
                      cache_l0_dsa_idx_k, bs, page_table)
    w_out0 = bf(l0_w_out)
    w1_0, w2_0 = bf(mlp_w1[0]), bf(mlp_w2[0])
    xp = out_proj(xp, [oa_p, ob_p], w_out0, tmp, 1024)
    xs = out_proj(xs, [oa_s, ob_s], w_out0, tms, 1024)
    xp = mlp(xp, mlp_norm[0], w1_0, w2_0, final_norm, tmp, 512, False)
    xs = mlp(xs, mlp_norm[0], w1_0, w2_0, final_norm, tms, 512, False)

    w_in1 = bf(l1_w_in)
    z1p = norm_proj(xp, attn_norm[1], w_in1, tmp, 1024)
    z1s = norm_proj(xs, attn_norm[1], w_in1, tms, 1024)
    o1_p = diff_prompt(z1p, tp, rel_bias, lam_vecs, l1_head_norm, n, t_len)
    o1_s = diff_sample(z1s, cache_l1_diff_k.reshape(n_pool, PAGE, -1), cache_l1_diff_v.reshape(n_pool, PAGE, -1),
                       bs, lam_vecs, l1_head_norm, page_table)
    w_out1 = bf(l1_w_out)
    w1_1, w2_1 = bf(mlp_w1[1]), bf(mlp_w2[1])
    xp = out_proj(xp, [o1_p], w_out1, tmp, 1024)
    xs = out_proj(xs, [o1_s], w_out1, tms, 1024)
    y_prompt = mlp(xp, mlp_norm[1], w1_1, w2_1, final_norm, tmp, 512, True).reshape(n, t_len, d)
    y_sample = mlp(xs, mlp_norm[1], w1_1, w2_1, final_norm, tms, 512, True).reshape(bd, nq, d)

    row4 = lambda a, b: a.reshape(b, -1, A_KV, HEAD_DIM)
    win = min(WINDOW, t_len)
    outs = [y_prompt, y_sample]
    for name in ("kc", "vc", "ks", "vs"):
        outs += [row4(p_rows[name], n), row4(s_rows[name], bd)]
    for name, state in (("kw", state_l0_nsa_win_k), ("vw", state_l0_nsa_win_v)):
        outs += [row4(p_rows[name], n)[:, t_len - win:],
                 jnp.concatenate([state, row4(s_rows[name], bd)], axis=1)[:, -wb:]]
    for name in ("kb", "vb"):
        outs += [row4(p_rows[name], n), row4(s_rows[name], bd)]
    outs += [cut(zp, C_TAIL + T_KI, IDX_DIM).reshape(n, t_len, IDX_DIM),
             cut(zs, C_TAIL + T_KI, IDX_DIM).reshape(bd, nq, IDX_DIM)]
    k_cols, v_cols = C_KV * 2 * HEAD_DIM, C_KV * C_VDIM
    q_cols = C_HEADS * 2 * HEAD_DIM
    outs += [cut(z1p, q_cols, k_cols).reshape(n, t_len, C_KV, 2, HEAD_DIM),
             cut(z1s, q_cols, k_cols).reshape(bd, nq, C_KV, 2, HEAD_DIM),
             cut(z1p, q_cols + k_cols, v_cols).reshape(n, t_len, C_KV, C_VDIM),
             cut(z1s, q_cols + k_cols, v_cols).reshape(bd, nq, C_KV, C_VDIM)]
    return tuple(outs)
```

```python
import functools
import math

import jax
import jax.numpy as jnp
from jax import lax
from jax.experimental import pallas as pl
from jax.experimental.pallas import tpu as pltpu

F32 = jnp.float32
I32 = jnp.int32
MXU_DT = jnp.bfloat16

HEAD_DIM = 128
A_HEADS, A_KV, A_GROUP = 8, 2, 4
B_HEADS, B_KV, B_GROUP = 8, 2, 4
C_HEADS, C_KV, C_GROUP, C_VDIM = 8, 4, 2, 256
CMP_STRIDE, CMP_BLOCK, CMP_HIDDEN = 16, 32, 256
SEL_BLOCK, N_SEL_BLOCKS, WINDOW, N_GATES = 64, 16, 512, 3
IDX_HEADS, IDX_DIM, DSA_TOPK = 4, 64, 256
NUM_BUCKETS, MAX_DISTANCE = 32, 128
LAMBDA_INIT = 0.8 - 0.6 * math.exp(-0.3 * 1)
RMS_EPS = 1e-6
NEG = -1e30
SCALE = HEAD_DIM ** -0.5
QB = 128
PAGE = 128
assert QB >= MAX_DISTANCE and WINDOW % QB == 0 and WINDOW >= 2 * QB

C_QA, C_QB, C_KC, C_VC, C_KS, C_VS, C_KW, C_VW, C_KB, C_VB, C_QI, C_TAIL = (
    0, 1024, 2048, 2304, 2560, 2816, 3072, 3328, 3584, 3840, 4096, 4352)
T_KI, T_GA, T_WI = 0, 64, 88
L0_COLS = 4608
VMEM_LIMIT = 56 * 1024 * 1024


def _cparams(*sem):
    return pltpu.CompilerParams(dimension_semantics=sem, vmem_limit_bytes=VMEM_LIMIT)


def _nt(a, b):
    return lax.dot_general(a, b, (((1,), (1,)), ((), ())), preferred_element_type=F32)


def _mm(a, b):
    return jnp.dot(a, b, preferred_element_type=F32)


def _rms(x, g):
    return x * lax.rsqrt(jnp.mean(x * x, axis=-1, keepdims=True) + RMS_EPS) * g


def _norm_proj_kernel(x_ref, g_ref, w_ref, o_ref, xn_ref):
    @pl.when(pl.program_id(1) == 0)
    def _():
        xn_ref[...] = _rms(x_ref[...], g_ref[...]).astype(xn_ref.dtype)

    o_ref[...] = _mm(xn_ref[...], w_ref[...])


def norm_proj(x, gain, w, tm, tn):
    rows, d = x.shape
    n = w.shape[1]
    return pl.pallas_call(
        _norm_proj_kernel,
        grid=(rows // tm, n // tn),
        in_specs=[pl.BlockSpec((tm, d), lambda i, j: (i, 0)),
                  pl.BlockSpec((1, d), lambda i, j: (0, 0)),
                  pl.BlockSpec((d, tn), lambda i, j: (0, j))],
        out_specs=pl.BlockSpec((tm, tn), lambda i, j: (i, j)),
        out_shape=jax.ShapeDtypeStruct((rows, n), F32),
        scratch_shapes=[pltpu.VMEM((tm, d), MXU_DT)],
        compiler_params=_cparams("parallel", "arbitrary"),
        name="norm_proj",
    )(x, gain.reshape(1, d), w)


def _out_proj_kernel(*refs, n_in):
    x_ref, o_refs, w_refs, y_ref = refs[0], refs[1:1 + n_in], refs[1 + n_in:1 + 2 * n_in], refs[-1]
    acc = x_ref[...]
    for o_ref, w_ref in zip(o_refs, w_refs):
        acc = acc + _mm(o_ref[...].astype(MXU_DT), w_ref[...])
    y_ref[...] = acc


def out_proj(x, outs, w, tm, tn):
    rows, d = x.shape
    o_specs, w_specs, row0 = [], [], 0
    for o in outs:
        k = o.shape[1]
        o_specs.append(pl.BlockSpec((tm, k), lambda i, j: (i, 0)))
        w_specs.append(pl.BlockSpec((k, tn), lambda i, j, rb=row0 // k: (rb, j)))
        row0 += k
    return pl.pallas_call(
        functools.partial(_out_proj_kernel, n_in=len(outs)),
        grid=(rows // tm, d // tn),
        in_specs=[pl.BlockSpec((tm, tn), lambda i, j: (i, j))] + o_specs + w_specs,
        out_specs=pl.BlockSpec((tm, tn), lambda i, j: (i, j)),
        out_shape=jax.ShapeDtypeStruct((rows, d), F32),
        compiler_params=_cparams("parallel", "arbitrary"),
        name="out_proj",
    )(x, *outs, *([w] * len(outs)))


def _mlp_kernel(x_ref, g_ref, w1_ref, w2_ref, gf_ref, y_ref, xn_ref, *, final_norm):
    j = pl.program_id(1)

    @pl.when(j == 0)
    def _():
        x = x_ref[...]
        xn_ref[...] = _rms(x, g_ref[...]).astype(xn_ref.dtype)
        y_ref[...] = x

    h = jnp.square(jnp.maximum(_mm(xn_ref[...], w1_ref[...]), 0.0))
    y_ref[...] += _mm(h.astype(w2_ref.dtype), w2_ref[...])

    if final_norm:
        @pl.when(j == pl.num_programs(1) - 1)
        def _():
            y_ref[...] = _rms(y_ref[...], gf_ref[...])


def mlp(x, gain, w1, w2, final_gain, tm, tf, final_norm):
    rows, d = x.shape
    ff = w1.shape[1]
    return pl.pallas_call(
        functools.partial(_mlp_kernel, final_norm=final_norm),
        grid=(rows // tm, ff // tf),
        in_specs=[pl.BlockSpec((tm, d), lambda i, j: (i, 0)),
                  pl.BlockSpec((1, d), lambda i, j: (0, 0)),
                  pl.BlockSpec((d, tf), lambda i, j: (0, j)),
                  pl.BlockSpec((tf, d), lambda i, j: (j, 0)),
                  pl.BlockSpec((1, d), lambda i, j: (0, 0))],
        out_specs=pl.BlockSpec((tm, d), lambda i, j: (i, 0)),
        out_shape=jax.ShapeDtypeStruct((rows, d), F32),
        scratch_shapes=[pltpu.VMEM((tm, d), MXU_DT)],
        compiler_params=_cparams("parallel", "arbitrary"),
        name="mlp",
    )(x, gain.reshape(1, d), w1, w2, final_gain.reshape(1, d))


def _bucket(dist):
    n = jnp.maximum(dist, 0)
    max_exact = NUM_BUCKETS // 2
    nf = jnp.maximum(n, 1).astype(F32)
    large = max_exact + (jnp.log(nf / max_exact) / math.log(MAX_DISTANCE / max_exact)
                         * (NUM_BUCKETS - max_exact)).astype(I32)
    large = jnp.minimum(large, NUM_BUCKETS - 1)
    return jnp.where(n < max_exact, n, large)


def _lookup(tbl_ref, col, buckets):
    def body(b, accs):
        v = tbl_ref[b, col]
        return tuple(jnp.where(bk == b, v, acc) for bk, acc in zip(buckets, accs))
    return lax.fori_loop(0, NUM_BUCKETS, body, tuple(jnp.zeros(bk.shape, F32) for bk in buckets))


def _bias_tiles_kernel(tbl_ref, tp_ref, bs_ref, *, q0):
    h = pl.program_id(0)
    t = lax.broadcasted_iota(I32, (QB, QB), 0)
    k = lax.broadcasted_iota(I32, (QB, QB), 1)
    ts = lax.broadcasted_iota(I32, bs_ref.shape[1:], 0)
    ks = lax.broadcasted_iota(I32, bs_ref.shape[1:], 1)
    d0, d1, ds = _lookup(tbl_ref, h, (_bucket(t - k), _bucket(QB + t - k), _bucket(q0 + ts - ks)))
    tp_ref[0, 0] = d0
    tp_ref[0, 1] = d1
    bs_ref[0] = ds


def bias_tiles(rel_bias, q0, n_q, lp):
    nh = rel_bias.shape[1]
    return pl.pallas_call(
        functools.partial(_bias_tiles_kernel, q0=q0),
        grid=(nh,),
        in_specs=[pl.BlockSpec(memory_space=pltpu.SMEM)],
        out_specs=[pl.BlockSpec((1, 2, QB, QB), lambda h: (h, 0, 0, 0)),
                   pl.BlockSpec((1, n_q, lp), lambda h: (h, 0, 0))],
        out_shape=[jax.ShapeDtypeStruct((nh, 2, QB, QB), F32),
                   jax.ShapeDtypeStruct((nh, n_q, lp), F32)],
        compiler_params=_cparams("arbitrary"),
        name="bias_tiles",
    )(rel_bias)


def _bias_cmp_kernel(tbl_ref, bp_ref, bs_ref, *, q0):
    h = pl.program_id(0)
    tp = lax.broadcasted_iota(I32, bp_ref.shape[1:], 0)
    cp = lax.broadcasted_iota(I32, bp_ref.shape[1:], 1)
    ts = lax.broadcasted_iota(I32, bs_ref.shape[1:], 0)
    cs = lax.broadcasted_iota(I32, bs_ref.shape[1:], 1)
    end = CMP_BLOCK - 1
    bp, bs = _lookup(tbl_ref, h, (_bucket(tp - (cp * CMP_STRIDE + end)),
                                  _bucket(q0 + ts - (cs * CMP_STRIDE + end))))
    bp_ref[0] = bp
    bs_ref[0] = bs


def bias_cmp(rel_bias, t_len, mc_p, q0, n_q, mc_s):
    return pl.pallas_call(
        functools.partial(_bias_cmp_kernel, q0=q0),
        grid=(A_HEADS,),
        in_specs=[pl.BlockSpec(memory_space=pltpu.SMEM)],
        out_specs=[pl.BlockSpec((1, t_len, mc_p), lambda h: (h, 0, 0)),
                   pl.BlockSpec((1, n_q, mc_s), lambda h: (h, 0, 0))],
        out_shape=[jax.ShapeDtypeStruct((A_HEADS, t_len, mc_p), F32),
                   jax.ShapeDtypeStruct((A_HEADS, n_q, mc_s), F32)],
        compiler_params=_cparams("arbitrary"),
        name="bias_cmp",
    )(rel_bias)


def _softmax_rows(z, mask):
    z = jnp.where(mask, z, NEG)
    m = jnp.max(z, axis=-1, keepdims=True)
    e = jnp.where(mask, jnp.exp(z - m), 0.0)
    l = jnp.sum(e, axis=-1, keepdims=True)
    return e / jnp.where(l > 0.0, l, 1.0)


def _gelu_tanh(x):
    return 0.5 * x * (1.0 + jnp.tanh(math.sqrt(2.0 / math.pi) * (x + 0.044715 * (x * x * x))))


def _compress(x_ref, w1a_ref, w1b_ref, w2_ref, pe_ref, o_ref):
    m = x_ref.shape[0]
    pe = pe_ref[...].astype(MXU_DT)
    pos = _mm(pe, w1a_ref[...])[0:1] + _mm(pe, w1b_ref[...])[1:2]
    last = lax.broadcasted_iota(I32, (m, 1), 0) == m - 1
    for g in range(A_KV):
        xg = jnp.concatenate(
            [x_ref[:, (2 * j + g) * HEAD_DIM:(2 * j + g + 1) * HEAD_DIM].astype(MXU_DT)
             for j in range(CMP_STRIDE)], axis=1)
        first = _mm(xg, w1a_ref[...])
        second = pltpu.roll(_mm(xg, w1b_ref[...]), m - 1, 0)
        hid = _gelu_tanh(first + second + pos)
        out = _mm(hid.astype(MXU_DT), w2_ref[...])
        o_ref[:, g * HEAD_DIM:(g + 1) * HEAD_DIM] = jnp.where(last, 0.0, out)


def _overlap(mc, jn, n_cmp, n_slc):
    c = lax.broadcasted_iota(I32, (mc, jn), 0)
    j = lax.broadcasted_iota(I32, (mc, jn), 1)
    ov = ((c * CMP_STRIDE < j * SEL_BLOCK + SEL_BLOCK) & (c * CMP_STRIDE + CMP_BLOCK > j * SEL_BLOCK)
          & (c < n_cmp) & (j < n_slc))
    return jnp.where(ov, 1.0, 0.0)


def _select_blocks(imp, pos, n_slc):
    jn = imp.shape[1]
    jidx = lax.broadcasted_iota(I32, (1, jn), 1)
    cur = pos // SEL_BLOCK
    forced = (jidx == 0) | (jidx == cur) | (jidx == cur - 1)
    future = jidx * SEL_BLOCK > pos
    score = jnp.where(future, -1.0, jnp.where(forced, 1e3, imp))
    score = jnp.where(jidx < n_slc, score, -2.0)

    def body(i, rank):
        col = jnp.sum(jnp.where(jidx == i, score, 0.0), axis=-1, keepdims=True)
        beats = jnp.where(col > score, 1.0, jnp.where(col == score, jnp.where(i < jidx, 1.0, 0.0), 0.0))
        return rank + beats

    rank = lax.fori_loop(0, n_slc, body, jnp.zeros(score.shape, F32))
    n_sel = min(N_SEL_BLOCKS, n_slc)
    return jnp.where((rank < n_sel) & (jidx < n_slc), 1.0, 0.0)


def _topk_mask(s, key_ref, k, nbits):
    s = jnp.where(s == 0.0, 0.0, s)
    bits = lax.bitcast_convert_type(s, I32)
    key_ref[...] = jnp.where(bits < 0, bits ^ jnp.int32(0x7FFFFFFF), bits)
    kf = jnp.float32(k)

    def count_ge(c):
        return jnp.sum(jnp.where(key_ref[...] >= c, 1.0, 0.0), axis=-1, keepdims=True)

    int_min = jnp.int32(-2 ** 31)
    thr0 = jnp.where(count_ge(jnp.int32(0)) >= kf, jnp.int32(0), int_min)

    def vbody(i, thr):
        cand = thr | lax.shift_left(jnp.int32(1), 30 - i)
        return jnp.where(count_ge(cand) >= kf, cand, thr)

    thr = lax.fori_loop(0, 31, vbody, thr0)
    key = key_ref[...]
    gt = key > thr
    eq = key == thr
    need = kf - jnp.sum(jnp.where(gt, 1.0, 0.0), axis=-1, keepdims=True)
    idx = lax.broadcasted_iota(I32, s.shape, 1)

    def ibody(i, c):
        cand = c | lax.shift_left(jnp.int32(1), nbits - 1 - i)
        cnt = jnp.sum(jnp.where(eq & (idx < cand), 1.0, 0.0), axis=-1, keepdims=True)
        return jnp.where(cnt <= need, cand, c)

    cut = lax.fori_loop(0, nbits, ibody, jnp.zeros((s.shape[0], 1), I32))
    return gt | (eq & (idx < cut))


def _flash_step(carry, q, k, v, bias, mask):
    m, l, acc = carry
    s = _nt(q, k) * SCALE + bias
    if mask is not None:
        s = jnp.where(mask, s, NEG)
    mn = jnp.maximum(m, jnp.max(s, axis=-1, keepdims=True))
    p = jnp.exp(s - mn)
    if mask is not None:
        p = jnp.where(mask, p, 0.0)
    a = jnp.exp(m - mn)
    return mn, a * l + jnp.sum(p, axis=-1, keepdims=True), a * acc + _mm(p.astype(MXU_DT), v)


def _flash(q, kv_fn, dv, lo, qb, far_fn, near_fn, per):
    m_rows = q.shape[0]
    n_far = jnp.maximum(qb - 1 - lo, 0)
    n_big = n_far // per

    def far(k0, w, carry):
        k0 = pl.multiple_of(k0, QB)
        return _flash_step(carry, q, *kv_fn(k0, w), *far_fn(k0, w))

    def near(k0, diag, carry):
        k0 = pl.multiple_of(k0, QB)
        return _flash_step(carry, q, *kv_fn(k0, QB), *near_fn(k0, diag))

    carry = (jnp.full((m_rows, 1), NEG, F32), jnp.zeros((m_rows, 1), F32), jnp.zeros((m_rows, dv), F32))
    carry = lax.fori_loop(0, n_big, lambda i, c: far((lo + i * per) * QB, per * QB, c), carry)
    carry = lax.fori_loop(0, n_far - n_big * per, lambda i, c: far((lo + n_big * per + i) * QB, QB, c), carry)
    carry = lax.fori_loop(0, jnp.where(qb - 1 >= lo, 1, 0), lambda i, c: near((qb - 1) * QB, False, c), carry)
    _, l, acc = near(qb * QB, True, carry)
    return acc / jnp.where(l > 0.0, l, 1.0)


def _far_bias(tbl_ref, cols):
    return jnp.concatenate([jnp.full((QB, 1), tbl_ref[NUM_BUCKETS - 1, c], F32) for c in cols], axis=0)


def _near_bias(tp_ref, cols, diag):
    return jnp.concatenate([tp_ref[c, 0 if diag else 1] for c in cols], axis=0)


def _stack_heads(ref, col0, n):
    return jnp.concatenate([ref[:, col0 + r * HEAD_DIM:col0 + (r + 1) * HEAD_DIM] for r in range(n)], axis=0)


def _compress_prompt_kernel(xk_ref, xv_ref, w1ak, w1bk, w2k, pek, w1av, w1bv, w2v, pev, ok_ref, ov_ref):
    _compress(xk_ref.at[0], w1ak, w1bk, w2k, pek, ok_ref.at[0])
    _compress(xv_ref.at[0], w1av, w1bv, w2v, pev, ov_ref.at[0])


def _cmp_weight_specs():
    full = lambda shape: pl.BlockSpec(shape, lambda *_: (0,) * len(shape))
    half = CMP_STRIDE * HEAD_DIM
    one = [full((half, CMP_HIDDEN)), full((half, CMP_HIDDEN)), full((CMP_HIDDEN, HEAD_DIM)), full((16, half))]
    return one + one


def compress_prompt(xk, xv, cw):
    n, m, w = xk.shape
    spec = pl.BlockSpec((1, m, w), lambda i: (i, 0, 0))
    ospec = pl.BlockSpec((1, m, A_KV * HEAD_DIM), lambda i: (i, 0, 0))
    osh = jax.ShapeDtypeStruct((n, m, A_KV * HEAD_DIM), F32)
    return pl.pallas_call(
        _compress_prompt_kernel,
        grid=(n,),
        in_specs=[spec, spec] + _cmp_weight_specs(),
        out_specs=[ospec, ospec],
        out_shape=[osh, osh],
        compiler_params=_cparams("parallel"),
        name="compress_prompt",
    )(xk, xv, *cw)


def _nsa_prompt_kernel(tbl_ref, q_ref, tail_ref, kc_ref, vc_ref, ks_ref, vs_ref, kw_ref, vw_ref,
                       tp_ref, bc_ref, o_ref, *, t_len):
    qb = pl.program_id(1)
    mc = kc_ref.shape[1]
    n_cmp = (t_len - CMP_BLOCK) // CMP_STRIDE + 1
    n_slc = -(-t_len // SEL_BLOCK)
    pos = qb * QB + lax.broadcasted_iota(I32, (QB, 1), 0)
    pos4 = jnp.concatenate([pos] * A_GROUP, axis=0)
    klane = lax.broadcasted_iota(I32, (1, QB), 1)
    jrow4 = jnp.concatenate([lax.broadcasted_iota(I32, (QB, 1), 0)] * A_GROUP, axis=0)
    cidx = lax.broadcasted_iota(I32, (1, mc), 1)
    gates = jax.nn.sigmoid(tail_ref[...])
    overlap = _overlap(mc, QB, n_cmp, n_slc)

    for g in range(A_KV):
        cols = [g * A_GROUP + r for r in range(A_GROUP)]
        q = _stack_heads(q_ref, g * A_GROUP * HEAD_DIM, A_GROUP).astype(MXU_DT)
        kc = kc_ref[0, :, g * HEAD_DIM:(g + 1) * HEAD_DIM].astype(MXU_DT)
        vc = vc_ref[0, :, g * HEAD_DIM:(g + 1) * HEAD_DIM].astype(MXU_DT)
        lc = _nt(q, kc) * SCALE + jnp.concatenate([bc_ref[c] for c in cols], axis=0)
        p_cmp = _softmax_rows(lc, (pos4 >= cidx * CMP_STRIDE + (CMP_BLOCK - 1)) & (cidx < n_cmp))
        o_cmp = _mm(p_cmp.astype(MXU_DT), vc)
        p_sum = sum(p_cmp[r * QB:(r + 1) * QB] for r in range(A_GROUP))
        imp = jnp.dot(p_sum, overlap, preferred_element_type=F32, precision=lax.Precision.HIGHEST)
        sel = _select_blocks(imp, pos, n_slc).astype(MXU_DT)

        gl = slice(g * HEAD_DIM, (g + 1) * HEAD_DIM)
        far_bias = _far_bias(tbl_ref, cols)

        def chosen(k0, w, sel=sel):
            blk = (k0 + lax.broadcasted_iota(I32, (QB, w), 1)) // SEL_BLOCK
            onehot = jnp.where(lax.broadcasted_iota(I32, (QB, w), 0) == blk, 1.0, 0.0).astype(MXU_DT)
            return jnp.concatenate([_mm(sel, onehot) > 0.5] * A_GROUP, axis=0)

        def slc_kv(k0, w):
            return ks_ref[pl.ds(k0, w), gl].astype(MXU_DT), vs_ref[pl.ds(k0, w), gl].astype(MXU_DT)

        def slc_near(k0, diag, cols=cols):
            mask = chosen(k0, QB)
            return _near_bias(tp_ref, cols, diag), (mask & (klane <= jrow4) if diag else mask)

        o_slc = _flash(q, slc_kv, HEAD_DIM, 0, qb, lambda k0, w: (far_bias, chosen(k0, w)), slc_near, 2)

        def win_kv(k0, w):
            return kw_ref[pl.ds(k0, w), gl].astype(MXU_DT), vw_ref[pl.ds(k0, w), gl].astype(MXU_DT)

        def win_far(k0, w):
            return far_bias, pos4 - (k0 + lax.broadcasted_iota(I32, (1, w), 1)) < WINDOW

        def win_near(k0, diag, cols=cols):
            return _near_bias(tp_ref, cols, diag), (klane <= jrow4 if diag else None)

        o_win = _flash(q, win_kv, HEAD_DIM, jnp.maximum(qb - WINDOW // QB, 0), qb, win_far, win_near,
                       WINDOW // QB)

        for r in range(A_GROUP):
            h = g * A_GROUP + r
            gl = T_GA + h * N_GATES
            rows = slice(r * QB, (r + 1) * QB)
            o = (gates[:, gl:gl + 1] * o_cmp[rows] + gates[:, gl + 1:gl + 2] * o_slc[rows]
                 + gates[:, gl + 2:gl + 3] * o_win[rows])
            o_ref[:, h * HEAD_DIM:(h + 1) * HEAD_DIM] = o.astype(o_ref.dtype)


def nsa_prompt(z, k_cmp, v_cmp, tp, bc, rel_bias, n, t_len):
    nb = t_len // QB
    mc = k_cmp.shape[1]
    kv = lambda c: pl.BlockSpec((t_len, 256), lambda i, j: (i, c // 256))
    cmp_spec = pl.BlockSpec((1, mc, 256), lambda i, j: (i, 0, 0))
    return pl.pallas_call(
        functools.partial(_nsa_prompt_kernel, t_len=t_len),
        grid=(n, nb),
        in_specs=[pl.BlockSpec(memory_space=pltpu.SMEM),
                  pl.BlockSpec((QB, 1024), lambda i, j: (i * nb + j, C_QA // 1024)),
                  pl.BlockSpec((QB, 128), lambda i, j: (i * nb + j, C_TAIL // 128)),
                  cmp_spec, cmp_spec, kv(C_KS), kv(C_VS), kv(C_KW), kv(C_VW),
                  pl.BlockSpec(tp.shape, lambda i, j: (0, 0, 0, 0)),
                  pl.BlockSpec((A_HEADS, QB, mc), lambda i, j: (0, j, 0))],
        out_specs=pl.BlockSpec((QB, 1024), lambda i, j: (i * nb + j, 0)),
        out_shape=jax.ShapeDtypeStruct((n * t_len, 1024), MXU_DT),
        compiler_params=_cparams("parallel", "arbitrary"),
        name="nsa_prompt",
    )(rel_bias, z, z, k_cmp, v_cmp, z, z, z, z, tp, bc)


def _dsa_prompt_kernel(tbl_ref, q_ref, qi_ref, tailq_ref, tailk_ref, kb_ref, vb_ref, tp_ref, o_ref,
                       key_ref, sel_ref, *, topk, nbits):
    qb = pl.program_id(1)
    t_len = tailk_ref.shape[0]
    pos = qb * QB + lax.broadcasted_iota(I32, (QB, 1), 0)
    kpos = lax.broadcasted_iota(I32, (1, t_len), 1)
    ki = tailk_ref[:, T_KI:T_KI + IDX_DIM].astype(MXU_DT)
    wi = tailq_ref[:, T_WI:T_WI + IDX_HEADS]
    score = jnp.zeros((QB, t_len), F32)
    for h in range(IDX_HEADS):
        qi = qi_ref[:, h * IDX_DIM:(h + 1) * IDX_DIM].astype(MXU_DT)
        score = score + jnp.maximum(_nt(qi, ki), 0.0) * wi[:, h:h + 1]
    score = score * (IDX_DIM ** -0.5 * IDX_HEADS ** -0.5)
    causal = kpos <= pos
    sel = _topk_mask(jnp.where(causal, score, NEG), key_ref, topk, nbits) & causal
    sel_ref[...] = jnp.where(sel, 1.0, 0.0)

    for g in range(B_KV):
        cols = [A_HEADS + g * B_GROUP + r for r in range(B_GROUP)]
        q = _stack_heads(q_ref, g * B_GROUP * HEAD_DIM, B_GROUP).astype(MXU_DT)

        gl = slice(g * HEAD_DIM, (g + 1) * HEAD_DIM)
        far_bias = _far_bias(tbl_ref, cols)

        def chosen(k0, w):
            return jnp.concatenate([sel_ref[:, pl.ds(k0, w)] > 0.5] * B_GROUP, axis=0)

        def kv(k0, w, gl=gl):
            return kb_ref[pl.ds(k0, w), gl].astype(MXU_DT), vb_ref[pl.ds(k0, w), gl].astype(MXU_DT)

        o = _flash(q, kv, HEAD_DIM, 0, qb, lambda k0, w, fb=far_bias: (fb, chosen(k0, w)),
                   lambda k0, diag, cols=cols: (_near_bias(tp_ref, cols, diag), chosen(k0, QB)), 2)
        for r in range(B_GROUP):
            h = g * B_GROUP + r
            o_ref[:, h * HEAD_DIM:(h + 1) * HEAD_DIM] = o[r * QB:(r + 1) * QB].astype(o_ref.dtype)


def dsa_prompt(z, tp, rel_bias, n, t_len):
    nb = t_len // QB
    topk = min(DSA_TOPK, t_len // 4)
    nbits = int(t_len).bit_length()
    return pl.pallas_call(
        functools.partial(_dsa_prompt_kernel, topk=topk, nbits=nbits),
        grid=(n, nb),
        in_specs=[pl.BlockSpec(memory_space=pltpu.SMEM),
                  pl.BlockSpec((QB, 1024), lambda i, j: (i * nb + j, C_QB // 1024)),
                  pl.BlockSpec((QB, 256), lambda i, j: (i * nb + j, C_QI // 256)),
                  pl.BlockSpec((QB, 128), lambda i, j: (i * nb + j, C_TAIL // 128)),
                  pl.BlockSpec((t_len, 128), lambda i, j: (i, C_TAIL // 128)),
                  pl.BlockSpec((t_len, 256), lambda i, j: (i, C_KB // 256)),
                  pl.BlockSpec((t_len, 256), lambda i, j: (i, C_VB // 256)),
                  pl.BlockSpec(tp.shape, lambda i, j: (0, 0, 0, 0))],
        out_specs=pl.BlockSpec((QB, 1024), lambda i, j: (i * nb + j, 0)),
        out_shape=jax.ShapeDtypeStruct((n * t_len, 1024), MXU_DT),
        scratch_shapes=[pltpu.VMEM((QB, t_len), I32), pltpu.VMEM((QB, t_len), F32)],
        compiler_params=_cparams("parallel", "arbitrary"),
        name="dsa_prompt",
    )(rel_bias, z, z, z, z, z, z, tp)


def _diff_lambda(lam_ref):
    v = lam_ref[...]
    e1 = jnp.exp(jnp.sum(v[0:1] * v[1:2], axis=-1, keepdims=True))
    e2 = jnp.exp(jnp.sum(v[2:3] * v[3:4], axis=-1, keepdims=True))
    return e1 - e2 + LAMBDA_INIT


def _diff_finish(o, hn_ref):
    return _rms(o, hn_ref[...]) * (1.0 - LAMBDA_INIT)


def _diff_prompt_kernel(tbl_ref, q_ref, k_ref, v_ref, tp_ref, lam_ref, hn_ref, o_ref):
    g = pl.program_id(1)
    qb = pl.program_id(2)
    causal = (lax.broadcasted_iota(I32, (1, QB), 1)
              <= jnp.concatenate([lax.broadcasted_iota(I32, (QB, 1), 0)] * C_GROUP, axis=0))
    outs = []
    for m in range(2):
        cols = [m * C_HEADS + g * C_GROUP + r for r in range(C_GROUP)]
        q = jnp.concatenate([q_ref[:, (r * 2 + m) * HEAD_DIM:(r * 2 + m + 1) * HEAD_DIM]
                             for r in range(C_GROUP)], axis=0).astype(MXU_DT)
        far_bias = _far_bias(tbl_ref, cols)

        def kv(k0, w, m=m):
            return (k_ref[pl.ds(k0, w), m * HEAD_DIM:(m + 1) * HEAD_DIM].astype(MXU_DT),
                    v_ref[pl.ds(k0, w), :].astype(MXU_DT))

        outs.append(_flash(q, kv, C_VDIM, 0, qb, lambda k0, w, fb=far_bias: (fb, None),
                           lambda k0, diag, cols=cols: (_near_bias(tp_ref, cols, diag), causal if diag else None),
                           4))
    o = _diff_finish(outs[0] - _diff_lambda(lam_ref) * outs[1], hn_ref)
    for r in range(C_GROUP):
        o_ref[:, r * C_VDIM:(r + 1) * C_VDIM] = o[r * QB:(r + 1) * QB].astype(o_ref.dtype)


def diff_prompt(z1, tp, rel_bias, lam_vecs, head_norm, n, t_len):
    nb = t_len // QB
    return pl.pallas_call(
        _diff_prompt_kernel,
        grid=(n, C_KV, nb),
        in_specs=[pl.BlockSpec(memory_space=pltpu.SMEM),
                  pl.BlockSpec((QB, 512), lambda i, g, j: (i * nb + j, g)),
                  pl.BlockSpec((t_len, 256), lambda i, g, j: (i, 2048 // 256 + g)),
                  pl.BlockSpec((t_len, 256), lambda i, g, j: (i, 3072 // 256 + g)),
                  pl.BlockSpec(tp.shape, lambda i, g, j: (0, 0, 0, 0)),
                  pl.BlockSpec((4, HEAD_DIM), lambda i, g, j: (0, 0)),
                  pl.BlockSpec((1, C_VDIM), lambda i, g, j: (0, 0))],
        out_specs=pl.BlockSpec((QB, 512), lambda i, g, j: (i * nb + j, g)),
        out_shape=jax.ShapeDtypeStruct((n * t_len, C_HEADS * C_VDIM), MXU_DT),
        compiler_params=_cparams("parallel", "parallel", "arbitrary"),
        name="diff_prompt",
    )(rel_bias, z1, z1, z1, tp, lam_vecs, head_norm.reshape(1, C_VDIM))


def _gather_start(src_fn, pt_ref, b, n_pages, buf_ref, rows, sem):
    def body(p, _):
        dst = buf_ref.at[pl.ds(pl.multiple_of(p * rows, rows), rows)]
        pltpu.make_async_copy(src_fn(pt_ref[b, p]), dst, sem).start()
        return 0
    lax.fori_loop(0, n_pages, body, 0)


def _gather_wait(src_fn, n_pages, buf_ref, rows, sem):
    def body(p, _):
        pltpu.make_async_copy(src_fn(0), buf_ref.at[pl.ds(0, rows)], sem).wait()
        return 0
    lax.fori_loop(0, n_pages, body, 0)


def _pad_rows(x, rows):
    return jnp.concatenate([x, jnp.zeros((rows - x.shape[0], x.shape[1]), x.dtype)], axis=0)


def _page_rows(pool_ref, rows):
    return lambda pg: pool_ref.at[pl.ds(pl.multiple_of(pg * rows, rows), rows)]


def _interleaved(buf_ref, n, j):
    return lambda c0, ch: buf_ref[pl.ds(c0 * n + j, ch, stride=n), :]


def _sample_scores(q, k_fn, knew, bias_fn, mask_fn, s_ref, past, ch, scale=SCALE):
    def body(c, _):
        c0 = pl.multiple_of(c * ch, ch)
        k = k_fn(c0, ch).astype(MXU_DT)
        s = _nt(q, k) * scale + bias_fn(c0, ch)
        s_ref[:, pl.ds(c0, ch)] = jnp.where(mask_fn(c0, ch, False), s, NEG)
        return 0

    lax.fori_loop(0, past // ch, body, 0)
    s = _nt(q, _pad_rows(knew, 128).astype(MXU_DT)) * scale + bias_fn(past, 128)
    s_ref[:, past:past + 128] = jnp.where(mask_fn(past, 128, True), s, NEG)


def _sample_softmax(s_ref):
    z = s_ref[...]
    m = jnp.max(z, axis=-1, keepdims=True)
    e = jnp.where(z > 0.5 * NEG, jnp.exp(z - m), 0.0)
    l = jnp.sum(e, axis=-1, keepdims=True)
    return e / jnp.where(l > 0.0, l, 1.0)


def _sample_pv(p_ref, v_fn, vnew, past, ch):
    def body(c, acc):
        c0 = pl.multiple_of(c * ch, ch)
        return acc + _mm(p_ref[:, pl.ds(c0, ch)].astype(MXU_DT), v_fn(c0, ch).astype(MXU_DT))

    acc = lax.fori_loop(0, past // ch, body, jnp.zeros((p_ref.shape[0], vnew.shape[1]), F32))
    return acc + _mm(p_ref[:, past:past + 128].astype(MXU_DT), _pad_rows(vnew, 128).astype(MXU_DT))


def _new_key_mask(nq, rep):
    t = lax.broadcasted_iota(I32, (nq, 128), 0)
    j = lax.broadcasted_iota(I32, (nq, 128), 1)
    return jnp.concatenate([(j <= t) & (j < nq)] * rep, axis=0)


def _compress_sample_kernel(pt_ref, pk_ref, pv_ref, w1ak, w1bk, w2k, pek, w1av, w1bv, w2v, pev,
                            ok_ref, ov_ref, bk_ref, bv_ref, sem, *, n_pages):
    b = pl.program_id(0)
    rows = PAGE // CMP_STRIDE
    _gather_start(lambda pg: pk_ref.at[pg], pt_ref, b, n_pages, bk_ref, rows, sem.at[0])
    _gather_start(lambda pg: pv_ref.at[pg], pt_ref, b, n_pages, bv_ref, rows, sem.at[1])
    _gather_wait(lambda pg: pk_ref.at[pg], n_pages, bk_ref, rows, sem.at[0])
    _compress(bk_ref, w1ak, w1bk, w2k, pek, ok_ref.at[0])
    _gather_wait(lambda pg: pv_ref.at[pg], n_pages, bv_ref, rows, sem.at[1])
    _compress(bv_ref, w1av, w1bv, w2v, pev, ov_ref.at[0])


def compress_sample(pool_k, pool_v, page_table, cw):
    bd, n_pages = page_table.shape
    rows = PAGE // CMP_STRIDE
    m = n_pages * rows
    w = pool_k.shape[2]
    ospec = pl.BlockSpec((1, m, A_KV * HEAD_DIM), lambda i, pt: (i, 0, 0))
    osh = jax.ShapeDtypeStruct((bd, m, A_KV * HEAD_DIM), F32)
    return pl.pallas_call(
        functools.partial(_compress_sample_kernel, n_pages=n_pages),
        grid_spec=pltpu.PrefetchScalarGridSpec(
            num_scalar_prefetch=1, grid=(bd,),
            in_specs=[pl.BlockSpec(memory_space=pl.ANY), pl.BlockSpec(memory_space=pl.ANY)] + _cmp_weight_specs(),
            out_specs=[ospec, ospec],
            scratch_shapes=[pltpu.VMEM((m, w), F32), pltpu.VMEM((m, w), F32), pltpu.SemaphoreType.DMA((2,))]),
        out_shape=[osh, osh],
        compiler_params=_cparams("arbitrary"),
        name="compress_sample",
    )(page_table, pool_k, pool_v, *cw)


def _nsa_sample_kernel(pt_ref, z_ref, kc_ref, vc_ref, pks_ref, pvs_ref, wk_ref, wv_ref, bs_ref, bc_ref,
                       o_ref, kbuf, vbuf, s_ref, sw_ref, sem, *, n_pages, ch):
    b = pl.program_id(0)
    past = n_pages * PAGE
    nq = z_ref.shape[0]
    mc = kc_ref.shape[1]
    t_len = past + nq
    n_cmp = (t_len - CMP_BLOCK) // CMP_STRIDE + 1
    n_slc = -(-t_len // SEL_BLOCK)
    jn = 128 * (-(-n_slc // 128))
    wb = wk_ref.shape[0] // A_KV
    page_rows = PAGE * A_KV
    _gather_start(_page_rows(pks_ref, page_rows), pt_ref, b, n_pages, kbuf, page_rows, sem.at[0])
    _gather_start(_page_rows(pvs_ref, page_rows), pt_ref, b, n_pages, vbuf, page_rows, sem.at[1])

    pos = past + lax.broadcasted_iota(I32, (nq, 1), 0)
    pos4 = jnp.concatenate([pos] * A_GROUP, axis=0)
    cidx = lax.broadcasted_iota(I32, (1, mc), 1)
    gates = jax.nn.sigmoid(z_ref[:, C_TAIL:C_TAIL + 128])
    overlap = _overlap(mc, jn, n_cmp, n_slc)
    new_mask = _new_key_mask(nq, A_GROUP)
    waited = False

    for g in range(A_KV):
        cols = [g * A_GROUP + r for r in range(A_GROUP)]
        q = _stack_heads(z_ref, C_QA + g * A_GROUP * HEAD_DIM, A_GROUP).astype(MXU_DT)
        gl = slice(g * HEAD_DIM, (g + 1) * HEAD_DIM)
        lc = (_nt(q, kc_ref[0, :, gl].astype(MXU_DT)) * SCALE
              + jnp.concatenate([bc_ref[c] for c in cols], axis=0))
        p_cmp = _softmax_rows(lc, (pos4 >= cidx * CMP_STRIDE + (CMP_BLOCK - 1)) & (cidx < n_cmp))
        o_cmp = _mm(p_cmp.astype(MXU_DT), vc_ref[0, :, gl].astype(MXU_DT))
        p_sum = sum(p_cmp[r * nq:(r + 1) * nq] for r in range(A_GROUP))
        imp = jnp.dot(p_sum, overlap, preferred_element_type=F32, precision=lax.Precision.HIGHEST)
        sel = _select_blocks(imp, pos, n_slc).astype(MXU_DT)
        def win_bias(c0, w, cols=cols):
            return jnp.concatenate([bs_ref[c, :, pl.ds(past - wb + c0, w)] for c in cols], axis=0)

        def win_mask(c0, w, is_new):
            dist = pos4 - (past - wb + c0 + lax.broadcasted_iota(I32, (1, w), 1))
            valid = (dist >= 0) & (dist < WINDOW)
            return valid & new_mask if is_new else valid

        _sample_scores(q, _interleaved(wk_ref, A_KV, g),
                       z_ref[:, C_KW + g * HEAD_DIM:C_KW + (g + 1) * HEAD_DIM],
                       win_bias, win_mask, sw_ref, wb, wb)
        sw_ref[...] = _sample_softmax(sw_ref)
        o_win = _sample_pv(sw_ref, _interleaved(wv_ref, A_KV, g),
                           z_ref[:, C_VW + g * HEAD_DIM:C_VW + (g + 1) * HEAD_DIM], wb, wb)
        if not waited:
            _gather_wait(_page_rows(pks_ref, page_rows), n_pages, kbuf, page_rows, sem.at[0])
            _gather_wait(_page_rows(pvs_ref, page_rows), n_pages, vbuf, page_rows, sem.at[1])
            waited = True

        def slc_bias(c0, w, cols=cols):
            return jnp.concatenate([bs_ref[c, :, pl.ds(c0, w)] for c in cols], axis=0)

        def slc_mask(c0, w, is_new, sel=sel):
            blk = (c0 + lax.broadcasted_iota(I32, (jn, w), 1)) // SEL_BLOCK
            onehot = jnp.where(lax.broadcasted_iota(I32, (jn, w), 0) == blk, 1.0, 0.0).astype(MXU_DT)
            chosen = jnp.concatenate([_mm(sel, onehot) > 0.5] * A_GROUP, axis=0)
            return chosen & new_mask if is_new else chosen

        _sample_scores(q, _interleaved(kbuf, A_KV, g),
                       z_ref[:, C_KS + g * HEAD_DIM:C_KS + (g + 1) * HEAD_DIM],
                       slc_bias, slc_mask, s_ref, past, ch)
        s_ref[...] = _sample_softmax(s_ref)
        o_slc = _sample_pv(s_ref, _interleaved(vbuf, A_KV, g),
                           z_ref[:, C_VS + g * HEAD_DIM:C_VS + (g + 1) * HEAD_DIM], past, ch)
        for r in range(A_GROUP):
            h = g * A_GROUP + r
            c = T_GA + h * N_GATES
            rows = slice(r * nq, (r + 1) * nq)
            o_ref[:, h * HEAD_DIM:(h + 1) * HEAD_DIM] = (
                gates[:, c:c + 1] * o_cmp[rows] + gates[:, c + 1:c + 2] * o_slc[rows]
                + gates[:, c + 2:c + 3] * o_win[rows])


def nsa_sample(zs, k_cmp, v_cmp, pool_ks, pool_vs, win_k, win_v, bs, bc, page_table, ch=1024):
    bd, n_pages = page_table.shape
    nq = zs.shape[0] // bd
    past = n_pages * PAGE
    mc = k_cmp.shape[1]
    wrows = win_k.shape[0] // bd
    wb = wrows // A_KV
    im3 = lambda i, pt: (i, 0, 0)
    win_spec = pl.BlockSpec((wrows, HEAD_DIM), lambda i, pt: (i, 0))
    return pl.pallas_call(
        functools.partial(_nsa_sample_kernel, n_pages=n_pages, ch=ch),
        grid_spec=pltpu.PrefetchScalarGridSpec(
            num_scalar_prefetch=1, grid=(bd,),
            in_specs=[pl.BlockSpec((nq, zs.shape[1]), lambda i, pt: (i, 0)),
                      pl.BlockSpec((1, mc, 256), im3), pl.BlockSpec((1, mc, 256), im3),
                      pl.BlockSpec(memory_space=pl.ANY), pl.BlockSpec(memory_space=pl.ANY),
                      win_spec, win_spec,
                      pl.BlockSpec((A_HEADS,) + bs.shape[1:], lambda i, pt: (0, 0, 0)),
                      pl.BlockSpec(bc.shape, lambda i, pt: (0, 0, 0))],
            out_specs=pl.BlockSpec((nq, 1024), lambda i, pt: (i, 0)),
            scratch_shapes=[pltpu.VMEM((past * A_KV, HEAD_DIM), F32), pltpu.VMEM((past * A_KV, HEAD_DIM), F32),
                            pltpu.VMEM((A_GROUP * nq, past + 128), F32),
                            pltpu.VMEM((A_GROUP * nq, wb + 128), F32),
                            pltpu.SemaphoreType.DMA((2,))]),
        out_shape=jax.ShapeDtypeStruct((bd * nq, 1024), F32),
        compiler_params=_cparams("arbitrary"),
        name="nsa_sample",
    )(page_table, zs, k_cmp, v_cmp, pool_ks, pool_vs, win_k, win_v, bs, bc)


def _dsa_sample_kernel(pt_ref, z_ref, pk_ref, pv_ref, pi_ref, bs_ref, o_ref,
                       kbuf, vbuf, ibuf, s_ref, sc_ref, key_ref, sel_ref, sem, *, n_pages, ch, topk, nbits):
    b = pl.program_id(0)
    past = n_pages * PAGE
    nq = z_ref.shape[0]
    page_rows = PAGE * B_KV
    _gather_start(lambda pg: pi_ref.at[pg], pt_ref, b, n_pages, ibuf, PAGE, sem.at[2])
    _gather_start(_page_rows(pk_ref, page_rows), pt_ref, b, n_pages, kbuf, page_rows, sem.at[0])
    _gather_start(_page_rows(pv_ref, page_rows), pt_ref, b, n_pages, vbuf, page_rows, sem.at[1])
    qi = jnp.concatenate([z_ref[:, C_QI + h * IDX_DIM:C_QI + (h + 1) * IDX_DIM] for h in range(IDX_HEADS)],
                         axis=0).astype(MXU_DT)
    wi = z_ref[:, C_TAIL + T_WI:C_TAIL + T_WI + IDX_HEADS]
    _gather_wait(lambda pg: pi_ref.at[pg], n_pages, ibuf, PAGE, sem.at[2])
    zero = lambda c0, w: jnp.zeros((IDX_HEADS * nq, w), F32)
    true = lambda c0, w, is_new: jnp.full((IDX_HEADS * nq, w), True)
    _sample_scores(qi, lambda c0, w: ibuf[pl.ds(c0, w), :], z_ref[:, C_TAIL + T_KI:C_TAIL + T_KI + IDX_DIM],
                   zero, true, s_ref, past, ch, scale=1.0)
    rel = jnp.maximum(s_ref[...], 0.0)
    score = sum(rel[h * nq:(h + 1) * nq] * wi[:, h:h + 1] for h in range(IDX_HEADS))
    score = score * (IDX_DIM ** -0.5 * IDX_HEADS ** -0.5)
    new_j = lax.broadcasted_iota(I32, score.shape, 1) - past
    causal = (new_j < 0) | ((new_j <= lax.broadcasted_iota(I32, score.shape, 0)) & (new_j < nq))
    sel = _topk_mask(jnp.where(causal, score, NEG), key_ref, topk, nbits) & causal
    sel_ref[...] = jnp.where(sel, 1.0, 0.0)

    _gather_wait(_page_rows(pk_ref, page_rows), n_pages, kbuf, page_rows, sem.at[0])
    _gather_wait(_page_rows(pv_ref, page_rows), n_pages, vbuf, page_rows, sem.at[1])
    for g in range(B_KV):
        cols = [g * B_GROUP + r for r in range(B_GROUP)]
        q = _stack_heads(z_ref, C_QB + g * B_GROUP * HEAD_DIM, B_GROUP).astype(MXU_DT)

        def bias(c0, w, cols=cols):
            return jnp.concatenate([bs_ref[c, :, pl.ds(c0, w)] for c in cols], axis=0)

        def mask(c0, w, is_new):
            return jnp.concatenate([sel_ref[:, pl.ds(c0, w)] > 0.5] * B_GROUP, axis=0)

        _sample_scores(q, _interleaved(kbuf, B_KV, g), z_ref[:, C_KB + g * HEAD_DIM:C_KB + (g + 1) * HEAD_DIM],
                       bias, mask, sc_ref, past, ch)
        sc_ref[...] = _sample_softmax(sc_ref)
        o = _sample_pv(sc_ref, _interleaved(vbuf, B_KV, g),
                       z_ref[:, C_VB + g * HEAD_DIM:C_VB + (g + 1) * HEAD_DIM], past, ch)
        for r in range(B_GROUP):
            h = g * B_GROUP + r
            o_ref[:, h * HEAD_DIM:(h + 1) * HEAD_DIM] = o[r * nq:(r + 1) * nq]


def dsa_sample(zs, pool_k, pool_v, pool_i, bs, page_table, ch=1024):
    bd, n_pages = page_table.shape
    nq = zs.shape[0] // bd
    past = n_pages * PAGE
    lp = past + 128
    topk = min(DSA_TOPK, (past + nq) // 4)
    return pl.pallas_call(
        functools.partial(_dsa_sample_kernel, n_pages=n_pages, ch=ch, topk=topk, nbits=int(lp).bit_length()),
        grid_spec=pltpu.PrefetchScalarGridSpec(
            num_scalar_prefetch=1, grid=(bd,),
            in_specs=[pl.BlockSpec((nq, zs.shape[1]), lambda i, pt: (i, 0)),
                      pl.BlockSpec(memory_space=pl.ANY), pl.BlockSpec(memory_space=pl.ANY),
                      pl.BlockSpec(memory_space=pl.ANY),
                      pl.BlockSpec((B_HEADS,) + bs.shape[1:], lambda i, pt: (1, 0, 0))],
            out_specs=pl.BlockSpec((nq, 1024), lambda i, pt: (i, 0)),
            scratch_shapes=[pltpu.VMEM((past * B_KV, HEAD_DIM), F32), pltpu.VMEM((past * B_KV, HEAD_DIM), F32),
                            pltpu.VMEM((past, IDX_DIM), F32),
                            pltpu.VMEM((IDX_HEADS * nq, lp), F32), pltpu.VMEM((B_GROUP * nq, lp), F32),
                            pltpu.VMEM((nq, lp), I32), pltpu.VMEM((nq, lp), F32),
                            pltpu.SemaphoreType.DMA((3,))]),
        out_shape=jax.ShapeDtypeStruct((bd * nq, 1024), F32),
        compiler_params=_cparams("arbitrary"),
        name="dsa_sample",
    )(page_table, zs, pool_k, pool_v, pool_i, bs)


def _diff_sample_kernel(pt_ref, q_ref, kn_ref, vn_ref, pk_ref, pv_ref, bs_ref, lam_ref, hn_ref, o_ref,
                        kbuf, vbuf, sem, *, n_pages, cp):
    b = pl.program_id(0)
    nb = pl.num_programs(0)
    nq = q_ref.shape[0]
    pieces = C_KV * 2
    page_rows = PAGE * pieces
    slot_rows = cp * page_rows
    ch = cp * PAGE
    n_ch = n_pages // cp
    past = n_pages * PAGE
    rows = C_GROUP * nq

    def copies(bb, c, slot):
        out = []
        for i in range(cp):
            pg = pt_ref[bb, c * cp + i]
            dst = pl.ds(pl.multiple_of(slot * slot_rows + i * page_rows, page_rows), page_rows)
            out.append(pltpu.make_async_copy(_page_rows(pk_ref, page_rows)(pg), kbuf.at[dst], sem.at[0, slot]))
            out.append(pltpu.make_async_copy(_page_rows(pv_ref, page_rows)(pg), vbuf.at[dst], sem.at[1, slot]))
        return out

    @pl.when(b == 0)
    def _():
        for cpy in copies(0, 0, 0):
            cpy.start()

    qs = [jnp.concatenate([q_ref[:, ((g * C_GROUP + r) * 2 + m) * HEAD_DIM:((g * C_GROUP + r) * 2 + m + 1) * HEAD_DIM]
                           for r in range(C_GROUP)], axis=0).astype(MXU_DT)
          for g in range(C_KV) for m in range(2)]

    def update(carry, k_fn, v_fn, c0, w, mask):
        m_all, l_all, acc_all = carry
        new_m, new_l, new_acc = [], [], []
        for g in range(C_KV):
            ps, alphas = [], []
            for m in range(2):
                gm = g * 2 + m
                rs = slice(gm * rows, (gm + 1) * rows)
                bias = jnp.concatenate([bs_ref[m * C_HEADS + g * C_GROUP + r, :, pl.ds(c0, w)]
                                        for r in range(C_GROUP)], axis=0)
                s = _nt(qs[gm], k_fn(g, m).astype(MXU_DT)) * SCALE + bias
                if mask is not None:
                    s = jnp.where(mask, s, NEG)
                mn = jnp.maximum(m_all[rs], jnp.max(s, axis=-1, keepdims=True))
                p = jnp.exp(s - mn)
                if mask is not None:
                    p = jnp.where(mask, p, 0.0)
                a = jnp.exp(m_all[rs] - mn)
                new_m.append(mn)
                new_l.append(a * l_all[rs] + jnp.sum(p, axis=-1, keepdims=True))
                ps.append(p)
                alphas.append(a)
            pst = jnp.concatenate(ps, axis=0).astype(MXU_DT)
            pv = jnp.concatenate([_mm(pst, v_fn(g, h).astype(MXU_DT)) for h in range(2)], axis=1)
            for m in range(2):
                rs = slice((g * 2 + m) * rows, (g * 2 + m + 1) * rows)
                new_acc.append(alphas[m] * acc_all[rs] + pv[m * rows:(m + 1) * rows])
        return (jnp.concatenate(new_m, axis=0), jnp.concatenate(new_l, axis=0),
                jnp.concatenate(new_acc, axis=0))

    def chunk(c, carry):
        slot = c % 2
        for cpy in copies(b, c, slot):
            cpy.wait()

        @pl.when(c + 1 < n_ch)
        def _():
            for cpy in copies(b, c + 1, 1 - slot):
                cpy.start()

        @pl.when((c + 1 == n_ch) & (b + 1 < nb))
        def _():
            for cpy in copies(b + 1, 0, 1 - slot):
                cpy.start()

        base = slot * slot_rows
        return update(carry,
                      lambda g, m: kbuf[pl.ds(base + g * 2 + m, ch, stride=pieces), :],
                      lambda g, h: vbuf[pl.ds(base + h * C_KV + g, ch, stride=pieces), :],
                      pl.multiple_of(c * ch, ch), ch, None)

    n_rows = pieces * rows
    carry = (jnp.full((n_rows, 1), NEG, F32), jnp.zeros((n_rows, 1), F32), jnp.zeros((n_rows, C_VDIM), F32))
    carry = lax.fori_loop(0, n_ch, chunk, carry)
    _, l_all, acc_all = update(
        carry,
        lambda g, m: _pad_rows(kn_ref[:, (g * 2 + m) * HEAD_DIM:(g * 2 + m + 1) * HEAD_DIM], 128),
        lambda g, h: _pad_rows(vn_ref[:, g * C_VDIM + h * HEAD_DIM:g * C_VDIM + (h + 1) * HEAD_DIM], 128),
        past, 128, _new_key_mask(nq, C_GROUP))
    o_all = acc_all / l_all
    lam = _diff_lambda(lam_ref)
    for g in range(C_KV):
        r0 = g * 2 * rows
        o = _diff_finish(o_all[r0:r0 + rows] - lam * o_all[r0 + rows:r0 + 2 * rows], hn_ref)
        for r in range(C_GROUP):
            col = (g * C_GROUP + r) * C_VDIM
            o_ref[:, col:col + C_VDIM] = o[r * nq:(r + 1) * nq]


def diff_sample(z1s, pool_k, pool_v, bs, lam_vecs, head_norm, page_table, cp=8):
    bd, n_pages = page_table.shape
    nq = z1s.shape[0] // bd
    assert n_pages % (2 * cp) == 0
    slot_rows = cp * PAGE * C_KV * 2
    q_cols = C_HEADS * 2 * HEAD_DIM
    kv_cols = C_KV * C_VDIM
    return pl.pallas_call(
        functools.partial(_diff_sample_kernel, n_pages=n_pages, cp=cp),
        grid_spec=pltpu.PrefetchScalarGridSpec(
            num_scalar_prefetch=1, grid=(bd,),
            in_specs=[pl.BlockSpec((nq, q_cols), lambda i, pt: (i, 0)),
                      pl.BlockSpec((nq, kv_cols), lambda i, pt: (i, q_cols // kv_cols)),
                      pl.BlockSpec((nq, kv_cols), lambda i, pt: (i, q_cols // kv_cols + 1)),
                      pl.BlockSpec(memory_space=pl.ANY), pl.BlockSpec(memory_space=pl.ANY),
                      pl.BlockSpec(bs.shape, lambda i, pt: (0, 0, 0)),
                      pl.BlockSpec((4, HEAD_DIM), lambda i, pt: (0, 0)),
                      pl.BlockSpec((1, C_VDIM), lambda i, pt: (0, 0))],
            out_specs=pl.BlockSpec((nq, C_HEADS * C_VDIM), lambda i, pt: (i, 0)),
            scratch_shapes=[pltpu.VMEM((2 * slot_rows, HEAD_DIM), F32), pltpu.VMEM((2 * slot_rows, HEAD_DIM), F32),
                            pltpu.SemaphoreType.DMA((2, 2))]),
        out_shape=jax.ShapeDtypeStruct((bd * nq, C_HEADS * C_VDIM), F32),
        compiler_params=_cparams("arbitrary"),
        name="diff_sample",
    )(page_table, z1s, z1s, z1s, pool_k, pool_v, bs, lam_vecs, head_norm.reshape(1, C_VDIM))


def _row_tile(rows, cap=1024):
    tm = min(rows, cap)
    assert rows % tm == 0
    return tm


def _reorder_l0_weight(w):
    sizes = (A_HEADS * HEAD_DIM,) + (A_KV * HEAD_DIM,) * 6 + (
        N_GATES * A_HEADS, B_HEADS * HEAD_DIM, B_KV * HEAD_DIM, B_KV * HEAD_DIM,
        IDX_HEADS * IDX_DIM, IDX_DIM, IDX_HEADS)
    offs = [0]
    for s in sizes:
        offs.append(offs[-1] + s)
    piece = lambda i, j=None: w[:, offs[i]:offs[(i if j is None else j) + 1]]
    qa, six, ga, qb, kvb, qi, ki, wi = piece(0), piece(1, 6), piece(7), piece(8), piece(9, 10), piece(11), \
        piece(12), piece(13)
    pad = jnp.zeros((w.shape[0], L0_COLS - offs[-1]), w.dtype)
    return jnp.concatenate([qa, qb, six, kvb, qi, ki, ga, wi, pad], axis=1).astype(MXU_DT)


def _compress_weights(pe, w1, w2):
    half = CMP_STRIDE * HEAD_DIM
    w1 = w1.reshape(2, half, CMP_HIDDEN).astype(MXU_DT)
    pe_rows = jnp.zeros((16, half), F32).at[0:2].set(pe.reshape(2, half))
    return w1[0], w1[1], w2.astype(MXU_DT), pe_rows


def kernel(x_prompt, x_sample, cache_l0_nsa_cmp_k, cache_l0_nsa_cmp_v, cache_l0_nsa_slc_k, cache_l0_nsa_slc_v, state_l0_nsa_win_k, state_l0_nsa_win_v, cache_l0_dsa_k, cache_l0_dsa_v, cache_l0_dsa_idx_k, cache_l1_diff_k, cache_l1_diff_v, page_table, rel_bias, attn_norm, mlp_norm, mlp_w1, mlp_w2, l0_w_in, l0_w_out, l0_cmp_pe_k, l0_cmp_w1_k, l0_cmp_w2_k, l0_cmp_pe_v, l0_cmp_w1_v, l0_cmp_w2_v, l1_w_in, l1_w_out, l1_lambda_q1, l1_lambda_k1, l1_lambda_q2, l1_lambda_k2, l1_head_norm, final_norm):
    n, t_len, d = x_prompt.shape
    bd, nq, _ = x_sample.shape
    n_pool = cache_l0_nsa_cmp_k.shape[0]
    n_pages = page_table.shape[1]
    past = n_pages * PAGE
    lp = past + 128
    kv_w = A_KV * HEAD_DIM
    assert t_len % QB == 0 and nq <= 8 and state_l0_nsa_win_k.shape[1] == min(WINDOW, past)

    xp = x_prompt.reshape(n * t_len, d)
    xs = x_sample.reshape(bd * nq, d)
    tmp, tms = _row_tile(xp.shape[0]), _row_tile(xs.shape[0])
    w0 = _reorder_l0_weight(l0_w_in)
    cw = (_compress_weights(l0_cmp_pe_k, l0_cmp_w1_k, l0_cmp_w2_k)
          + _compress_weights(l0_cmp_pe_v, l0_cmp_w1_v, l0_cmp_w2_v))
    lam_vecs = jnp.stack([l1_lambda_q1, l1_lambda_k1, l1_lambda_q2, l1_lambda_k2])
    bf = lambda a: a.astype(MXU_DT)

    tp, bs = bias_tiles(rel_bias, past, nq, lp)
    bc_p, bc_s = bias_cmp(rel_bias, t_len, t_len // CMP_STRIDE, past, nq, past // CMP_STRIDE)

    zp = norm_proj(xp, attn_norm[0], w0, tmp, 768)
    zs = norm_proj(xs, attn_norm[0], w0, tms, 768)
    cut = lambda z, c, w: z[:, c:c + w]
    p_rows = {name: cut(zp, c, kv_w) for name, c in
              (("kc", C_KC), ("vc", C_VC), ("ks", C_KS), ("vs", C_VS), ("kw", C_KW), ("vw", C_VW),
               ("kb", C_KB), ("vb", C_VB))}
    s_rows = {name: cut(zs, c, kv_w) for name, c in
              (("kc", C_KC), ("vc", C_VC), ("ks", C_KS), ("vs", C_VS), ("kw", C_KW), ("vw", C_VW),
               ("kb", C_KB), ("vb", C_VB))}
    chunk_w = CMP_STRIDE * kv_w
    kc_p, vc_p = compress_prompt(p_rows["kc"].reshape(n, t_len // CMP_STRIDE, chunk_w),
                                 p_rows["vc"].reshape(n, t_len // CMP_STRIDE, chunk_w), cw)
    kc_s, vc_s = compress_sample(cache_l0_nsa_cmp_k.reshape(n_pool, PAGE // CMP_STRIDE, chunk_w),
                                 cache_l0_nsa_cmp_v.reshape(n_pool, PAGE // CMP_STRIDE, chunk_w),
                                 page_table, cw)
    oa_p = nsa_prompt(zp, kc_p, vc_p, tp, bc_p, rel_bias, n, t_len)
    ob_p = dsa_prompt(zp, tp, rel_bias, n, t_len)
    wb = state_l0_nsa_win_k.shape[1]
    lanes = lambda a: a.reshape(-1, HEAD_DIM)
    oa_s = nsa_sample(zs, kc_s, vc_s, lanes(cache_l0_nsa_slc_k), lanes(cache_l0_nsa_slc_v),
                      lanes(state_l0_nsa_win_k), lanes(state_l0_nsa_win_v), bs, bc_s, page_table)
    ob_s = dsa_sample(zs, lanes(cache_l0_dsa_k), lanes(cache_l0_dsa_v), cache_l0_dsa_idx_k, bs, page_table)
    w_out0 = bf(l0_w_out)
    w1_0, w2_0 = bf(mlp_w1[0]), bf(mlp_w2[0])
    xp = out_proj(xp, [oa_p, ob_p], w_out0, tmp, 1024)
    xs = out_proj(xs, [oa_s, ob_s], w_out0, tms, 1024)
    xp = mlp(xp, mlp_norm[0], w1_0, w2_0, final_norm, tmp, 512, False)
    xs = mlp(xs, mlp_norm[0], w1_0, w2_0, final_norm, tms, 512, False)

    w_in1 = bf(l1_w_in)
    z1p = norm_proj(xp, attn_norm[1], w_in1, tmp, 1024)
    z1s = norm_proj(xs, attn_norm[1], w_in1, tms, 1024)
    o1_p = diff_prompt(z1p, tp, rel_bias, lam_vecs, l1_head_norm, n, t_len)
    v_halves = cache_l1_diff_v.reshape(n_pool, PAGE, C_KV, 2, HEAD_DIM).transpose(0, 1, 3, 2, 4)
    o1_s = diff_sample(z1s, lanes(cache_l1_diff_k), lanes(v_halves), bs, lam_vecs, l1_head_norm, page_table)
    w_out1 = bf(l1_w_out)
    w1_1, w2_1 = bf(mlp_w1[1]), bf(mlp_w2[1])
    xp = out_proj(xp, [o1_p], w_out1, tmp, 1024)
    xs = out_proj(xs, [o1_s], w_out1, tms, 1024)
    y_prompt = mlp(xp, mlp_norm[1], w1_1, w2_1, final_norm, tmp, 512, True).reshape(n, t_len, d)
    y_sample = mlp(xs, mlp_norm[1], w1_1, w2_1, final_norm, tms, 512, True).reshape(bd, nq, d)

    row4 = lambda a, b: a.reshape(b, -1, A_KV, HEAD_DIM)
    win = min(WINDOW, t_len)
    outs = [y_prompt, y_sample]
    for name in ("kc", "vc", "ks", "vs"):
        outs += [row4(p_rows[name], n), row4(s_rows[name], bd)]
    for name, state in (("kw", state_l0_nsa_win_k), ("vw", state_l0_nsa_win_v)):
        outs += [row4(p_rows[name], n)[:, t_len - win:],
                 jnp.concatenate([state, row4(s_rows[name], bd)], axis=1)[:, -wb:]]
    for name in ("kb", "vb"):
        outs += [row4(p_rows[name], n), row4(s_rows[name], bd)]
    outs += [cut(zp, C_TAIL + T_KI, IDX_DIM).reshape(n, t_len, IDX_DIM),
             cut(zs, C_TAIL + T_KI, IDX_DIM).reshape(bd, nq, IDX_DIM)]
    k_cols, v_cols = C_KV * 2 * HEAD_DIM, C_KV * C_VDIM
    q_cols = C_HEADS * 2 * HEAD_DIM
    outs += [cut(z1p, q_cols, k_cols).reshape(n, t_len, C_KV, 2, HEAD_DIM),
             cut(z1s, q_cols, k_cols).reshape(bd, nq, C_KV, 2, HEAD_DIM),
             cut(z1p, q_cols + k_cols, v_cols).reshape(n, t_len, C_KV, C_VDIM),
             cut(z1s, q_cols + k_cols, v_cols).reshape(bd, nq, C_KV, C_VDIM)]
    return tuple(outs)
```

```python
import functools
import math

import jax
import jax.numpy as jnp
from jax import lax
from jax.experimental import pallas as pl
from jax.experimental.pallas import tpu as pltpu

F32 = jnp.float32
I32 = jnp.int32
MXU_DT = jnp.bfloat16

HEAD_DIM = 128
A_HEADS, A_KV, A_GROUP = 8, 2, 4
B_HEADS, B_KV, B_GROUP = 8, 2, 4
C_HEADS, C_KV, C_GROUP, C_VDIM = 8, 4, 2, 256
CMP_STRIDE, CMP_BLOCK, CMP_HIDDEN = 16, 32, 256
SEL_BLOCK, N_SEL_BLOCKS, WINDOW, N_GATES = 64, 16, 512, 3
IDX_HEADS, IDX_DIM, DSA_TOPK = 4, 64, 256
NUM_BUCKETS, MAX_DISTANCE = 32, 128
LAMBDA_INIT = 0.8 - 0.6 * math.exp(-0.3 * 1)
RMS_EPS = 1e-6
NEG = -1e30
SCALE = HEAD_DIM ** -0.5
QB = 128
CHUNK = 512
PAGE = 128
CHUNK_PITCH = CMP_STRIDE * A_KV + 8
assert QB >= MAX_DISTANCE and WINDOW % QB == 0 and WINDOW >= 2 * QB

C_QA, C_QB, C_KC, C_VC, C_KS, C_VS, C_KW, C_VW, C_KB, C_VB, C_QI, C_TAIL = (
    0, 1024, 2048, 2304, 2560, 2816, 3072, 3328, 3584, 3840, 4096, 4352)
T_KI, T_GA, T_WI = 0, 64, 88
L0_COLS = 4608
VMEM_LIMIT = 56 * 1024 * 1024


def _cparams(*sem):
    return pltpu.CompilerParams(dimension_semantics=sem, vmem_limit_bytes=VMEM_LIMIT)


def _nt(a, b):
    return lax.dot_general(a, b, (((1,), (1,)), ((), ())), preferred_element_type=F32)


def _mm(a, b):
    return jnp.dot(a, b, preferred_element_type=F32)


def _rms(x, g):
    return x * lax.rsqrt(jnp.mean(x * x, axis=-1, keepdims=True) + RMS_EPS) * g


def _norm_proj_kernel(x_ref, g_ref, w_ref, o_ref, xn_ref):
    @pl.when(pl.program_id(1) == 0)
    def _():
        xn_ref[...] = _rms(x_ref[...], g_ref[...]).astype(xn_ref.dtype)

    o_ref[...] = _mm(xn_ref[...], w_ref[...])


def norm_proj(x, gain, w, tm, tn):
    rows, d = x.shape
    n = w.shape[1]
    return pl.pallas_call(
        _norm_proj_kernel,
        grid=(rows // tm, n // tn),
        in_specs=[pl.BlockSpec((tm, d), lambda i, j: (i, 0)),
                  pl.BlockSpec((1, d), lambda i, j: (0, 0)),
                  pl.BlockSpec((d, tn), lambda i, j: (0, j))],
        out_specs=pl.BlockSpec((tm, tn), lambda i, j: (i, j)),
        out_shape=jax.ShapeDtypeStruct((rows, n), F32),
        scratch_shapes=[pltpu.VMEM((tm, d), MXU_DT)],
        compiler_params=_cparams("parallel", "arbitrary"),
        name="norm_proj",
    )(x, gain.reshape(1, d), w)


def _out_proj_kernel(*refs, n_in):
    x_ref, o_refs, w_refs, y_ref = refs[0], refs[1:1 + n_in], refs[1 + n_in:1 + 2 * n_in], refs[-1]
    acc = x_ref[...]
    for o_ref, w_ref in zip(o_refs, w_refs):
        acc = acc + _mm(o_ref[...].astype(MXU_DT), w_ref[...])
    y_ref[...] = acc


def out_proj(x, outs, w, tm, tn):
    rows, d = x.shape
    o_specs, w_specs, row0 = [], [], 0
    for o in outs:
        k = o.shape[1]
        o_specs.append(pl.BlockSpec((tm, k), lambda i, j: (i, 0)))
        w_specs.append(pl.BlockSpec((k, tn), lambda i, j, rb=row0 // k: (rb, j)))
        row0 += k
    return pl.pallas_call(
        functools.partial(_out_proj_kernel, n_in=len(outs)),
        grid=(rows // tm, d // tn),
        in_specs=[pl.BlockSpec((tm, tn), lambda i, j: (i, j))] + o_specs + w_specs,
        out_specs=pl.BlockSpec((tm, tn), lambda i, j: (i, j)),
        out_shape=jax.ShapeDtypeStruct((rows, d), F32),
        compiler_params=_cparams("parallel", "arbitrary"),
        name="out_proj",
    )(x, *outs, *([w] * len(outs)))


def _mlp_kernel(x_ref, g_ref, w1_ref, w2_ref, gf_ref, y_ref, xn_ref, *, final_norm):
    j = pl.program_id(1)

    @pl.when(j == 0)
    def _():
        x = x_ref[...]
        xn_ref[...] = _rms(x, g_ref[...]).astype(xn_ref.dtype)
        y_ref[...] = x

    h = jnp.square(jnp.maximum(_mm(xn_ref[...], w1_ref[...]), 0.0))
    y_ref[...] += _mm(h.astype(w2_ref.dtype), w2_ref[...])

    if final_norm:
        @pl.when(j == pl.num_programs(1) - 1)
        def _():
            y_ref[...] = _rms(y_ref[...], gf_ref[...])


def mlp(x, gain, w1, w2, final_gain, tm, tf, final_norm):
    rows, d = x.shape
    ff = w1.shape[1]
    return pl.pallas_call(
        functools.partial(_mlp_kernel, final_norm=final_norm),
        grid=(rows // tm, ff // tf),
        in_specs=[pl.BlockSpec((tm, d), lambda i, j: (i, 0)),
                  pl.BlockSpec((1, d), lambda i, j: (0, 0)),
                  pl.BlockSpec((d, tf), lambda i, j: (0, j)),
                  pl.BlockSpec((tf, d), lambda i, j: (j, 0)),
                  pl.BlockSpec((1, d), lambda i, j: (0, 0))],
        out_specs=pl.BlockSpec((tm, d), lambda i, j: (i, 0)),
        out_shape=jax.ShapeDtypeStruct((rows, d), F32),
        scratch_shapes=[pltpu.VMEM((tm, d), MXU_DT)],
        compiler_params=_cparams("parallel", "arbitrary"),
        name="mlp",
    )(x, gain.reshape(1, d), w1, w2, final_gain.reshape(1, d))


def _bucket(dist):
    n = jnp.maximum(dist, 0)
    max_exact = NUM_BUCKETS // 2
    nf = jnp.maximum(n, 1).astype(F32)
    large = max_exact + (jnp.log(nf / max_exact) / math.log(MAX_DISTANCE / max_exact)
                         * (NUM_BUCKETS - max_exact)).astype(I32)
    large = jnp.minimum(large, NUM_BUCKETS - 1)
    return jnp.where(n < max_exact, n, large)


def _lookup(tbl_ref, col, buckets):
    def body(b, accs):
        v = tbl_ref[b, col]
        return tuple(jnp.where(bk == b, v, acc) for bk, acc in zip(buckets, accs))
    return lax.fori_loop(0, NUM_BUCKETS, body, tuple(jnp.zeros(bk.shape, F32) for bk in buckets))


def _bias_tiles_kernel(tbl_ref, tp_ref, bs_ref, *, q0):
    h = pl.program_id(0)
    t = lax.broadcasted_iota(I32, (QB, QB), 0)
    k = lax.broadcasted_iota(I32, (QB, QB), 1)
    ts = lax.broadcasted_iota(I32, bs_ref.shape[1:], 0)
    ks = lax.broadcasted_iota(I32, bs_ref.shape[1:], 1)
    d0, d1, ds = _lookup(tbl_ref, h, (_bucket(t - k), _bucket(QB + t - k), _bucket(q0 + ts - ks)))
    tp_ref[0, 0] = d0
    tp_ref[0, 1] = d1
    bs_ref[0] = ds


def bias_tiles(rel_bias, q0, n_q, lp):
    nh = rel_bias.shape[1]
    return pl.pallas_call(
        functools.partial(_bias_tiles_kernel, q0=q0),
        grid=(nh,),
        in_specs=[pl.BlockSpec(memory_space=pltpu.SMEM)],
        out_specs=[pl.BlockSpec((1, 2, QB, QB), lambda h: (h, 0, 0, 0)),
                   pl.BlockSpec((1, n_q, lp), lambda h: (h, 0, 0))],
        out_shape=[jax.ShapeDtypeStruct((nh, 2, QB, QB), F32),
                   jax.ShapeDtypeStruct((nh, n_q, lp), F32)],
        compiler_params=_cparams("arbitrary"),
        name="bias_tiles",
    )(rel_bias)


def _bias_cmp_kernel(tbl_ref, bp_ref, bs_ref, *, q0):
    h = pl.program_id(0)
    tp = lax.broadcasted_iota(I32, bp_ref.shape[1:], 0)
    cp = lax.broadcasted_iota(I32, bp_ref.shape[1:], 1)
    ts = lax.broadcasted_iota(I32, bs_ref.shape[1:], 0)
    cs = lax.broadcasted_iota(I32, bs_ref.shape[1:], 1)
    end = CMP_BLOCK - 1
    bp, bs = _lookup(tbl_ref, h, (_bucket(tp - (cp * CMP_STRIDE + end)),
                                  _bucket(q0 + ts - (cs * CMP_STRIDE + end))))
    bp_ref[0] = bp
    bs_ref[0] = bs


def bias_cmp(rel_bias, t_len, mc_p, q0, n_q, mc_s):
    return pl.pallas_call(
        functools.partial(_bias_cmp_kernel, q0=q0),
        grid=(A_HEADS,),
        in_specs=[pl.BlockSpec(memory_space=pltpu.SMEM)],
        out_specs=[pl.BlockSpec((1, t_len, mc_p), lambda h: (h, 0, 0)),
                   pl.BlockSpec((1, n_q, mc_s), lambda h: (h, 0, 0))],
        out_shape=[jax.ShapeDtypeStruct((A_HEADS, t_len, mc_p), F32),
                   jax.ShapeDtypeStruct((A_HEADS, n_q, mc_s), F32)],
        compiler_params=_cparams("arbitrary"),
        name="bias_cmp",
    )(rel_bias)


def _softmax_rows(z, mask):
    z = jnp.where(mask, z, NEG)
    m = jnp.max(z, axis=-1, keepdims=True)
    e = jnp.where(mask, jnp.exp(z - m), 0.0)
    l = jnp.sum(e, axis=-1, keepdims=True)
    return e / jnp.where(l > 0.0, l, 1.0)


def _gelu_tanh(x):
    return 0.5 * x * (1.0 + jnp.tanh(math.sqrt(2.0 / math.pi) * (x + 0.044715 * (x * x * x))))


def _compress(x_fn, m, w1a_ref, w1b_ref, w2_ref, pe_ref, o_ref):
    pe = pe_ref[...].astype(MXU_DT)
    pos = _mm(pe, w1a_ref[...])[0:1] + _mm(pe, w1b_ref[...])[1:2]
    last = lax.broadcasted_iota(I32, (m, 1), 0) == m - 1
    for g in range(A_KV):
        xg = jnp.concatenate([x_fn(j, g).astype(MXU_DT) for j in range(CMP_STRIDE)], axis=1)
        first = _mm(xg, w1a_ref[...])
        second = pltpu.roll(_mm(xg, w1b_ref[...]), m - 1, 0)
        hid = _gelu_tanh(first + second + pos)
        out = _mm(hid.astype(MXU_DT), w2_ref[...])
        o_ref[:, g * HEAD_DIM:(g + 1) * HEAD_DIM] = jnp.where(last, 0.0, out)


def _overlap(mc, jn, n_cmp, n_slc):
    c = lax.broadcasted_iota(I32, (mc, jn), 0)
    j = lax.broadcasted_iota(I32, (mc, jn), 1)
    ov = ((c * CMP_STRIDE < j * SEL_BLOCK + SEL_BLOCK) & (c * CMP_STRIDE + CMP_BLOCK > j * SEL_BLOCK)
          & (c < n_cmp) & (j < n_slc))
    return jnp.where(ov, 1.0, 0.0)


def _select_blocks(imp, pos, n_slc):
    jn = imp.shape[1]
    jidx = lax.broadcasted_iota(I32, (1, jn), 1)
    cur = pos // SEL_BLOCK
    forced = (jidx == 0) | (jidx == cur) | (jidx == cur - 1)
    future = jidx * SEL_BLOCK > pos
    score = jnp.where(future, -1.0, jnp.where(forced, 1e3, imp))
    score = jnp.where(jidx < n_slc, score, -2.0)

    def body(i, rank):
        col = jnp.sum(jnp.where(jidx == i, score, 0.0), axis=-1, keepdims=True)
        beats = jnp.where(col > score, 1.0, jnp.where(col == score, jnp.where(i < jidx, 1.0, 0.0), 0.0))
        return rank + beats

    rank = lax.fori_loop(0, n_slc, body, jnp.zeros(score.shape, F32), unroll=8)
    n_sel = min(N_SEL_BLOCKS, n_slc)
    return jnp.where((rank < n_sel) & (jidx < n_slc), 1.0, 0.0)


def _sort_key(s):
    bits = lax.bitcast_convert_type(jnp.where(s == 0.0, 0.0, s), I32)
    return jnp.where(bits < 0, bits ^ jnp.int32(0x7FFFFFFF), bits)


def _topk_madd(key_ref, madd_ref, valid_fn, nch, cw, k, nbits):
    n_rows = key_ref.shape[0]
    kf = jnp.float32(k)

    def count(fn):
        def body(c, acc):
            c0 = pl.multiple_of(c * cw, cw)
            hit = jnp.where(fn(c0, key_ref[:, pl.ds(c0, cw)]), 1.0, 0.0)
            return acc + sum(hit[:, i:i + 128] for i in range(0, cw, 128))
        acc = lax.fori_loop(0, nch, body, jnp.zeros((n_rows, 128), F32))
        return jnp.sum(acc, axis=-1, keepdims=True)

    int_min = jnp.int32(-2 ** 31)
    thr0 = jnp.where(count(lambda c0, key: key >= 0) >= kf, jnp.int32(0), int_min)

    def vbody(i, thr):
        cand = thr | lax.shift_left(jnp.int32(1), 30 - i)
        return jnp.where(count(lambda c0, key: key >= cand) >= kf, cand, thr)

    thr = lax.fori_loop(0, 31, vbody, thr0)
    need = kf - count(lambda c0, key: key > thr)

    def idx(c0):
        return c0 + lax.broadcasted_iota(I32, (1, cw), 1)

    def ibody(i, cut):
        cand = cut | lax.shift_left(jnp.int32(1), nbits - 1 - i)
        return jnp.where(count(lambda c0, key: (key == thr) & (idx(c0) < cand)) <= need, cand, cut)

    cut = lax.fori_loop(0, nbits, ibody, jnp.zeros((n_rows, 1), I32))

    def write(c, _):
        c0 = pl.multiple_of(c * cw, cw)
        key = key_ref[:, pl.ds(c0, cw)]
        sel = ((key > thr) | ((key == thr) & (idx(c0) < cut))) & valid_fn(c0)
        madd_ref[:, pl.ds(c0, cw)] = jnp.where(sel, 0.0, NEG)
        return 0

    lax.fori_loop(0, nch, write, 0)


def _causal_attn(streams, dv, qb):
    m_rows = streams[0][0].shape[0]
    per = CHUNK // QB
    nact = qb // per + 1

    def stage(st, k0, w, bias):
        q, k_fn, _, _, _, madd_fn, s_ref = st
        s = _nt(q, k_fn(k0, w)) * SCALE + bias
        madd = madd_fn(k0, w)
        s_ref[:, pl.ds(k0, w)] = s if madd is None else s + madd

    def far(c, _):
        for st in streams:
            stage(st, pl.multiple_of(c * CHUNK, CHUNK), CHUNK, st[3])
        return 0

    lax.fori_loop(0, nact, far, 0)
    for st in streams:
        stage(st, pl.multiple_of(qb * QB, QB), QB, st[4](True))

    @pl.when(qb >= 1)
    def _():
        for st in streams:
            stage(st, pl.multiple_of((qb - 1) * QB, QB), QB, st[4](False))

    for j in range(1, per):
        @pl.when(qb % per + j < per)
        def _():
            for st in streams:
                st[6][:, pl.ds(pl.multiple_of((qb + j) * QB, QB), QB)] = jnp.full((m_rows, QB), NEG, F32)

    def row_max(c, ms):
        k0 = pl.multiple_of(c * CHUNK, CHUNK)
        return tuple(jnp.maximum(m, jnp.max(st[6][:, pl.ds(k0, CHUNK)], axis=-1, keepdims=True))
                     for st, m in zip(streams, ms))

    ms = lax.fori_loop(0, nact, row_max, tuple(jnp.full((m_rows, 1), NEG, F32) for _ in streams))

    def pv(c, carry):
        k0 = pl.multiple_of(c * CHUNK, CHUNK)
        out = []
        for st, m, (l, acc) in zip(streams, ms, carry):
            p = jnp.exp(st[6][:, pl.ds(k0, CHUNK)] - m)
            out.append((l + jnp.sum(p, axis=-1, keepdims=True), acc + _mm(p.astype(MXU_DT), st[2](k0, CHUNK))))
        return tuple(out)

    init = tuple((jnp.zeros((m_rows, 1), F32), jnp.zeros((m_rows, dv), F32)) for _ in streams)
    return [acc / l for l, acc in lax.fori_loop(0, nact, pv, init)]


def _causal_add(rep):
    t = lax.broadcasted_iota(I32, (QB, QB), 0)
    k = lax.broadcasted_iota(I32, (QB, QB), 1)
    return jnp.concatenate([jnp.where(k <= t, 0.0, NEG)] * rep, axis=0)


def _far_bias(tbl_ref, cols):
    return jnp.concatenate([jnp.full((QB, 1), tbl_ref[NUM_BUCKETS - 1, c], F32) for c in cols], axis=0)


def _near_bias(tp_ref, cols, diag):
    return jnp.concatenate([tp_ref[c, 0 if diag else 1] for c in cols], axis=0)


def _stack_heads(ref, col0, n):
    return jnp.concatenate([ref[:, col0 + r * HEAD_DIM:col0 + (r + 1) * HEAD_DIM] for r in range(n)], axis=0)


def _compress_prompt_kernel(xk_ref, xv_ref, w1ak, w1bk, w2k, pek, w1av, w1bv, w2v, pev, ok_ref, ov_ref):
    lanes = lambda x_ref: (lambda j, g: x_ref[0, :, (2 * j + g) * HEAD_DIM:(2 * j + g + 1) * HEAD_DIM])
    _compress(lanes(xk_ref), xk_ref.shape[1], w1ak, w1bk, w2k, pek, ok_ref.at[0])
    _compress(lanes(xv_ref), xv_ref.shape[1], w1av, w1bv, w2v, pev, ov_ref.at[0])


def _cmp_weight_specs():
    full = lambda shape: pl.BlockSpec(shape, lambda *_: (0,) * len(shape))
    half = CMP_STRIDE * HEAD_DIM
    one = [full((half, CMP_HIDDEN)), full((half, CMP_HIDDEN)), full((CMP_HIDDEN, HEAD_DIM)), full((16, half))]
    return one + one


def compress_prompt(xk, xv, cw):
    n, m, w = xk.shape
    spec = pl.BlockSpec((1, m, w), lambda i: (i, 0, 0))
    ospec = pl.BlockSpec((1, m, A_KV * HEAD_DIM), lambda i: (i, 0, 0))
    osh = jax.ShapeDtypeStruct((n, m, A_KV * HEAD_DIM), F32)
    return pl.pallas_call(
        _compress_prompt_kernel,
        grid=(n,),
        in_specs=[spec, spec] + _cmp_weight_specs(),
        out_specs=[ospec, ospec],
        out_shape=[osh, osh],
        compiler_params=_cparams("parallel"),
        name="compress_prompt",
    )(xk, xv, *cw)


def _window_attn(q, kw_ref, vw_ref, gl, qb, pos4, tbl_ref, tp_ref, cols):
    n_tiles = WINDOW // QB + 1
    width = n_tiles * QB
    lo = jnp.maximum(qb - (n_tiles - 1), 0)
    w0 = pl.multiple_of(lo * QB, QB)
    tiles = []
    for j in range(n_tiles):
        rel = qb - (lo + j)
        tiles.append(jnp.concatenate(
            [jnp.where(rel == 0, tp_ref[c, 0], jnp.where(rel == 1, tp_ref[c, 1], tbl_ref[NUM_BUCKETS - 1, c]))
             for c in cols], axis=0))
    s = _nt(q, kw_ref[pl.ds(w0, width), gl].astype(MXU_DT)) * SCALE + jnp.concatenate(tiles, axis=1)
    dist = pos4 - (w0 + lax.broadcasted_iota(I32, (1, width), 1))
    p = _softmax_rows(s, (dist >= 0) & (dist < WINDOW))
    return _mm(p.astype(MXU_DT), vw_ref[pl.ds(w0, width), gl].astype(MXU_DT))


def _nsa_prompt_kernel(tbl_ref, q_ref, tail_ref, kc_ref, vc_ref, ks_ref, vs_ref, kw_ref, vw_ref,
                       tp_ref, bc_ref, o_ref, s_ref, madd_ref, *, t_len):
    qb = pl.program_id(1)
    mc = kc_ref.shape[1]
    n_cmp = (t_len - CMP_BLOCK) // CMP_STRIDE + 1
    n_slc = -(-t_len // SEL_BLOCK)
    pos = qb * QB + lax.broadcasted_iota(I32, (QB, 1), 0)
    pos4 = jnp.concatenate([pos] * A_GROUP, axis=0)
    cidx = lax.broadcasted_iota(I32, (1, mc), 1)
    gates = jax.nn.sigmoid(tail_ref[...])
    overlap = _overlap(mc, QB, n_cmp, n_slc)
    causal_add = _causal_add(A_GROUP)
    onehot = jnp.where(lax.broadcasted_iota(I32, (QB, t_len), 0)
                       == lax.broadcasted_iota(I32, (QB, t_len), 1) // SEL_BLOCK, 1.0, 0.0).astype(MXU_DT)

    o_cmp, o_win, streams = [], [], []
    for g in range(A_KV):
        cols = [g * A_GROUP + r for r in range(A_GROUP)]
        gl = slice(g * HEAD_DIM, (g + 1) * HEAD_DIM)
        q = _stack_heads(q_ref, g * A_GROUP * HEAD_DIM, A_GROUP).astype(MXU_DT)
        lc = (_nt(q, kc_ref[0, :, gl].astype(MXU_DT)) * SCALE
              + jnp.concatenate([bc_ref[c] for c in cols], axis=0))
        p_cmp = _softmax_rows(lc, (pos4 >= cidx * CMP_STRIDE + (CMP_BLOCK - 1)) & (cidx < n_cmp))
        o_cmp.append(_mm(p_cmp.astype(MXU_DT), vc_ref[0, :, gl].astype(MXU_DT)))
        p_sum = sum(p_cmp[r * QB:(r + 1) * QB] for r in range(A_GROUP))
        imp = jnp.dot(p_sum, overlap, preferred_element_type=F32, precision=lax.Precision.HIGHEST)
        sel = _select_blocks(imp, pos, n_slc).astype(MXU_DT)
        madd_ref[g] = jnp.where(_mm(sel, onehot) > 0.5, 0.0, NEG)
        o_win.append(_window_attn(q, kw_ref, vw_ref, gl, qb, pos4, tbl_ref, tp_ref, cols))
        streams.append((
            q, lambda k0, w, gl=gl: ks_ref[pl.ds(k0, w), gl].astype(MXU_DT),
            lambda k0, w, gl=gl: vs_ref[pl.ds(k0, w), gl].astype(MXU_DT), _far_bias(tbl_ref, cols),
            lambda diag, cols=cols: _near_bias(tp_ref, cols, diag) + (causal_add if diag else 0.0),
            lambda k0, w, g=g: jnp.concatenate([madd_ref[g, :, pl.ds(k0, w)]] * A_GROUP, axis=0),
            s_ref.at[g]))
    o_slc = _causal_attn(streams, HEAD_DIM, qb)

    for h in range(A_HEADS):
        g, r = divmod(h, A_GROUP)
        c = T_GA + h * N_GATES
        rows = slice(r * QB, (r + 1) * QB)
        o = (gates[:, c:c + 1] * o_cmp[g][rows] + gates[:, c + 1:c + 2] * o_slc[g][rows]
             + gates[:, c + 2:c + 3] * o_win[g][rows])
        o_ref[:, h * HEAD_DIM:(h + 1) * HEAD_DIM] = o.astype(o_ref.dtype)


def nsa_prompt(z, k_cmp, v_cmp, tp, bc, rel_bias, n, t_len):
    nb = t_len // QB
    mc = k_cmp.shape[1]
    kv = lambda c: pl.BlockSpec((t_len, 256), lambda i, j: (i, c // 256))
    cmp_spec = pl.BlockSpec((1, mc, 256), lambda i, j: (i, 0, 0))
    return pl.pallas_call(
        functools.partial(_nsa_prompt_kernel, t_len=t_len),
        grid=(n, nb),
        in_specs=[pl.BlockSpec(memory_space=pltpu.SMEM),
                  pl.BlockSpec((QB, 1024), lambda i, j: (i * nb + j, C_QA // 1024)),
                  pl.BlockSpec((QB, 128), lambda i, j: (i * nb + j, C_TAIL // 128)),
                  cmp_spec, cmp_spec, kv(C_KS), kv(C_VS), kv(C_KW), kv(C_VW),
                  pl.BlockSpec(tp.shape, lambda i, j: (0, 0, 0, 0)),
                  pl.BlockSpec((A_HEADS, QB, mc), lambda i, j: (0, j, 0))],
        out_specs=pl.BlockSpec((QB, 1024), lambda i, j: (i * nb + j, 0)),
        out_shape=jax.ShapeDtypeStruct((n * t_len, 1024), MXU_DT),
        scratch_shapes=[pltpu.VMEM((A_KV, A_GROUP * QB, t_len), F32), pltpu.VMEM((A_KV, QB, t_len), F32)],
        compiler_params=_cparams("parallel", "arbitrary"),
        name="nsa_prompt",
    )(rel_bias, z, z, k_cmp, v_cmp, z, z, z, z, tp, bc)


def _dsa_prompt_kernel(tbl_ref, q_ref, qi_ref, tailq_ref, tailk_ref, kb_ref, vb_ref, tp_ref, o_ref,
                       key_ref, madd_ref, s_ref, *, topk, nbits):
    qb = pl.program_id(1)
    nact = qb // (CHUNK // QB) + 1
    pos = qb * QB + lax.broadcasted_iota(I32, (QB, 1), 0)
    wi = tailq_ref[:, T_WI:T_WI + IDX_HEADS]
    qis = [qi_ref[:, h * IDX_DIM:(h + 1) * IDX_DIM].astype(MXU_DT) for h in range(IDX_HEADS)]

    def causal(c0):
        return c0 + lax.broadcasted_iota(I32, (1, CHUNK), 1) <= pos

    def index_chunk(c, _):
        c0 = pl.multiple_of(c * CHUNK, CHUNK)
        ki = tailk_ref[pl.ds(c0, CHUNK), T_KI:T_KI + IDX_DIM].astype(MXU_DT)
        score = sum(jnp.maximum(_nt(qis[h], ki), 0.0) * wi[:, h:h + 1] for h in range(IDX_HEADS))
        score = score * (IDX_DIM ** -0.5 * IDX_HEADS ** -0.5)
        key_ref[:, pl.ds(c0, CHUNK)] = _sort_key(jnp.where(causal(c0), score, NEG))
        return 0

    lax.fori_loop(0, nact, index_chunk, 0)
    _topk_madd(key_ref, madd_ref, causal, nact, CHUNK, topk, nbits)

    streams = []
    for g in range(B_KV):
        cols = [A_HEADS + g * B_GROUP + r for r in range(B_GROUP)]
        gl = slice(g * HEAD_DIM, (g + 1) * HEAD_DIM)
        streams.append((
            _stack_heads(q_ref, g * B_GROUP * HEAD_DIM, B_GROUP).astype(MXU_DT),
            lambda k0, w, gl=gl: kb_ref[pl.ds(k0, w), gl].astype(MXU_DT),
            lambda k0, w, gl=gl: vb_ref[pl.ds(k0, w), gl].astype(MXU_DT),
            _far_bias(tbl_ref, cols), lambda diag, cols=cols: _near_bias(tp_ref, cols, diag),
            lambda k0, w: jnp.concatenate([madd_ref[:, pl.ds(k0, w)]] * B_GROUP, axis=0), s_ref.at[g]))
    outs = _causal_attn(streams, HEAD_DIM, qb)
    for h in range(B_HEADS):
        g, r = divmod(h, B_GROUP)
        o_ref[:, h * HEAD_DIM:(h + 1) * HEAD_DIM] = outs[g][r * QB:(r + 1) * QB].astype(o_ref.dtype)


def dsa_prompt(z, tp, rel_bias, n, t_len):
    nb = t_len // QB
    topk = min(DSA_TOPK, t_len // 4)
    nbits = int(t_len).bit_length()
    return pl.pallas_call(
        functools.partial(_dsa_prompt_kernel, topk=topk, nbits=nbits),
        grid=(n, nb),
        in_specs=[pl.BlockSpec(memory_space=pltpu.SMEM),
                  pl.BlockSpec((QB, 1024), lambda i, j: (i * nb + j, C_QB // 1024)),
                  pl.BlockSpec((QB, 256), lambda i, j: (i * nb + j, C_QI // 256)),
                  pl.BlockSpec((QB, 128), lambda i, j: (i * nb + j, C_TAIL // 128)),
                  pl.BlockSpec((t_len, 128), lambda i, j: (i, C_TAIL // 128)),
                  pl.BlockSpec((t_len, 256), lambda i, j: (i, C_KB // 256)),
                  pl.BlockSpec((t_len, 256), lambda i, j: (i, C_VB // 256)),
                  pl.BlockSpec(tp.shape, lambda i, j: (0, 0, 0, 0))],
        out_specs=pl.BlockSpec((QB, 1024), lambda i, j: (i * nb + j, 0)),
        out_shape=jax.ShapeDtypeStruct((n * t_len, 1024), MXU_DT),
        scratch_shapes=[pltpu.VMEM((QB, t_len), I32), pltpu.VMEM((QB, t_len), F32),
                        pltpu.VMEM((B_KV, B_GROUP * QB, t_len), F32)],
        compiler_params=_cparams("parallel", "arbitrary"),
        name="dsa_prompt",
    )(rel_bias, z, z, z, z, z, z, tp)


def _diff_lambda(lam_ref):
    v = lam_ref[...]
    e1 = jnp.exp(jnp.sum(v[0:1] * v[1:2], axis=-1, keepdims=True))
    e2 = jnp.exp(jnp.sum(v[2:3] * v[3:4], axis=-1, keepdims=True))
    return e1 - e2 + LAMBDA_INIT


def _diff_finish(o, hn_ref):
    return _rms(o, hn_ref[...]) * (1.0 - LAMBDA_INIT)


def _diff_prompt_kernel(tbl_ref, q_ref, k_ref, v_ref, tp_ref, lam_ref, hn_ref, o_ref, s_ref):
    g = pl.program_id(1)
    qb = pl.program_id(2)
    causal_add = _causal_add(C_GROUP)
    streams = []
    for m in range(2):
        cols = [m * C_HEADS + g * C_GROUP + r for r in range(C_GROUP)]
        q = jnp.concatenate([q_ref[:, (r * 2 + m) * HEAD_DIM:(r * 2 + m + 1) * HEAD_DIM]
                             for r in range(C_GROUP)], axis=0).astype(MXU_DT)
        streams.append((
            q, lambda k0, w, m=m: k_ref[pl.ds(k0, w), m * HEAD_DIM:(m + 1) * HEAD_DIM].astype(MXU_DT),
            lambda k0, w: v_ref[pl.ds(k0, w), :].astype(MXU_DT), _far_bias(tbl_ref, cols),
            lambda diag, cols=cols: _near_bias(tp_ref, cols, diag) + (causal_add if diag else 0.0),
            lambda k0, w: None, s_ref.at[m]))
    outs = _causal_attn(streams, C_VDIM, qb)
    o = _diff_finish(outs[0] - _diff_lambda(lam_ref) * outs[1], hn_ref)
    for r in range(C_GROUP):
        o_ref[:, r * C_VDIM:(r + 1) * C_VDIM] = o[r * QB:(r + 1) * QB].astype(o_ref.dtype)


def diff_prompt(z1, tp, rel_bias, lam_vecs, head_norm, n, t_len):
    nb = t_len // QB
    return pl.pallas_call(
        _diff_prompt_kernel,
        grid=(n, C_KV, nb),
        in_specs=[pl.BlockSpec(memory_space=pltpu.SMEM),
                  pl.BlockSpec((QB, 512), lambda i, g, j: (i * nb + j, g)),
                  pl.BlockSpec((t_len, 256), lambda i, g, j: (i, 2048 // 256 + g)),
                  pl.BlockSpec((t_len, 256), lambda i, g, j: (i, 3072 // 256 + g)),
                  pl.BlockSpec(tp.shape, lambda i, g, j: (0, 0, 0, 0)),
                  pl.BlockSpec((4, HEAD_DIM), lambda i, g, j: (0, 0)),
                  pl.BlockSpec((1, C_VDIM), lambda i, g, j: (0, 0))],
        out_specs=pl.BlockSpec((QB, 512), lambda i, g, j: (i * nb + j, g)),
        out_shape=jax.ShapeDtypeStruct((n * t_len, C_HEADS * C_VDIM), MXU_DT),
        scratch_shapes=[pltpu.VMEM((2, C_GROUP * QB, t_len), F32)],
        compiler_params=_cparams("parallel", "parallel", "arbitrary"),
        name="diff_prompt",
    )(rel_bias, z1, z1, z1, tp, lam_vecs, head_norm.reshape(1, C_VDIM))


def _gather_start(src_fn, pt_ref, b, n_pages, buf_ref, rows, sem):
    def body(p, _):
        dst = buf_ref.at[pl.ds(pl.multiple_of(p * rows, rows), rows)]
        pltpu.make_async_copy(src_fn(pt_ref[b, p]), dst, sem).start()
        return 0
    lax.fori_loop(0, n_pages, body, 0)


def _gather_wait(src_fn, n_pages, buf_ref, rows, sem):
    def body(p, _):
        pltpu.make_async_copy(src_fn(0), buf_ref.at[pl.ds(0, rows)], sem).wait()
        return 0
    lax.fori_loop(0, n_pages, body, 0)


def _pad_rows(x, rows):
    return jnp.concatenate([x, jnp.zeros((rows - x.shape[0], x.shape[1]), x.dtype)], axis=0)


def _page_rows(pool_ref, rows):
    return lambda pg: pool_ref.at[pl.ds(pl.multiple_of(pg * rows, rows), rows)]


def _interleaved(buf_ref, n, j):
    return lambda c0, ch: buf_ref[pl.ds(c0 * n + j, ch, stride=n), :]


def _sample_scores(q, k_fn, knew, bias_fn, mask_fn, s_ref, past, ch, scale=SCALE):
    def body(c, _):
        c0 = pl.multiple_of(c * ch, ch)
        k = k_fn(c0, ch).astype(MXU_DT)
        s = _nt(q, k) * scale + bias_fn(c0, ch)
        s_ref[:, pl.ds(c0, ch)] = jnp.where(mask_fn(c0, ch, False), s, NEG)
        return 0

    lax.fori_loop(0, past // ch, body, 0)
    s = _nt(q, _pad_rows(knew, 128).astype(MXU_DT)) * scale + bias_fn(past, 128)
    s_ref[:, past:past + 128] = jnp.where(mask_fn(past, 128, True), s, NEG)


def _sample_softmax(s_ref):
    z = s_ref[...]
    m = jnp.max(z, axis=-1, keepdims=True)
    e = jnp.where(z > 0.5 * NEG, jnp.exp(z - m), 0.0)
    l = jnp.sum(e, axis=-1, keepdims=True)
    return e / jnp.where(l > 0.0, l, 1.0)


def _sample_pv(p_ref, v_fn, vnew, past, ch):
    def body(c, acc):
        c0 = pl.multiple_of(c * ch, ch)
        return acc + _mm(p_ref[:, pl.ds(c0, ch)].astype(MXU_DT), v_fn(c0, ch).astype(MXU_DT))

    acc = lax.fori_loop(0, past // ch, body, jnp.zeros((p_ref.shape[0], vnew.shape[1]), F32))
    return acc + _mm(p_ref[:, past:past + 128].astype(MXU_DT), _pad_rows(vnew, 128).astype(MXU_DT))


def _new_key_mask(nq, rep):
    t = lax.broadcasted_iota(I32, (nq, 128), 0)
    j = lax.broadcasted_iota(I32, (nq, 128), 1)
    return jnp.concatenate([(j <= t) & (j < nq)] * rep, axis=0)


def _compress_sample_kernel(pt_ref, pk_ref, pv_ref, w1ak, w1bk, w2k, pek, w1av, w1bv, w2v, pev,
                            ok_ref, ov_ref, bk_ref, bv_ref, sem, *, n_pages):
    b = pl.program_id(0)
    cpp = PAGE // CMP_STRIDE
    rows = CMP_STRIDE * A_KV

    def copy(pool_ref, buf_ref, sem_k, pg, p, i):
        src = pool_ref.at[pl.ds(pl.multiple_of((pg * cpp + i) * rows, rows), rows)]
        dst = buf_ref.at[pl.ds(pl.multiple_of((p * cpp + i) * CHUNK_PITCH, 8), rows)]
        return pltpu.make_async_copy(src, dst, sem_k)

    def start(p, _):
        for i in range(cpp):
            copy(pk_ref, bk_ref, sem.at[0], pt_ref[b, p], p, i).start()
            copy(pv_ref, bv_ref, sem.at[1], pt_ref[b, p], p, i).start()
        return 0

    lax.fori_loop(0, n_pages, start, 0)
    m = n_pages * cpp
    for pool_ref, buf_ref, k, w, o in ((pk_ref, bk_ref, 0, (w1ak, w1bk, w2k, pek), ok_ref),
                                       (pv_ref, bv_ref, 1, (w1av, w1bv, w2v, pev), ov_ref)):
        def wait(c, _, pool_ref=pool_ref, buf_ref=buf_ref, k=k):
            copy(pool_ref, buf_ref, sem.at[k], 0, 0, 0).wait()
            return 0

        lax.fori_loop(0, m, wait, 0)
        _compress(lambda j, g, buf_ref=buf_ref: buf_ref[pl.ds(2 * j + g, m, stride=CHUNK_PITCH), :], m, *w, o.at[0])


def compress_sample(pool_k, pool_v, page_table, cw):
    bd, n_pages = page_table.shape
    m = n_pages * (PAGE // CMP_STRIDE)
    buf = pltpu.VMEM((m * CHUNK_PITCH, HEAD_DIM), F32)
    ospec = pl.BlockSpec((1, m, A_KV * HEAD_DIM), lambda i, pt: (i, 0, 0))
    osh = jax.ShapeDtypeStruct((bd, m, A_KV * HEAD_DIM), F32)
    return pl.pallas_call(
        functools.partial(_compress_sample_kernel, n_pages=n_pages),
        grid_spec=pltpu.PrefetchScalarGridSpec(
            num_scalar_prefetch=1, grid=(bd,),
            in_specs=[pl.BlockSpec(memory_space=pl.ANY), pl.BlockSpec(memory_space=pl.ANY)] + _cmp_weight_specs(),
            out_specs=[ospec, ospec],
            scratch_shapes=[buf, buf, pltpu.SemaphoreType.DMA((2,))]),
        out_shape=[osh, osh],
        compiler_params=_cparams("arbitrary"),
        name="compress_sample",
    )(page_table, pool_k, pool_v, *cw)


def _nsa_sample_kernel(pt_ref, z_ref, kc_ref, vc_ref, pks_ref, pvs_ref, wk_ref, wv_ref, bs_ref, bc_ref,
                       o_ref, kbuf, vbuf, s_ref, sw_ref, sem, *, n_pages, ch):
    b = pl.program_id(0)
    past = n_pages * PAGE
    nq = z_ref.shape[0]
    mc = kc_ref.shape[1]
    t_len = past + nq
    n_cmp = (t_len - CMP_BLOCK) // CMP_STRIDE + 1
    n_slc = -(-t_len // SEL_BLOCK)
    jn = 128 * (-(-n_slc // 128))
    wb = wk_ref.shape[0] // A_KV
    page_rows = PAGE * A_KV
    _gather_start(_page_rows(pks_ref, page_rows), pt_ref, b, n_pages, kbuf, page_rows, sem.at[0])
    _gather_start(_page_rows(pvs_ref, page_rows), pt_ref, b, n_pages, vbuf, page_rows, sem.at[1])

    pos = past + lax.broadcasted_iota(I32, (nq, 1), 0)
    pos4 = jnp.concatenate([pos] * A_GROUP, axis=0)
    cidx = lax.broadcasted_iota(I32, (1, mc), 1)
    gates = jax.nn.sigmoid(z_ref[:, C_TAIL:C_TAIL + 128])
    overlap = _overlap(mc, jn, n_cmp, n_slc)
    new_mask = _new_key_mask(nq, A_GROUP)
    waited = False

    for g in range(A_KV):
        cols = [g * A_GROUP + r for r in range(A_GROUP)]
        q = _stack_heads(z_ref, C_QA + g * A_GROUP * HEAD_DIM, A_GROUP).astype(MXU_DT)
        gl = slice(g * HEAD_DIM, (g + 1) * HEAD_DIM)
        lc = (_nt(q, kc_ref[0, :, gl].astype(MXU_DT)) * SCALE
              + jnp.concatenate([bc_ref[c] for c in cols], axis=0))
        p_cmp = _softmax_rows(lc, (pos4 >= cidx * CMP_STRIDE + (CMP_BLOCK - 1)) & (cidx < n_cmp))
        o_cmp = _mm(p_cmp.astype(MXU_DT), vc_ref[0, :, gl].astype(MXU_DT))
        p_sum = sum(p_cmp[r * nq:(r + 1) * nq] for r in range(A_GROUP))
        imp = jnp.dot(p_sum, overlap, preferred_element_type=F32, precision=lax.Precision.HIGHEST)
        sel = _select_blocks(imp, pos, n_slc).astype(MXU_DT)
        def win_bias(c0, w, cols=cols):
            return jnp.concatenate([bs_ref[c, :, pl.ds(past - wb + c0, w)] for c in cols], axis=0)

        def win_mask(c0, w, is_new):
            dist = pos4 - (past - wb + c0 + lax.broadcasted_iota(I32, (1, w), 1))
            valid = (dist >= 0) & (dist < WINDOW)
            return valid & new_mask if is_new else valid

        _sample_scores(q, _interleaved(wk_ref, A_KV, g),
                       z_ref[:, C_KW + g * HEAD_DIM:C_KW + (g + 1) * HEAD_DIM],
                       win_bias, win_mask, sw_ref, wb, wb)
        sw_ref[...] = _sample_softmax(sw_ref)
        o_win = _sample_pv(sw_ref, _interleaved(wv_ref, A_KV, g),
                           z_ref[:, C_VW + g * HEAD_DIM:C_VW + (g + 1) * HEAD_DIM], wb, wb)
        if not waited:
            _gather_wait(_page_rows(pks_ref, page_rows), n_pages, kbuf, page_rows, sem.at[0])
            _gather_wait(_page_rows(pvs_ref, page_rows), n_pages, vbuf, page_rows, sem.at[1])
            waited = True

        def slc_bias(c0, w, cols=cols):
            return jnp.concatenate([bs_ref[c, :, pl.ds(c0, w)] for c in cols], axis=0)

        def slc_mask(c0, w, is_new, sel=sel):
            blk = (c0 + lax.broadcasted_iota(I32, (jn, w), 1)) // SEL_BLOCK
            onehot = jnp.where(lax.broadcasted_iota(I32, (jn, w), 0) == blk, 1.0, 0.0).astype(MXU_DT)
            chosen = jnp.concatenate([_mm(sel, onehot) > 0.5] * A_GROUP, axis=0)
            return chosen & new_mask if is_new else chosen

        _sample_scores(q, _interleaved(kbuf, A_KV, g),
                       z_ref[:, C_KS + g * HEAD_DIM:C_KS + (g + 1) * HEAD_DIM],
                       slc_bias, slc_mask, s_ref, past, ch)
        s_ref[...] = _sample_softmax(s_ref)
        o_slc = _sample_pv(s_ref, _interleaved(vbuf, A_KV, g),
                           z_ref[:, C_VS + g * HEAD_DIM:C_VS + (g + 1) * HEAD_DIM], past, ch)
        for r in range(A_GROUP):
            h = g * A_GROUP + r
            c = T_GA + h * N_GATES
            rows = slice(r * nq, (r + 1) * nq)
            o_ref[:, h * HEAD_DIM:(h + 1) * HEAD_DIM] = (
                gates[:, c:c + 1] * o_cmp[rows] + gates[:, c + 1:c + 2] * o_slc[rows]
                + gates[:, c + 2:c + 3] * o_win[rows])


def nsa_sample(zs, k_cmp, v_cmp, pool_ks, pool_vs, win_k, win_v, bs, bc, page_table, ch=1024):
    bd, n_pages = page_table.shape
    nq = zs.shape[0] // bd
    past = n_pages * PAGE
    mc = k_cmp.shape[1]
    wrows = win_k.shape[0] // bd
    wb = wrows // A_KV
    im3 = lambda i, pt: (i, 0, 0)
    win_spec = pl.BlockSpec((wrows, HEAD_DIM), lambda i, pt: (i, 0))
    return pl.pallas_call(
        functools.partial(_nsa_sample_kernel, n_pages=n_pages, ch=ch),
        grid_spec=pltpu.PrefetchScalarGridSpec(
            num_scalar_prefetch=1, grid=(bd,),
            in_specs=[pl.BlockSpec((nq, zs.shape[1]), lambda i, pt: (i, 0)),
                      pl.BlockSpec((1, mc, 256), im3), pl.BlockSpec((1, mc, 256), im3),
                      pl.BlockSpec(memory_space=pl.ANY), pl.BlockSpec(memory_space=pl.ANY),
                      win_spec, win_spec,
                      pl.BlockSpec((A_HEADS,) + bs.shape[1:], lambda i, pt: (0, 0, 0)),
                      pl.BlockSpec(bc.shape, lambda i, pt: (0, 0, 0))],
            out_specs=pl.BlockSpec((nq, 1024), lambda i, pt: (i, 0)),
            scratch_shapes=[pltpu.VMEM((past * A_KV, HEAD_DIM), F32), pltpu.VMEM((past * A_KV, HEAD_DIM), F32),
                            pltpu.VMEM((A_GROUP * nq, past + 128), F32),
                            pltpu.VMEM((A_GROUP * nq, wb + 128), F32),
                            pltpu.SemaphoreType.DMA((2,))]),
        out_shape=jax.ShapeDtypeStruct((bd * nq, 1024), F32),
        compiler_params=_cparams("arbitrary"),
        name="nsa_sample",
    )(page_table, zs, k_cmp, v_cmp, pool_ks, pool_vs, win_k, win_v, bs, bc)


def _dsa_sample_kernel(pt_ref, z_ref, pk_ref, pv_ref, pi_ref, bs_ref, o_ref,
                       kbuf, vbuf, ibuf, s_ref, sc_ref, key_ref, sel_ref, sem, *, n_pages, ch, topk, nbits):
    b = pl.program_id(0)
    past = n_pages * PAGE
    nq = z_ref.shape[0]
    page_rows = PAGE * B_KV
    _gather_start(lambda pg: pi_ref.at[pg], pt_ref, b, n_pages, ibuf, PAGE, sem.at[2])
    _gather_start(_page_rows(pk_ref, page_rows), pt_ref, b, n_pages, kbuf, page_rows, sem.at[0])
    _gather_start(_page_rows(pv_ref, page_rows), pt_ref, b, n_pages, vbuf, page_rows, sem.at[1])
    qi = jnp.concatenate([z_ref[:, C_QI + h * IDX_DIM:C_QI + (h + 1) * IDX_DIM] for h in range(IDX_HEADS)],
                         axis=0).astype(MXU_DT)
    wi = z_ref[:, C_TAIL + T_WI:C_TAIL + T_WI + IDX_HEADS]
    _gather_wait(lambda pg: pi_ref.at[pg], n_pages, ibuf, PAGE, sem.at[2])
    zero = lambda c0, w: jnp.zeros((IDX_HEADS * nq, w), F32)
    true = lambda c0, w, is_new: jnp.full((IDX_HEADS * nq, w), True)
    _sample_scores(qi, lambda c0, w: ibuf[pl.ds(c0, w), :], z_ref[:, C_TAIL + T_KI:C_TAIL + T_KI + IDX_DIM],
                   zero, true, s_ref, past, ch, scale=1.0)
    rel = jnp.maximum(s_ref[...], 0.0)
    score = sum(rel[h * nq:(h + 1) * nq] * wi[:, h:h + 1] for h in range(IDX_HEADS))
    score = score * (IDX_DIM ** -0.5 * IDX_HEADS ** -0.5)
    new_j = lax.broadcasted_iota(I32, score.shape, 1) - past
    causal = (new_j < 0) | ((new_j <= lax.broadcasted_iota(I32, score.shape, 0)) & (new_j < nq))
    key_ref[...] = _sort_key(jnp.where(causal, score, NEG))
    _topk_madd(key_ref, sel_ref, lambda c0: causal, 1, score.shape[1], topk, nbits)

    _gather_wait(_page_rows(pk_ref, page_rows), n_pages, kbuf, page_rows, sem.at[0])
    _gather_wait(_page_rows(pv_ref, page_rows), n_pages, vbuf, page_rows, sem.at[1])
    for g in range(B_KV):
        cols = [g * B_GROUP + r for r in range(B_GROUP)]
        q = _stack_heads(z_ref, C_QB + g * B_GROUP * HEAD_DIM, B_GROUP).astype(MXU_DT)

        def bias(c0, w, cols=cols):
            return jnp.concatenate([bs_ref[c, :, pl.ds(c0, w)] for c in cols], axis=0)

        def mask(c0, w, is_new):
            return jnp.concatenate([sel_ref[:, pl.ds(c0, w)] > 0.5 * NEG] * B_GROUP, axis=0)

        _sample_scores(q, _interleaved(kbuf, B_KV, g), z_ref[:, C_KB + g * HEAD_DIM:C_KB + (g + 1) * HEAD_DIM],
                       bias, mask, sc_ref, past, ch)
        sc_ref[...] = _sample_softmax(sc_ref)
        o = _sample_pv(sc_ref, _interleaved(vbuf, B_KV, g),
                       z_ref[:, C_VB + g * HEAD_DIM:C_VB + (g + 1) * HEAD_DIM], past, ch)
        for r in range(B_GROUP):
            h = g * B_GROUP + r
            o_ref[:, h * HEAD_DIM:(h + 1) * HEAD_DIM] = o[r * nq:(r + 1) * nq]


def dsa_sample(zs, pool_k, pool_v, pool_i, bs, page_table, ch=1024):
    bd, n_pages = page_table.shape
    nq = zs.shape[0] // bd
    past = n_pages * PAGE
    lp = past + 128
    topk = min(DSA_TOPK, (past + nq) // 4)
    return pl.pallas_call(
        functools.partial(_dsa_sample_kernel, n_pages=n_pages, ch=ch, topk=topk, nbits=int(lp).bit_length()),
        grid_spec=pltpu.PrefetchScalarGridSpec(
            num_scalar_prefetch=1, grid=(bd,),
            in_specs=[pl.BlockSpec((nq, zs.shape[1]), lambda i, pt: (i, 0)),
                      pl.BlockSpec(memory_space=pl.ANY), pl.BlockSpec(memory_space=pl.ANY),
                      pl.BlockSpec(memory_space=pl.ANY),
                      pl.BlockSpec((B_HEADS,) + bs.shape[1:], lambda i, pt: (1, 0, 0))],
            out_specs=pl.BlockSpec((nq, 1024), lambda i, pt: (i, 0)),
            scratch_shapes=[pltpu.VMEM((past * B_KV, HEAD_DIM), F32), pltpu.VMEM((past * B_KV, HEAD_DIM), F32),
                            pltpu.VMEM((past, IDX_DIM), F32),
                            pltpu.VMEM((IDX_HEADS * nq, lp), F32), pltpu.VMEM((B_GROUP * nq, lp), F32),
                            pltpu.VMEM((nq, lp), I32), pltpu.VMEM((nq, lp), F32),
                            pltpu.SemaphoreType.DMA((3,))]),
        out_shape=jax.ShapeDtypeStruct((bd * nq, 1024), F32),
        compiler_params=_cparams("arbitrary"),
        name="dsa_sample",
    )(page_table, zs, pool_k, pool_v, pool_i, bs)


def _diff_sample_kernel(pt_ref, q_ref, kn_ref, vn_ref, pk_ref, pv_ref, bs_ref, lam_ref, hn_ref, o_ref,
                        kbuf, vbuf, sem, *, n_pages, cp):
    b = pl.program_id(0)
    nb = pl.num_programs(0)
    nq = q_ref.shape[0]
    pieces = C_KV * 2
    page_rows = PAGE * pieces
    slot_rows = cp * page_rows
    ch = cp * PAGE
    n_ch = n_pages // cp
    past = n_pages * PAGE
    rows = C_GROUP * nq

    def copies(bb, c, slot):
        out = []
        for i in range(cp):
            pg = pt_ref[bb, c * cp + i]
            dst = pl.ds(pl.multiple_of(slot * slot_rows + i * page_rows, page_rows), page_rows)
            out.append(pltpu.make_async_copy(_page_rows(pk_ref, page_rows)(pg), kbuf.at[dst], sem.at[0, slot]))
            out.append(pltpu.make_async_copy(_page_rows(pv_ref, page_rows)(pg), vbuf.at[dst], sem.at[1, slot]))
        return out

    @pl.when(b == 0)
    def _():
        for cpy in copies(0, 0, 0):
            cpy.start()

    qs = [jnp.concatenate([q_ref[:, ((g * C_GROUP + r) * 2 + m) * HEAD_DIM:((g * C_GROUP + r) * 2 + m + 1) * HEAD_DIM]
                           for r in range(C_GROUP)], axis=0).astype(MXU_DT)
          for g in range(C_KV) for m in range(2)]

    def update(carry, k_fn, v_fn, c0, w, mask):
        m_all, l_all, acc_all = carry
        new_m, new_l, new_acc = [], [], []
        for g in range(C_KV):
            ps, alphas = [], []
            for m in range(2):
                gm = g * 2 + m
                rs = slice(gm * rows, (gm + 1) * rows)
                bias = jnp.concatenate([bs_ref[m * C_HEADS + g * C_GROUP + r, :, pl.ds(c0, w)]
                                        for r in range(C_GROUP)], axis=0)
                s = _nt(qs[gm], k_fn(g, m).astype(MXU_DT)) * SCALE + bias
                if mask is not None:
                    s = jnp.where(mask, s, NEG)
                mn = jnp.maximum(m_all[rs], jnp.max(s, axis=-1, keepdims=True))
                p = jnp.exp(s - mn)
                if mask is not None:
                    p = jnp.where(mask, p, 0.0)
                a = jnp.exp(m_all[rs] - mn)
                new_m.append(mn)
                new_l.append(a * l_all[rs] + jnp.sum(p, axis=-1, keepdims=True))
                ps.append(p)
                alphas.append(a)
            pst = jnp.concatenate(ps, axis=0).astype(MXU_DT)
            pv = jnp.concatenate([_mm(pst, v_fn(g, h).astype(MXU_DT)) for h in range(2)], axis=1)
            for m in range(2):
                rs = slice((g * 2 + m) * rows, (g * 2 + m + 1) * rows)
                new_acc.append(alphas[m] * acc_all[rs] + pv[m * rows:(m + 1) * rows])
        return (jnp.concatenate(new_m, axis=0), jnp.concatenate(new_l, axis=0),
                jnp.concatenate(new_acc, axis=0))

    def chunk(c, carry):
        slot = c % 2
        for cpy in copies(b, c, slot):
            cpy.wait()

        @pl.when(c + 1 < n_ch)
        def _():
            for cpy in copies(b, c + 1, 1 - slot):
                cpy.start()

        @pl.when((c + 1 == n_ch) & (b + 1 < nb))
        def _():
            for cpy in copies(b + 1, 0, 1 - slot):
                cpy.start()

        base = slot * slot_rows
        return update(carry,
                      lambda g, m: kbuf[pl.ds(base + g * 2 + m, ch, stride=pieces), :],
                      lambda g, h: vbuf[pl.ds(base + h * C_KV + g, ch, stride=pieces), :],
                      pl.multiple_of(c * ch, ch), ch, None)

    n_rows = pieces * rows
    carry = (jnp.full((n_rows, 1), NEG, F32), jnp.zeros((n_rows, 1), F32), jnp.zeros((n_rows, C_VDIM), F32))
    carry = lax.fori_loop(0, n_ch, chunk, carry)
    _, l_all, acc_all = update(
        carry,
        lambda g, m: _pad_rows(kn_ref[:, (g * 2 + m) * HEAD_DIM:(g * 2 + m + 1) * HEAD_DIM], 128),
        lambda g, h: _pad_rows(vn_ref[:, g * C_VDIM + h * HEAD_DIM:g * C_VDIM + (h + 1) * HEAD_DIM], 128),
        past, 128, _new_key_mask(nq, C_GROUP))
    o_all = acc_all / l_all
    lam = _diff_lambda(lam_ref)
    for g in range(C_KV):
        r0 = g * 2 * rows
        o = _diff_finish(o_all[r0:r0 + rows] - lam * o_all[r0 + rows:r0 + 2 * rows], hn_ref)
        for r in range(C_GROUP):
            col = (g * C_GROUP + r) * C_VDIM
            o_ref[:, col:col + C_VDIM] = o[r * nq:(r + 1) * nq]


def diff_sample(z1s, pool_k, pool_v, bs, lam_vecs, head_norm, page_table, cp=8):
    bd, n_pages = page_table.shape
    nq = z1s.shape[0] // bd
    assert n_pages % (2 * cp) == 0
    slot_rows = cp * PAGE * C_KV * 2
    q_cols = C_HEADS * 2 * HEAD_DIM
    kv_cols = C_KV * C_VDIM
    return pl.pallas_call(
        functools.partial(_diff_sample_kernel, n_pages=n_pages, cp=cp),
        grid_spec=pltpu.PrefetchScalarGridSpec(
            num_scalar_prefetch=1, grid=(bd,),
            in_specs=[pl.BlockSpec((nq, q_cols), lambda i, pt: (i, 0)),
                      pl.BlockSpec((nq, kv_cols), lambda i, pt: (i, q_cols // kv_cols)),
                      pl.BlockSpec((nq, kv_cols), lambda i, pt: (i, q_cols // kv_cols + 1)),
                      pl.BlockSpec(memory_space=pl.ANY), pl.BlockSpec(memory_space=pl.ANY),
                      pl.BlockSpec(bs.shape, lambda i, pt: (0, 0, 0)),
                      pl.BlockSpec((4, HEAD_DIM), lambda i, pt: (0, 0)),
                      pl.BlockSpec((1, C_VDIM), lambda i, pt: (0, 0))],
            out_specs=pl.BlockSpec((nq, C_HEADS * C_VDIM), lambda i, pt: (i, 0)),
            scratch_shapes=[pltpu.VMEM((2 * slot_rows, HEAD_DIM), F32), pltpu.VMEM((2 * slot_rows, HEAD_DIM), F32),
                            pltpu.SemaphoreType.DMA((2, 2))]),
        out_shape=jax.ShapeDtypeStruct((bd * nq, C_HEADS * C_VDIM), F32),
        compiler_params=_cparams("arbitrary"),
        name="diff_sample",
    )(page_table, z1s, z1s, z1s, pool_k, pool_v, bs, lam_vecs, head_norm.reshape(1, C_VDIM))


def _row_tile(rows, cap=1024):
    tm = min(rows, cap)
    assert rows % tm == 0
    return tm


def _reorder_l0_weight(w):
    sizes = (A_HEADS * HEAD_DIM,) + (A_KV * HEAD_DIM,) * 6 + (
        N_GATES * A_HEADS, B_HEADS * HEAD_DIM, B_KV * HEAD_DIM, B_KV * HEAD_DIM,
        IDX_HEADS * IDX_DIM, IDX_DIM, IDX_HEADS)
    offs = [0]
    for s in sizes:
        offs.append(offs[-1] + s)
    piece = lambda i, j=None: w[:, offs[i]:offs[(i if j is None else j) + 1]]
    qa, six, ga, qb, kvb, qi, ki, wi = piece(0), piece(1, 6), piece(7), piece(8), piece(9, 10), piece(11), \
        piece(12), piece(13)
    pad = jnp.zeros((w.shape[0], L0_COLS - offs[-1]), w.dtype)
    return jnp.concatenate([qa, qb, six, kvb, qi, ki, ga, wi, pad], axis=1).astype(MXU_DT)


def _compress_weights(pe, w1, w2):
    half = CMP_STRIDE * HEAD_DIM
    w1 = w1.reshape(2, half, CMP_HIDDEN).astype(MXU_DT)
    pe_rows = jnp.zeros((16, half), F32).at[0:2].set(pe.reshape(2, half))
    return w1[0], w1[1], w2.astype(MXU_DT), pe_rows


def kernel(x_prompt, x_sample, cache_l0_nsa_cmp_k, cache_l0_nsa_cmp_v, cache_l0_nsa_slc_k, cache_l0_nsa_slc_v, state_l0_nsa_win_k, state_l0_nsa_win_v, cache_l0_dsa_k, cache_l0_dsa_v, cache_l0_dsa_idx_k, cache_l1_diff_k, cache_l1_diff_v, page_table, rel_bias, attn_norm, mlp_norm, mlp_w1, mlp_w2, l0_w_in, l0_w_out, l0_cmp_pe_k, l0_cmp_w1_k, l0_cmp_w2_k, l0_cmp_pe_v, l0_cmp_w1_v, l0_cmp_w2_v, l1_w_in, l1_w_out, l1_lambda_q1, l1_lambda_k1, l1_lambda_q2, l1_lambda_k2, l1_head_norm, final_norm):
    n, t_len, d = x_prompt.shape
    bd, nq, _ = x_sample.shape
    n_pool = cache_l0_nsa_cmp_k.shape[0]
    n_pages = page_table.shape[1]
    past = n_pages * PAGE
    lp = past + 128
    kv_w = A_KV * HEAD_DIM
    assert t_len % CHUNK == 0 and t_len >= WINDOW + QB and nq <= 8
    assert state_l0_nsa_win_k.shape[1] == min(WINDOW, past)

    xp = x_prompt.reshape(n * t_len, d)
    xs = x_sample.reshape(bd * nq, d)
    tmp, tms = _row_tile(xp.shape[0]), _row_tile(xs.shape[0])
    w0 = _reorder_l0_weight(l0_w_in)
    cw = (_compress_weights(l0_cmp_pe_k, l0_cmp_w1_k, l0_cmp_w2_k)
          + _compress_weights(l0_cmp_pe_v, l0_cmp_w1_v, l0_cmp_w2_v))
    lam_vecs = jnp.stack([l1_lambda_q1, l1_lambda_k1, l1_lambda_q2, l1_lambda_k2])
    bf = lambda a: a.astype(MXU_DT)

    tp, bs = bias_tiles(rel_bias, past, nq, lp)
    bc_p, bc_s = bias_cmp(rel_bias, t_len, t_len // CMP_STRIDE, past, nq, past // CMP_STRIDE)

    zp = norm_proj(xp, attn_norm[0], w0, tmp, 768)
    zs = norm_proj(xs, attn_norm[0], w0, tms, 768)
    cut = lambda z, c, w: z[:, c:c + w]
    p_rows = {name: cut(zp, c, kv_w) for name, c in
              (("kc", C_KC), ("vc", C_VC), ("ks", C_KS), ("vs", C_VS), ("kw", C_KW), ("vw", C_VW),
               ("kb", C_KB), ("vb", C_VB))}
    s_rows = {name: cut(zs, c, kv_w) for name, c in
              (("kc", C_KC), ("vc", C_VC), ("ks", C_KS), ("vs", C_VS), ("kw", C_KW), ("vw", C_VW),
               ("kb", C_KB), ("vb", C_VB))}
    chunk_w = CMP_STRIDE * kv_w
    kc_p, vc_p = compress_prompt(p_rows["kc"].reshape(n, t_len // CMP_STRIDE, chunk_w),
                                 p_rows["vc"].reshape(n, t_len // CMP_STRIDE, chunk_w), cw)
    lanes = lambda a: a.reshape(-1, HEAD_DIM)
    kc_s, vc_s = compress_sample(lanes(cache_l0_nsa_cmp_k), lanes(cache_l0_nsa_cmp_v), page_table, cw)
    oa_p = nsa_prompt(zp, kc_p, vc_p, tp, bc_p, rel_bias, n, t_len)
    ob_p = dsa_prompt(zp, tp, rel_bias, n, t_len)
    wb = state_l0_nsa_win_k.shape[1]
    oa_s = nsa_sample(zs, kc_s, vc_s, lanes(cache_l0_nsa_slc_k), lanes(cache_l0_nsa_slc_v),
                      lanes(state_l0_nsa_win_k), lanes(state_l0_nsa_win_v), bs, bc_s, page_table)
    ob_s = dsa_sample(zs, lanes(cache_l0_dsa_k), lanes(cache_l0_dsa_v), cache_l0_dsa_idx_k, bs, page_table)
    w_out0 = bf(l0_w_out)
    w1_0, w2_0 = bf(mlp_w1[0]), bf(mlp_w2[0])
    xp = out_proj(xp, [oa_p, ob_p], w_out0, tmp, 1024)
    xs = out_proj(xs, [oa_s, ob_s], w_out0, tms, 1024)
    xp = mlp(xp, mlp_norm[0], w1_0, w2_0, final_norm, tmp, 512, False)
    xs = mlp(xs, mlp_norm[0], w1_0, w2_0, final_norm, tms, 512, False)

    w_in1 = bf(l1_w_in)
    z1p = norm_proj(xp, attn_norm[1], w_in1, tmp, 1024)
    z1s = norm_proj(xs, attn_norm[1], w_in1, tms, 1024)
    o1_p = diff_prompt(z1p, tp, rel_bias, lam_vecs, l1_head_norm, n, t_len)
    v_halves = cache_l1_diff_v.reshape(n_pool, PAGE, C_KV, 2, HEAD_DIM).transpose(0, 1, 3, 2, 4)
    o1_s = diff_sample(z1s, lanes(cache_l1_diff_k), lanes(v_halves), bs, lam_vecs, l1_head_norm, page_table)
    w_out1 = bf(l1_w_out)
    w1_1, w2_1 = bf(mlp_w1[1]), bf(mlp_w2[1])
    xp = out_proj(xp, [o1_p], w_out1, tmp, 1024)
    xs = out_proj(xs, [o1_s], w_out1, tms, 1024)
    y_prompt = mlp(xp, mlp_norm[1], w1_1, w2_1, final_norm, tmp, 512, True).reshape(n, t_len, d)
    y_sample = mlp(xs, mlp_norm[1], w1_1, w2_1, final_norm, tms, 512, True).reshape(bd, nq, d)

    row4 = lambda a, b: a.reshape(b, -1, A_KV, HEAD_DIM)
    win = min(WINDOW, t_len)
    outs = [y_prompt, y_sample]
    for name in ("kc", "vc", "ks", "vs"):
        outs += [row4(p_rows[name], n), row4(s_rows[name], bd)]
    for name, state in (("kw", state_l0_nsa_win_k), ("vw", state_l0_nsa_win_v)):
        outs += [row4(p_rows[name], n)[:, t_len - win:],
                 jnp.concatenate([state, row4(s_rows[name], bd)], axis=1)[:, -wb:]]
    for name in ("kb", "vb"):
        outs += [row4(p_rows[name], n), row4(s_rows[name], bd)]
    outs += [cut(zp, C_TAIL + T_KI, IDX_DIM).reshape(n, t_len, IDX_DIM),
             cut(zs, C_TAIL + T_KI, IDX_DIM).reshape(bd, nq, IDX_DIM)]
    k_cols, v_cols = C_KV * 2 * HEAD_DIM, C_KV * C_VDIM
    q_cols = C_HEADS * 2 * HEAD_DIM
    outs += [cut(z1p, q_cols, k_cols).reshape(n, t_len, C_KV, 2, HEAD_DIM),
             cut(z1s, q_cols, k_cols).reshape(bd, nq, C_KV, 2, HEAD_DIM),
             cut(z1p, q_cols + k_cols, v_cols).reshape(n, t_len, C_KV, C_VDIM),
             cut(z1s, q_cols + k_cols, v_cols).reshape(bd, nq, C_KV, C_VDIM)]
    return tuple(outs)
```

```python
import functools
import math

import jax
import jax.numpy as jnp
from jax import lax
from jax.experimental import pallas as pl
from jax.experimental.pallas import tpu as pltpu

F32 = jnp.float32
I32 = jnp.int32
MXU_DT = jnp.bfloat16

HEAD_DIM = 128
A_HEADS, A_KV, A_GROUP = 8, 2, 4
B_HEADS, B_KV, B_GROUP = 8, 2, 4
C_HEADS, C_KV, C_GROUP, C_VDIM = 8, 4, 2, 256
CMP_STRIDE, CMP_BLOCK, CMP_HIDDEN = 16, 32, 256
SEL_BLOCK, N_SEL_BLOCKS, WINDOW, N_GATES = 64, 16, 512, 3
IDX_HEADS, IDX_DIM, DSA_TOPK = 4, 64, 256
NUM_BUCKETS, MAX_DISTANCE = 32, 128
LAMBDA_INIT = 0.8 - 0.6 * math.exp(-0.3 * 1)
RMS_EPS = 1e-6
NEG = -1e30
SCALE = HEAD_DIM ** -0.5
QB = 128
CHUNK = 512
PAGE = 128
CHUNK_PITCH = CMP_STRIDE * A_KV + 8
assert QB >= MAX_DISTANCE and WINDOW % QB == 0 and WINDOW >= 2 * QB

C_QA, C_QB, C_KC, C_VC, C_KS, C_VS, C_KW, C_VW, C_KB, C_VB, C_QI, C_TAIL = (
    0, 1024, 2048, 2304, 2560, 2816, 3072, 3328, 3584, 3840, 4096, 4352)
T_KI, T_GA, T_WI = 0, 64, 88
L0_COLS = 4608
VMEM_LIMIT = 56 * 1024 * 1024


def _cparams(*sem):
    return pltpu.CompilerParams(dimension_semantics=sem, vmem_limit_bytes=VMEM_LIMIT)


def _nt(a, b):
    return lax.dot_general(a, b, (((1,), (1,)), ((), ())), preferred_element_type=F32)


def _mm(a, b):
    return jnp.dot(a, b, preferred_element_type=F32)


def _rms(x, g):
    return x * lax.rsqrt(jnp.mean(x * x, axis=-1, keepdims=True) + RMS_EPS) * g


def _norm_proj_kernel(x_ref, g_ref, w_ref, o_ref, xn_ref):
    @pl.when(pl.program_id(1) == 0)
    def _():
        xn_ref[...] = _rms(x_ref[...], g_ref[...]).astype(xn_ref.dtype)

    o_ref[...] = _mm(xn_ref[...], w_ref[...])


def norm_proj(x, gain, w, tm, tn):
    rows, d = x.shape
    n = w.shape[1]
    return pl.pallas_call(
        _norm_proj_kernel,
        grid=(rows // tm, n // tn),
        in_specs=[pl.BlockSpec((tm, d), lambda i, j: (i, 0)),
                  pl.BlockSpec((1, d), lambda i, j: (0, 0)),
                  pl.BlockSpec((d, tn), lambda i, j: (0, j))],
        out_specs=pl.BlockSpec((tm, tn), lambda i, j: (i, j)),
        out_shape=jax.ShapeDtypeStruct((rows, n), F32),
        scratch_shapes=[pltpu.VMEM((tm, d), MXU_DT)],
        compiler_params=_cparams("parallel", "arbitrary"),
        name="norm_proj",
    )(x, gain.reshape(1, d), w)


def _out_proj_kernel(*refs, n_in):
    x_ref, o_refs, w_refs, y_ref = refs[0], refs[1:1 + n_in], refs[1 + n_in:1 + 2 * n_in], refs[-1]
    acc = x_ref[...]
    for o_ref, w_ref in zip(o_refs, w_refs):
        acc = acc + _mm(o_ref[...].astype(MXU_DT), w_ref[...])
    y_ref[...] = acc


def out_proj(x, outs, w, tm, tn):
    rows, d = x.shape
    o_specs, w_specs, row0 = [], [], 0
    for o in outs:
        k = o.shape[1]
        o_specs.append(pl.BlockSpec((tm, k), lambda i, j: (i, 0)))
        w_specs.append(pl.BlockSpec((k, tn), lambda i, j, rb=row0 // k: (rb, j)))
        row0 += k
    return pl.pallas_call(
        functools.partial(_out_proj_kernel, n_in=len(outs)),
        grid=(rows // tm, d // tn),
        in_specs=[pl.BlockSpec((tm, tn), lambda i, j: (i, j))] + o_specs + w_specs,
        out_specs=pl.BlockSpec((tm, tn), lambda i, j: (i, j)),
        out_shape=jax.ShapeDtypeStruct((rows, d), F32),
        compiler_params=_cparams("parallel", "arbitrary"),
        name="out_proj",
    )(x, *outs, *([w] * len(outs)))


def _mlp_kernel(x_ref, g_ref, w1_ref, w2_ref, gf_ref, y_ref, xn_ref, *, final_norm):
    j = pl.program_id(1)

    @pl.when(j == 0)
    def _():
        x = x_ref[...]
        xn_ref[...] = _rms(x, g_ref[...]).astype(xn_ref.dtype)
        y_ref[...] = x

    h = jnp.square(jnp.maximum(_mm(xn_ref[...], w1_ref[...]), 0.0))
    y_ref[...] += _mm(h.astype(w2_ref.dtype), w2_ref[...])

    if final_norm:
        @pl.when(j == pl.num_programs(1) - 1)
        def _():
            y_ref[...] = _rms(y_ref[...], gf_ref[...])


def mlp(x, gain, w1, w2, final_gain, tm, tf, final_norm):
    rows, d = x.shape
    ff = w1.shape[1]
    return pl.pallas_call(
        functools.partial(_mlp_kernel, final_norm=final_norm),
        grid=(rows // tm, ff // tf),
        in_specs=[pl.BlockSpec((tm, d), lambda i, j: (i, 0)),
                  pl.BlockSpec((1, d), lambda i, j: (0, 0)),
                  pl.BlockSpec((d, tf), lambda i, j: (0, j)),
                  pl.BlockSpec((tf, d), lambda i, j: (j, 0)),
                  pl.BlockSpec((1, d), lambda i, j: (0, 0))],
        out_specs=pl.BlockSpec((tm, d), lambda i, j: (i, 0)),
        out_shape=jax.ShapeDtypeStruct((rows, d), F32),
        scratch_shapes=[pltpu.VMEM((tm, d), MXU_DT)],
        compiler_params=_cparams("parallel", "arbitrary"),
        name="mlp",
    )(x, gain.reshape(1, d), w1, w2, final_gain.reshape(1, d))


def _bucket(dist):
    n = jnp.maximum(dist, 0)
    max_exact = NUM_BUCKETS // 2
    nf = jnp.maximum(n, 1).astype(F32)
    large = max_exact + (jnp.log(nf / max_exact) / math.log(MAX_DISTANCE / max_exact)
                         * (NUM_BUCKETS - max_exact)).astype(I32)
    large = jnp.minimum(large, NUM_BUCKETS - 1)
    return jnp.where(n < max_exact, n, large)


def _lookup(tbl_ref, col, buckets):
    def body(b, accs):
        v = tbl_ref[b, col]
        return tuple(jnp.where(bk == b, v, acc) for bk, acc in zip(buckets, accs))
    return lax.fori_loop(0, NUM_BUCKETS, body, tuple(jnp.zeros(bk.shape, F32) for bk in buckets))


def _bias_tiles_kernel(tbl_ref, tp_ref, bs_ref, *, q0):
    h = pl.program_id(0)
    t = lax.broadcasted_iota(I32, (QB, QB), 0)
    k = lax.broadcasted_iota(I32, (QB, QB), 1)
    ts = lax.broadcasted_iota(I32, bs_ref.shape[1:], 0)
    ks = lax.broadcasted_iota(I32, bs_ref.shape[1:], 1)
    d0, d1, ds = _lookup(tbl_ref, h, (_bucket(t - k), _bucket(QB + t - k), _bucket(q0 + ts - ks)))
    tp_ref[0, 0] = d0
    tp_ref[0, 1] = d1
    bs_ref[0] = ds


def bias_tiles(rel_bias, q0, n_q, lp):
    nh = rel_bias.shape[1]
    return pl.pallas_call(
        functools.partial(_bias_tiles_kernel, q0=q0),
        grid=(nh,),
        in_specs=[pl.BlockSpec(memory_space=pltpu.SMEM)],
        out_specs=[pl.BlockSpec((1, 2, QB, QB), lambda h: (h, 0, 0, 0)),
                   pl.BlockSpec((1, n_q, lp), lambda h: (h, 0, 0))],
        out_shape=[jax.ShapeDtypeStruct((nh, 2, QB, QB), F32),
                   jax.ShapeDtypeStruct((nh, n_q, lp), F32)],
        compiler_params=_cparams("arbitrary"),
        name="bias_tiles",
    )(rel_bias)


def _bias_cmp_kernel(tbl_ref, bp_ref, bs_ref, *, q0):
    h = pl.program_id(0)
    tp = lax.broadcasted_iota(I32, bp_ref.shape[1:], 0)
    cp = lax.broadcasted_iota(I32, bp_ref.shape[1:], 1)
    ts = lax.broadcasted_iota(I32, bs_ref.shape[1:], 0)
    cs = lax.broadcasted_iota(I32, bs_ref.shape[1:], 1)
    end = CMP_BLOCK - 1
    bp, bs = _lookup(tbl_ref, h, (_bucket(tp - (cp * CMP_STRIDE + end)),
                                  _bucket(q0 + ts - (cs * CMP_STRIDE + end))))
    bp_ref[0] = bp
    bs_ref[0] = bs


def bias_cmp(rel_bias, t_len, mc_p, q0, n_q, mc_s):
    return pl.pallas_call(
        functools.partial(_bias_cmp_kernel, q0=q0),
        grid=(A_HEADS,),
        in_specs=[pl.BlockSpec(memory_space=pltpu.SMEM)],
        out_specs=[pl.BlockSpec((1, t_len, mc_p), lambda h: (h, 0, 0)),
                   pl.BlockSpec((1, n_q, mc_s), lambda h: (h, 0, 0))],
        out_shape=[jax.ShapeDtypeStruct((A_HEADS, t_len, mc_p), F32),
                   jax.ShapeDtypeStruct((A_HEADS, n_q, mc_s), F32)],
        compiler_params=_cparams("arbitrary"),
        name="bias_cmp",
    )(rel_bias)


def _softmax_rows(z, mask):
    z = jnp.where(mask, z, NEG)
    m = jnp.max(z, axis=-1, keepdims=True)
    e = jnp.where(mask, jnp.exp(z - m), 0.0)
    l = jnp.sum(e, axis=-1, keepdims=True)
    return e / jnp.where(l > 0.0, l, 1.0)


def _gelu_tanh(x):
    return 0.5 * x * (1.0 + jnp.tanh(math.sqrt(2.0 / math.pi) * (x + 0.044715 * (x * x * x))))


def _compress(x_fn, m, w1a_ref, w1b_ref, w2_ref, pe_ref, o_ref):
    pe = pe_ref[...].astype(MXU_DT)
    pos = _mm(pe, w1a_ref[...])[0:1] + _mm(pe, w1b_ref[...])[1:2]
    last = lax.broadcasted_iota(I32, (m, 1), 0) == m - 1
    for g in range(A_KV):
        xg = jnp.concatenate([x_fn(j, g).astype(MXU_DT) for j in range(CMP_STRIDE)], axis=1)
        first = _mm(xg, w1a_ref[...])
        second = pltpu.roll(_mm(xg, w1b_ref[...]), m - 1, 0)
        hid = _gelu_tanh(first + second + pos)
        out = _mm(hid.astype(MXU_DT), w2_ref[...])
        o_ref[:, g * HEAD_DIM:(g + 1) * HEAD_DIM] = jnp.where(last, 0.0, out)


def _overlap(mc, jn, n_cmp, n_slc):
    c = lax.broadcasted_iota(I32, (mc, jn), 0)
    j = lax.broadcasted_iota(I32, (mc, jn), 1)
    ov = ((c * CMP_STRIDE < j * SEL_BLOCK + SEL_BLOCK) & (c * CMP_STRIDE + CMP_BLOCK > j * SEL_BLOCK)
          & (c < n_cmp) & (j < n_slc))
    return jnp.where(ov, 1.0, 0.0)


def _select_blocks(imp, pos, n_slc):
    jn = imp.shape[1]
    jidx = lax.broadcasted_iota(I32, (1, jn), 1)
    cur = pos // SEL_BLOCK
    forced = (jidx == 0) | (jidx == cur) | (jidx == cur - 1)
    future = jidx * SEL_BLOCK > pos
    score = jnp.where(future, -1.0, jnp.where(forced, 1e3, imp))
    score = jnp.where(jidx < n_slc, score, -2.0)

    def body(i, rank):
        col = jnp.sum(jnp.where(jidx == i, score, 0.0), axis=-1, keepdims=True)
        beats = jnp.where(col > score, 1.0, jnp.where(col == score, jnp.where(i < jidx, 1.0, 0.0), 0.0))
        return rank + beats

    rank = lax.fori_loop(0, n_slc, body, jnp.zeros(score.shape, F32), unroll=8)
    n_sel = min(N_SEL_BLOCKS, n_slc)
    return jnp.where((rank < n_sel) & (jidx < n_slc), 1.0, 0.0)


def _tree_sum(xs):
    xs = list(xs)
    while len(xs) > 1:
        xs = [xs[i] + xs[i + 1] for i in range(0, len(xs) - 1, 2)] + ([xs[-1]] if len(xs) % 2 else [])
    return xs[0]


def _sort_key(s):
    bits = lax.bitcast_convert_type(jnp.where(s == 0.0, 0.0, s), I32)
    return jnp.where(bits < 0, bits ^ jnp.int32(0x7FFFFFFF), bits)


def _topk_madd(key_ref, madd_ref, valid_fn, nch, cw, k, nbits):
    n_rows = key_ref.shape[0]
    kf = jnp.float32(k)

    def count(fn):
        def body(c, acc):
            c0 = pl.multiple_of(c * cw, cw)
            hit = jnp.where(fn(c0, key_ref[:, pl.ds(c0, cw)]), 1.0, 0.0)
            return acc + _tree_sum(hit[:, i:i + 128] for i in range(0, cw, 128))
        acc = lax.fori_loop(0, nch, body, jnp.zeros((n_rows, 128), F32))
        return jnp.sum(acc, axis=-1, keepdims=True)

    int_min = jnp.int32(-2 ** 31)
    thr0 = jnp.where(count(lambda c0, key: key >= 0) >= kf, jnp.int32(0), int_min)

    def vbody(i, thr):
        cand = thr | lax.shift_left(jnp.int32(1), 30 - i)
        return jnp.where(count(lambda c0, key: key >= cand) >= kf, cand, thr)

    thr = lax.fori_loop(0, 31, vbody, thr0)
    need = kf - count(lambda c0, key: key > thr)

    def idx(c0):
        return c0 + lax.broadcasted_iota(I32, (1, cw), 1)

    def ibody(i, cut):
        cand = cut | lax.shift_left(jnp.int32(1), nbits - 1 - i)
        return jnp.where(count(lambda c0, key: (key == thr) & (idx(c0) < cand)) <= need, cand, cut)

    tied = jnp.max(count(lambda c0, key: key == thr) - need) > 0.0
    cut = lax.cond(tied, lambda: lax.fori_loop(0, nbits, ibody, jnp.zeros((n_rows, 1), I32)),
                   lambda: jnp.full((n_rows, 1), 2 ** nbits - 1, I32))

    def write(c, _):
        c0 = pl.multiple_of(c * cw, cw)
        key = key_ref[:, pl.ds(c0, cw)]
        sel = ((key > thr) | ((key == thr) & (idx(c0) < cut))) & valid_fn(c0)
        madd_ref[:, pl.ds(c0, cw)] = jnp.where(sel, 0.0, NEG)
        return 0

    lax.fori_loop(0, nch, write, 0)


def _topk_madd_t(key_ref, madd_ref, valid_fn, nch, cw, k, nbits):
    n_rows = key_ref.shape[1]
    kf = jnp.float32(k)

    def count(fn):
        def body(c, acc):
            c0 = pl.multiple_of(c * cw, cw)
            hit = jnp.where(fn(c0, key_ref[pl.ds(c0, cw), :]), 1.0, 0.0)
            return acc + _tree_sum(hit[i:i + 8] for i in range(0, cw, 8))
        acc = lax.fori_loop(0, nch, body, jnp.zeros((8, n_rows), F32))
        return jnp.sum(acc, axis=0, keepdims=True)

    int_min = jnp.int32(-2 ** 31)
    thr0 = jnp.where(count(lambda c0, key: key >= 0) >= kf, jnp.int32(0), int_min)

    def vbody(i, thr):
        cand = thr | lax.shift_left(jnp.int32(1), 30 - i)
        return jnp.where(count(lambda c0, key: key >= cand) >= kf, cand, thr)

    thr = lax.fori_loop(0, 31, vbody, thr0)
    need = kf - count(lambda c0, key: key > thr)

    def idx(c0):
        return c0 + lax.broadcasted_iota(I32, (cw, 1), 0)

    def ibody(i, cut):
        cand = cut | lax.shift_left(jnp.int32(1), nbits - 1 - i)
        return jnp.where(count(lambda c0, key: (key == thr) & (idx(c0) < cand)) <= need, cand, cut)

    tied = jnp.max(count(lambda c0, key: key == thr) - need) > 0.0
    cut = lax.cond(tied, lambda: lax.fori_loop(0, nbits, ibody, jnp.zeros((1, n_rows), I32)),
                   lambda: jnp.full((1, n_rows), 2 ** nbits - 1, I32))

    def write(c, _):
        c0 = pl.multiple_of(c * cw, cw)
        key = key_ref[pl.ds(c0, cw), :]
        sel = ((key > thr) | ((key == thr) & (idx(c0) < cut))) & valid_fn(c0)
        madd_ref[:, pl.ds(c0, cw)] = jnp.where(sel, 0.0, NEG).T
        return 0

    lax.fori_loop(0, nch, write, 0)


def _causal_attn(streams, dv, qb):
    m_rows = streams[0][0].shape[0]
    per = CHUNK // QB
    nact = qb // per + 1

    def stage(st, k0, w, bias):
        q, k_fn, _, _, _, madd_fn, s_ref = st
        s = _nt(q, k_fn(k0, w)) * SCALE + bias
        madd = madd_fn(k0, w)
        s_ref[:, pl.ds(k0, w)] = s if madd is None else s + madd

    def far(c, _):
        for st in streams:
            stage(st, pl.multiple_of(c * CHUNK, CHUNK), CHUNK, st[3])
        return 0

    lax.fori_loop(0, nact, far, 0)
    for st in streams:
        stage(st, pl.multiple_of(qb * QB, QB), QB, st[4](True))

    @pl.when(qb >= 1)
    def _():
        for st in streams:
            stage(st, pl.multiple_of((qb - 1) * QB, QB), QB, st[4](False))

    for j in range(1, per):
        @pl.when(qb % per + j < per)
        def _():
            for st in streams:
                st[6][:, pl.ds(pl.multiple_of((qb + j) * QB, QB), QB)] = jnp.full((m_rows, QB), NEG, F32)

    def row_max(c, ms):
        k0 = pl.multiple_of(c * CHUNK, CHUNK)
        return tuple(jnp.maximum(m, jnp.max(st[6][:, pl.ds(k0, CHUNK)], axis=-1, keepdims=True))
                     for st, m in zip(streams, ms))

    ms = lax.fori_loop(0, nact, row_max, tuple(jnp.full((m_rows, 1), NEG, F32) for _ in streams))

    def pv(c, carry):
        k0 = pl.multiple_of(c * CHUNK, CHUNK)
        out = []
        for st, m, (l, acc) in zip(streams, ms, carry):
            p = jnp.exp(st[6][:, pl.ds(k0, CHUNK)] - m)
            out.append((l + jnp.sum(p, axis=-1, keepdims=True), acc + _mm(p.astype(MXU_DT), st[2](k0, CHUNK))))
        return tuple(out)

    init = tuple((jnp.zeros((m_rows, 1), F32), jnp.zeros((m_rows, dv), F32)) for _ in streams)
    return [acc / l for l, acc in lax.fori_loop(0, nact, pv, init)]


def _causal_add(rep):
    t = lax.broadcasted_iota(I32, (QB, QB), 0)
    k = lax.broadcasted_iota(I32, (QB, QB), 1)
    return jnp.concatenate([jnp.where(k <= t, 0.0, NEG)] * rep, axis=0)


def _far_bias(tbl_ref, cols):
    return jnp.concatenate([jnp.full((QB, 1), tbl_ref[NUM_BUCKETS - 1, c], F32) for c in cols], axis=0)


def _near_bias(tp_ref, cols, diag):
    return jnp.concatenate([tp_ref[c, 0 if diag else 1] for c in cols], axis=0)


def _stack_heads(ref, col0, n):
    return jnp.concatenate([ref[:, col0 + r * HEAD_DIM:col0 + (r + 1) * HEAD_DIM] for r in range(n)], axis=0)


def _compress_prompt_kernel(xk_ref, xv_ref, w1a, w1b, w2, pe, ok_ref, ov_ref):
    lanes = lambda x_ref: (lambda j, g: x_ref[0, :, (2 * j + g) * HEAD_DIM:(2 * j + g + 1) * HEAD_DIM])
    for i, (x_ref, o) in enumerate(((xk_ref, ok_ref), (xv_ref, ov_ref))):
        _compress(lanes(x_ref), x_ref.shape[1], w1a.at[i], w1b.at[i], w2.at[i], pe.at[i], o.at[0])


def _cmp_weight_specs():
    full = lambda *shape: pl.BlockSpec(shape, lambda *_: (0,) * len(shape))
    half = CMP_STRIDE * HEAD_DIM
    return [full(2, half, CMP_HIDDEN), full(2, half, CMP_HIDDEN), full(2, CMP_HIDDEN, HEAD_DIM), full(2, 16, half)]


def compress_prompt(xk, xv, cw):
    n, m, w = xk.shape
    spec = pl.BlockSpec((1, m, w), lambda i: (i, 0, 0))
    ospec = pl.BlockSpec((1, m, A_KV * HEAD_DIM), lambda i: (i, 0, 0))
    osh = jax.ShapeDtypeStruct((n, m, A_KV * HEAD_DIM), F32)
    return pl.pallas_call(
        _compress_prompt_kernel,
        grid=(n,),
        in_specs=[spec, spec] + _cmp_weight_specs(),
        out_specs=[ospec, ospec],
        out_shape=[osh, osh],
        compiler_params=_cparams("parallel"),
        name="compress_prompt",
    )(xk, xv, *cw)


def _window_attn(q, kw_ref, vw_ref, gl, qb, pos4, tbl_ref, tp_ref, cols):
    n_tiles = WINDOW // QB + 1
    width = n_tiles * QB
    lo = jnp.maximum(qb - (n_tiles - 1), 0)
    w0 = pl.multiple_of(lo * QB, QB)
    tiles = []
    for j in range(n_tiles):
        rel = qb - (lo + j)
        tiles.append(jnp.concatenate(
            [jnp.where(rel == 0, tp_ref[c, 0], jnp.where(rel == 1, tp_ref[c, 1], tbl_ref[NUM_BUCKETS - 1, c]))
             for c in cols], axis=0))
    s = _nt(q, kw_ref[pl.ds(w0, width), gl].astype(MXU_DT)) * SCALE + jnp.concatenate(tiles, axis=1)
    dist = pos4 - (w0 + lax.broadcasted_iota(I32, (1, width), 1))
    p = _softmax_rows(s, (dist >= 0) & (dist < WINDOW))
    return _mm(p.astype(MXU_DT), vw_ref[pl.ds(w0, width), gl].astype(MXU_DT))


def _nsa_prompt_kernel(tbl_ref, q_ref, tail_ref, kc_ref, vc_ref, ks_ref, vs_ref, kw_ref, vw_ref,
                       tp_ref, bc_ref, o_ref, s_ref, madd_ref, *, t_len):
    qb = pl.program_id(1)
    mc = kc_ref.shape[1]
    n_cmp = (t_len - CMP_BLOCK) // CMP_STRIDE + 1
    n_slc = -(-t_len // SEL_BLOCK)
    pos = qb * QB + lax.broadcasted_iota(I32, (QB, 1), 0)
    pos4 = jnp.concatenate([pos] * A_GROUP, axis=0)
    cidx = lax.broadcasted_iota(I32, (1, mc), 1)
    gates = jax.nn.sigmoid(tail_ref[...])
    overlap = _overlap(mc, QB, n_cmp, n_slc)
    causal_add = _causal_add(A_GROUP)
    onehot = jnp.where(lax.broadcasted_iota(I32, (QB, t_len), 0)
                       == lax.broadcasted_iota(I32, (QB, t_len), 1) // SEL_BLOCK, 1.0, 0.0).astype(MXU_DT)

    o_cmp, o_win, streams = [], [], []
    for g in range(A_KV):
        cols = [g * A_GROUP + r for r in range(A_GROUP)]
        gl = slice(g * HEAD_DIM, (g + 1) * HEAD_DIM)
        q = _stack_heads(q_ref, g * A_GROUP * HEAD_DIM, A_GROUP).astype(MXU_DT)
        lc = (_nt(q, kc_ref[0, :, gl].astype(MXU_DT)) * SCALE
              + jnp.concatenate([bc_ref[c] for c in cols], axis=0))
        p_cmp = _softmax_rows(lc, (pos4 >= cidx * CMP_STRIDE + (CMP_BLOCK - 1)) & (cidx < n_cmp))
        o_cmp.append(_mm(p_cmp.astype(MXU_DT), vc_ref[0, :, gl].astype(MXU_DT)))
        p_sum = sum(p_cmp[r * QB:(r + 1) * QB] for r in range(A_GROUP))
        imp = jnp.dot(p_sum, overlap, preferred_element_type=F32, precision=lax.Precision.HIGHEST)
        sel = _select_blocks(imp, pos, n_slc).astype(MXU_DT)
        madd_ref[g] = jnp.where(_mm(sel, onehot) > 0.5, 0.0, NEG)
        o_win.append(_window_attn(q, kw_ref, vw_ref, gl, qb, pos4, tbl_ref, tp_ref, cols))
        streams.append((
            q, lambda k0, w, gl=gl: ks_ref[pl.ds(k0, w), gl].astype(MXU_DT),
            lambda k0, w, gl=gl: vs_ref[pl.ds(k0, w), gl].astype(MXU_DT), _far_bias(tbl_ref, cols),
            lambda diag, cols=cols: _near_bias(tp_ref, cols, diag) + (causal_add if diag else 0.0),
            lambda k0, w, g=g: jnp.concatenate([madd_ref[g, :, pl.ds(k0, w)]] * A_GROUP, axis=0),
            s_ref.at[g]))
    o_slc = _causal_attn(streams, HEAD_DIM, qb)

    for h in range(A_HEADS):
        g, r = divmod(h, A_GROUP)
        c = T_GA + h * N_GATES
        rows = slice(r * QB, (r + 1) * QB)
        o = (gates[:, c:c + 1] * o_cmp[g][rows] + gates[:, c + 1:c + 2] * o_slc[g][rows]
             + gates[:, c + 2:c + 3] * o_win[g][rows])
        o_ref[:, h * HEAD_DIM:(h + 1) * HEAD_DIM] = o.astype(o_ref.dtype)


def nsa_prompt(z, k_cmp, v_cmp, tp, bc, rel_bias, n, t_len):
    nb = t_len // QB
    mc = k_cmp.shape[1]
    kv = lambda c: pl.BlockSpec((t_len, 256), lambda i, j: (i, c // 256))
    cmp_spec = pl.BlockSpec((1, mc, 256), lambda i, j: (i, 0, 0))
    return pl.pallas_call(
        functools.partial(_nsa_prompt_kernel, t_len=t_len),
        grid=(n, nb),
        in_specs=[pl.BlockSpec(memory_space=pltpu.SMEM),
                  pl.BlockSpec((QB, 1024), lambda i, j: (i * nb + j, C_QA // 1024)),
                  pl.BlockSpec((QB, 128), lambda i, j: (i * nb + j, C_TAIL // 128)),
                  cmp_spec, cmp_spec, kv(C_KS), kv(C_VS), kv(C_KW), kv(C_VW),
                  pl.BlockSpec(tp.shape, lambda i, j: (0, 0, 0, 0)),
                  pl.BlockSpec((A_HEADS, QB, mc), lambda i, j: (0, j, 0))],
        out_specs=pl.BlockSpec((QB, 1024), lambda i, j: (i * nb + j, 0)),
        out_shape=jax.ShapeDtypeStruct((n * t_len, 1024), MXU_DT),
        scratch_shapes=[pltpu.VMEM((A_KV, A_GROUP * QB, t_len), F32), pltpu.VMEM((A_KV, QB, t_len), F32)],
        compiler_params=_cparams("parallel", "arbitrary"),
        name="nsa_prompt",
    )(rel_bias, z, z, k_cmp, v_cmp, z, z, z, z, tp, bc)


def _dsa_prompt_kernel(tbl_ref, q_ref, qi_ref, tailq_ref, tailk_ref, kb_ref, vb_ref, tp_ref, o_ref,
                       key_ref, madd_ref, s_ref, *, topk, nbits):
    qb = pl.program_id(1)
    nact = qb // (CHUNK // QB) + 1
    pos = qb * QB + lax.broadcasted_iota(I32, (1, QB), 1)
    wi_t = tailq_ref[...].T[T_WI:T_WI + IDX_HEADS]
    qis = [qi_ref[:, h * IDX_DIM:(h + 1) * IDX_DIM].astype(MXU_DT) for h in range(IDX_HEADS)]

    def causal(c0):
        return c0 + lax.broadcasted_iota(I32, (CHUNK, 1), 0) <= pos

    def index_chunk(c, _):
        c0 = pl.multiple_of(c * CHUNK, CHUNK)
        ki = tailk_ref[pl.ds(c0, CHUNK), T_KI:T_KI + IDX_DIM].astype(MXU_DT)
        score = sum(jnp.maximum(_nt(ki, qis[h]), 0.0) * wi_t[h:h + 1] for h in range(IDX_HEADS))
        score = score * (IDX_DIM ** -0.5 * IDX_HEADS ** -0.5)
        key_ref[pl.ds(c0, CHUNK), :] = _sort_key(jnp.where(causal(c0), score, NEG))
        return 0

    lax.fori_loop(0, nact, index_chunk, 0)
    _topk_madd_t(key_ref, madd_ref, causal, nact, CHUNK, topk, nbits)

    streams = []
    for g in range(B_KV):
        cols = [A_HEADS + g * B_GROUP + r for r in range(B_GROUP)]
        gl = slice(g * HEAD_DIM, (g + 1) * HEAD_DIM)
        streams.append((
            _stack_heads(q_ref, g * B_GROUP * HEAD_DIM, B_GROUP).astype(MXU_DT),
            lambda k0, w, gl=gl: kb_ref[pl.ds(k0, w), gl].astype(MXU_DT),
            lambda k0, w, gl=gl: vb_ref[pl.ds(k0, w), gl].astype(MXU_DT),
            _far_bias(tbl_ref, cols), lambda diag, cols=cols: _near_bias(tp_ref, cols, diag),
            lambda k0, w: jnp.concatenate([madd_ref[:, pl.ds(k0, w)]] * B_GROUP, axis=0), s_ref.at[g]))
    outs = _causal_attn(streams, HEAD_DIM, qb)
    for h in range(B_HEADS):
        g, r = divmod(h, B_GROUP)
        o_ref[:, h * HEAD_DIM:(h + 1) * HEAD_DIM] = outs[g][r * QB:(r + 1) * QB].astype(o_ref.dtype)


def dsa_prompt(z, tp, rel_bias, n, t_len):
    nb = t_len // QB
    topk = min(DSA_TOPK, t_len // 4)
    nbits = int(t_len).bit_length()
    return pl.pallas_call(
        functools.partial(_dsa_prompt_kernel, topk=topk, nbits=nbits),
        grid=(n, nb),
        in_specs=[pl.BlockSpec(memory_space=pltpu.SMEM),
                  pl.BlockSpec((QB, 1024), lambda i, j: (i * nb + j, C_QB // 1024)),
                  pl.BlockSpec((QB, 256), lambda i, j: (i * nb + j, C_QI // 256)),
                  pl.BlockSpec((QB, 128), lambda i, j: (i * nb + j, C_TAIL // 128)),
                  pl.BlockSpec((t_len, 128), lambda i, j: (i, C_TAIL // 128)),
                  pl.BlockSpec((t_len, 256), lambda i, j: (i, C_KB // 256)),
                  pl.BlockSpec((t_len, 256), lambda i, j: (i, C_VB // 256)),
                  pl.BlockSpec(tp.shape, lambda i, j: (0, 0, 0, 0))],
        out_specs=pl.BlockSpec((QB, 1024), lambda i, j: (i * nb + j, 0)),
        out_shape=jax.ShapeDtypeStruct((n * t_len, 1024), MXU_DT),
        scratch_shapes=[pltpu.VMEM((t_len, QB), I32), pltpu.VMEM((QB, t_len), F32),
                        pltpu.VMEM((B_KV, B_GROUP * QB, t_len), F32)],
        compiler_params=_cparams("parallel", "arbitrary"),
        name="dsa_prompt",
    )(rel_bias, z, z, z, z, z, z, tp)


def _diff_lambda(lam_ref):
    v = lam_ref[...]
    e1 = jnp.exp(jnp.sum(v[0:1] * v[1:2], axis=-1, keepdims=True))
    e2 = jnp.exp(jnp.sum(v[2:3] * v[3:4], axis=-1, keepdims=True))
    return e1 - e2 + LAMBDA_INIT


def _diff_finish(o, hn_ref):
    return _rms(o, hn_ref[...]) * (1.0 - LAMBDA_INIT)


def _diff_prompt_kernel(tbl_ref, q_ref, k_ref, v_ref, tp_ref, lam_ref, hn_ref, o_ref, s_ref):
    g = pl.program_id(1)
    qb = pl.program_id(2)
    causal_add = _causal_add(C_GROUP)
    streams = []
    for m in range(2):
        cols = [m * C_HEADS + g * C_GROUP + r for r in range(C_GROUP)]
        q = jnp.concatenate([q_ref[:, (r * 2 + m) * HEAD_DIM:(r * 2 + m + 1) * HEAD_DIM]
                             for r in range(C_GROUP)], axis=0).astype(MXU_DT)
        streams.append((
            q, lambda k0, w, m=m: k_ref[pl.ds(k0, w), m * HEAD_DIM:(m + 1) * HEAD_DIM].astype(MXU_DT),
            lambda k0, w: v_ref[pl.ds(k0, w), :].astype(MXU_DT), _far_bias(tbl_ref, cols),
            lambda diag, cols=cols: _near_bias(tp_ref, cols, diag) + (causal_add if diag else 0.0),
            lambda k0, w: None, s_ref.at[m]))
    outs = _causal_attn(streams, C_VDIM, qb)
    o = _diff_finish(outs[0] - _diff_lambda(lam_ref) * outs[1], hn_ref)
    for r in range(C_GROUP):
        o_ref[:, r * C_VDIM:(r + 1) * C_VDIM] = o[r * QB:(r + 1) * QB].astype(o_ref.dtype)


def diff_prompt(z1, tp, rel_bias, lam_vecs, head_norm, n, t_len):
    nb = t_len // QB
    return pl.pallas_call(
        _diff_prompt_kernel,
        grid=(n, C_KV, nb),
        in_specs=[pl.BlockSpec(memory_space=pltpu.SMEM),
                  pl.BlockSpec((QB, 512), lambda i, g, j: (i * nb + j, g)),
                  pl.BlockSpec((t_len, 256), lambda i, g, j: (i, 2048 // 256 + g)),
                  pl.BlockSpec((t_len, 256), lambda i, g, j: (i, 3072 // 256 + g)),
                  pl.BlockSpec(tp.shape, lambda i, g, j: (0, 0, 0, 0)),
                  pl.BlockSpec((4, HEAD_DIM), lambda i, g, j: (0, 0)),
                  pl.BlockSpec((1, C_VDIM), lambda i, g, j: (0, 0))],
        out_specs=pl.BlockSpec((QB, 512), lambda i, g, j: (i * nb + j, g)),
        out_shape=jax.ShapeDtypeStruct((n * t_len, C_HEADS * C_VDIM), MXU_DT),
        scratch_shapes=[pltpu.VMEM((2, C_GROUP * QB, t_len), F32)],
        compiler_params=_cparams("parallel", "parallel", "arbitrary"),
        name="diff_prompt",
    )(rel_bias, z1, z1, z1, tp, lam_vecs, head_norm.reshape(1, C_VDIM))


def _page_gather(pt_ref, n_pages, items, sem):
    def copy(i, pg, p, slot):
        pool_ref, buf_ref, rows = items[i]
        src = pool_ref.at[pl.ds(pl.multiple_of(pg * rows, rows), rows)]
        dst = buf_ref.at[pl.ds(pl.multiple_of((slot * n_pages + p) * rows, rows), rows)]
        return pltpu.make_async_copy(src, dst, sem.at[i, slot])

    def start(bb, slot):
        def body(p, _):
            for i in range(len(items)):
                copy(i, pt_ref[bb, p], p, slot).start()
            return 0
        lax.fori_loop(0, n_pages, body, 0)

    def wait(i, slot):
        def body(p, _):
            copy(i, 0, 0, slot).wait()
            return 0
        lax.fori_loop(0, n_pages, body, 0)

    return start, wait


def _prefetch(b, nb, start):
    slot = b % 2

    @pl.when(b == 0)
    def _():
        start(0, 0)

    @pl.when(b + 1 < nb)
    def _():
        start(b + 1, 1 - slot)

    return slot


def _pad_rows(x, rows):
    return jnp.concatenate([x, jnp.zeros((rows - x.shape[0], x.shape[1]), x.dtype)], axis=0)


def _page_rows(pool_ref, rows):
    return lambda pg: pool_ref.at[pl.ds(pl.multiple_of(pg * rows, rows), rows)]


def _interleaved(buf_ref, n, j, row0=0):
    return lambda c0, ch: buf_ref[pl.ds(row0 + c0 * n + j, ch, stride=n), :]


def _sample_scores(q, k_fn, knew, bias_fn, mask_fn, s_ref, past, ch, scale=SCALE):
    def body(c, _):
        c0 = pl.multiple_of(c * ch, ch)
        k = k_fn(c0, ch).astype(MXU_DT)
        s = _nt(q, k) * scale + bias_fn(c0, ch)
        s_ref[:, pl.ds(c0, ch)] = jnp.where(mask_fn(c0, ch, False), s, NEG)
        return 0

    lax.fori_loop(0, past // ch, body, 0)
    s = _nt(q, _pad_rows(knew, 128).astype(MXU_DT)) * scale + bias_fn(past, 128)
    s_ref[:, past:past + 128] = jnp.where(mask_fn(past, 128, True), s, NEG)


def _sample_softmax(s_ref):
    z = s_ref[...]
    m = jnp.max(z, axis=-1, keepdims=True)
    e = jnp.where(z > 0.5 * NEG, jnp.exp(z - m), 0.0)
    l = jnp.sum(e, axis=-1, keepdims=True)
    return e / jnp.where(l > 0.0, l, 1.0)


def _sample_pv(p_ref, v_fn, vnew, past, ch):
    def body(c, acc):
        c0 = pl.multiple_of(c * ch, ch)
        return acc + _mm(p_ref[:, pl.ds(c0, ch)].astype(MXU_DT), v_fn(c0, ch).astype(MXU_DT))

    acc = lax.fori_loop(0, past // ch, body, jnp.zeros((p_ref.shape[0], vnew.shape[1]), F32))
    return acc + _mm(p_ref[:, past:past + 128].astype(MXU_DT), _pad_rows(vnew, 128).astype(MXU_DT))


def _new_key_mask(nq, rep):
    t = lax.broadcasted_iota(I32, (nq, 128), 0)
    j = lax.broadcasted_iota(I32, (nq, 128), 1)
    return jnp.concatenate([(j <= t) & (j < nq)] * rep, axis=0)


def _compress_sample_kernel(pt_ref, pk_ref, pv_ref, w1a, w1b, w2, pe, o_ref, buf, sem, *, n_pages):
    step = pl.program_id(0)
    cpp = PAGE // CMP_STRIDE
    rows = CMP_STRIDE * A_KV
    m = n_pages * cpp

    def copy(pool_ref, pg, p, i, slot):
        src = pool_ref.at[pl.ds(pl.multiple_of((pg * cpp + i) * rows, rows), rows)]
        dst = buf.at[pl.ds(pl.multiple_of((slot * m + p * cpp + i) * CHUNK_PITCH, 8), rows)]
        return pltpu.make_async_copy(src, dst, sem.at[slot])

    def start(st, slot):
        for which, pool_ref in enumerate((pk_ref, pv_ref)):
            @pl.when(st % 2 == which)
            def _(pool_ref=pool_ref):
                def body(p, _):
                    for i in range(cpp):
                        copy(pool_ref, pt_ref[st // 2, p], p, i, slot).start()
                    return 0
                lax.fori_loop(0, n_pages, body, 0)

    slot = _prefetch(step, pl.num_programs(0), start)

    def wait(c, _):
        copy(pk_ref, 0, 0, 0, slot).wait()
        return 0

    lax.fori_loop(0, m, wait, 0)
    which = step % 2
    row0 = slot * m * CHUNK_PITCH
    _compress(lambda j, g: buf[pl.ds(row0 + 2 * j + g, m, stride=CHUNK_PITCH), :], m,
              w1a.at[which], w1b.at[which], w2.at[which], pe.at[which], o_ref.at[0, 0])


def compress_sample(pool_k, pool_v, page_table, cw):
    bd, n_pages = page_table.shape
    m = n_pages * (PAGE // CMP_STRIDE)
    return pl.pallas_call(
        functools.partial(_compress_sample_kernel, n_pages=n_pages),
        grid_spec=pltpu.PrefetchScalarGridSpec(
            num_scalar_prefetch=1, grid=(2 * bd,),
            in_specs=[pl.BlockSpec(memory_space=pl.ANY), pl.BlockSpec(memory_space=pl.ANY)] + _cmp_weight_specs(),
            out_specs=pl.BlockSpec((1, 1, m, A_KV * HEAD_DIM), lambda s, pt: (s // 2, s % 2, 0, 0)),
            scratch_shapes=[pltpu.VMEM((2 * m * CHUNK_PITCH, HEAD_DIM), F32), pltpu.SemaphoreType.DMA((2,))]),
        out_shape=jax.ShapeDtypeStruct((bd, 2, m, A_KV * HEAD_DIM), F32),
        compiler_params=_cparams("arbitrary"),
        name="compress_sample",
    )(page_table, pool_k, pool_v, *cw)


def _nsa_sample_kernel(pt_ref, z_ref, kc_ref, vc_ref, pks_ref, pvs_ref, wk_ref, wv_ref, bs_ref, bc_ref,
                       o_ref, kbuf, vbuf, s_ref, sw_ref, sem, *, n_pages, ch):
    past = n_pages * PAGE
    nq = z_ref.shape[0]
    mc = kc_ref.shape[2]
    t_len = past + nq
    n_cmp = (t_len - CMP_BLOCK) // CMP_STRIDE + 1
    n_slc = -(-t_len // SEL_BLOCK)
    jn = 128 * (-(-n_slc // 128))
    wb = wk_ref.shape[0] // A_KV
    start, wait = _page_gather(pt_ref, n_pages, ((pks_ref, kbuf, PAGE * A_KV), (pvs_ref, vbuf, PAGE * A_KV)), sem)
    slot = _prefetch(pl.program_id(0), pl.num_programs(0), start)
    row0 = slot * past * A_KV

    pos = past + lax.broadcasted_iota(I32, (nq, 1), 0)
    pos4 = jnp.concatenate([pos] * A_GROUP, axis=0)
    cidx = lax.broadcasted_iota(I32, (1, mc), 1)
    gates = jax.nn.sigmoid(z_ref[:, C_TAIL:C_TAIL + 128])
    overlap = _overlap(mc, jn, n_cmp, n_slc)
    new_mask = _new_key_mask(nq, A_GROUP)
    waited = False

    for g in range(A_KV):
        cols = [g * A_GROUP + r for r in range(A_GROUP)]
        q = _stack_heads(z_ref, C_QA + g * A_GROUP * HEAD_DIM, A_GROUP).astype(MXU_DT)
        gl = slice(g * HEAD_DIM, (g + 1) * HEAD_DIM)
        lc = (_nt(q, kc_ref[0, 0, :, gl].astype(MXU_DT)) * SCALE
              + jnp.concatenate([bc_ref[c] for c in cols], axis=0))
        p_cmp = _softmax_rows(lc, (pos4 >= cidx * CMP_STRIDE + (CMP_BLOCK - 1)) & (cidx < n_cmp))
        o_cmp = _mm(p_cmp.astype(MXU_DT), vc_ref[0, 0, :, gl].astype(MXU_DT))
        p_sum = sum(p_cmp[r * nq:(r + 1) * nq] for r in range(A_GROUP))
        imp = jnp.dot(p_sum, overlap, preferred_element_type=F32, precision=lax.Precision.HIGHEST)
        sel = _select_blocks(imp, pos, n_slc).astype(MXU_DT)
        def win_bias(c0, w, cols=cols):
            return jnp.concatenate([bs_ref[c, :, pl.ds(past - wb + c0, w)] for c in cols], axis=0)

        def win_mask(c0, w, is_new):
            dist = pos4 - (past - wb + c0 + lax.broadcasted_iota(I32, (1, w), 1))
            valid = (dist >= 0) & (dist < WINDOW)
            return valid & new_mask if is_new else valid

        _sample_scores(q, _interleaved(wk_ref, A_KV, g),
                       z_ref[:, C_KW + g * HEAD_DIM:C_KW + (g + 1) * HEAD_DIM],
                       win_bias, win_mask, sw_ref, wb, wb)
        sw_ref[...] = _sample_softmax(sw_ref)
        o_win = _sample_pv(sw_ref, _interleaved(wv_ref, A_KV, g),
                           z_ref[:, C_VW + g * HEAD_DIM:C_VW + (g + 1) * HEAD_DIM], wb, wb)
        if not waited:
            wait(0, slot)
            wait(1, slot)
            waited = True

        def slc_bias(c0, w, cols=cols):
            return jnp.concatenate([bs_ref[c, :, pl.ds(c0, w)] for c in cols], axis=0)

        def slc_mask(c0, w, is_new, sel=sel):
            blk = (c0 + lax.broadcasted_iota(I32, (jn, w), 1)) // SEL_BLOCK
            onehot = jnp.where(lax.broadcasted_iota(I32, (jn, w), 0) == blk, 1.0, 0.0).astype(MXU_DT)
            chosen = jnp.concatenate([_mm(sel, onehot) > 0.5] * A_GROUP, axis=0)
            return chosen & new_mask if is_new else chosen

        _sample_scores(q, _interleaved(kbuf, A_KV, g, row0),
                       z_ref[:, C_KS + g * HEAD_DIM:C_KS + (g + 1) * HEAD_DIM],
                       slc_bias, slc_mask, s_ref, past, ch)
        s_ref[...] = _sample_softmax(s_ref)
        o_slc = _sample_pv(s_ref, _interleaved(vbuf, A_KV, g, row0),
                           z_ref[:, C_VS + g * HEAD_DIM:C_VS + (g + 1) * HEAD_DIM], past, ch)
        for r in range(A_GROUP):
            h = g * A_GROUP + r
            c = T_GA + h * N_GATES
            rows = slice(r * nq, (r + 1) * nq)
            o_ref[:, h * HEAD_DIM:(h + 1) * HEAD_DIM] = (
                gates[:, c:c + 1] * o_cmp[rows] + gates[:, c + 1:c + 2] * o_slc[rows]
                + gates[:, c + 2:c + 3] * o_win[rows])


def nsa_sample(zs, kv_cmp, pool_ks, pool_vs, win_k, win_v, bs, bc, page_table, ch=1024):
    bd, n_pages = page_table.shape
    nq = zs.shape[0] // bd
    past = n_pages * PAGE
    mc = kv_cmp.shape[2]
    wrows = win_k.shape[0] // bd
    wb = wrows // A_KV
    win_spec = pl.BlockSpec((wrows, HEAD_DIM), lambda i, pt: (i, 0))
    buf = pltpu.VMEM((2 * past * A_KV, HEAD_DIM), F32)
    return pl.pallas_call(
        functools.partial(_nsa_sample_kernel, n_pages=n_pages, ch=ch),
        grid_spec=pltpu.PrefetchScalarGridSpec(
            num_scalar_prefetch=1, grid=(bd,),
            in_specs=[pl.BlockSpec((nq, zs.shape[1]), lambda i, pt: (i, 0)),
                      pl.BlockSpec((1, 1, mc, 256), lambda i, pt: (i, 0, 0, 0)),
                      pl.BlockSpec((1, 1, mc, 256), lambda i, pt: (i, 1, 0, 0)),
                      pl.BlockSpec(memory_space=pl.ANY), pl.BlockSpec(memory_space=pl.ANY),
                      win_spec, win_spec,
                      pl.BlockSpec((A_HEADS,) + bs.shape[1:], lambda i, pt: (0, 0, 0)),
                      pl.BlockSpec(bc.shape, lambda i, pt: (0, 0, 0))],
            out_specs=pl.BlockSpec((nq, 1024), lambda i, pt: (i, 0)),
            scratch_shapes=[buf, buf,
                            pltpu.VMEM((A_GROUP * nq, past + 128), F32),
                            pltpu.VMEM((A_GROUP * nq, wb + 128), F32),
                            pltpu.SemaphoreType.DMA((2, 2))]),
        out_shape=jax.ShapeDtypeStruct((bd * nq, 1024), F32),
        compiler_params=_cparams("arbitrary"),
        name="nsa_sample",
    )(page_table, zs, kv_cmp, kv_cmp, pool_ks, pool_vs, win_k, win_v, bs, bc)


def _dsa_sample_kernel(pt_ref, z_ref, pk_ref, pv_ref, pi_ref, bs_ref, o_ref,
                       kbuf, vbuf, ibuf, s_ref, sc_ref, key_ref, sel_ref, sem, *, n_pages, ch, topk, nbits):
    past = n_pages * PAGE
    nq = z_ref.shape[0]
    start, wait = _page_gather(pt_ref, n_pages, ((pk_ref, kbuf, PAGE * B_KV), (pv_ref, vbuf, PAGE * B_KV),
                                                 (pi_ref, ibuf, PAGE)), sem)
    slot = _prefetch(pl.program_id(0), pl.num_programs(0), start)
    row0 = slot * past * B_KV
    qi = jnp.concatenate([z_ref[:, C_QI + h * IDX_DIM:C_QI + (h + 1) * IDX_DIM] for h in range(IDX_HEADS)],
                         axis=0).astype(MXU_DT)
    wi = z_ref[:, C_TAIL + T_WI:C_TAIL + T_WI + IDX_HEADS]
    wait(2, slot)
    zero = lambda c0, w: jnp.zeros((IDX_HEADS * nq, w), F32)
    true = lambda c0, w, is_new: jnp.full((IDX_HEADS * nq, w), True)
    _sample_scores(qi, lambda c0, w: ibuf[pl.ds(slot * past + c0, w), :],
                   z_ref[:, C_TAIL + T_KI:C_TAIL + T_KI + IDX_DIM], zero, true, s_ref, past, ch, scale=1.0)
    rel = jnp.maximum(s_ref[...], 0.0)
    score = sum(rel[h * nq:(h + 1) * nq] * wi[:, h:h + 1] for h in range(IDX_HEADS))
    score = score * (IDX_DIM ** -0.5 * IDX_HEADS ** -0.5)
    new_j = lax.broadcasted_iota(I32, score.shape, 1) - past
    causal = (new_j < 0) | ((new_j <= lax.broadcasted_iota(I32, score.shape, 0)) & (new_j < nq))
    key_ref[...] = _sort_key(jnp.where(causal, score, NEG))
    _topk_madd(key_ref, sel_ref, lambda c0: causal, 1, score.shape[1], topk, nbits)

    wait(0, slot)
    wait(1, slot)
    for g in range(B_KV):
        cols = [g * B_GROUP + r for r in range(B_GROUP)]
        q = _stack_heads(z_ref, C_QB + g * B_GROUP * HEAD_DIM, B_GROUP).astype(MXU_DT)

        def bias(c0, w, cols=cols):
            return jnp.concatenate([bs_ref[c, :, pl.ds(c0, w)] for c in cols], axis=0)

        def mask(c0, w, is_new):
            return jnp.concatenate([sel_ref[:, pl.ds(c0, w)] > 0.5 * NEG] * B_GROUP, axis=0)

        _sample_scores(q, _interleaved(kbuf, B_KV, g, row0),
                       z_ref[:, C_KB + g * HEAD_DIM:C_KB + (g + 1) * HEAD_DIM], bias, mask, sc_ref, past, ch)
        sc_ref[...] = _sample_softmax(sc_ref)
        o = _sample_pv(sc_ref, _interleaved(vbuf, B_KV, g, row0),
                       z_ref[:, C_VB + g * HEAD_DIM:C_VB + (g + 1) * HEAD_DIM], past, ch)
        for r in range(B_GROUP):
            h = g * B_GROUP + r
            o_ref[:, h * HEAD_DIM:(h + 1) * HEAD_DIM] = o[r * nq:(r + 1) * nq]


def dsa_sample(zs, pool_k, pool_v, pool_i, bs, page_table, ch=1024):
    bd, n_pages = page_table.shape
    nq = zs.shape[0] // bd
    past = n_pages * PAGE
    lp = past + 128
    topk = min(DSA_TOPK, (past + nq) // 4)
    return pl.pallas_call(
        functools.partial(_dsa_sample_kernel, n_pages=n_pages, ch=ch, topk=topk, nbits=int(lp).bit_length()),
        grid_spec=pltpu.PrefetchScalarGridSpec(
            num_scalar_prefetch=1, grid=(bd,),
            in_specs=[pl.BlockSpec((nq, zs.shape[1]), lambda i, pt: (i, 0)),
                      pl.BlockSpec(memory_space=pl.ANY), pl.BlockSpec(memory_space=pl.ANY),
                      pl.BlockSpec(memory_space=pl.ANY),
                      pl.BlockSpec((B_HEADS,) + bs.shape[1:], lambda i, pt: (1, 0, 0))],
            out_specs=pl.BlockSpec((nq, 1024), lambda i, pt: (i, 0)),
            scratch_shapes=[pltpu.VMEM((2 * past * B_KV, HEAD_DIM), F32), pltpu.VMEM((2 * past * B_KV, HEAD_DIM), F32),
                            pltpu.VMEM((2 * past, IDX_DIM), F32),
                            pltpu.VMEM((IDX_HEADS * nq, lp), F32), pltpu.VMEM((B_GROUP * nq, lp), F32),
                            pltpu.VMEM((nq, lp), I32), pltpu.VMEM((nq, lp), F32),
                            pltpu.SemaphoreType.DMA((3, 2))]),
        out_shape=jax.ShapeDtypeStruct((bd * nq, 1024), F32),
        compiler_params=_cparams("arbitrary"),
        name="dsa_sample",
    )(page_table, zs, pool_k, pool_v, pool_i, bs)


def _diff_sample_kernel(pt_ref, q_ref, kn_ref, vn_ref, pk_ref, pv_ref, bs_ref, lam_ref, hn_ref, o_ref,
                        kbuf, vbuf, sem, *, n_pages, cp):
    b = pl.program_id(0)
    nb = pl.num_programs(0)
    nq = q_ref.shape[0]
    pieces = C_KV * 2
    page_rows = PAGE * pieces
    slot_rows = cp * page_rows
    ch = cp * PAGE
    n_ch = n_pages // cp
    past = n_pages * PAGE
    rows = C_GROUP * nq

    def copies(bb, c, slot):
        out = []
        for i in range(cp):
            pg = pt_ref[bb, c * cp + i]
            dst = pl.ds(pl.multiple_of(slot * slot_rows + i * page_rows, page_rows), page_rows)
            out.append(pltpu.make_async_copy(_page_rows(pk_ref, page_rows)(pg), kbuf.at[dst], sem.at[0, slot]))
            out.append(pltpu.make_async_copy(_page_rows(pv_ref, page_rows)(pg), vbuf.at[dst], sem.at[1, slot]))
        return out

    @pl.when(b == 0)
    def _():
        for cpy in copies(0, 0, 0):
            cpy.start()

    qs = [jnp.concatenate([q_ref[:, ((g * C_GROUP + r) * 2 + m) * HEAD_DIM:((g * C_GROUP + r) * 2 + m + 1) * HEAD_DIM]
                           for r in range(C_GROUP)], axis=0).astype(MXU_DT)
          for g in range(C_KV) for m in range(2)]

    def update(carry, k_fn, v_fn, c0, w, mask):
        m_all, l_all, acc_all = carry
        new_m, new_l, new_acc = [], [], []
        for g in range(C_KV):
            ps, alphas = [], []
            for m in range(2):
                gm = g * 2 + m
                rs = slice(gm * rows, (gm + 1) * rows)
                bias = jnp.concatenate([bs_ref[m * C_HEADS + g * C_GROUP + r, :, pl.ds(c0, w)]
                                        for r in range(C_GROUP)], axis=0)
                s = _nt(qs[gm], k_fn(g, m).astype(MXU_DT)) * SCALE + bias
                if mask is not None:
                    s = jnp.where(mask, s, NEG)
                mn = jnp.maximum(m_all[rs], jnp.max(s, axis=-1, keepdims=True))
                p = jnp.exp(s - mn)
                if mask is not None:
                    p = jnp.where(mask, p, 0.0)
                a = jnp.exp(m_all[rs] - mn)
                new_m.append(mn)
                new_l.append(a * l_all[rs] + jnp.sum(p, axis=-1, keepdims=True))
                ps.append(p)
                alphas.append(a)
            pst = jnp.concatenate(ps, axis=0).astype(MXU_DT)
            pv = jnp.concatenate([_mm(pst, v_fn(g, h).astype(MXU_DT)) for h in range(2)], axis=1)
            for m in range(2):
                rs = slice((g * 2 + m) * rows, (g * 2 + m + 1) * rows)
                new_acc.append(alphas[m] * acc_all[rs] + pv[m * rows:(m + 1) * rows])
        return (jnp.concatenate(new_m, axis=0), jnp.concatenate(new_l, axis=0),
                jnp.concatenate(new_acc, axis=0))

    def chunk(c, carry):
        slot = c % 2
        for cpy in copies(b, c, slot):
            cpy.wait()

        @pl.when(c + 1 < n_ch)
        def _():
            for cpy in copies(b, c + 1, 1 - slot):
                cpy.start()

        @pl.when((c + 1 == n_ch) & (b + 1 < nb))
        def _():
            for cpy in copies(b + 1, 0, 1 - slot):
                cpy.start()

        base = slot * slot_rows
        return update(carry,
                      lambda g, m: kbuf[pl.ds(base + g * 2 + m, ch, stride=pieces), :],
                      lambda g, h: vbuf[pl.ds(base + h * C_KV + g, ch, stride=pieces), :],
                      pl.multiple_of(c * ch, ch), ch, None)

    n_rows = pieces * rows
    carry = (jnp.full((n_rows, 1), NEG, F32), jnp.zeros((n_rows, 1), F32), jnp.zeros((n_rows, C_VDIM), F32))
    carry = lax.fori_loop(0, n_ch, chunk, carry)
    _, l_all, acc_all = update(
        carry,
        lambda g, m: _pad_rows(kn_ref[:, (g * 2 + m) * HEAD_DIM:(g * 2 + m + 1) * HEAD_DIM], 128),
        lambda g, h: _pad_rows(vn_ref[:, g * C_VDIM + h * HEAD_DIM:g * C_VDIM + (h + 1) * HEAD_DIM], 128),
        past, 128, _new_key_mask(nq, C_GROUP))
    o_all = acc_all / l_all
    lam = _diff_lambda(lam_ref)
    for g in range(C_KV):
        r0 = g * 2 * rows
        o = _diff_finish(o_all[r0:r0 + rows] - lam * o_all[r0 + rows:r0 + 2 * rows], hn_ref)
        for r in range(C_GROUP):
            col = (g * C_GROUP + r) * C_VDIM
            o_ref[:, col:col + C_VDIM] = o[r * nq:(r + 1) * nq]


def diff_sample(z1s, pool_k, pool_v, bs, lam_vecs, head_norm, page_table, cp=8):
    bd, n_pages = page_table.shape
    nq = z1s.shape[0] // bd
    assert n_pages % (2 * cp) == 0
    slot_rows = cp * PAGE * C_KV * 2
    q_cols = C_HEADS * 2 * HEAD_DIM
    kv_cols = C_KV * C_VDIM
    return pl.pallas_call(
        functools.partial(_diff_sample_kernel, n_pages=n_pages, cp=cp),
        grid_spec=pltpu.PrefetchScalarGridSpec(
            num_scalar_prefetch=1, grid=(bd,),
            in_specs=[pl.BlockSpec((nq, q_cols), lambda i, pt: (i, 0)),
                      pl.BlockSpec((nq, kv_cols), lambda i, pt: (i, q_cols // kv_cols)),
                      pl.BlockSpec((nq, kv_cols), lambda i, pt: (i, q_cols // kv_cols + 1)),
                      pl.BlockSpec(memory_space=pl.ANY), pl.BlockSpec(memory_space=pl.ANY),
                      pl.BlockSpec(bs.shape, lambda i, pt: (0, 0, 0)),
                      pl.BlockSpec((4, HEAD_DIM), lambda i, pt: (0, 0)),
                      pl.BlockSpec((1, C_VDIM), lambda i, pt: (0, 0))],
            out_specs=pl.BlockSpec((nq, C_HEADS * C_VDIM), lambda i, pt: (i, 0)),
            scratch_shapes=[pltpu.VMEM((2 * slot_rows, HEAD_DIM), F32), pltpu.VMEM((2 * slot_rows, HEAD_DIM), F32),
                            pltpu.SemaphoreType.DMA((2, 2))]),
        out_shape=jax.ShapeDtypeStruct((bd * nq, C_HEADS * C_VDIM), F32),
        compiler_params=_cparams("arbitrary"),
        name="diff_sample",
    )(page_table, z1s, z1s, z1s, pool_k, pool_v, bs, lam_vecs, head_norm.reshape(1, C_VDIM))


def _row_tile(rows, cap=1024):
    tm = min(rows, cap)
    assert rows % tm == 0
    return tm


def _reorder_l0_weight(w):
    sizes = (A_HEADS * HEAD_DIM,) + (A_KV * HEAD_DIM,) * 6 + (
        N_GATES * A_HEADS, B_HEADS * HEAD_DIM, B_KV * HEAD_DIM, B_KV * HEAD_DIM,
        IDX_HEADS * IDX_DIM, IDX_DIM, IDX_HEADS)
    offs = [0]
    for s in sizes:
        offs.append(offs[-1] + s)
    piece = lambda i, j=None: w[:, offs[i]:offs[(i if j is None else j) + 1]]
    qa, six, ga, qb, kvb, qi, ki, wi = piece(0), piece(1, 6), piece(7), piece(8), piece(9, 10), piece(11), \
        piece(12), piece(13)
    pad = jnp.zeros((w.shape[0], L0_COLS - offs[-1]), w.dtype)
    return jnp.concatenate([qa, qb, six, kvb, qi, ki, ga, wi, pad], axis=1).astype(MXU_DT)


def _compress_weights(pe, w1, w2):
    half = CMP_STRIDE * HEAD_DIM
    w1 = w1.reshape(2, half, CMP_HIDDEN).astype(MXU_DT)
    pe_rows = jnp.zeros((16, half), F32).at[0:2].set(pe.reshape(2, half))
    return w1[0], w1[1], w2.astype(MXU_DT), pe_rows


def kernel(x_prompt, x_sample, cache_l0_nsa_cmp_k, cache_l0_nsa_cmp_v, cache_l0_nsa_slc_k, cache_l0_nsa_slc_v, state_l0_nsa_win_k, state_l0_nsa_win_v, cache_l0_dsa_k, cache_l0_dsa_v, cache_l0_dsa_idx_k, cache_l1_diff_k, cache_l1_diff_v, page_table, rel_bias, attn_norm, mlp_norm, mlp_w1, mlp_w2, l0_w_in, l0_w_out, l0_cmp_pe_k, l0_cmp_w1_k, l0_cmp_w2_k, l0_cmp_pe_v, l0_cmp_w1_v, l0_cmp_w2_v, l1_w_in, l1_w_out, l1_lambda_q1, l1_lambda_k1, l1_lambda_q2, l1_lambda_k2, l1_head_norm, final_norm):
    n, t_len, d = x_prompt.shape
    bd, nq, _ = x_sample.shape
    n_pool = cache_l0_nsa_cmp_k.shape[0]
    n_pages = page_table.shape[1]
    past = n_pages * PAGE
    lp = past + 128
    kv_w = A_KV * HEAD_DIM
    assert t_len % CHUNK == 0 and t_len >= WINDOW + QB and nq <= 8
    assert state_l0_nsa_win_k.shape[1] == min(WINDOW, past)

    xp = x_prompt.reshape(n * t_len, d)
    xs = x_sample.reshape(bd * nq, d)
    tmp, tms = _row_tile(xp.shape[0]), _row_tile(xs.shape[0])
    w0 = _reorder_l0_weight(l0_w_in)
    cw = [jnp.stack(pair) for pair in zip(_compress_weights(l0_cmp_pe_k, l0_cmp_w1_k, l0_cmp_w2_k),
                                          _compress_weights(l0_cmp_pe_v, l0_cmp_w1_v, l0_cmp_w2_v))]
    lam_vecs = jnp.stack([l1_lambda_q1, l1_lambda_k1, l1_lambda_q2, l1_lambda_k2])
    bf = lambda a: a.astype(MXU_DT)

    tp, bs = bias_tiles(rel_bias, past, nq, lp)
    bc_p, bc_s = bias_cmp(rel_bias, t_len, t_len // CMP_STRIDE, past, nq, past // CMP_STRIDE)

    zp = norm_proj(xp, attn_norm[0], w0, tmp, 768)
    zs = norm_proj(xs, attn_norm[0], w0, tms, 768)
    cut = lambda z, c, w: z[:, c:c + w]
    p_rows = {name: cut(zp, c, kv_w) for name, c in
              (("kc", C_KC), ("vc", C_VC), ("ks", C_KS), ("vs", C_VS), ("kw", C_KW), ("vw", C_VW),
               ("kb", C_KB), ("vb", C_VB))}
    s_rows = {name: cut(zs, c, kv_w) for name, c in
              (("kc", C_KC), ("vc", C_VC), ("ks", C_KS), ("vs", C_VS), ("kw", C_KW), ("vw", C_VW),
               ("kb", C_KB), ("vb", C_VB))}
    chunk_w = CMP_STRIDE * kv_w
    kc_p, vc_p = compress_prompt(p_rows["kc"].reshape(n, t_len // CMP_STRIDE, chunk_w),
                                 p_rows["vc"].reshape(n, t_len // CMP_STRIDE, chunk_w), cw)
    lanes = lambda a: a.reshape(-1, HEAD_DIM)
    kv_cmp_s = compress_sample(lanes(cache_l0_nsa_cmp_k), lanes(cache_l0_nsa_cmp_v), page_table, cw)
    oa_p = nsa_prompt(zp, kc_p, vc_p, tp, bc_p, rel_bias, n, t_len)
    ob_p = dsa_prompt(zp, tp, rel_bias, n, t_len)
    wb = state_l0_nsa_win_k.shape[1]
    oa_s = nsa_sample(zs, kv_cmp_s, lanes(cache_l0_nsa_slc_k), lanes(cache_l0_nsa_slc_v),
                      lanes(state_l0_nsa_win_k), lanes(state_l0_nsa_win_v), bs, bc_s, page_table)
    ob_s = dsa_sample(zs, lanes(cache_l0_dsa_k), lanes(cache_l0_dsa_v),
                      cache_l0_dsa_idx_k.reshape(-1, IDX_DIM), bs, page_table)
    w_out0 = bf(l0_w_out)
    w1_0, w2_0 = bf(mlp_w1[0]), bf(mlp_w2[0])
    xp = out_proj(xp, [oa_p, ob_p], w_out0, tmp, 1024)
    xs = out_proj(xs, [oa_s, ob_s], w_out0, tms, 1024)
    xp = mlp(xp, mlp_norm[0], w1_0, w2_0, final_norm, tmp, 512, False)
    xs = mlp(xs, mlp_norm[0], w1_0, w2_0, final_norm, tms, 512, False)

    w_in1 = bf(l1_w_in)
    z1p = norm_proj(xp, attn_norm[1], w_in1, tmp, 1024)
    z1s = norm_proj(xs, attn_norm[1], w_in1, tms, 1024)
    o1_p = diff_prompt(z1p, tp, rel_bias, lam_vecs, l1_head_norm, n, t_len)
    v_halves = cache_l1_diff_v.reshape(n_pool, PAGE, C_KV, 2, HEAD_DIM).transpose(0, 1, 3, 2, 4)
    o1_s = diff_sample(z1s, lanes(cache_l1_diff_k), lanes(v_halves), bs, lam_vecs, l1_head_norm, page_table)
    w_out1 = bf(l1_w_out)
    w1_1, w2_1 = bf(mlp_w1[1]), bf(mlp_w2[1])
    xp = out_proj(xp, [o1_p], w_out1, tmp, 1024)
    xs = out_proj(xs, [o1_s], w_out1, tms, 1024)
    y_prompt = mlp(xp, mlp_norm[1], w1_1, w2_1, final_norm, tmp, 512, True).reshape(n, t_len, d)
    y_sample = mlp(xs, mlp_norm[1], w1_1, w2_1, final_norm, tms, 512, True).reshape(bd, nq, d)

    row4 = lambda a, b: a.reshape(b, -1, A_KV, HEAD_DIM)
    win = min(WINDOW, t_len)
    outs = [y_prompt, y_sample]
    for name in ("kc", "vc", "ks", "vs"):
        outs += [row4(p_rows[name], n), row4(s_rows[name], bd)]
    for name, state in (("kw", state_l0_nsa_win_k), ("vw", state_l0_nsa_win_v)):
        outs += [row4(p_rows[name], n)[:, t_len - win:],
                 jnp.concatenate([state, row4(s_rows[name], bd)], axis=1)[:, -wb:]]
    for name in ("kb", "vb"):
        outs += [row4(p_rows[name], n), row4(s_rows[name], bd)]
    outs += [cut(zp, C_TAIL + T_KI, IDX_DIM).reshape(n, t_len, IDX_DIM),
             cut(zs, C_TAIL + T_KI, IDX_DIM).reshape(bd, nq, IDX_DIM)]
    k_cols, v_cols = C_KV * 2 * HEAD_DIM, C_KV * C_VDIM
    q_cols = C_HEADS * 2 * HEAD_DIM
    outs += [cut(z1p, q_cols, k_cols).reshape(n, t_len, C_KV, 2, HEAD_DIM),
             cut(z1s, q_cols, k_cols).reshape(bd, nq, C_KV, 2, HEAD_DIM),
             cut(z1p, q_cols + k_cols, v_cols).reshape(n, t_len, C_KV, C_VDIM),
             cut(z1s, q_cols + k_cols, v_cols).reshape(bd, nq, C_KV, C_VDIM)]
    return tuple(outs)
```

```python
import functools
import math

import jax
import jax.numpy as jnp
from jax import lax
from jax.experimental import pallas as pl
from jax.experimental.pallas import tpu as pltpu

F32 = jnp.float32
I32 = jnp.int32
MXU_DT = jnp.bfloat16

HEAD_DIM = 128
A_HEADS, A_KV, A_GROUP = 8, 2, 4
B_HEADS, B_KV, B_GROUP = 8, 2, 4
C_HEADS, C_KV, C_GROUP, C_VDIM = 8, 4, 2, 256
CMP_STRIDE, CMP_BLOCK, CMP_HIDDEN = 16, 32, 256
SEL_BLOCK, N_SEL_BLOCKS, WINDOW, N_GATES = 64, 16, 512, 3
IDX_HEADS, IDX_DIM, DSA_TOPK = 4, 64, 256
NUM_BUCKETS, MAX_DISTANCE = 32, 128
LAMBDA_INIT = 0.8 - 0.6 * math.exp(-0.3 * 1)
RMS_EPS = 1e-6
NEG = -1e30
SCALE = HEAD_DIM ** -0.5
QB = 128
CHUNK = 512
PAGE = 128
PAGE_PITCH = PAGE * A_KV + 8
assert QB >= MAX_DISTANCE and WINDOW % QB == 0 and WINDOW >= 2 * QB and 2 * SEL_BLOCK == QB

C_QA, C_QB, C_KC, C_VC, C_KS, C_VS, C_KW, C_VW, C_KB, C_VB, C_QI, C_TAIL = (
    0, 1024, 2048, 2304, 2560, 2816, 3072, 3328, 3584, 3840, 4096, 4352)
T_KI, T_GA, T_WI = 0, 64, 88
L0_COLS = 4608
VMEM_LIMIT = 56 * 1024 * 1024


def _cparams(*sem):
    return pltpu.CompilerParams(dimension_semantics=sem, vmem_limit_bytes=VMEM_LIMIT)


def _nt(a, b):
    return lax.dot_general(a, b, (((1,), (1,)), ((), ())), preferred_element_type=F32)


def _mm(a, b):
    return jnp.dot(a, b, preferred_element_type=F32)


def _rms(x, g):
    return x * lax.rsqrt(jnp.mean(x * x, axis=-1, keepdims=True) + RMS_EPS) * g


def _norm_proj_kernel(x_ref, g_ref, w_ref, o_ref, xn_ref):
    @pl.when(pl.program_id(1) == 0)
    def _():
        xn_ref[...] = _rms(x_ref[...], g_ref[...]).astype(xn_ref.dtype)

    o_ref[...] = _mm(xn_ref[...], w_ref[...])


def norm_proj(x, gain, w, tm, tn):
    rows, d = x.shape
    n = w.shape[1]
    return pl.pallas_call(
        _norm_proj_kernel,
        grid=(rows // tm, n // tn),
        in_specs=[pl.BlockSpec((tm, d), lambda i, j: (i, 0)),
                  pl.BlockSpec((1, d), lambda i, j: (0, 0)),
                  pl.BlockSpec((d, tn), lambda i, j: (0, j))],
        out_specs=pl.BlockSpec((tm, tn), lambda i, j: (i, j)),
        out_shape=jax.ShapeDtypeStruct((rows, n), F32),
        scratch_shapes=[pltpu.VMEM((tm, d), MXU_DT)],
        compiler_params=_cparams("parallel", "arbitrary"),
        name="norm_proj",
    )(x, gain.reshape(1, d), w)


def _out_proj_kernel(*refs, n_in):
    x_ref, o_refs, w_refs, y_ref = refs[0], refs[1:1 + n_in], refs[1 + n_in:1 + 2 * n_in], refs[-1]
    acc = x_ref[...]
    for o_ref, w_ref in zip(o_refs, w_refs):
        acc = acc + _mm(o_ref[...].astype(MXU_DT), w_ref[...])
    y_ref[...] = acc


def out_proj(x, outs, w, tm, tn):
    rows, d = x.shape
    o_specs, w_specs, row0 = [], [], 0
    for o in outs:
        k = o.shape[1]
        o_specs.append(pl.BlockSpec((tm, k), lambda i, j: (i, 0)))
        w_specs.append(pl.BlockSpec((k, tn), lambda i, j, rb=row0 // k: (rb, j)))
        row0 += k
    return pl.pallas_call(
        functools.partial(_out_proj_kernel, n_in=len(outs)),
        grid=(rows // tm, d // tn),
        in_specs=[pl.BlockSpec((tm, tn), lambda i, j: (i, j))] + o_specs + w_specs,
        out_specs=pl.BlockSpec((tm, tn), lambda i, j: (i, j)),
        out_shape=jax.ShapeDtypeStruct((rows, d), F32),
        compiler_params=_cparams("parallel", "arbitrary"),
        name="out_proj",
    )(x, *outs, *([w] * len(outs)))


def _mlp_kernel(x_ref, g_ref, w1_ref, w2_ref, gf_ref, y_ref, xn_ref, *, final_norm):
    j = pl.program_id(1)

    @pl.when(j == 0)
    def _():
        x = x_ref[...]
        xn_ref[...] = _rms(x, g_ref[...]).astype(xn_ref.dtype)
        y_ref[...] = x

    h = jnp.square(jnp.maximum(_mm(xn_ref[...], w1_ref[...]), 0.0))
    y_ref[...] += _mm(h.astype(w2_ref.dtype), w2_ref[...])

    if final_norm:
        @pl.when(j == pl.num_programs(1) - 1)
        def _():
            y_ref[...] = _rms(y_ref[...], gf_ref[...])


def mlp(x, gain, w1, w2, final_gain, tm, tf, final_norm):
    rows, d = x.shape
    ff = w1.shape[1]
    return pl.pallas_call(
        functools.partial(_mlp_kernel, final_norm=final_norm),
        grid=(rows // tm, ff // tf),
        in_specs=[pl.BlockSpec((tm, d), lambda i, j: (i, 0)),
                  pl.BlockSpec((1, d), lambda i, j: (0, 0)),
                  pl.BlockSpec((d, tf), lambda i, j: (0, j)),
                  pl.BlockSpec((tf, d), lambda i, j: (j, 0)),
                  pl.BlockSpec((1, d), lambda i, j: (0, 0))],
        out_specs=pl.BlockSpec((tm, d), lambda i, j: (i, 0)),
        out_shape=jax.ShapeDtypeStruct((rows, d), F32),
        scratch_shapes=[pltpu.VMEM((tm, d), MXU_DT)],
        compiler_params=_cparams("parallel", "arbitrary"),
        name="mlp",
    )(x, gain.reshape(1, d), w1, w2, final_gain.reshape(1, d))


def _bucket(dist):
    n = jnp.maximum(dist, 0)
    max_exact = NUM_BUCKETS // 2
    nf = jnp.maximum(n, 1).astype(F32)
    large = max_exact + (jnp.log(nf / max_exact) / math.log(MAX_DISTANCE / max_exact)
                         * (NUM_BUCKETS - max_exact)).astype(I32)
    large = jnp.minimum(large, NUM_BUCKETS - 1)
    return jnp.where(n < max_exact, n, large)


def _lookup(tbl_ref, col, buckets):
    def body(b, accs):
        v = tbl_ref[b, col]
        return tuple(jnp.where(bk == b, v, acc) for bk, acc in zip(buckets, accs))
    return lax.fori_loop(0, NUM_BUCKETS, body, tuple(jnp.zeros(bk.shape, F32) for bk in buckets))


def _bias_tiles_kernel(tbl_ref, tp_ref, bs_ref, *, q0):
    h = pl.program_id(0)
    t = lax.broadcasted_iota(I32, (QB, QB), 0)
    k = lax.broadcasted_iota(I32, (QB, QB), 1)
    ts = lax.broadcasted_iota(I32, bs_ref.shape[1:], 0)
    ks = lax.broadcasted_iota(I32, bs_ref.shape[1:], 1)
    d0, d1, ds = _lookup(tbl_ref, h, (_bucket(t - k), _bucket(QB + t - k), _bucket(q0 + ts - ks)))
    tp_ref[0, 0] = d0
    tp_ref[0, 1] = d1
    bs_ref[0] = ds


def bias_tiles(rel_bias, q0, n_q, lp):
    nh = rel_bias.shape[1]
    return pl.pallas_call(
        functools.partial(_bias_tiles_kernel, q0=q0),
        grid=(nh,),
        in_specs=[pl.BlockSpec(memory_space=pltpu.SMEM)],
        out_specs=[pl.BlockSpec((1, 2, QB, QB), lambda h: (h, 0, 0, 0)),
                   pl.BlockSpec((1, n_q, lp), lambda h: (h, 0, 0))],
        out_shape=[jax.ShapeDtypeStruct((nh, 2, QB, QB), F32),
                   jax.ShapeDtypeStruct((nh, n_q, lp), F32)],
        compiler_params=_cparams("arbitrary"),
        name="bias_tiles",
    )(rel_bias)


def _bias_cmp_kernel(tbl_ref, bp_ref, bs_ref, *, q0):
    h = pl.program_id(0)
    tp = lax.broadcasted_iota(I32, bp_ref.shape[1:], 0)
    cp = lax.broadcasted_iota(I32, bp_ref.shape[1:], 1)
    ts = lax.broadcasted_iota(I32, bs_ref.shape[1:], 0)
    cs = lax.broadcasted_iota(I32, bs_ref.shape[1:], 1)
    end = CMP_BLOCK - 1
    bp, bs = _lookup(tbl_ref, h, (_bucket(tp - (cp * CMP_STRIDE + end)),
                                  _bucket(q0 + ts - (cs * CMP_STRIDE + end))))
    bp_ref[0] = bp
    bs_ref[0] = bs


def bias_cmp(rel_bias, t_len, mc_p, q0, n_q, mc_s):
    return pl.pallas_call(
        functools.partial(_bias_cmp_kernel, q0=q0),
        grid=(A_HEADS,),
        in_specs=[pl.BlockSpec(memory_space=pltpu.SMEM)],
        out_specs=[pl.BlockSpec((1, t_len, mc_p), lambda h: (h, 0, 0)),
                   pl.BlockSpec((1, n_q, mc_s), lambda h: (h, 0, 0))],
        out_shape=[jax.ShapeDtypeStruct((A_HEADS, t_len, mc_p), F32),
                   jax.ShapeDtypeStruct((A_HEADS, n_q, mc_s), F32)],
        compiler_params=_cparams("arbitrary"),
        name="bias_cmp",
    )(rel_bias)


def _softmax_rows(z, mask):
    z = jnp.where(mask, z, NEG)
    m = jnp.max(z, axis=-1, keepdims=True)
    e = jnp.where(mask, jnp.exp(z - m), 0.0)
    l = jnp.sum(e, axis=-1, keepdims=True)
    return e / jnp.where(l > 0.0, l, 1.0)


def _gelu_tanh(x):
    return 0.5 * x * (1.0 + jnp.tanh(math.sqrt(2.0 / math.pi) * (x + 0.044715 * (x * x * x))))


def _compress(x_fn, m, w1a_ref, w1b_ref, w2_ref, pe_ref, next_fn, out_fn):
    pe = pe_ref[...].astype(MXU_DT)
    pos = _mm(pe, w1a_ref[...])[0:1] + _mm(pe, w1b_ref[...])[1:2]
    last = lax.broadcasted_iota(I32, (m, 1), 0) == m - 1
    for g in range(A_KV):
        xg = jnp.concatenate([x_fn(j, g).astype(MXU_DT) for j in range(CMP_STRIDE)], axis=1)
        hid = _gelu_tanh(_mm(xg, w1a_ref[...]) + next_fn(_mm(xg, w1b_ref[...])) + pos)
        out_fn(g, jnp.where(last, 0.0, _mm(hid.astype(MXU_DT), w2_ref[...])))


def _overlap(mc, jn, n_cmp, n_slc):
    c = lax.broadcasted_iota(I32, (mc, jn), 0)
    j = lax.broadcasted_iota(I32, (mc, jn), 1)
    ov = ((c * CMP_STRIDE < j * SEL_BLOCK + SEL_BLOCK) & (c * CMP_STRIDE + CMP_BLOCK > j * SEL_BLOCK)
          & (c < n_cmp) & (j < n_slc))
    return jnp.where(ov, 1.0, 0.0)


def _select_blocks(imp, pos, n_slc):
    jn = imp.shape[1]
    jidx = lax.broadcasted_iota(I32, (1, jn), 1)
    cur = pos // SEL_BLOCK
    forced = (jidx == 0) | (jidx == cur) | (jidx == cur - 1)
    future = jidx * SEL_BLOCK > pos
    score = jnp.where(future, -1.0, jnp.where(forced, 1e3, imp))
    score = jnp.where(jidx < n_slc, score, -2.0)

    def body(i, rank):
        col = jnp.sum(jnp.where(jidx == i, score, 0.0), axis=-1, keepdims=True)
        beats = jnp.where(col > score, 1.0, jnp.where(col == score, jnp.where(i < jidx, 1.0, 0.0), 0.0))
        return rank + beats

    rank = lax.fori_loop(0, n_slc, body, jnp.zeros(score.shape, F32), unroll=8)
    n_sel = min(N_SEL_BLOCKS, n_slc)
    return jnp.where((rank < n_sel) & (jidx < n_slc), 1.0, 0.0)


def _tree_sum(xs):
    xs = list(xs)
    while len(xs) > 1:
        xs = [xs[i] + xs[i + 1] for i in range(0, len(xs) - 1, 2)] + ([xs[-1]] if len(xs) % 2 else [])
    return xs[0]


def _sort_key(s):
    bits = lax.bitcast_convert_type(jnp.where(s == 0.0, 0.0, s), I32)
    return jnp.where(bits < 0, bits ^ jnp.int32(0x7FFFFFFF), bits)


def _topk_madd(key_ref, madd_ref, valid_fn, nch, cw, k, nbits):
    n_rows = key_ref.shape[0]
    kf = jnp.float32(k)

    def count(fn):
        def body(c, acc):
            c0 = pl.multiple_of(c * cw, cw)
            hit = jnp.where(fn(c0, key_ref[:, pl.ds(c0, cw)]), 1.0, 0.0)
            return acc + _tree_sum(hit[:, i:i + 128] for i in range(0, cw, 128))
        acc = lax.fori_loop(0, nch, body, jnp.zeros((n_rows, 128), F32))
        return jnp.sum(acc, axis=-1, keepdims=True)

    int_min = jnp.int32(-2 ** 31)
    thr0 = jnp.where(count(lambda c0, key: key >= 0) >= kf, jnp.int32(0), int_min)

    def vbody(i, thr):
        cand = thr | lax.shift_left(jnp.int32(1), 30 - i)
        return jnp.where(count(lambda c0, key: key >= cand) >= kf, cand, thr)

    thr = lax.fori_loop(0, 31, vbody, thr0)
    need = kf - count(lambda c0, key: key > thr)

    def idx(c0):
        return c0 + lax.broadcasted_iota(I32, (1, cw), 1)

    def ibody(i, cut):
        cand = cut | lax.shift_left(jnp.int32(1), nbits - 1 - i)
        return jnp.where(count(lambda c0, key: (key == thr) & (idx(c0) < cand)) <= need, cand, cut)

    tied = jnp.max(count(lambda c0, key: key == thr) - need) > 0.0
    cut = lax.cond(tied, lambda: lax.fori_loop(0, nbits, ibody, jnp.zeros((n_rows, 1), I32)),
                   lambda: jnp.full((n_rows, 1), 2 ** nbits - 1, I32))

    def write(c, _):
        c0 = pl.multiple_of(c * cw, cw)
        key = key_ref[:, pl.ds(c0, cw)]
        sel = ((key > thr) | ((key == thr) & (idx(c0) < cut))) & valid_fn(c0)
        madd_ref[:, pl.ds(c0, cw)] = jnp.where(sel, 0.0, NEG)
        return 0

    lax.fori_loop(0, nch, write, 0)


def _topk_madd_t(key_ref, madd_ref, valid_fn, nch, cw, k, nbits):
    n_rows = key_ref.shape[1]
    kf = jnp.float32(k)

    def count(fn):
        def body(c, acc):
            c0 = pl.multiple_of(c * cw, cw)
            hit = jnp.where(fn(c0, key_ref[pl.ds(c0, cw), :]), 1.0, 0.0)
            return acc + _tree_sum(hit[i:i + 8] for i in range(0, cw, 8))
        acc = lax.fori_loop(0, nch, body, jnp.zeros((8, n_rows), F32))
        return jnp.sum(acc, axis=0, keepdims=True)

    int_min = jnp.int32(-2 ** 31)
    thr0 = jnp.where(count(lambda c0, key: key >= 0) >= kf, jnp.int32(0), int_min)

    def vbody(i, thr):
        cand = thr | lax.shift_left(jnp.int32(1), 30 - i)
        return jnp.where(count(lambda c0, key: key >= cand) >= kf, cand, thr)

    thr = lax.fori_loop(0, 31, vbody, thr0)
    need = kf - count(lambda c0, key: key > thr)

    def idx(c0):
        return c0 + lax.broadcasted_iota(I32, (cw, 1), 0)

    def ibody(i, cut):
        cand = cut | lax.shift_left(jnp.int32(1), nbits - 1 - i)
        return jnp.where(count(lambda c0, key: (key == thr) & (idx(c0) < cand)) <= need, cand, cut)

    tied = jnp.max(count(lambda c0, key: key == thr) - need) > 0.0
    cut = lax.cond(tied, lambda: lax.fori_loop(0, nbits, ibody, jnp.zeros((1, n_rows), I32)),
                   lambda: jnp.full((1, n_rows), 2 ** nbits - 1, I32))

    def write(c, _):
        c0 = pl.multiple_of(c * cw, cw)
        key = key_ref[pl.ds(c0, cw), :]
        sel = ((key > thr) | ((key == thr) & (idx(c0) < cut))) & valid_fn(c0)
        madd_ref[:, pl.ds(c0, cw)] = jnp.where(sel, 0.0, NEG).T
        return 0

    lax.fori_loop(0, nch, write, 0)


def _causal_attn(streams, dv, qb):
    m_rows = streams[0][0].shape[0]
    per = CHUNK // QB
    nact = qb // per + 1

    def stage(st, k0, w, bias):
        q, k_fn, _, _, _, madd_fn, s_ref = st
        s = _nt(q, k_fn(k0, w)) * SCALE + bias
        madd = madd_fn(k0, w)
        s_ref[:, pl.ds(k0, w)] = s if madd is None else s + madd

    def far(c, _):
        for st in streams:
            stage(st, pl.multiple_of(c * CHUNK, CHUNK), CHUNK, st[3])
        return 0

    lax.fori_loop(0, nact, far, 0)
    for st in streams:
        stage(st, pl.multiple_of(qb * QB, QB), QB, st[4](True))

    @pl.when(qb >= 1)
    def _():
        for st in streams:
            stage(st, pl.multiple_of((qb - 1) * QB, QB), QB, st[4](False))

    for j in range(1, per):
        @pl.when(qb % per + j < per)
        def _():
            for st in streams:
                st[6][:, pl.ds(pl.multiple_of((qb + j) * QB, QB), QB)] = jnp.full((m_rows, QB), NEG, F32)

    def row_max(c, ms):
        k0 = pl.multiple_of(c * CHUNK, CHUNK)
        return tuple(jnp.maximum(m, jnp.max(st[6][:, pl.ds(k0, CHUNK)], axis=-1, keepdims=True))
                     for st, m in zip(streams, ms))

    ms = lax.fori_loop(0, nact, row_max, tuple(jnp.full((m_rows, 1), NEG, F32) for _ in streams))

    def pv(c, carry):
        k0 = pl.multiple_of(c * CHUNK, CHUNK)
        out = []
        for st, m, (l, acc) in zip(streams, ms, carry):
            p = jnp.exp(st[6][:, pl.ds(k0, CHUNK)] - m)
            out.append((l + jnp.sum(p, axis=-1, keepdims=True), acc + _mm(p.astype(MXU_DT), st[2](k0, CHUNK))))
        return tuple(out)

    init = tuple((jnp.zeros((m_rows, 1), F32), jnp.zeros((m_rows, dv), F32)) for _ in streams)
    return [acc / l for l, acc in lax.fori_loop(0, nact, pv, init)]


def _causal_add(rep):
    t = lax.broadcasted_iota(I32, (QB, QB), 0)
    k = lax.broadcasted_iota(I32, (QB, QB), 1)
    return jnp.concatenate([jnp.where(k <= t, 0.0, NEG)] * rep, axis=0)


def _far_bias(tbl_ref, cols):
    return jnp.concatenate([jnp.full((QB, 1), tbl_ref[NUM_BUCKETS - 1, c], F32) for c in cols], axis=0)


def _near_bias(tp_ref, cols, diag):
    return jnp.concatenate([tp_ref[c, 0 if diag else 1] for c in cols], axis=0)


def _stack_heads(ref, col0, n):
    return jnp.concatenate([ref[:, col0 + r * HEAD_DIM:col0 + (r + 1) * HEAD_DIM] for r in range(n)], axis=0)


def _compress_prompt_kernel(xk_ref, xv_ref, w1a, w1b, w2, pe, ok_ref, ov_ref):
    lanes = lambda x_ref: (lambda j, g: x_ref[0, :, (2 * j + g) * HEAD_DIM:(2 * j + g + 1) * HEAD_DIM])
    m = xk_ref.shape[1]
    for i, (x_ref, o_ref) in enumerate(((xk_ref, ok_ref), (xv_ref, ov_ref))):
        def store(g, tokens, o_ref=o_ref):
            o_ref[0, :, g * HEAD_DIM:(g + 1) * HEAD_DIM] = tokens

        _compress(lanes(x_ref), m, w1a.at[i], w1b.at[i], w2.at[i], pe.at[i],
                  lambda y: pltpu.roll(y, m - 1, 0), store)


def _cmp_weight_specs():
    full = lambda *shape: pl.BlockSpec(shape, lambda *_: (0,) * len(shape))
    half = CMP_STRIDE * HEAD_DIM
    return [full(2, half, CMP_HIDDEN), full(2, half, CMP_HIDDEN), full(2, CMP_HIDDEN, HEAD_DIM), full(2, 16, half)]


def compress_prompt(xk, xv, cw):
    n, m, w = xk.shape
    spec = pl.BlockSpec((1, m, w), lambda i: (i, 0, 0))
    ospec = pl.BlockSpec((1, m, A_KV * HEAD_DIM), lambda i: (i, 0, 0))
    osh = jax.ShapeDtypeStruct((n, m, A_KV * HEAD_DIM), F32)
    return pl.pallas_call(
        _compress_prompt_kernel,
        grid=(n,),
        in_specs=[spec, spec] + _cmp_weight_specs(),
        out_specs=[ospec, ospec],
        out_shape=[osh, osh],
        compiler_params=_cparams("parallel"),
        name="compress_prompt",
    )(xk, xv, *cw)


def _window_attn(q, kw_ref, vw_ref, gl, qb, pos4, tbl_ref, tp_ref, cols):
    n_tiles = WINDOW // QB + 1
    width = n_tiles * QB
    lo = jnp.maximum(qb - (n_tiles - 1), 0)
    w0 = pl.multiple_of(lo * QB, QB)
    tiles = []
    for j in range(n_tiles):
        rel = qb - (lo + j)
        tiles.append(jnp.concatenate(
            [jnp.where(rel == 0, tp_ref[c, 0], jnp.where(rel == 1, tp_ref[c, 1], tbl_ref[NUM_BUCKETS - 1, c]))
             for c in cols], axis=0))
    s = _nt(q, kw_ref[pl.ds(w0, width), gl].astype(MXU_DT)) * SCALE + jnp.concatenate(tiles, axis=1)
    dist = pos4 - (w0 + lax.broadcasted_iota(I32, (1, width), 1))
    p = _softmax_rows(s, (dist >= 0) & (dist < WINDOW))
    return _mm(p.astype(MXU_DT), vw_ref[pl.ds(w0, width), gl].astype(MXU_DT))


def _nsa_prompt_kernel(tbl_ref, q_ref, tail_ref, kc_ref, vc_ref, ks_ref, vs_ref, kw_ref, vw_ref,
                       tp_ref, bc_ref, o_ref, s_ref, madd_ref, *, t_len):
    qb = pl.program_id(1)
    mc = kc_ref.shape[1]
    n_cmp = (t_len - CMP_BLOCK) // CMP_STRIDE + 1
    n_slc = -(-t_len // SEL_BLOCK)
    pos = qb * QB + lax.broadcasted_iota(I32, (QB, 1), 0)
    pos4 = jnp.concatenate([pos] * A_GROUP, axis=0)
    cidx = lax.broadcasted_iota(I32, (1, mc), 1)
    gates = jax.nn.sigmoid(tail_ref[...])
    overlap = _overlap(mc, QB, n_cmp, n_slc)
    causal_add = _causal_add(A_GROUP)
    onehot = jnp.where(lax.broadcasted_iota(I32, (QB, t_len), 0)
                       == lax.broadcasted_iota(I32, (QB, t_len), 1) // SEL_BLOCK, 1.0, 0.0).astype(MXU_DT)

    o_cmp, o_win, streams = [], [], []
    for g in range(A_KV):
        cols = [g * A_GROUP + r for r in range(A_GROUP)]
        gl = slice(g * HEAD_DIM, (g + 1) * HEAD_DIM)
        q = _stack_heads(q_ref, g * A_GROUP * HEAD_DIM, A_GROUP).astype(MXU_DT)
        lc = (_nt(q, kc_ref[0, :, gl].astype(MXU_DT)) * SCALE
              + jnp.concatenate([bc_ref[c] for c in cols], axis=0))
        p_cmp = _softmax_rows(lc, (pos4 >= cidx * CMP_STRIDE + (CMP_BLOCK - 1)) & (cidx < n_cmp))
        o_cmp.append(_mm(p_cmp.astype(MXU_DT), vc_ref[0, :, gl].astype(MXU_DT)))
        p_sum = sum(p_cmp[r * QB:(r + 1) * QB] for r in range(A_GROUP))
        imp = jnp.dot(p_sum, overlap, preferred_element_type=F32, precision=lax.Precision.HIGHEST)
        sel = _select_blocks(imp, pos, n_slc).astype(MXU_DT)
        madd_ref[g] = jnp.where(_mm(sel, onehot) > 0.5, 0.0, NEG)
        o_win.append(_window_attn(q, kw_ref, vw_ref, gl, qb, pos4, tbl_ref, tp_ref, cols))
        streams.append((
            q, lambda k0, w, gl=gl: ks_ref[pl.ds(k0, w), gl].astype(MXU_DT),
            lambda k0, w, gl=gl: vs_ref[pl.ds(k0, w), gl].astype(MXU_DT), _far_bias(tbl_ref, cols),
            lambda diag, cols=cols: _near_bias(tp_ref, cols, diag) + (causal_add if diag else 0.0),
            lambda k0, w, g=g: jnp.concatenate([madd_ref[g, :, pl.ds(k0, w)]] * A_GROUP, axis=0),
            s_ref.at[g]))
    o_slc = _causal_attn(streams, HEAD_DIM, qb)

    for h in range(A_HEADS):
        g, r = divmod(h, A_GROUP)
        c = T_GA + h * N_GATES
        rows = slice(r * QB, (r + 1) * QB)
        o = (gates[:, c:c + 1] * o_cmp[g][rows] + gates[:, c + 1:c + 2] * o_slc[g][rows]
             + gates[:, c + 2:c + 3] * o_win[g][rows])
        o_ref[:, h * HEAD_DIM:(h + 1) * HEAD_DIM] = o.astype(o_ref.dtype)


def nsa_prompt(z, k_cmp, v_cmp, tp, bc, rel_bias, n, t_len):
    nb = t_len // QB
    mc = k_cmp.shape[1]
    kv = lambda c: pl.BlockSpec((t_len, 256), lambda i, j: (i, c // 256))
    cmp_spec = pl.BlockSpec((1, mc, 256), lambda i, j: (i, 0, 0))
    return pl.pallas_call(
        functools.partial(_nsa_prompt_kernel, t_len=t_len),
        grid=(n, nb),
        in_specs=[pl.BlockSpec(memory_space=pltpu.SMEM),
                  pl.BlockSpec((QB, 1024), lambda i, j: (i * nb + j, C_QA // 1024)),
                  pl.BlockSpec((QB, 128), lambda i, j: (i * nb + j, C_TAIL // 128)),
                  cmp_spec, cmp_spec, kv(C_KS), kv(C_VS), kv(C_KW), kv(C_VW),
                  pl.BlockSpec(tp.shape, lambda i, j: (0, 0, 0, 0)),
                  pl.BlockSpec((A_HEADS, QB, mc), lambda i, j: (0, j, 0))],
        out_specs=pl.BlockSpec((QB, 1024), lambda i, j: (i * nb + j, 0)),
        out_shape=jax.ShapeDtypeStruct((n * t_len, 1024), MXU_DT),
        scratch_shapes=[pltpu.VMEM((A_KV, A_GROUP * QB, t_len), F32), pltpu.VMEM((A_KV, QB, t_len), F32)],
        compiler_params=_cparams("parallel", "arbitrary"),
        name="nsa_prompt",
    )(rel_bias, z, z, k_cmp, v_cmp, z, z, z, z, tp, bc)


def _dsa_prompt_kernel(tbl_ref, q_ref, qi_ref, tailq_ref, tailk_ref, kb_ref, vb_ref, tp_ref, o_ref,
                       key_ref, madd_ref, s_ref, *, topk, nbits):
    qb = pl.program_id(1)
    nact = qb // (CHUNK // QB) + 1
    pos = qb * QB + lax.broadcasted_iota(I32, (1, QB), 1)
    wi_t = tailq_ref[...].T[T_WI:T_WI + IDX_HEADS]
    qis = [qi_ref[:, h * IDX_DIM:(h + 1) * IDX_DIM].astype(MXU_DT) for h in range(IDX_HEADS)]

    def causal(c0):
        return c0 + lax.broadcasted_iota(I32, (CHUNK, 1), 0) <= pos

    def index_chunk(c, _):
        c0 = pl.multiple_of(c * CHUNK, CHUNK)
        ki = tailk_ref[pl.ds(c0, CHUNK), T_KI:T_KI + IDX_DIM].astype(MXU_DT)
        score = sum(jnp.maximum(_nt(ki, qis[h]), 0.0) * wi_t[h:h + 1] for h in range(IDX_HEADS))
        score = score * (IDX_DIM ** -0.5 * IDX_HEADS ** -0.5)
        key_ref[pl.ds(c0, CHUNK), :] = _sort_key(jnp.where(causal(c0), score, NEG))
        return 0

    lax.fori_loop(0, nact, index_chunk, 0)
    _topk_madd_t(key_ref, madd_ref, causal, nact, CHUNK, topk, nbits)

    streams = []
    for g in range(B_KV):
        cols = [A_HEADS + g * B_GROUP + r for r in range(B_GROUP)]
        gl = slice(g * HEAD_DIM, (g + 1) * HEAD_DIM)
        streams.append((
            _stack_heads(q_ref, g * B_GROUP * HEAD_DIM, B_GROUP).astype(MXU_DT),
            lambda k0, w, gl=gl: kb_ref[pl.ds(k0, w), gl].astype(MXU_DT),
            lambda k0, w, gl=gl: vb_ref[pl.ds(k0, w), gl].astype(MXU_DT),
            _far_bias(tbl_ref, cols), lambda diag, cols=cols: _near_bias(tp_ref, cols, diag),
            lambda k0, w: jnp.concatenate([madd_ref[:, pl.ds(k0, w)]] * B_GROUP, axis=0), s_ref.at[g]))
    outs = _causal_attn(streams, HEAD_DIM, qb)
    for h in range(B_HEADS):
        g, r = divmod(h, B_GROUP)
        o_ref[:, h * HEAD_DIM:(h + 1) * HEAD_DIM] = outs[g][r * QB:(r + 1) * QB].astype(o_ref.dtype)


def dsa_prompt(z, tp, rel_bias, n, t_len):
    nb = t_len // QB
    topk = min(DSA_TOPK, t_len // 4)
    nbits = int(t_len).bit_length()
    return pl.pallas_call(
        functools.partial(_dsa_prompt_kernel, topk=topk, nbits=nbits),
        grid=(n, nb),
        in_specs=[pl.BlockSpec(memory_space=pltpu.SMEM),
                  pl.BlockSpec((QB, 1024), lambda i, j: (i * nb + j, C_QB // 1024)),
                  pl.BlockSpec((QB, 256), lambda i, j: (i * nb + j, C_QI // 256)),
                  pl.BlockSpec((QB, 128), lambda i, j: (i * nb + j, C_TAIL // 128)),
                  pl.BlockSpec((t_len, 128), lambda i, j: (i, C_TAIL // 128)),
                  pl.BlockSpec((t_len, 256), lambda i, j: (i, C_KB // 256)),
                  pl.BlockSpec((t_len, 256), lambda i, j: (i, C_VB // 256)),
                  pl.BlockSpec(tp.shape, lambda i, j: (0, 0, 0, 0))],
        out_specs=pl.BlockSpec((QB, 1024), lambda i, j: (i * nb + j, 0)),
        out_shape=jax.ShapeDtypeStruct((n * t_len, 1024), MXU_DT),
        scratch_shapes=[pltpu.VMEM((t_len, QB), I32), pltpu.VMEM((QB, t_len), F32),
                        pltpu.VMEM((B_KV, B_GROUP * QB, t_len), F32)],
        compiler_params=_cparams("parallel", "arbitrary"),
        name="dsa_prompt",
    )(rel_bias, z, z, z, z, z, z, tp)


def _diff_lambda(lam_ref):
    v = lam_ref[...]
    e1 = jnp.exp(jnp.sum(v[0:1] * v[1:2], axis=-1, keepdims=True))
    e2 = jnp.exp(jnp.sum(v[2:3] * v[3:4], axis=-1, keepdims=True))
    return e1 - e2 + LAMBDA_INIT


def _diff_finish(o, hn_ref):
    return _rms(o, hn_ref[...]) * (1.0 - LAMBDA_INIT)


def _diff_prompt_kernel(tbl_ref, q_ref, k_ref, v_ref, tp_ref, lam_ref, hn_ref, o_ref, s_ref):
    g = pl.program_id(1)
    qb = pl.program_id(2)
    causal_add = _causal_add(C_GROUP)
    streams = []
    for m in range(2):
        cols = [m * C_HEADS + g * C_GROUP + r for r in range(C_GROUP)]
        q = jnp.concatenate([q_ref[:, (r * 2 + m) * HEAD_DIM:(r * 2 + m + 1) * HEAD_DIM]
                             for r in range(C_GROUP)], axis=0).astype(MXU_DT)
        streams.append((
            q, lambda k0, w, m=m: k_ref[pl.ds(k0, w), m * HEAD_DIM:(m + 1) * HEAD_DIM].astype(MXU_DT),
            lambda k0, w: v_ref[pl.ds(k0, w), :].astype(MXU_DT), _far_bias(tbl_ref, cols),
            lambda diag, cols=cols: _near_bias(tp_ref, cols, diag) + (causal_add if diag else 0.0),
            lambda k0, w: None, s_ref.at[m]))
    outs = _causal_attn(streams, C_VDIM, qb)
    o = _diff_finish(outs[0] - _diff_lambda(lam_ref) * outs[1], hn_ref)
    for r in range(C_GROUP):
        o_ref[:, r * C_VDIM:(r + 1) * C_VDIM] = o[r * QB:(r + 1) * QB].astype(o_ref.dtype)


def diff_prompt(z1, tp, rel_bias, lam_vecs, head_norm, n, t_len):
    nb = t_len // QB
    return pl.pallas_call(
        _diff_prompt_kernel,
        grid=(n, C_KV, nb),
        in_specs=[pl.BlockSpec(memory_space=pltpu.SMEM),
                  pl.BlockSpec((QB, 512), lambda i, g, j: (i * nb + j, g)),
                  pl.BlockSpec((t_len, 256), lambda i, g, j: (i, 2048 // 256 + g)),
                  pl.BlockSpec((t_len, 256), lambda i, g, j: (i, 3072 // 256 + g)),
                  pl.BlockSpec(tp.shape, lambda i, g, j: (0, 0, 0, 0)),
                  pl.BlockSpec((4, HEAD_DIM), lambda i, g, j: (0, 0)),
                  pl.BlockSpec((1, C_VDIM), lambda i, g, j: (0, 0))],
        out_specs=pl.BlockSpec((QB, 512), lambda i, g, j: (i * nb + j, g)),
        out_shape=jax.ShapeDtypeStruct((n * t_len, C_HEADS * C_VDIM), MXU_DT),
        scratch_shapes=[pltpu.VMEM((2, C_GROUP * QB, t_len), F32)],
        compiler_params=_cparams("parallel", "parallel", "arbitrary"),
        name="diff_prompt",
    )(rel_bias, z1, z1, z1, tp, lam_vecs, head_norm.reshape(1, C_VDIM))


def _page_gather(pt_ref, n_pages, items, sem):
    def copy(i, pg, p, slot):
        pool_ref, buf_ref, rows = items[i]
        src = pool_ref.at[pl.ds(pl.multiple_of(pg * rows, rows), rows)]
        dst = buf_ref.at[pl.ds(pl.multiple_of((slot * n_pages + p) * rows, rows), rows)]
        return pltpu.make_async_copy(src, dst, sem.at[i, slot])

    def start(bb, slot):
        def body(p, _):
            for i in range(len(items)):
                copy(i, pt_ref[bb, p], p, slot).start()
            return 0
        lax.fori_loop(0, n_pages, body, 0)

    def wait(i, slot):
        def body(p, _):
            copy(i, 0, 0, slot).wait()
            return 0
        lax.fori_loop(0, n_pages, body, 0)

    return start, wait


def _prefetch(b, nb, start):
    slot = b % 2

    @pl.when(b == 0)
    def _():
        start(0, 0)

    @pl.when(b + 1 < nb)
    def _():
        start(b + 1, 1 - slot)

    return slot


def _pad_rows(x, rows):
    return jnp.concatenate([x, jnp.zeros((rows - x.shape[0], x.shape[1]), x.dtype)], axis=0)


def _page_rows(pool_ref, rows):
    return lambda pg: pool_ref.at[pl.ds(pl.multiple_of(pg * rows, rows), rows)]


def _interleaved(buf_ref, n, j, row0=0):
    return lambda c0, ch: buf_ref[pl.ds(row0 + c0 * n + j, ch, stride=n), :]


def _sample_scores(q, k_fn, knew, bias_fn, mask_fn, s_ref, past, ch, scale=SCALE):
    def body(c, _):
        c0 = pl.multiple_of(c * ch, ch)
        k = k_fn(c0, ch).astype(MXU_DT)
        s = _nt(q, k) * scale + bias_fn(c0, ch)
        s_ref[:, pl.ds(c0, ch)] = jnp.where(mask_fn(c0, ch, False), s, NEG)
        return 0

    lax.fori_loop(0, past // ch, body, 0, unroll=4)
    s = _nt(q, _pad_rows(knew, 128).astype(MXU_DT)) * scale + bias_fn(past, 128)
    s_ref[:, past:past + 128] = jnp.where(mask_fn(past, 128, True), s, NEG)


def _sample_softmax(s_ref):
    z = s_ref[...]
    m = jnp.max(z, axis=-1, keepdims=True)
    e = jnp.where(z > 0.5 * NEG, jnp.exp(z - m), 0.0)
    l = jnp.sum(e, axis=-1, keepdims=True)
    return e / jnp.where(l > 0.0, l, 1.0)


def _sample_pv(p_ref, v_fn, vnew, past, ch):
    def body(c, acc):
        c0 = pl.multiple_of(c * ch, ch)
        return acc + _mm(p_ref[:, pl.ds(c0, ch)].astype(MXU_DT), v_fn(c0, ch).astype(MXU_DT))

    acc = lax.fori_loop(0, past // ch, body, jnp.zeros((p_ref.shape[0], vnew.shape[1]), F32), unroll=4)
    return acc + _mm(p_ref[:, past:past + 128].astype(MXU_DT), _pad_rows(vnew, 128).astype(MXU_DT))


def _new_key_mask(nq, rep):
    t = lax.broadcasted_iota(I32, (nq, 128), 0)
    j = lax.broadcasted_iota(I32, (nq, 128), 1)
    return jnp.concatenate([(j <= t) & (j < nq)] * rep, axis=0)


def _compress_sample_kernel(pt_ref, pk_ref, pv_ref, w1a, w1b, w2, pe, o_ref, buf, sem, *, n_pages):
    step = pl.program_id(0)
    cpp = PAGE // CMP_STRIDE
    rows = CMP_STRIDE * A_KV
    m = n_pages * cpp

    def copy(pool_ref, pg, p, slot):
        src = pool_ref.at[pl.ds(pl.multiple_of(pg * PAGE * A_KV, PAGE * A_KV), PAGE * A_KV)]
        dst = buf.at[pl.ds(pl.multiple_of((slot * n_pages + p) * PAGE_PITCH, 8), PAGE * A_KV)]
        return pltpu.make_async_copy(src, dst, sem.at[slot])

    def start(st, slot):
        for which, pool_ref in enumerate((pk_ref, pv_ref)):
            @pl.when(st % 2 == which)
            def _(pool_ref=pool_ref):
                def body(p, _):
                    copy(pool_ref, pt_ref[st // 2, p], p, slot).start()
                    return 0
                lax.fori_loop(0, n_pages, body, 0)

    slot = _prefetch(step, pl.num_programs(0), start)

    def wait(p, _):
        copy(pk_ref, 0, 0, slot).wait()
        return 0

    lax.fori_loop(0, n_pages, wait, 0)
    which = step % 2
    row0 = slot * n_pages * PAGE_PITCH

    def x_fn(j, g):
        return jnp.concatenate([buf[pl.ds(row0 + i * rows + 2 * j + g, n_pages, stride=PAGE_PITCH), :]
                                for i in range(cpp)], axis=0)

    def next_fn(y):
        return jnp.concatenate([y[n_pages:], pltpu.roll(y[:n_pages], n_pages - 1, 0)], axis=0)

    def store(g, tokens):
        for i in range(cpp):
            o_ref[0, 0, g, pl.ds(i, n_pages, stride=cpp), :] = tokens[i * n_pages:(i + 1) * n_pages]

    _compress(x_fn, m, w1a.at[which], w1b.at[which], w2.at[which], pe.at[which], next_fn, store)


def compress_sample(pool_k, pool_v, page_table, cw):
    bd, n_pages = page_table.shape
    m = n_pages * (PAGE // CMP_STRIDE)
    return pl.pallas_call(
        functools.partial(_compress_sample_kernel, n_pages=n_pages),
        grid_spec=pltpu.PrefetchScalarGridSpec(
            num_scalar_prefetch=1, grid=(2 * bd,),
            in_specs=[pl.BlockSpec(memory_space=pl.ANY), pl.BlockSpec(memory_space=pl.ANY)] + _cmp_weight_specs(),
            out_specs=pl.BlockSpec((1, 1, A_KV, m, HEAD_DIM), lambda s, pt: (s // 2, s % 2, 0, 0, 0)),
            scratch_shapes=[pltpu.VMEM((2 * n_pages * PAGE_PITCH, HEAD_DIM), F32), pltpu.SemaphoreType.DMA((2,))]),
        out_shape=jax.ShapeDtypeStruct((bd, 2, A_KV, m, HEAD_DIM), F32),
        compiler_params=_cparams("arbitrary"),
        name="compress_sample",
    )(page_table, pool_k, pool_v, *cw)


def _nsa_sample_kernel(pt_ref, z_ref, kc_ref, vc_ref, pks_ref, pvs_ref, wk_ref, wv_ref, bs_ref, bc_ref,
                       o_ref, kbuf, vbuf, s_ref, sw_ref, chosen_ref, sem, *, n_pages, ch):
    past = n_pages * PAGE
    nq = z_ref.shape[0]
    mc = kc_ref.shape[3]
    t_len = past + nq
    n_cmp = (t_len - CMP_BLOCK) // CMP_STRIDE + 1
    n_slc = -(-t_len // SEL_BLOCK)
    jn = 128 * (-(-n_slc // 128))
    wb = wk_ref.shape[0] // A_KV
    start, wait = _page_gather(pt_ref, n_pages, ((pks_ref, kbuf, PAGE * A_KV), (pvs_ref, vbuf, PAGE * A_KV)), sem)
    slot = _prefetch(pl.program_id(0), pl.num_programs(0), start)
    row0 = slot * past * A_KV

    pos = past + lax.broadcasted_iota(I32, (nq, 1), 0)
    pos4 = jnp.concatenate([pos] * A_GROUP, axis=0)
    cidx = lax.broadcasted_iota(I32, (1, mc), 1)
    gates = jax.nn.sigmoid(z_ref[:, C_TAIL:C_TAIL + 128])
    overlap = _overlap(mc, jn, n_cmp, n_slc)
    new_mask = _new_key_mask(nq, A_GROUP)
    first_half = lax.broadcasted_iota(I32, (nq, 2 * SEL_BLOCK), 1) < SEL_BLOCK
    waited = False

    for g in range(A_KV):
        cols = [g * A_GROUP + r for r in range(A_GROUP)]
        q = _stack_heads(z_ref, C_QA + g * A_GROUP * HEAD_DIM, A_GROUP).astype(MXU_DT)
        gl = slice(g * HEAD_DIM, (g + 1) * HEAD_DIM)
        lc = (_nt(q, kc_ref[0, 0, g].astype(MXU_DT)) * SCALE
              + jnp.concatenate([bc_ref[c] for c in cols], axis=0))
        p_cmp = _softmax_rows(lc, (pos4 >= cidx * CMP_STRIDE + (CMP_BLOCK - 1)) & (cidx < n_cmp))
        o_cmp = _mm(p_cmp.astype(MXU_DT), vc_ref[0, 0, g].astype(MXU_DT))
        p_sum = sum(p_cmp[r * nq:(r + 1) * nq] for r in range(A_GROUP))
        imp = jnp.dot(p_sum, overlap, preferred_element_type=F32, precision=lax.Precision.HIGHEST)
        sel = _select_blocks(imp, pos, n_slc)
        def win_bias(c0, w, cols=cols):
            return jnp.concatenate([bs_ref[c, :, pl.ds(past - wb + c0, w)] for c in cols], axis=0)

        def win_mask(c0, w, is_new):
            dist = pos4 - (past - wb + c0 + lax.broadcasted_iota(I32, (1, w), 1))
            valid = (dist >= 0) & (dist < WINDOW)
            return valid & new_mask if is_new else valid

        _sample_scores(q, _interleaved(wk_ref, A_KV, g),
                       z_ref[:, C_KW + g * HEAD_DIM:C_KW + (g + 1) * HEAD_DIM],
                       win_bias, win_mask, sw_ref, wb, wb)
        sw_ref[...] = _sample_softmax(sw_ref)
        o_win = _sample_pv(sw_ref, _interleaved(wv_ref, A_KV, g),
                           z_ref[:, C_VW + g * HEAD_DIM:C_VW + (g + 1) * HEAD_DIM], wb, wb)
        if not waited:
            wait(0, slot)
            wait(1, slot)
            waited = True

        def slc_bias(c0, w, cols=cols):
            return jnp.concatenate([bs_ref[c, :, pl.ds(c0, w)] for c in cols], axis=0)

        for kk in range((past + 128) // 128):
            chosen_ref[:, kk * 128:(kk + 1) * 128] = jnp.where(
                first_half, sel[:, 2 * kk:2 * kk + 1], sel[:, 2 * kk + 1:2 * kk + 2])

        def slc_mask(c0, w, is_new):
            chosen = jnp.concatenate([chosen_ref[:, pl.ds(c0, w)] > 0.5] * A_GROUP, axis=0)
            return chosen & new_mask if is_new else chosen

        _sample_scores(q, _interleaved(kbuf, A_KV, g, row0),
                       z_ref[:, C_KS + g * HEAD_DIM:C_KS + (g + 1) * HEAD_DIM],
                       slc_bias, slc_mask, s_ref, past, ch)
        s_ref[...] = _sample_softmax(s_ref)
        o_slc = _sample_pv(s_ref, _interleaved(vbuf, A_KV, g, row0),
                           z_ref[:, C_VS + g * HEAD_DIM:C_VS + (g + 1) * HEAD_DIM], past, ch)
        for r in range(A_GROUP):
            h = g * A_GROUP + r
            c = T_GA + h * N_GATES
            rows = slice(r * nq, (r + 1) * nq)
            o_ref[:, h * HEAD_DIM:(h + 1) * HEAD_DIM] = (
                gates[:, c:c + 1] * o_cmp[rows] + gates[:, c + 1:c + 2] * o_slc[rows]
                + gates[:, c + 2:c + 3] * o_win[rows])


def nsa_sample(zs, kv_cmp, pool_ks, pool_vs, win_k, win_v, bs, bc, page_table, ch=1024):
    bd, n_pages = page_table.shape
    nq = zs.shape[0] // bd
    past = n_pages * PAGE
    mc = kv_cmp.shape[3]
    wrows = win_k.shape[0] // bd
    wb = wrows // A_KV
    win_spec = pl.BlockSpec((wrows, HEAD_DIM), lambda i, pt: (i, 0))
    buf = pltpu.VMEM((2 * past * A_KV, HEAD_DIM), F32)
    return pl.pallas_call(
        functools.partial(_nsa_sample_kernel, n_pages=n_pages, ch=ch),
        grid_spec=pltpu.PrefetchScalarGridSpec(
            num_scalar_prefetch=1, grid=(bd,),
            in_specs=[pl.BlockSpec((nq, zs.shape[1]), lambda i, pt: (i, 0)),
                      pl.BlockSpec((1, 1, A_KV, mc, HEAD_DIM), lambda i, pt: (i, 0, 0, 0, 0)),
                      pl.BlockSpec((1, 1, A_KV, mc, HEAD_DIM), lambda i, pt: (i, 1, 0, 0, 0)),
                      pl.BlockSpec(memory_space=pl.ANY), pl.BlockSpec(memory_space=pl.ANY),
                      win_spec, win_spec,
                      pl.BlockSpec((A_HEADS,) + bs.shape[1:], lambda i, pt: (0, 0, 0)),
                      pl.BlockSpec(bc.shape, lambda i, pt: (0, 0, 0))],
            out_specs=pl.BlockSpec((nq, 1024), lambda i, pt: (i, 0)),
            scratch_shapes=[buf, buf,
                            pltpu.VMEM((A_GROUP * nq, past + 128), F32),
                            pltpu.VMEM((A_GROUP * nq, wb + 128), F32),
                            pltpu.VMEM((nq, past + 128), F32),
                            pltpu.SemaphoreType.DMA((2, 2))]),
        out_shape=jax.ShapeDtypeStruct((bd * nq, 1024), F32),
        compiler_params=_cparams("arbitrary"),
        name="nsa_sample",
    )(page_table, zs, kv_cmp, kv_cmp, pool_ks, pool_vs, win_k, win_v, bs, bc)


def _dsa_sample_kernel(pt_ref, z_ref, pk_ref, pv_ref, pi_ref, bs_ref, o_ref,
                       kbuf, vbuf, ibuf, s_ref, sc_ref, key_ref, sel_ref, sem, *, n_pages, ch, topk, nbits):
    past = n_pages * PAGE
    nq = z_ref.shape[0]
    start, wait = _page_gather(pt_ref, n_pages, ((pk_ref, kbuf, PAGE * B_KV), (pv_ref, vbuf, PAGE * B_KV),
                                                 (pi_ref, ibuf, IDX_DIM)), sem)
    slot = _prefetch(pl.program_id(0), pl.num_programs(0), start)
    row0 = slot * past * B_KV
    qi = jnp.concatenate([z_ref[:, C_QI + h * IDX_DIM:C_QI + (h + 1) * IDX_DIM] for h in range(IDX_HEADS)],
                         axis=0).astype(MXU_DT)
    wi = z_ref[:, C_TAIL + T_WI:C_TAIL + T_WI + IDX_HEADS]
    wait(2, slot)

    def index_page(p, _):
        kt = ibuf[pl.ds(pl.multiple_of((slot * n_pages + p) * IDX_DIM, IDX_DIM), IDX_DIM), :]
        s_ref[:, pl.ds(pl.multiple_of(p * PAGE, PAGE), PAGE)] = _mm(qi, kt.astype(MXU_DT))
        return 0

    lax.fori_loop(0, n_pages, index_page, 0, unroll=8)
    ki_new = _pad_rows(z_ref[:, C_TAIL + T_KI:C_TAIL + T_KI + IDX_DIM], 128).astype(MXU_DT)
    s_ref[:, past:past + 128] = _nt(qi, ki_new)
    rel = jnp.maximum(s_ref[...], 0.0)
    score = sum(rel[h * nq:(h + 1) * nq] * wi[:, h:h + 1] for h in range(IDX_HEADS))
    score = score * (IDX_DIM ** -0.5 * IDX_HEADS ** -0.5)
    new_j = lax.broadcasted_iota(I32, score.shape, 1) - past
    causal = (new_j < 0) | ((new_j <= lax.broadcasted_iota(I32, score.shape, 0)) & (new_j < nq))
    key_ref[...] = _sort_key(jnp.where(causal, score, NEG))
    _topk_madd(key_ref, sel_ref, lambda c0: causal, 1, score.shape[1], topk, nbits)

    wait(0, slot)
    wait(1, slot)
    for g in range(B_KV):
        cols = [g * B_GROUP + r for r in range(B_GROUP)]
        q = _stack_heads(z_ref, C_QB + g * B_GROUP * HEAD_DIM, B_GROUP).astype(MXU_DT)

        def bias(c0, w, cols=cols):
            return jnp.concatenate([bs_ref[c, :, pl.ds(c0, w)] for c in cols], axis=0)

        def mask(c0, w, is_new):
            return jnp.concatenate([sel_ref[:, pl.ds(c0, w)] > 0.5 * NEG] * B_GROUP, axis=0)

        _sample_scores(q, _interleaved(kbuf, B_KV, g, row0),
                       z_ref[:, C_KB + g * HEAD_DIM:C_KB + (g + 1) * HEAD_DIM], bias, mask, sc_ref, past, ch)
        sc_ref[...] = _sample_softmax(sc_ref)
        o = _sample_pv(sc_ref, _interleaved(vbuf, B_KV, g, row0),
                       z_ref[:, C_VB + g * HEAD_DIM:C_VB + (g + 1) * HEAD_DIM], past, ch)
        for r in range(B_GROUP):
            h = g * B_GROUP + r
            o_ref[:, h * HEAD_DIM:(h + 1) * HEAD_DIM] = o[r * nq:(r + 1) * nq]


def dsa_sample(zs, pool_k, pool_v, pool_i, bs, page_table, ch=1024):
    bd, n_pages = page_table.shape
    nq = zs.shape[0] // bd
    past = n_pages * PAGE
    lp = past + 128
    topk = min(DSA_TOPK, (past + nq) // 4)
    return pl.pallas_call(
        functools.partial(_dsa_sample_kernel, n_pages=n_pages, ch=ch, topk=topk, nbits=int(lp).bit_length()),
        grid_spec=pltpu.PrefetchScalarGridSpec(
            num_scalar_prefetch=1, grid=(bd,),
            in_specs=[pl.BlockSpec((nq, zs.shape[1]), lambda i, pt: (i, 0)),
                      pl.BlockSpec(memory_space=pl.ANY), pl.BlockSpec(memory_space=pl.ANY),
                      pl.BlockSpec(memory_space=pl.ANY),
                      pl.BlockSpec((B_HEADS,) + bs.shape[1:], lambda i, pt: (1, 0, 0))],
            out_specs=pl.BlockSpec((nq, 1024), lambda i, pt: (i, 0)),
            scratch_shapes=[pltpu.VMEM((2 * past * B_KV, HEAD_DIM), F32), pltpu.VMEM((2 * past * B_KV, HEAD_DIM), F32),
                            pltpu.VMEM((2 * n_pages * IDX_DIM, PAGE), F32),
                            pltpu.VMEM((IDX_HEADS * nq, lp), F32), pltpu.VMEM((B_GROUP * nq, lp), F32),
                            pltpu.VMEM((nq, lp), I32), pltpu.VMEM((nq, lp), F32),
                            pltpu.SemaphoreType.DMA((3, 2))]),
        out_shape=jax.ShapeDtypeStruct((bd * nq, 1024), F32),
        compiler_params=_cparams("arbitrary"),
        name="dsa_sample",
    )(page_table, zs, pool_k, pool_v, pool_i, bs)


def _diff_sample_kernel(pt_ref, q_ref, kn_ref, vn_ref, pk_ref, pv_ref, bs_ref, lam_ref, hn_ref, o_ref,
                        kbuf, vbuf, sem, *, n_pages, cp):
    b = pl.program_id(0)
    nb = pl.num_programs(0)
    nq = q_ref.shape[0]
    pieces = C_KV * 2
    page_rows = PAGE * pieces
    slot_rows = cp * page_rows
    ch = cp * PAGE
    n_ch = n_pages // cp
    past = n_pages * PAGE
    rows = C_GROUP * nq

    def copies(bb, c, slot):
        out = []
        for i in range(cp):
            pg = pt_ref[bb, c * cp + i]
            dst = pl.ds(pl.multiple_of(slot * slot_rows + i * page_rows, page_rows), page_rows)
            out.append(pltpu.make_async_copy(_page_rows(pk_ref, page_rows)(pg), kbuf.at[dst], sem.at[0, slot]))
            out.append(pltpu.make_async_copy(_page_rows(pv_ref, page_rows)(pg), vbuf.at[dst], sem.at[1, slot]))
        return out

    @pl.when(b == 0)
    def _():
        for cpy in copies(0, 0, 0):
            cpy.start()

    qs = [jnp.concatenate([q_ref[:, ((g * C_GROUP + r) * 2 + m) * HEAD_DIM:((g * C_GROUP + r) * 2 + m + 1) * HEAD_DIM]
                           for r in range(C_GROUP)], axis=0).astype(MXU_DT)
          for g in range(C_KV) for m in range(2)]

    def update(carry, k_fn, v_fn, c0, w, mask):
        m_all, l_all, acc_all = carry
        new_m, new_l, new_acc = [], [], []
        for g in range(C_KV):
            ps, alphas = [], []
            for m in range(2):
                gm = g * 2 + m
                rs = slice(gm * rows, (gm + 1) * rows)
                bias = jnp.concatenate([bs_ref[m * C_HEADS + g * C_GROUP + r, :, pl.ds(c0, w)]
                                        for r in range(C_GROUP)], axis=0)
                s = _nt(qs[gm], k_fn(g, m).astype(MXU_DT)) * SCALE + bias
                if mask is not None:
                    s = jnp.where(mask, s, NEG)
                mn = jnp.maximum(m_all[rs], jnp.max(s, axis=-1, keepdims=True))
                p = jnp.exp(s - mn)
                if mask is not None:
                    p = jnp.where(mask, p, 0.0)
                a = jnp.exp(m_all[rs] - mn)
                new_m.append(mn)
                new_l.append(a * l_all[rs] + jnp.sum(p, axis=-1, keepdims=True))
                ps.append(p)
                alphas.append(a)
            pst = jnp.concatenate(ps, axis=0).astype(MXU_DT)
            pv = jnp.concatenate([_mm(pst, v_fn(g, h).astype(MXU_DT)) for h in range(2)], axis=1)
            for m in range(2):
                rs = slice((g * 2 + m) * rows, (g * 2 + m + 1) * rows)
                new_acc.append(alphas[m] * acc_all[rs] + pv[m * rows:(m + 1) * rows])
        return (jnp.concatenate(new_m, axis=0), jnp.concatenate(new_l, axis=0),
                jnp.concatenate(new_acc, axis=0))

    def chunk(c, carry):
        slot = c % 2
        for cpy in copies(b, c, slot):
            cpy.wait()

        @pl.when(c + 1 < n_ch)
        def _():
            for cpy in copies(b, c + 1, 1 - slot):
                cpy.start()

        @pl.when((c + 1 == n_ch) & (b + 1 < nb))
        def _():
            for cpy in copies(b + 1, 0, 1 - slot):
                cpy.start()

        base = slot * slot_rows
        return update(carry,
                      lambda g, m: kbuf[pl.ds(base + g * 2 + m, ch, stride=pieces), :],
                      lambda g, h: vbuf[pl.ds(base + h * C_KV + g, ch, stride=pieces), :],
                      pl.multiple_of(c * ch, ch), ch, None)

    n_rows = pieces * rows
    carry = (jnp.full((n_rows, 1), NEG, F32), jnp.zeros((n_rows, 1), F32), jnp.zeros((n_rows, C_VDIM), F32))
    carry = lax.fori_loop(0, n_ch, chunk, carry)
    _, l_all, acc_all = update(
        carry,
        lambda g, m: _pad_rows(kn_ref[:, (g * 2 + m) * HEAD_DIM:(g * 2 + m + 1) * HEAD_DIM], 128),
        lambda g, h: _pad_rows(vn_ref[:, g * C_VDIM + h * HEAD_DIM:g * C_VDIM + (h + 1) * HEAD_DIM], 128),
        past, 128, _new_key_mask(nq, C_GROUP))
    o_all = acc_all / l_all
    lam = _diff_lambda(lam_ref)
    for g in range(C_KV):
        r0 = g * 2 * rows
        o = _diff_finish(o_all[r0:r0 + rows] - lam * o_all[r0 + rows:r0 + 2 * rows], hn_ref)
        for r in range(C_GROUP):
            col = (g * C_GROUP + r) * C_VDIM
            o_ref[:, col:col + C_VDIM] = o[r * nq:(r + 1) * nq]


def diff_sample(z1s, pool_k, pool_v, bs, lam_vecs, head_norm, page_table, cp=16):
    bd, n_pages = page_table.shape
    nq = z1s.shape[0] // bd
    assert n_pages % (2 * cp) == 0
    slot_rows = cp * PAGE * C_KV * 2
    q_cols = C_HEADS * 2 * HEAD_DIM
    kv_cols = C_KV * C_VDIM
    return pl.pallas_call(
        functools.partial(_diff_sample_kernel, n_pages=n_pages, cp=cp),
        grid_spec=pltpu.PrefetchScalarGridSpec(
            num_scalar_prefetch=1, grid=(bd,),
            in_specs=[pl.BlockSpec((nq, q_cols), lambda i, pt: (i, 0)),
                      pl.BlockSpec((nq, kv_cols), lambda i, pt: (i, q_cols // kv_cols)),
                      pl.BlockSpec((nq, kv_cols), lambda i, pt: (i, q_cols // kv_cols + 1)),
                      pl.BlockSpec(memory_space=pl.ANY), pl.BlockSpec(memory_space=pl.ANY),
                      pl.BlockSpec(bs.shape, lambda i, pt: (0, 0, 0)),
                      pl.BlockSpec((4, HEAD_DIM), lambda i, pt: (0, 0)),
                      pl.BlockSpec((1, C_VDIM), lambda i, pt: (0, 0))],
            out_specs=pl.BlockSpec((nq, C_HEADS * C_VDIM), lambda i, pt: (i, 0)),
            scratch_shapes=[pltpu.VMEM((2 * slot_rows, HEAD_DIM), F32), pltpu.VMEM((2 * slot_rows, HEAD_DIM), F32),
                            pltpu.SemaphoreType.DMA((2, 2))]),
        out_shape=jax.ShapeDtypeStruct((bd * nq, C_HEADS * C_VDIM), F32),
        compiler_params=_cparams("arbitrary"),
        name="diff_sample",
    )(page_table, z1s, z1s, z1s, pool_k, pool_v, bs, lam_vecs, head_norm.reshape(1, C_VDIM))


def _row_tile(rows, cap=1024):
    tm = min(rows, cap)
    assert rows % tm == 0
    return tm


def _reorder_l0_weight(w):
    sizes = (A_HEADS * HEAD_DIM,) + (A_KV * HEAD_DIM,) * 6 + (
        N_GATES * A_HEADS, B_HEADS * HEAD_DIM, B_KV * HEAD_DIM, B_KV * HEAD_DIM,
        IDX_HEADS * IDX_DIM, IDX_DIM, IDX_HEADS)
    offs = [0]
    for s in sizes:
        offs.append(offs[-1] + s)
    piece = lambda i, j=None: w[:, offs[i]:offs[(i if j is None else j) + 1]]
    qa, six, ga, qb, kvb, qi, ki, wi = piece(0), piece(1, 6), piece(7), piece(8), piece(9, 10), piece(11), \
        piece(12), piece(13)
    pad = jnp.zeros((w.shape[0], L0_COLS - offs[-1]), w.dtype)
    return jnp.concatenate([qa, qb, six, kvb, qi, ki, ga, wi, pad], axis=1).astype(MXU_DT)


def _compress_weights(pe, w1, w2):
    half = CMP_STRIDE * HEAD_DIM
    w1 = w1.reshape(2, half, CMP_HIDDEN).astype(MXU_DT)
    pe_rows = jnp.zeros((16, half), F32).at[0:2].set(pe.reshape(2, half))
    return w1[0], w1[1], w2.astype(MXU_DT), pe_rows


def kernel(x_prompt, x_sample, cache_l0_nsa_cmp_k, cache_l0_nsa_cmp_v, cache_l0_nsa_slc_k, cache_l0_nsa_slc_v, state_l0_nsa_win_k, state_l0_nsa_win_v, cache_l0_dsa_k, cache_l0_dsa_v, cache_l0_dsa_idx_k, cache_l1_diff_k, cache_l1_diff_v, page_table, rel_bias, attn_norm, mlp_norm, mlp_w1, mlp_w2, l0_w_in, l0_w_out, l0_cmp_pe_k, l0_cmp_w1_k, l0_cmp_w2_k, l0_cmp_pe_v, l0_cmp_w1_v, l0_cmp_w2_v, l1_w_in, l1_w_out, l1_lambda_q1, l1_lambda_k1, l1_lambda_q2, l1_lambda_k2, l1_head_norm, final_norm):
    n, t_len, d = x_prompt.shape
    bd, nq, _ = x_sample.shape
    n_pool = cache_l0_nsa_cmp_k.shape[0]
    n_pages = page_table.shape[1]
    past = n_pages * PAGE
    lp = past + 128
    kv_w = A_KV * HEAD_DIM
    assert t_len % CHUNK == 0 and t_len >= WINDOW + QB and nq <= 8
    assert state_l0_nsa_win_k.shape[1] == min(WINDOW, past)

    xp = x_prompt.reshape(n * t_len, d)
    xs = x_sample.reshape(bd * nq, d)
    tmp, tms = _row_tile(xp.shape[0]), _row_tile(xs.shape[0])
    w0 = _reorder_l0_weight(l0_w_in)
    cw = [jnp.stack(pair) for pair in zip(_compress_weights(l0_cmp_pe_k, l0_cmp_w1_k, l0_cmp_w2_k),
                                          _compress_weights(l0_cmp_pe_v, l0_cmp_w1_v, l0_cmp_w2_v))]
    lam_vecs = jnp.stack([l1_lambda_q1, l1_lambda_k1, l1_lambda_q2, l1_lambda_k2])
    bf = lambda a: a.astype(MXU_DT)

    tp, bs = bias_tiles(rel_bias, past, nq, lp)
    bc_p, bc_s = bias_cmp(rel_bias, t_len, t_len // CMP_STRIDE, past, nq, past // CMP_STRIDE)

    zp = norm_proj(xp, attn_norm[0], w0, tmp, 768)
    zs = norm_proj(xs, attn_norm[0], w0, tms, 768)
    cut = lambda z, c, w: z[:, c:c + w]
    p_rows = {name: cut(zp, c, kv_w) for name, c in
              (("kc", C_KC), ("vc", C_VC), ("ks", C_KS), ("vs", C_VS), ("kw", C_KW), ("vw", C_VW),
               ("kb", C_KB), ("vb", C_VB))}
    s_rows = {name: cut(zs, c, kv_w) for name, c in
              (("kc", C_KC), ("vc", C_VC), ("ks", C_KS), ("vs", C_VS), ("kw", C_KW), ("vw", C_VW),
               ("kb", C_KB), ("vb", C_VB))}
    chunk_w = CMP_STRIDE * kv_w
    kc_p, vc_p = compress_prompt(p_rows["kc"].reshape(n, t_len // CMP_STRIDE, chunk_w),
                                 p_rows["vc"].reshape(n, t_len // CMP_STRIDE, chunk_w), cw)
    lanes = lambda a: a.reshape(-1, HEAD_DIM)
    kv_cmp_s = compress_sample(lanes(cache_l0_nsa_cmp_k), lanes(cache_l0_nsa_cmp_v), page_table, cw)
    oa_p = nsa_prompt(zp, kc_p, vc_p, tp, bc_p, rel_bias, n, t_len)
    ob_p = dsa_prompt(zp, tp, rel_bias, n, t_len)
    wb = state_l0_nsa_win_k.shape[1]
    oa_s = nsa_sample(zs, kv_cmp_s, lanes(cache_l0_nsa_slc_k), lanes(cache_l0_nsa_slc_v),
                      lanes(state_l0_nsa_win_k), lanes(state_l0_nsa_win_v), bs, bc_s, page_table)
    ob_s = dsa_sample(zs, lanes(cache_l0_dsa_k), lanes(cache_l0_dsa_v),
                      jnp.swapaxes(cache_l0_dsa_idx_k, 1, 2).reshape(-1, PAGE), bs, page_table)
    w_out0 = bf(l0_w_out)
    w1_0, w2_0 = bf(mlp_w1[0]), bf(mlp_w2[0])
    xp = out_proj(xp, [oa_p, ob_p], w_out0, tmp, 1024)
    xs = out_proj(xs, [oa_s, ob_s], w_out0, tms, 1024)
    xp = mlp(xp, mlp_norm[0], w1_0, w2_0, final_norm, tmp, 512, False)
    xs = mlp(xs, mlp_norm[0], w1_0, w2_0, final_norm, tms, 512, False)

    w_in1 = bf(l1_w_in)
    z1p = norm_proj(xp, attn_norm[1], w_in1, tmp, 1024)
    z1s = norm_proj(xs, attn_norm[1], w_in1, tms, 1024)
    o1_p = diff_prompt(z1p, tp, rel_bias, lam_vecs, l1_head_norm, n, t_len)
    v_halves = cache_l1_diff_v.reshape(n_pool, PAGE, C_KV, 2, HEAD_DIM).transpose(0, 1, 3, 2, 4)
    o1_s = diff_sample(z1s, lanes(cache_l1_diff_k), lanes(v_halves), bs, lam_vecs, l1_head_norm, page_table)
    w_out1 = bf(l1_w_out)
    w1_1, w2_1 = bf(mlp_w1[1]), bf(mlp_w2[1])
    xp = out_proj(xp, [o1_p], w_out1, tmp, 1024)
    xs = out_proj(xs, [o1_s], w_out1, tms, 1024)
    y_prompt = mlp(xp, mlp_norm[1], w1_1, w2_1, final_norm, tmp, 512, True).reshape(n, t_len, d)
    y_sample = mlp(xs, mlp_norm[1], w1_1, w2_1, final_norm, tms, 512, True).reshape(bd, nq, d)

    row4 = lambda a, b: a.reshape(b, -1, A_KV, HEAD_DIM)
    win = min(WINDOW, t_len)
    outs = [y_prompt, y_sample]
    for name in ("kc", "vc", "ks", "vs"):
        outs += [row4(p_rows[name], n), row4(s_rows[name], bd)]
    for name, state in (("kw", state_l0_nsa_win_k), ("vw", state_l0_nsa_win_v)):
        outs += [row4(p_rows[name], n)[:, t_len - win:],
                 jnp.concatenate([state, row4(s_rows[name], bd)], axis=1)[:, -wb:]]
    for name in ("kb", "vb"):
        outs += [row4(p_rows[name], n), row4(s_rows[name], bd)]
    outs += [cut(zp, C_TAIL + T_KI, IDX_DIM).reshape(n, t_len, IDX_DIM),
             cut(zs, C_TAIL + T_KI, IDX_DIM).reshape(bd, nq, IDX_DIM)]
    k_cols, v_cols = C_KV * 2 * HEAD_DIM, C_KV * C_VDIM
    q_cols = C_HEADS * 2 * HEAD_DIM
    outs += [cut(z1p, q_cols, k_cols).reshape(n, t_len, C_KV, 2, HEAD_DIM),
             cut(z1s, q_cols, k_cols).reshape(bd, nq, C_KV, 2, HEAD_DIM),
             cut(z1p, q_cols + k_cols, v_cols).reshape(n, t_len, C_KV, C_VDIM),
             cut(z1s, q_cols + k_cols, v_cols).reshape(bd, nq, C_KV, C_VDIM)]
    return tuple(outs)
```

```python
import functools
import math

import jax
import jax.numpy as jnp
from jax import lax
from jax.experimental import pallas as pl
from jax.experimental.pallas import tpu as pltpu

F32 = jnp.float32
I32 = jnp.int32
MXU_DT = jnp.bfloat16

HEAD_DIM = 128
A_HEADS, A_KV, A_GROUP = 8, 2, 4
B_HEADS, B_KV, B_GROUP = 8, 2, 4
C_HEADS, C_KV, C_GROUP, C_VDIM = 8, 4, 2, 256
CMP_STRIDE, CMP_BLOCK, CMP_HIDDEN = 16, 32, 256
SEL_BLOCK, N_SEL_BLOCKS, WINDOW, N_GATES = 64, 16, 512, 3
IDX_HEADS, IDX_DIM, DSA_TOPK = 4, 64, 256
NUM_BUCKETS, MAX_DISTANCE = 32, 128
LAMBDA_INIT = 0.8 - 0.6 * math.exp(-0.3 * 1)
RMS_EPS = 1e-6
NEG = -1e30
SCALE = HEAD_DIM ** -0.5
QB = 128
CHUNK = 512
PAGE = 128
PAGE_PITCH = PAGE * A_KV + 8
assert QB >= MAX_DISTANCE and WINDOW % QB == 0 and WINDOW >= 2 * QB and 2 * SEL_BLOCK == QB

C_QA, C_QB, C_KC, C_VC, C_KS, C_VS, C_KW, C_VW, C_KB, C_VB, C_QI, C_TAIL = (
    0, 1024, 2048, 2304, 2560, 2816, 3072, 3328, 3584, 3840, 4096, 4352)
T_KI, T_GA, T_WI = 0, 64, 88
L0_COLS = 4608
VMEM_LIMIT = 56 * 1024 * 1024


def _cparams(*sem):
    return pltpu.CompilerParams(dimension_semantics=sem, vmem_limit_bytes=VMEM_LIMIT)


def _nt(a, b):
    return lax.dot_general(a, b, (((1,), (1,)), ((), ())), preferred_element_type=F32)


def _mm(a, b):
    return jnp.dot(a, b, preferred_element_type=F32)


def _rms(x, g):
    return x * lax.rsqrt(jnp.mean(x * x, axis=-1, keepdims=True) + RMS_EPS) * g


def _norm_proj_kernel(x_ref, g_ref, w_ref, o_ref, xn_ref):
    @pl.when(pl.program_id(1) == 0)
    def _():
        xn_ref[...] = _rms(x_ref[...], g_ref[...]).astype(xn_ref.dtype)

    o_ref[...] = _mm(xn_ref[...], w_ref[...])


def norm_proj(x, gain, w, tm, tn):
    rows, d = x.shape
    n = w.shape[1]
    return pl.pallas_call(
        _norm_proj_kernel,
        grid=(rows // tm, n // tn),
        in_specs=[pl.BlockSpec((tm, d), lambda i, j: (i, 0)),
                  pl.BlockSpec((1, d), lambda i, j: (0, 0)),
                  pl.BlockSpec((d, tn), lambda i, j: (0, j))],
        out_specs=pl.BlockSpec((tm, tn), lambda i, j: (i, j)),
        out_shape=jax.ShapeDtypeStruct((rows, n), F32),
        scratch_shapes=[pltpu.VMEM((tm, d), MXU_DT)],
        compiler_params=_cparams("parallel", "arbitrary"),
        name="norm_proj",
    )(x, gain.reshape(1, d), w)


def _out_proj_kernel(*refs, n_in):
    x_ref, o_refs, w_refs, y_ref = refs[0], refs[1:1 + n_in], refs[1 + n_in:1 + 2 * n_in], refs[-1]
    acc = x_ref[...]
    for o_ref, w_ref in zip(o_refs, w_refs):
        acc = acc + _mm(o_ref[...].astype(MXU_DT), w_ref[...])
    y_ref[...] = acc


def out_proj(x, outs, w, tm, tn):
    rows, d = x.shape
    o_specs, w_specs, row0 = [], [], 0
    for o in outs:
        k = o.shape[1]
        o_specs.append(pl.BlockSpec((tm, k), lambda i, j: (i, 0)))
        w_specs.append(pl.BlockSpec((k, tn), lambda i, j, rb=row0 // k: (rb, j)))
        row0 += k
    return pl.pallas_call(
        functools.partial(_out_proj_kernel, n_in=len(outs)),
        grid=(rows // tm, d // tn),
        in_specs=[pl.BlockSpec((tm, tn), lambda i, j: (i, j))] + o_specs + w_specs,
        out_specs=pl.BlockSpec((tm, tn), lambda i, j: (i, j)),
        out_shape=jax.ShapeDtypeStruct((rows, d), F32),
        compiler_params=_cparams("parallel", "arbitrary"),
        name="out_proj",
    )(x, *outs, *([w] * len(outs)))


def _mlp_kernel(x_ref, g_ref, w1_ref, w2_ref, gf_ref, y_ref, xn_ref, *, final_norm):
    j = pl.program_id(1)

    @pl.when(j == 0)
    def _():
        x = x_ref[...]
        xn_ref[...] = _rms(x, g_ref[...]).astype(xn_ref.dtype)
        y_ref[...] = x

    h = jnp.square(jnp.maximum(_mm(xn_ref[...], w1_ref[...]), 0.0))
    y_ref[...] += _mm(h.astype(w2_ref.dtype), w2_ref[...])

    if final_norm:
        @pl.when(j == pl.num_programs(1) - 1)
        def _():
            y_ref[...] = _rms(y_ref[...], gf_ref[...])


def mlp(x, gain, w1, w2, final_gain, tm, tf, final_norm):
    rows, d = x.shape
    ff = w1.shape[1]
    return pl.pallas_call(
        functools.partial(_mlp_kernel, final_norm=final_norm),
        grid=(rows // tm, ff // tf),
        in_specs=[pl.BlockSpec((tm, d), lambda i, j: (i, 0)),
                  pl.BlockSpec((1, d), lambda i, j: (0, 0)),
                  pl.BlockSpec((d, tf), lambda i, j: (0, j)),
                  pl.BlockSpec((tf, d), lambda i, j: (j, 0)),
                  pl.BlockSpec((1, d), lambda i, j: (0, 0))],
        out_specs=pl.BlockSpec((tm, d), lambda i, j: (i, 0)),
        out_shape=jax.ShapeDtypeStruct((rows, d), F32),
        scratch_shapes=[pltpu.VMEM((tm, d), MXU_DT)],
        compiler_params=_cparams("parallel", "arbitrary"),
        name="mlp",
    )(x, gain.reshape(1, d), w1, w2, final_gain.reshape(1, d))


def _bucket(dist):
    n = jnp.maximum(dist, 0)
    max_exact = NUM_BUCKETS // 2
    nf = jnp.maximum(n, 1).astype(F32)
    large = max_exact + (jnp.log(nf / max_exact) / math.log(MAX_DISTANCE / max_exact)
                         * (NUM_BUCKETS - max_exact)).astype(I32)
    large = jnp.minimum(large, NUM_BUCKETS - 1)
    return jnp.where(n < max_exact, n, large)


def _lookup(tbl_ref, col, buckets):
    def body(b, accs):
        v = tbl_ref[b, col]
        return tuple(jnp.where(bk == b, v, acc) for bk, acc in zip(buckets, accs))
    return lax.fori_loop(0, NUM_BUCKETS, body, tuple(jnp.zeros(bk.shape, F32) for bk in buckets))


def _bias_tiles_kernel(tbl_ref, tp_ref, bs_ref, *, q0):
    h = pl.program_id(0)
    t = lax.broadcasted_iota(I32, (QB, QB), 0)
    k = lax.broadcasted_iota(I32, (QB, QB), 1)
    ts = lax.broadcasted_iota(I32, bs_ref.shape[1:], 0)
    ks = lax.broadcasted_iota(I32, bs_ref.shape[1:], 1)
    d0, d1, ds = _lookup(tbl_ref, h, (_bucket(t - k), _bucket(QB + t - k), _bucket(q0 + ts - ks)))
    tp_ref[0, 0] = d0
    tp_ref[0, 1] = d1
    bs_ref[0] = ds


def bias_tiles(rel_bias, q0, n_q, lp):
    nh = rel_bias.shape[1]
    return pl.pallas_call(
        functools.partial(_bias_tiles_kernel, q0=q0),
        grid=(nh,),
        in_specs=[pl.BlockSpec(memory_space=pltpu.SMEM)],
        out_specs=[pl.BlockSpec((1, 2, QB, QB), lambda h: (h, 0, 0, 0)),
                   pl.BlockSpec((1, n_q, lp), lambda h: (h, 0, 0))],
        out_shape=[jax.ShapeDtypeStruct((nh, 2, QB, QB), F32),
                   jax.ShapeDtypeStruct((nh, n_q, lp), F32)],
        compiler_params=_cparams("arbitrary"),
        name="bias_tiles",
    )(rel_bias)


def _bias_cmp_kernel(tbl_ref, bp_ref, bs_ref, *, q0):
    h = pl.program_id(0)
    tp = lax.broadcasted_iota(I32, bp_ref.shape[1:], 0)
    cp = lax.broadcasted_iota(I32, bp_ref.shape[1:], 1)
    ts = lax.broadcasted_iota(I32, bs_ref.shape[1:], 0)
    cs = lax.broadcasted_iota(I32, bs_ref.shape[1:], 1)
    end = CMP_BLOCK - 1
    bp, bs = _lookup(tbl_ref, h, (_bucket(tp - (cp * CMP_STRIDE + end)),
                                  _bucket(q0 + ts - (cs * CMP_STRIDE + end))))
    bp_ref[0] = bp
    bs_ref[0] = bs


def bias_cmp(rel_bias, t_len, mc_p, q0, n_q, mc_s):
    return pl.pallas_call(
        functools.partial(_bias_cmp_kernel, q0=q0),
        grid=(A_HEADS,),
        in_specs=[pl.BlockSpec(memory_space=pltpu.SMEM)],
        out_specs=[pl.BlockSpec((1, t_len, mc_p), lambda h: (h, 0, 0)),
                   pl.BlockSpec((1, n_q, mc_s), lambda h: (h, 0, 0))],
        out_shape=[jax.ShapeDtypeStruct((A_HEADS, t_len, mc_p), F32),
                   jax.ShapeDtypeStruct((A_HEADS, n_q, mc_s), F32)],
        compiler_params=_cparams("arbitrary"),
        name="bias_cmp",
    )(rel_bias)


def _row_reduce(fn, lane_fn, x):
    tiles = [x[:, i:i + 128] for i in range(0, x.shape[1], 128)]
    return lane_fn(_tree_reduce(fn, tiles), axis=-1, keepdims=True)


def _softmax_rows(z, mask):
    z = jnp.where(mask, z, NEG)
    e = jnp.where(mask, jnp.exp(z - _row_reduce(jnp.maximum, jnp.max, z)), 0.0)
    l = _row_reduce(jnp.add, jnp.sum, e)
    return e * (1.0 / jnp.where(l > 0.0, l, 1.0))


def _gelu_tanh(x):
    return 0.5 * x * (1.0 + jnp.tanh(math.sqrt(2.0 / math.pi) * (x + 0.044715 * (x * x * x))))


def _compress(x_fn, m, w1a_ref, w1b_ref, w2_ref, pe_ref, next_fn, out_fn):
    pe = pe_ref[...].astype(MXU_DT)
    pos = _mm(pe, w1a_ref[...])[0:1] + _mm(pe, w1b_ref[...])[1:2]
    last = lax.broadcasted_iota(I32, (m, 1), 0) == m - 1
    for g in range(A_KV):
        xg = jnp.concatenate([x_fn(j, g).astype(MXU_DT) for j in range(CMP_STRIDE)], axis=1)
        hid = _gelu_tanh(_mm(xg, w1a_ref[...]) + next_fn(_mm(xg, w1b_ref[...])) + pos)
        out_fn(g, jnp.where(last, 0.0, _mm(hid.astype(MXU_DT), w2_ref[...])))


def _overlap(mc, jn, n_cmp, n_slc):
    c = lax.broadcasted_iota(I32, (mc, jn), 0)
    j = lax.broadcasted_iota(I32, (mc, jn), 1)
    ov = ((c * CMP_STRIDE < j * SEL_BLOCK + SEL_BLOCK) & (c * CMP_STRIDE + CMP_BLOCK > j * SEL_BLOCK)
          & (c < n_cmp) & (j < n_slc))
    return jnp.where(ov, 1.0, 0.0)


def _select_blocks(imp, pos, n_slc):
    jn = imp.shape[1]
    jidx = lax.broadcasted_iota(I32, (1, jn), 1)
    cur = pos // SEL_BLOCK
    forced = (jidx == 0) | (jidx == cur) | (jidx == cur - 1)
    future = jidx * SEL_BLOCK > pos
    score = jnp.where(future, -1.0, jnp.where(forced, 1e3, imp))
    score = jnp.where(jidx < n_slc, score, -2.0)

    def body(i, rank):
        col = jnp.sum(jnp.where(jidx == i, score, 0.0), axis=-1, keepdims=True)
        beats = jnp.where(col > score, 1.0, jnp.where(col == score, jnp.where(i < jidx, 1.0, 0.0), 0.0))
        return rank + beats

    rank = lax.fori_loop(0, n_slc, body, jnp.zeros(score.shape, F32), unroll=8)
    n_sel = min(N_SEL_BLOCKS, n_slc)
    return jnp.where((rank < n_sel) & (jidx < n_slc), 1.0, 0.0)


def _tree_reduce(fn, xs):
    xs = list(xs)
    while len(xs) > 1:
        xs = [fn(xs[i], xs[i + 1]) for i in range(0, len(xs) - 1, 2)] + ([xs[-1]] if len(xs) % 2 else [])
    return xs[0]


def _tree_sum(xs):
    return _tree_reduce(jnp.add, xs)


def _sort_key(s):
    bits = lax.bitcast_convert_type(jnp.where(s == 0.0, 0.0, s), I32)
    return jnp.where(bits < 0, bits ^ jnp.int32(0x7FFFFFFF), bits)


def _topk_madd(key_ref, madd_ref, valid_fn, nch, cw, k, nbits):
    n_rows = key_ref.shape[0]
    kf = jnp.float32(k)

    def count(fn):
        def body(c, acc):
            c0 = pl.multiple_of(c * cw, cw)
            hit = jnp.where(fn(c0, key_ref[:, pl.ds(c0, cw)]), 1.0, 0.0)
            return acc + _tree_sum(hit[:, i:i + 128] for i in range(0, cw, 128))
        acc = lax.fori_loop(0, nch, body, jnp.zeros((n_rows, 128), F32))
        return jnp.sum(acc, axis=-1, keepdims=True)

    int_min = jnp.int32(-2 ** 31)
    thr0 = jnp.where(count(lambda c0, key: key >= 0) >= kf, jnp.int32(0), int_min)

    def vbody(i, thr):
        cand = thr | lax.shift_left(jnp.int32(1), 30 - i)
        return jnp.where(count(lambda c0, key: key >= cand) >= kf, cand, thr)

    thr = lax.fori_loop(0, 31, vbody, thr0)
    need = kf - count(lambda c0, key: key > thr)

    def idx(c0):
        return c0 + lax.broadcasted_iota(I32, (1, cw), 1)

    def ibody(i, cut):
        cand = cut | lax.shift_left(jnp.int32(1), nbits - 1 - i)
        return jnp.where(count(lambda c0, key: (key == thr) & (idx(c0) < cand)) <= need, cand, cut)

    tied = jnp.max(count(lambda c0, key: key == thr) - need) > 0.0
    cut = lax.cond(tied, lambda: lax.fori_loop(0, nbits, ibody, jnp.zeros((n_rows, 1), I32)),
                   lambda: jnp.full((n_rows, 1), 2 ** nbits - 1, I32))

    def write(c, _):
        c0 = pl.multiple_of(c * cw, cw)
        key = key_ref[:, pl.ds(c0, cw)]
        sel = ((key > thr) | ((key == thr) & (idx(c0) < cut))) & valid_fn(c0)
        madd_ref[:, pl.ds(c0, cw)] = jnp.where(sel, 0.0, NEG)
        return 0

    lax.fori_loop(0, nch, write, 0)


def _topk_madd_t(key_ref, madd_ref, valid_fn, nch, cw, k, nbits):
    n_rows = key_ref.shape[1]
    kf = jnp.float32(k)

    def count(fn):
        def body(c, acc):
            c0 = pl.multiple_of(c * cw, cw)
            hit = jnp.where(fn(c0, key_ref[pl.ds(c0, cw), :]), 1.0, 0.0)
            return acc + _tree_sum(hit[i:i + 8] for i in range(0, cw, 8))
        acc = lax.fori_loop(0, nch, body, jnp.zeros((8, n_rows), F32))
        return jnp.sum(acc, axis=0, keepdims=True)

    int_min = jnp.int32(-2 ** 31)
    thr0 = jnp.where(count(lambda c0, key: key >= 0) >= kf, jnp.int32(0), int_min)

    def vbody(i, thr):
        cand = thr | lax.shift_left(jnp.int32(1), 30 - i)
        return jnp.where(count(lambda c0, key: key >= cand) >= kf, cand, thr)

    thr = lax.fori_loop(0, 31, vbody, thr0)
    need = kf - count(lambda c0, key: key > thr)

    def idx(c0):
        return c0 + lax.broadcasted_iota(I32, (cw, 1), 0)

    def ibody(i, cut):
        cand = cut | lax.shift_left(jnp.int32(1), nbits - 1 - i)
        return jnp.where(count(lambda c0, key: (key == thr) & (idx(c0) < cand)) <= need, cand, cut)

    tied = jnp.max(count(lambda c0, key: key == thr) - need) > 0.0
    cut = lax.cond(tied, lambda: lax.fori_loop(0, nbits, ibody, jnp.zeros((1, n_rows), I32)),
                   lambda: jnp.full((1, n_rows), 2 ** nbits - 1, I32))

    def write(c, _):
        c0 = pl.multiple_of(c * cw, cw)
        key = key_ref[pl.ds(c0, cw), :]
        sel = ((key > thr) | ((key == thr) & (idx(c0) < cut))) & valid_fn(c0)
        madd_ref[:, pl.ds(c0, cw)] = jnp.where(sel, 0.0, NEG).T
        return 0

    lax.fori_loop(0, nch, write, 0)


def _causal_attn(streams, dv, qb, pairs=False):
    m_rows = streams[0][0].shape[0]
    per = CHUNK // QB
    nact = qb // per + 1

    def stage(st, k0, w, bias):
        q, k_fn, _, _, _, madd_fn, s_ref = st
        s = _nt(q, k_fn(k0, w)) * SCALE + bias
        madd = madd_fn(k0, w)
        s_ref[:, pl.ds(k0, w)] = s if madd is None else s + madd

    def chunk_loop(body, init):
        n2 = nact // 2 if pairs else 0
        carry = lax.fori_loop(
            0, n2, lambda i, c: body(pl.multiple_of(i * 2 * CHUNK, 2 * CHUNK), 2 * CHUNK, c), init) if pairs else init
        return lax.fori_loop(
            0, nact - 2 * n2, lambda i, c: body(pl.multiple_of((2 * n2 + i) * CHUNK, CHUNK), CHUNK, c), carry)

    def lane_tiles(x):
        return [x[:, i:i + 128] for i in range(0, x.shape[1], 128)]

    def far(k0, w, _):
        for st in streams:
            stage(st, k0, w, st[3])
        return 0

    chunk_loop(far, 0)
    for st in streams:
        stage(st, pl.multiple_of(qb * QB, QB), QB, st[4](True))

    @pl.when(qb >= 1)
    def _():
        for st in streams:
            stage(st, pl.multiple_of((qb - 1) * QB, QB), QB, st[4](False))

    for j in range(1, per):
        @pl.when(qb % per + j < per)
        def _():
            for st in streams:
                st[6][:, pl.ds(pl.multiple_of((qb + j) * QB, QB), QB)] = jnp.full((m_rows, QB), NEG, F32)

    stat_w = 128 if pairs else 1

    def lanes_or_row(fn, lane_fn, x):
        return _tree_reduce(fn, lane_tiles(x)) if pairs else lane_fn(x, axis=-1, keepdims=True)

    def row_max(k0, w, ms):
        return tuple(jnp.maximum(m, lanes_or_row(jnp.maximum, jnp.max, st[6][:, pl.ds(k0, w)]))
                     for st, m in zip(streams, ms))

    ms = chunk_loop(row_max, tuple(jnp.full((m_rows, stat_w), NEG, F32) for _ in streams))
    ms = [jnp.max(m, axis=-1, keepdims=True) for m in ms]

    def pv(k0, w, carry):
        out = []
        for st, m, (l, acc) in zip(streams, ms, carry):
            p = jnp.exp(st[6][:, pl.ds(k0, w)] - m)
            out.append((l + lanes_or_row(jnp.add, jnp.sum, p), acc + _mm(p.astype(MXU_DT), st[2](k0, w))))
        return tuple(out)

    init = tuple((jnp.zeros((m_rows, stat_w), F32), jnp.zeros((m_rows, dv), F32)) for _ in streams)
    return [acc / jnp.sum(l, axis=-1, keepdims=True) for l, acc in chunk_loop(pv, init)]


def _causal_add(rep):
    t = lax.broadcasted_iota(I32, (QB, QB), 0)
    k = lax.broadcasted_iota(I32, (QB, QB), 1)
    return jnp.concatenate([jnp.where(k <= t, 0.0, NEG)] * rep, axis=0)


def _far_bias(tbl_ref, cols):
    return jnp.concatenate([jnp.full((QB, 1), tbl_ref[NUM_BUCKETS - 1, c], F32) for c in cols], axis=0)


def _near_bias(tp_ref, cols, diag):
    return jnp.concatenate([tp_ref[c, 0 if diag else 1] for c in cols], axis=0)


def _stack_heads(ref, col0, n):
    return jnp.concatenate([ref[:, col0 + r * HEAD_DIM:col0 + (r + 1) * HEAD_DIM] for r in range(n)], axis=0)


def _compress_prompt_kernel(xk_ref, xv_ref, w1a, w1b, w2, pe, ok_ref, ov_ref):
    lanes = lambda x_ref: (lambda j, g: x_ref[0, :, (2 * j + g) * HEAD_DIM:(2 * j + g + 1) * HEAD_DIM])
    m = xk_ref.shape[1]
    for i, (x_ref, o_ref) in enumerate(((xk_ref, ok_ref), (xv_ref, ov_ref))):
        def store(g, tokens, o_ref=o_ref):
            o_ref[0, :, g * HEAD_DIM:(g + 1) * HEAD_DIM] = tokens

        _compress(lanes(x_ref), m, w1a.at[i], w1b.at[i], w2.at[i], pe.at[i],
                  lambda y: pltpu.roll(y, m - 1, 0), store)


def _cmp_weight_specs():
    full = lambda *shape: pl.BlockSpec(shape, lambda *_: (0,) * len(shape))
    half = CMP_STRIDE * HEAD_DIM
    return [full(2, half, CMP_HIDDEN), full(2, half, CMP_HIDDEN), full(2, CMP_HIDDEN, HEAD_DIM), full(2, 16, half)]


def compress_prompt(xk, xv, cw):
    n, m, w = xk.shape
    spec = pl.BlockSpec((1, m, w), lambda i: (i, 0, 0))
    ospec = pl.BlockSpec((1, m, A_KV * HEAD_DIM), lambda i: (i, 0, 0))
    osh = jax.ShapeDtypeStruct((n, m, A_KV * HEAD_DIM), F32)
    return pl.pallas_call(
        _compress_prompt_kernel,
        grid=(n,),
        in_specs=[spec, spec] + _cmp_weight_specs(),
        out_specs=[ospec, ospec],
        out_shape=[osh, osh],
        compiler_params=_cparams("parallel"),
        name="compress_prompt",
    )(xk, xv, *cw)


def _window_attn(q, kw_ref, vw_ref, gl, qb, pos4, tbl_ref, tp_ref, cols):
    n_tiles = WINDOW // QB + 1
    width = n_tiles * QB
    lo = jnp.maximum(qb - (n_tiles - 1), 0)
    w0 = pl.multiple_of(lo * QB, QB)
    tiles = []
    for j in range(n_tiles):
        rel = qb - (lo + j)
        tiles.append(jnp.concatenate(
            [jnp.where(rel == 0, tp_ref[c, 0], jnp.where(rel == 1, tp_ref[c, 1], tbl_ref[NUM_BUCKETS - 1, c]))
             for c in cols], axis=0))
    s = _nt(q, kw_ref[pl.ds(w0, width), gl].astype(MXU_DT)) * SCALE + jnp.concatenate(tiles, axis=1)
    dist = pos4 - (w0 + lax.broadcasted_iota(I32, (1, width), 1))
    p = _softmax_rows(s, (dist >= 0) & (dist < WINDOW))
    return _mm(p.astype(MXU_DT), vw_ref[pl.ds(w0, width), gl].astype(MXU_DT))


def _nsa_prompt_kernel(tbl_ref, q_ref, tail_ref, kc_ref, vc_ref, ks_ref, vs_ref, kw_ref, vw_ref,
                       tp_ref, bc_ref, o_ref, s_ref, madd_ref, *, t_len):
    qb = pl.program_id(1)
    mc = kc_ref.shape[1]
    n_cmp = (t_len - CMP_BLOCK) // CMP_STRIDE + 1
    n_slc = -(-t_len // SEL_BLOCK)
    pos = qb * QB + lax.broadcasted_iota(I32, (QB, 1), 0)
    pos4 = jnp.concatenate([pos] * A_GROUP, axis=0)
    cidx = lax.broadcasted_iota(I32, (1, mc), 1)
    gates = jax.nn.sigmoid(tail_ref[...])
    overlap = _overlap(mc, QB, n_cmp, n_slc)
    causal_add = _causal_add(A_GROUP)
    onehot = jnp.where(lax.broadcasted_iota(I32, (QB, t_len), 0)
                       == lax.broadcasted_iota(I32, (QB, t_len), 1) // SEL_BLOCK, 1.0, 0.0).astype(MXU_DT)

    o_cmp, o_win, streams = [], [], []
    for g in range(A_KV):
        cols = [g * A_GROUP + r for r in range(A_GROUP)]
        gl = slice(g * HEAD_DIM, (g + 1) * HEAD_DIM)
        q = _stack_heads(q_ref, g * A_GROUP * HEAD_DIM, A_GROUP).astype(MXU_DT)
        lc = (_nt(q, kc_ref[0, :, gl].astype(MXU_DT)) * SCALE
              + jnp.concatenate([bc_ref[c] for c in cols], axis=0))
        p_cmp = _softmax_rows(lc, (pos4 >= cidx * CMP_STRIDE + (CMP_BLOCK - 1)) & (cidx < n_cmp))
        o_cmp.append(_mm(p_cmp.astype(MXU_DT), vc_ref[0, :, gl].astype(MXU_DT)))
        p_sum = sum(p_cmp[r * QB:(r + 1) * QB] for r in range(A_GROUP))
        imp = jnp.dot(p_sum, overlap, preferred_element_type=F32, precision=lax.Precision.HIGHEST)
        sel = _select_blocks(imp, pos, n_slc).astype(MXU_DT)
        madd_ref[g] = jnp.where(_mm(sel, onehot) > 0.5, 0.0, NEG)
        o_win.append(_window_attn(q, kw_ref, vw_ref, gl, qb, pos4, tbl_ref, tp_ref, cols))
        streams.append((
            q, lambda k0, w, gl=gl: ks_ref[pl.ds(k0, w), gl].astype(MXU_DT),
            lambda k0, w, gl=gl: vs_ref[pl.ds(k0, w), gl].astype(MXU_DT), _far_bias(tbl_ref, cols),
            lambda diag, cols=cols: _near_bias(tp_ref, cols, diag) + (causal_add if diag else 0.0),
            lambda k0, w, g=g: jnp.concatenate([madd_ref[g, :, pl.ds(k0, w)]] * A_GROUP, axis=0),
            s_ref.at[g]))
    o_slc = _causal_attn(streams, HEAD_DIM, qb)

    for h in range(A_HEADS):
        g, r = divmod(h, A_GROUP)
        c = T_GA + h * N_GATES
        rows = slice(r * QB, (r + 1) * QB)
        o = (gates[:, c:c + 1] * o_cmp[g][rows] + gates[:, c + 1:c + 2] * o_slc[g][rows]
             + gates[:, c + 2:c + 3] * o_win[g][rows])
        o_ref[:, h * HEAD_DIM:(h + 1) * HEAD_DIM] = o.astype(o_ref.dtype)


def nsa_prompt(z, k_cmp, v_cmp, tp, bc, rel_bias, n, t_len):
    nb = t_len // QB
    mc = k_cmp.shape[1]
    kv = lambda c: pl.BlockSpec((t_len, 256), lambda i, j: (i, c // 256))
    cmp_spec = pl.BlockSpec((1, mc, 256), lambda i, j: (i, 0, 0))
    return pl.pallas_call(
        functools.partial(_nsa_prompt_kernel, t_len=t_len),
        grid=(n, nb),
        in_specs=[pl.BlockSpec(memory_space=pltpu.SMEM),
                  pl.BlockSpec((QB, 1024), lambda i, j: (i * nb + j, C_QA // 1024)),
                  pl.BlockSpec((QB, 128), lambda i, j: (i * nb + j, C_TAIL // 128)),
                  cmp_spec, cmp_spec, kv(C_KS), kv(C_VS), kv(C_KW), kv(C_VW),
                  pl.BlockSpec(tp.shape, lambda i, j: (0, 0, 0, 0)),
                  pl.BlockSpec((A_HEADS, QB, mc), lambda i, j: (0, j, 0))],
        out_specs=pl.BlockSpec((QB, 1024), lambda i, j: (i * nb + j, 0)),
        out_shape=jax.ShapeDtypeStruct((n * t_len, 1024), MXU_DT),
        scratch_shapes=[pltpu.VMEM((A_KV, A_GROUP * QB, t_len), F32), pltpu.VMEM((A_KV, QB, t_len), F32)],
        compiler_params=_cparams("parallel", "arbitrary"),
        name="nsa_prompt",
    )(rel_bias, z, z, k_cmp, v_cmp, z, z, z, z, tp, bc)


def _dsa_prompt_kernel(tbl_ref, q_ref, qi_ref, tailq_ref, tailk_ref, kb_ref, vb_ref, tp_ref, o_ref,
                       key_ref, madd_ref, s_ref, *, topk, nbits):
    qb = pl.program_id(1)
    nact = qb // (CHUNK // QB) + 1
    pos = qb * QB + lax.broadcasted_iota(I32, (1, QB), 1)
    wi_t = tailq_ref[...].T[T_WI:T_WI + IDX_HEADS]
    qis = [qi_ref[:, h * IDX_DIM:(h + 1) * IDX_DIM].astype(MXU_DT) for h in range(IDX_HEADS)]

    def causal(c0):
        return c0 + lax.broadcasted_iota(I32, (CHUNK, 1), 0) <= pos

    def index_chunk(c, _):
        c0 = pl.multiple_of(c * CHUNK, CHUNK)
        ki = tailk_ref[pl.ds(c0, CHUNK), T_KI:T_KI + IDX_DIM].astype(MXU_DT)
        score = sum(jnp.maximum(_nt(ki, qis[h]), 0.0) * wi_t[h:h + 1] for h in range(IDX_HEADS))
        score = score * (IDX_DIM ** -0.5 * IDX_HEADS ** -0.5)
        key_ref[pl.ds(c0, CHUNK), :] = _sort_key(jnp.where(causal(c0), score, NEG))
        return 0

    lax.fori_loop(0, nact, index_chunk, 0)
    _topk_madd_t(key_ref, madd_ref, causal, nact, CHUNK, topk, nbits)

    streams = []
    for g in range(B_KV):
        cols = [A_HEADS + g * B_GROUP + r for r in range(B_GROUP)]
        gl = slice(g * HEAD_DIM, (g + 1) * HEAD_DIM)
        streams.append((
            _stack_heads(q_ref, g * B_GROUP * HEAD_DIM, B_GROUP).astype(MXU_DT),
            lambda k0, w, gl=gl: kb_ref[pl.ds(k0, w), gl].astype(MXU_DT),
            lambda k0, w, gl=gl: vb_ref[pl.ds(k0, w), gl].astype(MXU_DT),
            _far_bias(tbl_ref, cols), lambda diag, cols=cols: _near_bias(tp_ref, cols, diag),
            lambda k0, w: jnp.concatenate([madd_ref[:, pl.ds(k0, w)]] * B_GROUP, axis=0), s_ref.at[g]))
    outs = _causal_attn(streams, HEAD_DIM, qb)
    for h in range(B_HEADS):
        g, r = divmod(h, B_GROUP)
        o_ref[:, h * HEAD_DIM:(h + 1) * HEAD_DIM] = outs[g][r * QB:(r + 1) * QB].astype(o_ref.dtype)


def dsa_prompt(z, tp, rel_bias, n, t_len):
    nb = t_len // QB
    topk = min(DSA_TOPK, t_len // 4)
    nbits = int(t_len).bit_length()
    return pl.pallas_call(
        functools.partial(_dsa_prompt_kernel, topk=topk, nbits=nbits),
        grid=(n, nb),
        in_specs=[pl.BlockSpec(memory_space=pltpu.SMEM),
                  pl.BlockSpec((QB, 1024), lambda i, j: (i * nb + j, C_QB // 1024)),
                  pl.BlockSpec((QB, 256), lambda i, j: (i * nb + j, C_QI // 256)),
                  pl.BlockSpec((QB, 128), lambda i, j: (i * nb + j, C_TAIL // 128)),
                  pl.BlockSpec((t_len, 128), lambda i, j: (i, C_TAIL // 128)),
                  pl.BlockSpec((t_len, 256), lambda i, j: (i, C_KB // 256)),
                  pl.BlockSpec((t_len, 256), lambda i, j: (i, C_VB // 256)),
                  pl.BlockSpec(tp.shape, lambda i, j: (0, 0, 0, 0))],
        out_specs=pl.BlockSpec((QB, 1024), lambda i, j: (i * nb + j, 0)),
        out_shape=jax.ShapeDtypeStruct((n * t_len, 1024), MXU_DT),
        scratch_shapes=[pltpu.VMEM((t_len, QB), I32), pltpu.VMEM((QB, t_len), F32),
                        pltpu.VMEM((B_KV, B_GROUP * QB, t_len), F32)],
        compiler_params=_cparams("parallel", "arbitrary"),
        name="dsa_prompt",
    )(rel_bias, z, z, z, z, z, z, tp)


def _diff_lambda(lam_ref):
    v = lam_ref[...]
    e1 = jnp.exp(jnp.sum(v[0:1] * v[1:2], axis=-1, keepdims=True))
    e2 = jnp.exp(jnp.sum(v[2:3] * v[3:4], axis=-1, keepdims=True))
    return e1 - e2 + LAMBDA_INIT


def _diff_finish(o, hn_ref):
    return _rms(o, hn_ref[...]) * (1.0 - LAMBDA_INIT)


def _diff_prompt_kernel(tbl_ref, q_ref, k_ref, v_ref, tp_ref, lam_ref, hn_ref, o_ref, s_ref, *, gps):
    g0 = pl.program_id(1) * gps
    qb = pl.program_id(2)
    causal_add = _causal_add(C_GROUP)
    streams = []
    for gi in range(gps):
        for m in range(2):
            cols = [m * C_HEADS + (g0 + gi) * C_GROUP + r for r in range(C_GROUP)]
            q = jnp.concatenate(
                [q_ref[:, ((gi * C_GROUP + r) * 2 + m) * HEAD_DIM:((gi * C_GROUP + r) * 2 + m + 1) * HEAD_DIM]
                 for r in range(C_GROUP)], axis=0).astype(MXU_DT)
            kl = slice((gi * 2 + m) * HEAD_DIM, (gi * 2 + m + 1) * HEAD_DIM)
            vl = slice(gi * C_VDIM, (gi + 1) * C_VDIM)
            streams.append((
                q, lambda k0, w, kl=kl: k_ref[pl.ds(k0, w), kl].astype(MXU_DT),
                lambda k0, w, vl=vl: v_ref[pl.ds(k0, w), vl].astype(MXU_DT), _far_bias(tbl_ref, cols),
                lambda diag, cols=cols: _near_bias(tp_ref, cols, diag) + (causal_add if diag else 0.0),
                lambda k0, w: None, s_ref.at[gi * 2 + m]))
    outs = _causal_attn(streams, C_VDIM, qb, pairs=True)
    lam = _diff_lambda(lam_ref)
    for gi in range(gps):
        o = _diff_finish(outs[gi * 2] - lam * outs[gi * 2 + 1], hn_ref)
        for r in range(C_GROUP):
            col = (gi * C_GROUP + r) * C_VDIM
            o_ref[:, col:col + C_VDIM] = o[r * QB:(r + 1) * QB].astype(o_ref.dtype)


def diff_prompt(z1, tp, rel_bias, lam_vecs, head_norm, n, t_len, gps=2):
    nb = t_len // QB
    qw = gps * C_GROUP * 2 * HEAD_DIM
    kw = gps * 2 * HEAD_DIM
    q_cols, k_cols = C_HEADS * 2 * HEAD_DIM, C_KV * 2 * HEAD_DIM
    return pl.pallas_call(
        functools.partial(_diff_prompt_kernel, gps=gps),
        grid=(n, C_KV // gps, nb),
        in_specs=[pl.BlockSpec(memory_space=pltpu.SMEM),
                  pl.BlockSpec((QB, qw), lambda i, g, j: (i * nb + j, g)),
                  pl.BlockSpec((t_len, kw), lambda i, g, j: (i, q_cols // kw + g)),
                  pl.BlockSpec((t_len, kw), lambda i, g, j: (i, (q_cols + k_cols) // kw + g)),
                  pl.BlockSpec(tp.shape, lambda i, g, j: (0, 0, 0, 0)),
                  pl.BlockSpec((4, HEAD_DIM), lambda i, g, j: (0, 0)),
                  pl.BlockSpec((1, C_VDIM), lambda i, g, j: (0, 0))],
        out_specs=pl.BlockSpec((QB, qw), lambda i, g, j: (i * nb + j, g)),
        out_shape=jax.ShapeDtypeStruct((n * t_len, C_HEADS * C_VDIM), MXU_DT),
        scratch_shapes=[pltpu.VMEM((2 * gps, C_GROUP * QB, t_len), F32)],
        compiler_params=_cparams("parallel", "parallel", "arbitrary"),
        name="diff_prompt",
    )(rel_bias, z1, z1, z1, tp, lam_vecs, head_norm.reshape(1, C_VDIM))


def _page_gather(pt_ref, n_pages, items, sem):
    def copy(i, pg, p, slot):
        pool_ref, buf_ref, rows = items[i]
        src = pool_ref.at[pl.ds(pl.multiple_of(pg * rows, rows), rows)]
        dst = buf_ref.at[pl.ds(pl.multiple_of((slot * n_pages + p) * rows, rows), rows)]
        return pltpu.make_async_copy(src, dst, sem.at[i, slot])

    def start(bb, slot):
        def body(p, _):
            for i in range(len(items)):
                copy(i, pt_ref[bb, p], p, slot).start()
            return 0
        lax.fori_loop(0, n_pages, body, 0)

    def wait(i, slot):
        def body(p, _):
            copy(i, 0, 0, slot).wait()
            return 0
        lax.fori_loop(0, n_pages, body, 0)

    return start, wait


def _prefetch(b, nb, start):
    slot = b % 2

    @pl.when(b == 0)
    def _():
        start(0, 0)

    @pl.when(b + 1 < nb)
    def _():
        start(b + 1, 1 - slot)

    return slot


def _pad_rows(x, rows):
    return jnp.concatenate([x, jnp.zeros((rows - x.shape[0], x.shape[1]), x.dtype)], axis=0)


def _page_rows(pool_ref, rows):
    return lambda pg: pool_ref.at[pl.ds(pl.multiple_of(pg * rows, rows), rows)]


def _interleaved(buf_ref, n, j, row0=0):
    return lambda c0, ch: buf_ref[pl.ds(row0 + c0 * n + j, ch, stride=n), :]


def _sample_scores(q, k_fn, knew, bias_fn, mask_fn, s_ref, past, ch, scale=SCALE):
    def body(c, _):
        c0 = pl.multiple_of(c * ch, ch)
        k = k_fn(c0, ch).astype(MXU_DT)
        s = _nt(q, k) * scale + bias_fn(c0, ch)
        s_ref[:, pl.ds(c0, ch)] = jnp.where(mask_fn(c0, ch, False), s, NEG)
        return 0

    lax.fori_loop(0, past // ch, body, 0, unroll=4)
    s = _nt(q, _pad_rows(knew, 128).astype(MXU_DT)) * scale + bias_fn(past, 128)
    s_ref[:, past:past + 128] = jnp.where(mask_fn(past, 128, True), s, NEG)


def _sample_softmax(s_ref):
    z = s_ref[...]
    e = jnp.where(z > 0.5 * NEG, jnp.exp(z - _row_reduce(jnp.maximum, jnp.max, z)), 0.0)
    l = _row_reduce(jnp.add, jnp.sum, e)
    return e * (1.0 / jnp.where(l > 0.0, l, 1.0))


def _sample_pv(p_ref, v_fn, vnew, past, ch):
    def body(c, acc):
        c0 = pl.multiple_of(c * ch, ch)
        return acc + _mm(p_ref[:, pl.ds(c0, ch)].astype(MXU_DT), v_fn(c0, ch).astype(MXU_DT))

    acc = lax.fori_loop(0, past // ch, body, jnp.zeros((p_ref.shape[0], vnew.shape[1]), F32), unroll=4)
    return acc + _mm(p_ref[:, past:past + 128].astype(MXU_DT), _pad_rows(vnew, 128).astype(MXU_DT))


def _new_key_mask(nq, rep):
    t = lax.broadcasted_iota(I32, (nq, 128), 0)
    j = lax.broadcasted_iota(I32, (nq, 128), 1)
    return jnp.concatenate([(j <= t) & (j < nq)] * rep, axis=0)


def _compress_sample_kernel(pt_ref, pk_ref, pv_ref, w1a, w1b, w2, pe, o_ref, buf, sem, *, n_pages):
    step = pl.program_id(0)
    cpp = PAGE // CMP_STRIDE
    rows = CMP_STRIDE * A_KV
    m = n_pages * cpp

    def copy(pool_ref, pg, p, slot):
        src = pool_ref.at[pl.ds(pl.multiple_of(pg * PAGE * A_KV, PAGE * A_KV), PAGE * A_KV)]
        dst = buf.at[pl.ds(pl.multiple_of((slot * n_pages + p) * PAGE_PITCH, 8), PAGE * A_KV)]
        return pltpu.make_async_copy(src, dst, sem.at[slot])

    def start(st, slot):
        for which, pool_ref in enumerate((pk_ref, pv_ref)):
            @pl.when(st % 2 == which)
            def _(pool_ref=pool_ref):
                def body(p, _):
                    copy(pool_ref, pt_ref[st // 2, p], p, slot).start()
                    return 0
                lax.fori_loop(0, n_pages, body, 0)

    slot = _prefetch(step, pl.num_programs(0), start)

    def wait(p, _):
        copy(pk_ref, 0, 0, slot).wait()
        return 0

    lax.fori_loop(0, n_pages, wait, 0)
    which = step % 2
    row0 = slot * n_pages * PAGE_PITCH

    def x_fn(j, g):
        return jnp.concatenate([buf[pl.ds(row0 + i * rows + 2 * j + g, n_pages, stride=PAGE_PITCH), :]
                                for i in range(cpp)], axis=0)

    def next_fn(y):
        return jnp.concatenate([y[n_pages:], pltpu.roll(y[:n_pages], n_pages - 1, 0)], axis=0)

    def store(g, tokens):
        for i in range(cpp):
            o_ref[0, 0, g, pl.ds(i, n_pages, stride=cpp), :] = tokens[i * n_pages:(i + 1) * n_pages]

    _compress(x_fn, m, w1a.at[which], w1b.at[which], w2.at[which], pe.at[which], next_fn, store)


def compress_sample(pool_k, pool_v, page_table, cw):
    bd, n_pages = page_table.shape
    m = n_pages * (PAGE // CMP_STRIDE)
    return pl.pallas_call(
        functools.partial(_compress_sample_kernel, n_pages=n_pages),
        grid_spec=pltpu.PrefetchScalarGridSpec(
            num_scalar_prefetch=1, grid=(2 * bd,),
            in_specs=[pl.BlockSpec(memory_space=pl.ANY), pl.BlockSpec(memory_space=pl.ANY)] + _cmp_weight_specs(),
            out_specs=pl.BlockSpec((1, 1, A_KV, m, HEAD_DIM), lambda s, pt: (s // 2, s % 2, 0, 0, 0)),
            scratch_shapes=[pltpu.VMEM((2 * n_pages * PAGE_PITCH, HEAD_DIM), F32), pltpu.SemaphoreType.DMA((2,))]),
        out_shape=jax.ShapeDtypeStruct((bd, 2, A_KV, m, HEAD_DIM), F32),
        compiler_params=_cparams("arbitrary"),
        name="compress_sample",
    )(page_table, pool_k, pool_v, *cw)


def _nsa_sample_kernel(pt_ref, z_ref, kc_ref, vc_ref, pks_ref, pvs_ref, wk_ref, wv_ref, bs_ref, bc_ref,
                       o_ref, kbuf, vbuf, s_ref, sw_ref, chosen_ref, sem, *, n_pages, ch):
    past = n_pages * PAGE
    nq = z_ref.shape[0]
    mc = kc_ref.shape[3]
    t_len = past + nq
    n_cmp = (t_len - CMP_BLOCK) // CMP_STRIDE + 1
    n_slc = -(-t_len // SEL_BLOCK)
    jn = 128 * (-(-n_slc // 128))
    wb = wk_ref.shape[0] // A_KV
    start, wait = _page_gather(pt_ref, n_pages, ((pks_ref, kbuf, PAGE * A_KV), (pvs_ref, vbuf, PAGE * A_KV)), sem)
    slot = _prefetch(pl.program_id(0), pl.num_programs(0), start)
    row0 = slot * past * A_KV

    pos = past + lax.broadcasted_iota(I32, (nq, 1), 0)
    pos4 = jnp.concatenate([pos] * A_GROUP, axis=0)
    cidx = lax.broadcasted_iota(I32, (1, mc), 1)
    gates = jax.nn.sigmoid(z_ref[:, C_TAIL:C_TAIL + 128])
    overlap = _overlap(mc, jn, n_cmp, n_slc)
    new_mask = _new_key_mask(nq, A_GROUP)
    first_half = lax.broadcasted_iota(I32, (nq, 2 * SEL_BLOCK), 1) < SEL_BLOCK
    waited = False

    for g in range(A_KV):
        cols = [g * A_GROUP + r for r in range(A_GROUP)]
        q = _stack_heads(z_ref, C_QA + g * A_GROUP * HEAD_DIM, A_GROUP).astype(MXU_DT)
        gl = slice(g * HEAD_DIM, (g + 1) * HEAD_DIM)
        lc = (_nt(q, kc_ref[0, 0, g].astype(MXU_DT)) * SCALE
              + jnp.concatenate([bc_ref[c] for c in cols], axis=0))
        p_cmp = _softmax_rows(lc, (pos4 >= cidx * CMP_STRIDE + (CMP_BLOCK - 1)) & (cidx < n_cmp))
        o_cmp = _mm(p_cmp.astype(MXU_DT), vc_ref[0, 0, g].astype(MXU_DT))
        p_sum = sum(p_cmp[r * nq:(r + 1) * nq] for r in range(A_GROUP))
        imp = jnp.dot(p_sum, overlap, preferred_element_type=F32, precision=lax.Precision.HIGHEST)
        sel = _select_blocks(imp, pos, n_slc)
        def win_bias(c0, w, cols=cols):
            return jnp.concatenate([bs_ref[c, :, pl.ds(past - wb + c0, w)] for c in cols], axis=0)

        def win_mask(c0, w, is_new):
            dist = pos4 - (past - wb + c0 + lax.broadcasted_iota(I32, (1, w), 1))
            valid = (dist >= 0) & (dist < WINDOW)
            return valid & new_mask if is_new else valid

        _sample_scores(q, _interleaved(wk_ref, A_KV, g),
                       z_ref[:, C_KW + g * HEAD_DIM:C_KW + (g + 1) * HEAD_DIM],
                       win_bias, win_mask, sw_ref, wb, wb)
        sw_ref[...] = _sample_softmax(sw_ref)
        o_win = _sample_pv(sw_ref, _interleaved(wv_ref, A_KV, g),
                           z_ref[:, C_VW + g * HEAD_DIM:C_VW + (g + 1) * HEAD_DIM], wb, wb)
        if not waited:
            wait(0, slot)
            wait(1, slot)
            waited = True

        def slc_bias(c0, w, cols=cols):
            return jnp.concatenate([bs_ref[c, :, pl.ds(c0, w)] for c in cols], axis=0)

        for kk in range((past + 128) // 128):
            chosen_ref[:, kk * 128:(kk + 1) * 128] = jnp.where(
                first_half, sel[:, 2 * kk:2 * kk + 1], sel[:, 2 * kk + 1:2 * kk + 2])

        def slc_mask(c0, w, is_new):
            chosen = jnp.concatenate([chosen_ref[:, pl.ds(c0, w)] > 0.5] * A_GROUP, axis=0)
            return chosen & new_mask if is_new else chosen

        _sample_scores(q, _interleaved(kbuf, A_KV, g, row0),
                       z_ref[:, C_KS + g * HEAD_DIM:C_KS + (g + 1) * HEAD_DIM],
                       slc_bias, slc_mask, s_ref, past, ch)
        s_ref[...] = _sample_softmax(s_ref)
        o_slc = _sample_pv(s_ref, _interleaved(vbuf, A_KV, g, row0),
                           z_ref[:, C_VS + g * HEAD_DIM:C_VS + (g + 1) * HEAD_DIM], past, ch)
        for r in range(A_GROUP):
            h = g * A_GROUP + r
            c = T_GA + h * N_GATES
            rows = slice(r * nq, (r + 1) * nq)
            o_ref[:, h * HEAD_DIM:(h + 1) * HEAD_DIM] = (
                gates[:, c:c + 1] * o_cmp[rows] + gates[:, c + 1:c + 2] * o_slc[rows]
                + gates[:, c + 2:c + 3] * o_win[rows])


def nsa_sample(zs, kv_cmp, pool_ks, pool_vs, win_k, win_v, bs, bc, page_table, ch=1024):
    bd, n_pages = page_table.shape
    nq = zs.shape[0] // bd
    past = n_pages * PAGE
    mc = kv_cmp.shape[3]
    wrows = win_k.shape[0] // bd
    wb = wrows // A_KV
    win_spec = pl.BlockSpec((wrows, HEAD_DIM), lambda i, pt: (i, 0))
    buf = pltpu.VMEM((2 * past * A_KV, HEAD_DIM), F32)
    return pl.pallas_call(
        functools.partial(_nsa_sample_kernel, n_pages=n_pages, ch=ch),
        grid_spec=pltpu.PrefetchScalarGridSpec(
            num_scalar_prefetch=1, grid=(bd,),
            in_specs=[pl.BlockSpec((nq, zs.shape[1]), lambda i, pt: (i, 0)),
                      pl.BlockSpec((1, 1, A_KV, mc, HEAD_DIM), lambda i, pt: (i, 0, 0, 0, 0)),
                      pl.BlockSpec((1, 1, A_KV, mc, HEAD_DIM), lambda i, pt: (i, 1, 0, 0, 0)),
                      pl.BlockSpec(memory_space=pl.ANY), pl.BlockSpec(memory_space=pl.ANY),
                      win_spec, win_spec,
                      pl.BlockSpec((A_HEADS,) + bs.shape[1:], lambda i, pt: (0, 0, 0)),
                      pl.BlockSpec(bc.shape, lambda i, pt: (0, 0, 0))],
            out_specs=pl.BlockSpec((nq, 1024), lambda i, pt: (i, 0)),
            scratch_shapes=[buf, buf,
                            pltpu.VMEM((A_GROUP * nq, past + 128), F32),
                            pltpu.VMEM((A_GROUP * nq, wb + 128), F32),
                            pltpu.VMEM((nq, past + 128), F32),
                            pltpu.SemaphoreType.DMA((2, 2))]),
        out_shape=jax.ShapeDtypeStruct((bd * nq, 1024), F32),
        compiler_params=_cparams("arbitrary"),
        name="nsa_sample",
    )(page_table, zs, kv_cmp, kv_cmp, pool_ks, pool_vs, win_k, win_v, bs, bc)


def _dsa_sample_kernel(pt_ref, z_ref, pk_ref, pv_ref, pi_ref, bs_ref, o_ref,
                       kbuf, vbuf, ibuf, s_ref, sc_ref, key_ref, sel_ref, sem, *, n_pages, ch, topk, nbits):
    past = n_pages * PAGE
    nq = z_ref.shape[0]
    start, wait = _page_gather(pt_ref, n_pages, ((pk_ref, kbuf, PAGE * B_KV), (pv_ref, vbuf, PAGE * B_KV),
                                                 (pi_ref, ibuf, IDX_DIM)), sem)
    slot = _prefetch(pl.program_id(0), pl.num_programs(0), start)
    row0 = slot * past * B_KV
    qi = jnp.concatenate([z_ref[:, C_QI + h * IDX_DIM:C_QI + (h + 1) * IDX_DIM] for h in range(IDX_HEADS)],
                         axis=0).astype(MXU_DT)
    wi = z_ref[:, C_TAIL + T_WI:C_TAIL + T_WI + IDX_HEADS]
    wait(2, slot)

    def index_page(p, _):
        kt = ibuf[pl.ds(pl.multiple_of((slot * n_pages + p) * IDX_DIM, IDX_DIM), IDX_DIM), :]
        s_ref[:, pl.ds(pl.multiple_of(p * PAGE, PAGE), PAGE)] = _mm(qi, kt.astype(MXU_DT))
        return 0

    lax.fori_loop(0, n_pages, index_page, 0, unroll=8)
    ki_new = _pad_rows(z_ref[:, C_TAIL + T_KI:C_TAIL + T_KI + IDX_DIM], 128).astype(MXU_DT)
    s_ref[:, past:past + 128] = _nt(qi, ki_new)
    rel = jnp.maximum(s_ref[...], 0.0)
    score = sum(rel[h * nq:(h + 1) * nq] * wi[:, h:h + 1] for h in range(IDX_HEADS))
    score = score * (IDX_DIM ** -0.5 * IDX_HEADS ** -0.5)
    new_j = lax.broadcasted_iota(I32, score.shape, 1) - past
    causal = (new_j < 0) | ((new_j <= lax.broadcasted_iota(I32, score.shape, 0)) & (new_j < nq))
    key_ref[...] = _sort_key(jnp.where(causal, score, NEG))
    _topk_madd(key_ref, sel_ref, lambda c0: causal, 1, score.shape[1], topk, nbits)

    wait(0, slot)
    wait(1, slot)
    for g in range(B_KV):
        cols = [g * B_GROUP + r for r in range(B_GROUP)]
        q = _stack_heads(z_ref, C_QB + g * B_GROUP * HEAD_DIM, B_GROUP).astype(MXU_DT)

        def bias(c0, w, cols=cols):
            return jnp.concatenate([bs_ref[c, :, pl.ds(c0, w)] for c in cols], axis=0)

        def mask(c0, w, is_new):
            return jnp.concatenate([sel_ref[:, pl.ds(c0, w)] > 0.5 * NEG] * B_GROUP, axis=0)

        _sample_scores(q, _interleaved(kbuf, B_KV, g, row0),
                       z_ref[:, C_KB + g * HEAD_DIM:C_KB + (g + 1) * HEAD_DIM], bias, mask, sc_ref, past, ch)
        sc_ref[...] = _sample_softmax(sc_ref)
        o = _sample_pv(sc_ref, _interleaved(vbuf, B_KV, g, row0),
                       z_ref[:, C_VB + g * HEAD_DIM:C_VB + (g + 1) * HEAD_DIM], past, ch)
        for r in range(B_GROUP):
            h = g * B_GROUP + r
            o_ref[:, h * HEAD_DIM:(h + 1) * HEAD_DIM] = o[r * nq:(r + 1) * nq]


def dsa_sample(zs, pool_k, pool_v, pool_i, bs, page_table, ch=1024):
    bd, n_pages = page_table.shape
    nq = zs.shape[0] // bd
    past = n_pages * PAGE
    lp = past + 128
    topk = min(DSA_TOPK, (past + nq) // 4)
    return pl.pallas_call(
        functools.partial(_dsa_sample_kernel, n_pages=n_pages, ch=ch, topk=topk, nbits=int(lp).bit_length()),
        grid_spec=pltpu.PrefetchScalarGridSpec(
            num_scalar_prefetch=1, grid=(bd,),
            in_specs=[pl.BlockSpec((nq, zs.shape[1]), lambda i, pt: (i, 0)),
                      pl.BlockSpec(memory_space=pl.ANY), pl.BlockSpec(memory_space=pl.ANY),
                      pl.BlockSpec(memory_space=pl.ANY),
                      pl.BlockSpec((B_HEADS,) + bs.shape[1:], lambda i, pt: (1, 0, 0))],
            out_specs=pl.BlockSpec((nq, 1024), lambda i, pt: (i, 0)),
            scratch_shapes=[pltpu.VMEM((2 * past * B_KV, HEAD_DIM), F32), pltpu.VMEM((2 * past * B_KV, HEAD_DIM), F32),
                            pltpu.VMEM((2 * n_pages * IDX_DIM, PAGE), F32),
                            pltpu.VMEM((IDX_HEADS * nq, lp), F32), pltpu.VMEM((B_GROUP * nq, lp), F32),
                            pltpu.VMEM((nq, lp), I32), pltpu.VMEM((nq, lp), F32),
                            pltpu.SemaphoreType.DMA((3, 2))]),
        out_shape=jax.ShapeDtypeStruct((bd * nq, 1024), F32),
        compiler_params=_cparams("arbitrary"),
        name="dsa_sample",
    )(page_table, zs, pool_k, pool_v, pool_i, bs)


def _diff_sample_kernel(pt_ref, q_ref, kn_ref, vn_ref, pk_ref, pv_ref, bs_ref, lam_ref, hn_ref, o_ref,
                        kbuf, vbuf, sem, *, n_pages, cp):
    b = pl.program_id(0)
    nb = pl.num_programs(0)
    nq = q_ref.shape[0]
    pieces = C_KV * 2
    page_rows = PAGE * pieces
    slot_rows = cp * page_rows
    ch = cp * PAGE
    n_ch = n_pages // cp
    past = n_pages * PAGE
    rows = C_GROUP * nq

    def copies(bb, c, slot):
        out = []
        for i in range(cp):
            pg = pt_ref[bb, c * cp + i]
            dst = pl.ds(pl.multiple_of(slot * slot_rows + i * page_rows, page_rows), page_rows)
            out.append(pltpu.make_async_copy(_page_rows(pk_ref, page_rows)(pg), kbuf.at[dst], sem.at[0, slot]))
            out.append(pltpu.make_async_copy(_page_rows(pv_ref, page_rows)(pg), vbuf.at[dst], sem.at[1, slot]))
        return out

    @pl.when(b == 0)
    def _():
        for cpy in copies(0, 0, 0):
            cpy.start()

    qs = [jnp.concatenate([q_ref[:, ((g * C_GROUP + r) * 2 + m) * HEAD_DIM:((g * C_GROUP + r) * 2 + m + 1) * HEAD_DIM]
                           for r in range(C_GROUP)], axis=0).astype(MXU_DT)
          for g in range(C_KV) for m in range(2)]

    def update(carry, k_fn, v_fn, c0, w, mask):
        m_all, l_all, acc_all = carry
        new_m, new_l, new_acc = [], [], []
        for g in range(C_KV):
            ps, alphas = [], []
            for m in range(2):
                gm = g * 2 + m
                rs = slice(gm * rows, (gm + 1) * rows)
                bias = jnp.concatenate([bs_ref[m * C_HEADS + g * C_GROUP + r, :, pl.ds(c0, w)]
                                        for r in range(C_GROUP)], axis=0)
                s = _nt(qs[gm], k_fn(g, m).astype(MXU_DT)) * SCALE + bias
                if mask is not None:
                    s = jnp.where(mask, s, NEG)
                mn = jnp.maximum(m_all[rs], jnp.max(s, axis=-1, keepdims=True))
                p = jnp.exp(s - mn)
                if mask is not None:
                    p = jnp.where(mask, p, 0.0)
                a = jnp.exp(m_all[rs] - mn)
                new_m.append(mn)
                new_l.append(a * l_all[rs] + jnp.sum(p, axis=-1, keepdims=True))
                ps.append(p)
                alphas.append(a)
            pst = jnp.concatenate(ps, axis=0).astype(MXU_DT)
            pv = jnp.concatenate([_mm(pst, v_fn(g, h).astype(MXU_DT)) for h in range(2)], axis=1)
            for m in range(2):
                rs = slice((g * 2 + m) * rows, (g * 2 + m + 1) * rows)
                new_acc.append(alphas[m] * acc_all[rs] + pv[m * rows:(m + 1) * rows])
        return (jnp.concatenate(new_m, axis=0), jnp.concatenate(new_l, axis=0),
                jnp.concatenate(new_acc, axis=0))

    def chunk(c, carry):
        slot = c % 2
        for cpy in copies(b, c, slot):
            cpy.wait()

        @pl.when(c + 1 < n_ch)
        def _():
            for cpy in copies(b, c + 1, 1 - slot):
                cpy.start()

        @pl.when((c + 1 == n_ch) & (b + 1 < nb))
        def _():
            for cpy in copies(b + 1, 0, 1 - slot):
                cpy.start()

        base = slot * slot_rows
        return update(carry,
                      lambda g, m: kbuf[pl.ds(base + g * 2 + m, ch, stride=pieces), :],
                      lambda g, h: vbuf[pl.ds(base + h * C_KV + g, ch, stride=pieces), :],
                      pl.multiple_of(c * ch, ch), ch, None)

    n_rows = pieces * rows
    carry = (jnp.full((n_rows, 1), NEG, F32), jnp.zeros((n_rows, 1), F32), jnp.zeros((n_rows, C_VDIM), F32))
    carry = lax.fori_loop(0, n_ch, chunk, carry)
    _, l_all, acc_all = update(
        carry,
        lambda g, m: _pad_rows(kn_ref[:, (g * 2 + m) * HEAD_DIM:(g * 2 + m + 1) * HEAD_DIM], 128),
        lambda g, h: _pad_rows(vn_ref[:, g * C_VDIM + h * HEAD_DIM:g * C_VDIM + (h + 1) * HEAD_DIM], 128),
        past, 128, _new_key_mask(nq, C_GROUP))
    o_all = acc_all / l_all
    lam = _diff_lambda(lam_ref)
    for g in range(C_KV):
        r0 = g * 2 * rows
        o = _diff_finish(o_all[r0:r0 + rows] - lam * o_all[r0 + rows:r0 + 2 * rows], hn_ref)
        for r in range(C_GROUP):
            col = (g * C_GROUP + r) * C_VDIM
            o_ref[:, col:col + C_VDIM] = o[r * nq:(r + 1) * nq]


def diff_sample(z1s, pool_k, pool_v, bs, lam_vecs, head_norm, page_table, cp=16):
    bd, n_pages = page_table.shape
    nq = z1s.shape[0] // bd
    assert n_pages % (2 * cp) == 0
    slot_rows = cp * PAGE * C_KV * 2
    q_cols = C_HEADS * 2 * HEAD_DIM
    kv_cols = C_KV * C_VDIM
    return pl.pallas_call(
        functools.partial(_diff_sample_kernel, n_pages=n_pages, cp=cp),
        grid_spec=pltpu.PrefetchScalarGridSpec(
            num_scalar_prefetch=1, grid=(bd,),
            in_specs=[pl.BlockSpec((nq, q_cols), lambda i, pt: (i, 0)),
                      pl.BlockSpec((nq, kv_cols), lambda i, pt: (i, q_cols // kv_cols)),
                      pl.BlockSpec((nq, kv_cols), lambda i, pt: (i, q_cols // kv_cols + 1)),
                      pl.BlockSpec(memory_space=pl.ANY), pl.BlockSpec(memory_space=pl.ANY),
                      pl.BlockSpec(bs.shape, lambda i, pt: (0, 0, 0)),
                      pl.BlockSpec((4, HEAD_DIM), lambda i, pt: (0, 0)),
                      pl.BlockSpec((1, C_VDIM), lambda i, pt: (0, 0))],
            out_specs=pl.BlockSpec((nq, C_HEADS * C_VDIM), lambda i, pt: (i, 0)),
            scratch_shapes=[pltpu.VMEM((2 * slot_rows, HEAD_DIM), F32), pltpu.VMEM((2 * slot_rows, HEAD_DIM), F32),
                            pltpu.SemaphoreType.DMA((2, 2))]),
        out_shape=jax.ShapeDtypeStruct((bd * nq, C_HEADS * C_VDIM), F32),
        compiler_params=_cparams("arbitrary"),
        name="diff_sample",
    )(page_table, z1s, z1s, z1s, pool_k, pool_v, bs, lam_vecs, head_norm.reshape(1, C_VDIM))


def _row_tile(rows, cap=1024):
    tm = min(rows, cap)
    assert rows % tm == 0
    return tm


def _reorder_l0_weight(w):
    sizes = (A_HEADS * HEAD_DIM,) + (A_KV * HEAD_DIM,) * 6 + (
        N_GATES * A_HEADS, B_HEADS * HEAD_DIM, B_KV * HEAD_DIM, B_KV * HEAD_DIM,
        IDX_HEADS * IDX_DIM, IDX_DIM, IDX_HEADS)
    offs = [0]
    for s in sizes:
        offs.append(offs[-1] + s)
    piece = lambda i, j=None: w[:, offs[i]:offs[(i if j is None else j) + 1]]
    qa, six, ga, qb, kvb, qi, ki, wi = piece(0), piece(1, 6), piece(7), piece(8), piece(9, 10), piece(11), \
        piece(12), piece(13)
    pad = jnp.zeros((w.shape[0], L0_COLS - offs[-1]), w.dtype)
    return jnp.concatenate([qa, qb, six, kvb, qi, ki, ga, wi, pad], axis=1).astype(MXU_DT)


def _compress_weights(pe, w1, w2):
    half = CMP_STRIDE * HEAD_DIM
    w1 = w1.reshape(2, half, CMP_HIDDEN).astype(MXU_DT)
    pe_rows = jnp.zeros((16, half), F32).at[0:2].set(pe.reshape(2, half))
    return w1[0], w1[1], w2.astype(MXU_DT), pe_rows


def kernel(x_prompt, x_sample, cache_l0_nsa_cmp_k, cache_l0_nsa_cmp_v, cache_l0_nsa_slc_k, cache_l0_nsa_slc_v, state_l0_nsa_win_k, state_l0_nsa_win_v, cache_l0_dsa_k, cache_l0_dsa_v, cache_l0_dsa_idx_k, cache_l1_diff_k, cache_l1_diff_v, page_table, rel_bias, attn_norm, mlp_norm, mlp_w1, mlp_w2, l0_w_in, l0_w_out, l0_cmp_pe_k, l0_cmp_w1_k, l0_cmp_w2_k, l0_cmp_pe_v, l0_cmp_w1_v, l0_cmp_w2_v, l1_w_in, l1_w_out, l1_lambda_q1, l1_lambda_k1, l1_lambda_q2, l1_lambda_k2, l1_head_norm, final_norm):
    n, t_len, d = x_prompt.shape
    bd, nq, _ = x_sample.shape
    n_pool = cache_l0_nsa_cmp_k.shape[0]
    n_pages = page_table.shape[1]
    past = n_pages * PAGE
    lp = past + 128
    kv_w = A_KV * HEAD_DIM
    assert t_len % CHUNK == 0 and t_len >= WINDOW + QB and nq <= 8
    assert state_l0_nsa_win_k.shape[1] == min(WINDOW, past)

    xp = x_prompt.reshape(n * t_len, d)
    xs = x_sample.reshape(bd * nq, d)
    tmp, tms = _row_tile(xp.shape[0]), _row_tile(xs.shape[0])
    w0 = _reorder_l0_weight(l0_w_in)
    cw = [jnp.stack(pair) for pair in zip(_compress_weights(l0_cmp_pe_k, l0_cmp_w1_k, l0_cmp_w2_k),
                                          _compress_weights(l0_cmp_pe_v, l0_cmp_w1_v, l0_cmp_w2_v))]
    lam_vecs = jnp.stack([l1_lambda_q1, l1_lambda_k1, l1_lambda_q2, l1_lambda_k2])
    bf = lambda a: a.astype(MXU_DT)

    tp, bs = bias_tiles(rel_bias, past, nq, lp)
    bc_p, bc_s = bias_cmp(rel_bias, t_len, t_len // CMP_STRIDE, past, nq, past // CMP_STRIDE)

    zp = norm_proj(xp, attn_norm[0], w0, tmp, 768)
    zs = norm_proj(xs, attn_norm[0], w0, tms, 768)
    cut = lambda z, c, w: z[:, c:c + w]
    p_rows = {name: cut(zp, c, kv_w) for name, c in
              (("kc", C_KC), ("vc", C_VC), ("ks", C_KS), ("vs", C_VS), ("kw", C_KW), ("vw", C_VW),
               ("kb", C_KB), ("vb", C_VB))}
    s_rows = {name: cut(zs, c, kv_w) for name, c in
              (("kc", C_KC), ("vc", C_VC), ("ks", C_KS), ("vs", C_VS), ("kw", C_KW), ("vw", C_VW),
               ("kb", C_KB), ("vb", C_VB))}
    chunk_w = CMP_STRIDE * kv_w
    kc_p, vc_p = compress_prompt(p_rows["kc"].reshape(n, t_len // CMP_STRIDE, chunk_w),
                                 p_rows["vc"].reshape(n, t_len // CMP_STRIDE, chunk_w), cw)
    lanes = lambda a: a.reshape(-1, HEAD_DIM)
    kv_cmp_s = compress_sample(lanes(cache_l0_nsa_cmp_k), lanes(cache_l0_nsa_cmp_v), page_table, cw)
    oa_p = nsa_prompt(zp, kc_p, vc_p, tp, bc_p, rel_bias, n, t_len)
    ob_p = dsa_prompt(zp, tp, rel_bias, n, t_len)
    wb = state_l0_nsa_win_k.shape[1]
    oa_s = nsa_sample(zs, kv_cmp_s, lanes(cache_l0_nsa_slc_k), lanes(cache_l0_nsa_slc_v),
                      lanes(state_l0_nsa_win_k), lanes(state_l0_nsa_win_v), bs, bc_s, page_table)
    ob_s = dsa_sample(zs, lanes(cache_l0_dsa_k), lanes(cache_l0_dsa_v),
                      jnp.swapaxes(cache_l0_dsa_idx_k, 1, 2).reshape(-1, PAGE), bs, page_table)
    w_out0 = bf(l0_w_out)
    w1_0, w2_0 = bf(mlp_w1[0]), bf(mlp_w2[0])
    xp = out_proj(xp, [oa_p, ob_p], w_out0, tmp, 1024)
    xs = out_proj(xs, [oa_s, ob_s], w_out0, tms, 1024)
    xp = mlp(xp, mlp_norm[0], w1_0, w2_0, final_norm, tmp, 512, False)
    xs = mlp(xs, mlp_norm[0], w1_0, w2_0, final_norm, tms, 512, False)

    w_in1 = bf(l1_w_in)
    z1p = norm_proj(xp, attn_norm[1], w_in1, tmp, 1024)
    z1s = norm_proj(xs, attn_norm[1], w_in1, tms, 1024)
    o1_p = diff_prompt(z1p, tp, rel_bias, lam_vecs, l1_head_norm, n, t_len)
    v_halves = cache_l1_diff_v.reshape(n_pool, PAGE, C_KV, 2, HEAD_DIM).transpose(0, 1, 3, 2, 4)
    o1_s = diff_sample(z1s, lanes(cache_l1_diff_k), lanes(v_halves), bs, lam_vecs, l1_head_norm, page_table)
    w_out1 = bf(l1_w_out)
    w1_1, w2_1 = bf(mlp_w1[1]), bf(mlp_w2[1])
    xp = out_proj(xp, [o1_p], w_out1, tmp, 1024)
    xs = out_proj(xs, [o1_s], w_out1, tms, 1024)
    y_prompt = mlp(xp, mlp_norm[1], w1_1, w2_1, final_norm, tmp, 512, True).reshape(n, t_len, d)
    y_sample = mlp(xs, mlp_norm[1], w1_1, w2_1, final_norm, tms, 512, True).reshape(bd, nq, d)

    row4 = lambda a, b: a.reshape(b, -1, A_KV, HEAD_DIM)
    win = min(WINDOW, t_len)
    outs = [y_prompt, y_sample]
    for name in ("kc", "vc", "ks", "vs"):
        outs += [row4(p_rows[name], n), row4(s_rows[name], bd)]
    for name, state in (("kw", state_l0_nsa_win_k), ("vw", state_l0_nsa_win_v)):
        outs += [row4(p_rows[name], n)[:, t_len - win:],
                 jnp.concatenate([state, row4(s_rows[name], bd)], axis=1)[:, -wb:]]
    for name in ("kb", "vb"):
        outs += [row4(p_rows[name], n), row4(s_rows[name], bd)]
    outs += [cut(zp, C_TAIL + T_KI, IDX_DIM).reshape(n, t_len, IDX_DIM),
             cut(zs, C_TAIL + T_KI, IDX_DIM).reshape(bd, nq, IDX_DIM)]
    k_cols, v_cols = C_KV * 2 * HEAD_DIM, C_KV * C_VDIM
    q_cols = C_HEADS * 2 * HEAD_DIM
    outs += [cut(z1p, q_cols, k_cols).reshape(n, t_len, C_KV, 2, HEAD_DIM),
             cut(z1s, q_cols, k_cols).reshape(bd, nq, C_KV, 2, HEAD_DIM),
             cut(z1p, q_cols + k_cols, v_cols).reshape(n, t_len, C_KV, C_VDIM),
             cut(z1s, q_cols + k_cols, v_cols).reshape(bd, nq, C_KV, C_VDIM)]
    return tuple(outs)
```

```python
import functools
import math

import jax
import jax.numpy as jnp
from jax import lax
from jax.experimental import pallas as pl
from jax.experimental.pallas import tpu as pltpu

F32 = jnp.float32
I32 = jnp.int32
MXU_DT = jnp.bfloat16

HEAD_DIM = 128
A_HEADS, A_KV, A_GROUP = 8, 2, 4
B_HEADS, B_KV, B_GROUP = 8, 2, 4
C_HEADS, C_KV, C_GROUP, C_VDIM = 8, 4, 2, 256
CMP_STRIDE, CMP_BLOCK, CMP_HIDDEN = 16, 32, 256
SEL_BLOCK, N_SEL_BLOCKS, WINDOW, N_GATES = 64, 16, 512, 3
IDX_HEADS, IDX_DIM, DSA_TOPK = 4, 64, 256
NUM_BUCKETS, MAX_DISTANCE = 32, 128
LAMBDA_INIT = 0.8 - 0.6 * math.exp(-0.3 * 1)
RMS_EPS = 1e-6
NEG = -1e30
SCALE = HEAD_DIM ** -0.5
QB = 128
CHUNK = 512
PAGE = 128
PAGE_PITCH = PAGE * A_KV + 8
assert QB >= MAX_DISTANCE and WINDOW % QB == 0 and WINDOW >= 2 * QB and 2 * SEL_BLOCK == QB

C_QA, C_QB, C_KC, C_VC, C_KS, C_VS, C_KW, C_VW, C_KB, C_VB, C_QI, C_TAIL = (
    0, 1024, 2048, 2304, 2560, 2816, 3072, 3328, 3584, 3840, 4096, 4352)
T_KI, T_GA, T_WI = 0, 64, 88
L0_COLS = 4608
VMEM_LIMIT = 56 * 1024 * 1024
ROW_TILE = 1024
L0_COL_TILE = 1536
COL_TILE = 1024
FF_TILE = 512


def _cparams(*sem):
    return pltpu.CompilerParams(dimension_semantics=sem, vmem_limit_bytes=VMEM_LIMIT)


def _nt(a, b):
    return lax.dot_general(a, b, (((1,), (1,)), ((), ())), preferred_element_type=F32)


def _mm(a, b):
    return jnp.dot(a, b, preferred_element_type=F32)


def _rms(x, g):
    return x * lax.rsqrt(jnp.mean(x * x, axis=-1, keepdims=True) + RMS_EPS) * g


def _norm_proj_kernel(x_ref, g_ref, w_ref, o_ref, xn_ref):
    @pl.when(pl.program_id(1) == 0)
    def _():
        xn_ref[...] = _rms(x_ref[...], g_ref[...]).astype(xn_ref.dtype)

    o_ref[...] = _mm(xn_ref[...], w_ref[...])


def norm_proj(x, gain, w, tm, tn):
    rows, d = x.shape
    n = w.shape[1]
    return pl.pallas_call(
        _norm_proj_kernel,
        grid=(rows // tm, n // tn),
        in_specs=[pl.BlockSpec((tm, d), lambda i, j: (i, 0)),
                  pl.BlockSpec((1, d), lambda i, j: (0, 0)),
                  pl.BlockSpec((d, tn), lambda i, j: (0, j))],
        out_specs=pl.BlockSpec((tm, tn), lambda i, j: (i, j)),
        out_shape=jax.ShapeDtypeStruct((rows, n), F32),
        scratch_shapes=[pltpu.VMEM((tm, d), MXU_DT)],
        compiler_params=_cparams("parallel", "arbitrary"),
        name="norm_proj",
    )(x, gain.reshape(1, d), w)


def _out_proj_kernel(*refs, n_in):
    x_ref, o_refs, w_refs, y_ref = refs[0], refs[1:1 + n_in], refs[1 + n_in:1 + 2 * n_in], refs[-1]
    acc = x_ref[...]
    for o_ref, w_ref in zip(o_refs, w_refs):
        acc = acc + _mm(o_ref[...].astype(MXU_DT), w_ref[...])
    y_ref[...] = acc


def out_proj(x, outs, w, tm, tn):
    rows, d = x.shape
    o_specs, w_specs, row0 = [], [], 0
    for o in outs:
        k = o.shape[1]
        o_specs.append(pl.BlockSpec((tm, k), lambda i, j: (i, 0)))
        w_specs.append(pl.BlockSpec((k, tn), lambda i, j, rb=row0 // k: (rb, j)))
        row0 += k
    return pl.pallas_call(
        functools.partial(_out_proj_kernel, n_in=len(outs)),
        grid=(rows // tm, d // tn),
        in_specs=[pl.BlockSpec((tm, tn), lambda i, j: (i, j))] + o_specs + w_specs,
        out_specs=pl.BlockSpec((tm, tn), lambda i, j: (i, j)),
        out_shape=jax.ShapeDtypeStruct((rows, d), F32),
        compiler_params=_cparams("parallel", "arbitrary"),
        name="out_proj",
    )(x, *outs, *([w] * len(outs)))


def _mlp_kernel(x_ref, g_ref, w1_ref, w2_ref, gf_ref, y_ref, xn_ref, *, final_norm):
    j = pl.program_id(1)

    @pl.when(j == 0)
    def _():
        x = x_ref[...]
        xn_ref[...] = _rms(x, g_ref[...]).astype(xn_ref.dtype)
        y_ref[...] = x

    h = jnp.square(jnp.maximum(_mm(xn_ref[...], w1_ref[...]), 0.0))
    y_ref[...] += _mm(h.astype(w2_ref.dtype), w2_ref[...])

    if final_norm:
        @pl.when(j == pl.num_programs(1) - 1)
        def _():
            y_ref[...] = _rms(y_ref[...], gf_ref[...])


def mlp(x, gain, w1, w2, final_gain, tm, tf, final_norm):
    rows, d = x.shape
    ff = w1.shape[1]
    return pl.pallas_call(
        functools.partial(_mlp_kernel, final_norm=final_norm),
        grid=(rows // tm, ff // tf),
        in_specs=[pl.BlockSpec((tm, d), lambda i, j: (i, 0)),
                  pl.BlockSpec((1, d), lambda i, j: (0, 0)),
                  pl.BlockSpec((d, tf), lambda i, j: (0, j)),
                  pl.BlockSpec((tf, d), lambda i, j: (j, 0)),
                  pl.BlockSpec((1, d), lambda i, j: (0, 0))],
        out_specs=pl.BlockSpec((tm, d), lambda i, j: (i, 0)),
        out_shape=jax.ShapeDtypeStruct((rows, d), F32),
        scratch_shapes=[pltpu.VMEM((tm, d), MXU_DT)],
        compiler_params=_cparams("parallel", "arbitrary"),
        name="mlp",
    )(x, gain.reshape(1, d), w1, w2, final_gain.reshape(1, d))


def _bucket(dist):
    n = jnp.maximum(dist, 0)
    max_exact = NUM_BUCKETS // 2
    nf = jnp.maximum(n, 1).astype(F32)
    large = max_exact + (jnp.log(nf / max_exact) / math.log(MAX_DISTANCE / max_exact)
                         * (NUM_BUCKETS - max_exact)).astype(I32)
    large = jnp.minimum(large, NUM_BUCKETS - 1)
    return jnp.where(n < max_exact, n, large)


def _lookup(tbl_ref, col, buckets):
    def body(b, accs):
        v = tbl_ref[b, col]
        return tuple(jnp.where(bk == b, v, acc) for bk, acc in zip(buckets, accs))
    return lax.fori_loop(0, NUM_BUCKETS, body, tuple(jnp.zeros(bk.shape, F32) for bk in buckets))


def _bias_tiles_kernel(tbl_ref, tp_ref, bs_ref, *, q0):
    h = pl.program_id(0)
    t = lax.broadcasted_iota(I32, (QB, QB), 0)
    k = lax.broadcasted_iota(I32, (QB, QB), 1)
    ts = lax.broadcasted_iota(I32, bs_ref.shape[1:], 0)
    ks = lax.broadcasted_iota(I32, bs_ref.shape[1:], 1)
    d0, d1, ds = _lookup(tbl_ref, h, (_bucket(t - k), _bucket(QB + t - k), _bucket(q0 + ts - ks)))
    tp_ref[0, 0] = d0
    tp_ref[0, 1] = d1
    bs_ref[0] = ds


def bias_tiles(rel_bias, q0, n_q, lp):
    nh = rel_bias.shape[1]
    return pl.pallas_call(
        functools.partial(_bias_tiles_kernel, q0=q0),
        grid=(nh,),
        in_specs=[pl.BlockSpec(memory_space=pltpu.SMEM)],
        out_specs=[pl.BlockSpec((1, 2, QB, QB), lambda h: (h, 0, 0, 0)),
                   pl.BlockSpec((1, n_q, lp), lambda h: (h, 0, 0))],
        out_shape=[jax.ShapeDtypeStruct((nh, 2, QB, QB), F32),
                   jax.ShapeDtypeStruct((nh, n_q, lp), F32)],
        compiler_params=_cparams("arbitrary"),
        name="bias_tiles",
    )(rel_bias)


def _bias_cmp_kernel(tbl_ref, bp_ref, bs_ref, *, q0):
    h = pl.program_id(0)
    tp = lax.broadcasted_iota(I32, bp_ref.shape[1:], 0)
    cp = lax.broadcasted_iota(I32, bp_ref.shape[1:], 1)
    ts = lax.broadcasted_iota(I32, bs_ref.shape[1:], 0)
    cs = lax.broadcasted_iota(I32, bs_ref.shape[1:], 1)
    end = CMP_BLOCK - 1
    bp, bs = _lookup(tbl_ref, h, (_bucket(tp - (cp * CMP_STRIDE + end)),
                                  _bucket(q0 + ts - (cs * CMP_STRIDE + end))))
    bp_ref[0] = bp
    bs_ref[0] = bs


def bias_cmp(rel_bias, t_len, mc_p, q0, n_q, mc_s):
    return pl.pallas_call(
        functools.partial(_bias_cmp_kernel, q0=q0),
        grid=(A_HEADS,),
        in_specs=[pl.BlockSpec(memory_space=pltpu.SMEM)],
        out_specs=[pl.BlockSpec((1, t_len, mc_p), lambda h: (h, 0, 0)),
                   pl.BlockSpec((1, n_q, mc_s), lambda h: (h, 0, 0))],
        out_shape=[jax.ShapeDtypeStruct((A_HEADS, t_len, mc_p), F32),
                   jax.ShapeDtypeStruct((A_HEADS, n_q, mc_s), F32)],
        compiler_params=_cparams("arbitrary"),
        name="bias_cmp",
    )(rel_bias)


def _row_reduce(fn, lane_fn, x):
    tiles = [x[:, i:i + 128] for i in range(0, x.shape[1], 128)]
    return lane_fn(_tree_reduce(fn, tiles), axis=-1, keepdims=True)


def _softmax_rows(z, mask):
    z = jnp.where(mask, z, NEG)
    e = jnp.where(mask, jnp.exp(z - _row_reduce(jnp.maximum, jnp.max, z)), 0.0)
    l = _row_reduce(jnp.add, jnp.sum, e)
    return e * (1.0 / jnp.where(l > 0.0, l, 1.0))


def _gelu_tanh(x):
    return 0.5 * x * (1.0 + jnp.tanh(math.sqrt(2.0 / math.pi) * (x + 0.044715 * (x * x * x))))


def _compress(x_fn, m, w1a_ref, w1b_ref, w2_ref, pe_ref, next_fn, out_fn):
    pe = pe_ref[...].astype(MXU_DT)
    pos = _mm(pe, w1a_ref[...])[0:1] + _mm(pe, w1b_ref[...])[1:2]
    last = lax.broadcasted_iota(I32, (m, 1), 0) == m - 1
    for g in range(A_KV):
        xg = jnp.concatenate([x_fn(j, g).astype(MXU_DT) for j in range(CMP_STRIDE)], axis=1)
        hid = _gelu_tanh(_mm(xg, w1a_ref[...]) + next_fn(_mm(xg, w1b_ref[...])) + pos)
        out_fn(g, jnp.where(last, 0.0, _mm(hid.astype(MXU_DT), w2_ref[...])))


def _overlap(mc, jn, n_cmp, n_slc):
    c = lax.broadcasted_iota(I32, (mc, jn), 0)
    j = lax.broadcasted_iota(I32, (mc, jn), 1)
    ov = ((c * CMP_STRIDE < j * SEL_BLOCK + SEL_BLOCK) & (c * CMP_STRIDE + CMP_BLOCK > j * SEL_BLOCK)
          & (c < n_cmp) & (j < n_slc))
    return jnp.where(ov, 1.0, 0.0)


def _select_blocks(imp, pos, n_slc):
    jn = imp.shape[1]
    jidx = lax.broadcasted_iota(I32, (1, jn), 1)
    cur = pos // SEL_BLOCK
    forced = (jidx == 0) | (jidx == cur) | (jidx == cur - 1)
    future = jidx * SEL_BLOCK > pos
    score = jnp.where(future, -1.0, jnp.where(forced, 1e3, imp))
    score = jnp.where(jidx < n_slc, score, -2.0)

    def body(i, rank):
        col = jnp.sum(jnp.where(jidx == i, score, 0.0), axis=-1, keepdims=True)
        beats = jnp.where(col > score, 1.0, jnp.where(col == score, jnp.where(i < jidx, 1.0, 0.0), 0.0))
        return rank + beats

    rank = lax.fori_loop(0, n_slc, body, jnp.zeros(score.shape, F32), unroll=32)
    n_sel = min(N_SEL_BLOCKS, n_slc)
    return jnp.where((rank < n_sel) & (jidx < n_slc), 1.0, 0.0)


def _tree_reduce(fn, xs):
    xs = list(xs)
    while len(xs) > 1:
        xs = [fn(xs[i], xs[i + 1]) for i in range(0, len(xs) - 1, 2)] + ([xs[-1]] if len(xs) % 2 else [])
    return xs[0]


def _tree_sum(xs):
    return _tree_reduce(jnp.add, xs)


def _sort_key(s):
    bits = lax.bitcast_convert_type(jnp.where(s == 0.0, 0.0, s), I32)
    return jnp.where(bits < 0, bits ^ jnp.int32(0x7FFFFFFF), bits)


def _topk_madd(key_ref, madd_ref, valid_fn, nch, cw, k, nbits):
    n_rows = key_ref.shape[0]
    kf = jnp.float32(k)

    def count(fn):
        def body(c, acc):
            c0 = pl.multiple_of(c * cw, cw)
            hit = jnp.where(fn(c0, key_ref[:, pl.ds(c0, cw)]), 1.0, 0.0)
            return acc + _tree_sum(hit[:, i:i + 128] for i in range(0, cw, 128))
        acc = lax.fori_loop(0, nch, body, jnp.zeros((n_rows, 128), F32))
        return jnp.sum(acc, axis=-1, keepdims=True)

    int_min = jnp.int32(-2 ** 31)
    thr0 = jnp.where(count(lambda c0, key: key >= 0) >= kf, jnp.int32(0), int_min)

    def vbody(i, thr):
        cand = thr | lax.shift_left(jnp.int32(1), 30 - i)
        return jnp.where(count(lambda c0, key: key >= cand) >= kf, cand, thr)

    thr = lax.fori_loop(0, 31, vbody, thr0)
    need = kf - count(lambda c0, key: key > thr)

    def idx(c0):
        return c0 + lax.broadcasted_iota(I32, (1, cw), 1)

    def ibody(i, cut):
        cand = cut | lax.shift_left(jnp.int32(1), nbits - 1 - i)
        return jnp.where(count(lambda c0, key: (key == thr) & (idx(c0) < cand)) <= need, cand, cut)

    tied = jnp.max(count(lambda c0, key: key == thr) - need) > 0.0
    cut = lax.cond(tied, lambda: lax.fori_loop(0, nbits, ibody, jnp.zeros((n_rows, 1), I32)),
                   lambda: jnp.full((n_rows, 1), 2 ** nbits - 1, I32))

    def write(c, _):
        c0 = pl.multiple_of(c * cw, cw)
        key = key_ref[:, pl.ds(c0, cw)]
        sel = ((key > thr) | ((key == thr) & (idx(c0) < cut))) & valid_fn(c0)
        madd_ref[:, pl.ds(c0, cw)] = jnp.where(sel, 0.0, NEG)
        return 0

    lax.fori_loop(0, nch, write, 0)


def _topk_madd_t(key_ref, madd_ref, valid_fn, nch, cw, k, nbits):
    n_rows = key_ref.shape[1]
    kf = jnp.float32(k)

    def count(fn):
        def body(c, acc):
            c0 = pl.multiple_of(c * cw, cw)
            hit = jnp.where(fn(c0, key_ref[pl.ds(c0, cw), :]), 1.0, 0.0)
            return acc + _tree_sum(hit[i:i + 8] for i in range(0, cw, 8))
        acc = lax.fori_loop(0, nch, body, jnp.zeros((8, n_rows), F32))
        return jnp.sum(acc, axis=0, keepdims=True)

    int_min = jnp.int32(-2 ** 31)
    thr0 = jnp.where(count(lambda c0, key: key >= 0) >= kf, jnp.int32(0), int_min)

    def vbody(i, thr):
        cand = thr | lax.shift_left(jnp.int32(1), 30 - i)
        return jnp.where(count(lambda c0, key: key >= cand) >= kf, cand, thr)

    thr = lax.fori_loop(0, 31, vbody, thr0)
    need = kf - count(lambda c0, key: key > thr)

    def idx(c0):
        return c0 + lax.broadcasted_iota(I32, (cw, 1), 0)

    def ibody(i, cut):
        cand = cut | lax.shift_left(jnp.int32(1), nbits - 1 - i)
        return jnp.where(count(lambda c0, key: (key == thr) & (idx(c0) < cand)) <= need, cand, cut)

    tied = jnp.max(count(lambda c0, key: key == thr) - need) > 0.0
    cut = lax.cond(tied, lambda: lax.fori_loop(0, nbits, ibody, jnp.zeros((1, n_rows), I32)),
                   lambda: jnp.full((1, n_rows), 2 ** nbits - 1, I32))

    def write(c, _):
        c0 = pl.multiple_of(c * cw, cw)
        key = key_ref[pl.ds(c0, cw), :]
        sel = ((key > thr) | ((key == thr) & (idx(c0) < cut))) & valid_fn(c0)
        madd_ref[:, pl.ds(c0, cw)] = jnp.where(sel, 0.0, NEG).T
        return 0

    lax.fori_loop(0, nch, write, 0)


def _causal_attn(streams, dv, qb, pairs=False):
    m_rows = streams[0][0].shape[0]
    per = CHUNK // QB
    nact = qb // per + 1

    def stage(st, k0, w, bias):
        q, k_fn, _, _, _, madd_fn, s_ref = st
        s = _nt(q, k_fn(k0, w)) * SCALE + bias
        madd = madd_fn(k0, w)
        s_ref[:, pl.ds(k0, w)] = s if madd is None else s + madd

    def chunk_loop(body, init):
        n2 = nact // 2 if pairs else 0
        carry = lax.fori_loop(
            0, n2, lambda i, c: body(pl.multiple_of(i * 2 * CHUNK, 2 * CHUNK), 2 * CHUNK, c), init) if pairs else init
        return lax.fori_loop(
            0, nact - 2 * n2, lambda i, c: body(pl.multiple_of((2 * n2 + i) * CHUNK, CHUNK), CHUNK, c), carry)

    def lane_tiles(x):
        return [x[:, i:i + 128] for i in range(0, x.shape[1], 128)]

    def far(k0, w, _):
        for st in streams:
            stage(st, k0, w, st[3])
        return 0

    chunk_loop(far, 0)
    for st in streams:
        stage(st, pl.multiple_of(qb * QB, QB), QB, st[4](True))

    @pl.when(qb >= 1)
    def _():
        for st in streams:
            stage(st, pl.multiple_of((qb - 1) * QB, QB), QB, st[4](False))

    for j in range(1, per):
        @pl.when(qb % per + j < per)
        def _():
            for st in streams:
                st[6][:, pl.ds(pl.multiple_of((qb + j) * QB, QB), QB)] = jnp.full((m_rows, QB), NEG, F32)

    stat_w = 128 if pairs else 1

    def lanes_or_row(fn, lane_fn, x):
        return _tree_reduce(fn, lane_tiles(x)) if pairs else lane_fn(x, axis=-1, keepdims=True)

    def row_max(k0, w, ms):
        return tuple(jnp.maximum(m, lanes_or_row(jnp.maximum, jnp.max, st[6][:, pl.ds(k0, w)]))
                     for st, m in zip(streams, ms))

    ms = chunk_loop(row_max, tuple(jnp.full((m_rows, stat_w), NEG, F32) for _ in streams))
    ms = [jnp.max(m, axis=-1, keepdims=True) for m in ms]

    def pv(k0, w, carry):
        out = []
        for st, m, (l, acc) in zip(streams, ms, carry):
            p = jnp.exp(st[6][:, pl.ds(k0, w)] - m)
            out.append((l + lanes_or_row(jnp.add, jnp.sum, p), acc + _mm(p.astype(MXU_DT), st[2](k0, w))))
        return tuple(out)

    init = tuple((jnp.zeros((m_rows, stat_w), F32), jnp.zeros((m_rows, dv), F32)) for _ in streams)
    return [acc / jnp.sum(l, axis=-1, keepdims=True) for l, acc in chunk_loop(pv, init)]


def _causal_add(rep):
    t = lax.broadcasted_iota(I32, (QB, QB), 0)
    k = lax.broadcasted_iota(I32, (QB, QB), 1)
    return jnp.concatenate([jnp.where(k <= t, 0.0, NEG)] * rep, axis=0)


def _far_bias(tbl_ref, cols):
    return jnp.concatenate([jnp.full((QB, 1), tbl_ref[NUM_BUCKETS - 1, c], F32) for c in cols], axis=0)


def _near_bias(tp_ref, cols, diag):
    return jnp.concatenate([tp_ref[c, 0 if diag else 1] for c in cols], axis=0)


def _stack_heads(ref, col0, n):
    return jnp.concatenate([ref[:, col0 + r * HEAD_DIM:col0 + (r + 1) * HEAD_DIM] for r in range(n)], axis=0)


def _compress_prompt_kernel(xk_ref, xv_ref, w1a, w1b, w2, pe, ok_ref, ov_ref):
    lanes = lambda x_ref: (lambda j, g: x_ref[0, :, (2 * j + g) * HEAD_DIM:(2 * j + g + 1) * HEAD_DIM])
    m = xk_ref.shape[1]
    for i, (x_ref, o_ref) in enumerate(((xk_ref, ok_ref), (xv_ref, ov_ref))):
        def store(g, tokens, o_ref=o_ref):
            o_ref[0, :, g * HEAD_DIM:(g + 1) * HEAD_DIM] = tokens

        _compress(lanes(x_ref), m, w1a.at[i], w1b.at[i], w2.at[i], pe.at[i],
                  lambda y: pltpu.roll(y, m - 1, 0), store)


def _cmp_weight_specs():
    full = lambda *shape: pl.BlockSpec(shape, lambda *_: (0,) * len(shape))
    half = CMP_STRIDE * HEAD_DIM
    return [full(2, half, CMP_HIDDEN), full(2, half, CMP_HIDDEN), full(2, CMP_HIDDEN, HEAD_DIM), full(2, 16, half)]


def compress_prompt(xk, xv, cw):
    n, m, w = xk.shape
    spec = pl.BlockSpec((1, m, w), lambda i: (i, 0, 0))
    ospec = pl.BlockSpec((1, m, A_KV * HEAD_DIM), lambda i: (i, 0, 0))
    osh = jax.ShapeDtypeStruct((n, m, A_KV * HEAD_DIM), F32)
    return pl.pallas_call(
        _compress_prompt_kernel,
        grid=(n,),
        in_specs=[spec, spec] + _cmp_weight_specs(),
        out_specs=[ospec, ospec],
        out_shape=[osh, osh],
        compiler_params=_cparams("parallel"),
        name="compress_prompt",
    )(xk, xv, *cw)


def _window_attn(q, kw_ref, vw_ref, gl, qb, pos4, tbl_ref, tp_ref, cols):
    n_tiles = WINDOW // QB + 1
    width = n_tiles * QB
    lo = jnp.maximum(qb - (n_tiles - 1), 0)
    w0 = pl.multiple_of(lo * QB, QB)
    tiles = []
    for j in range(n_tiles):
        rel = qb - (lo + j)
        tiles.append(jnp.concatenate(
            [jnp.where(rel == 0, tp_ref[c, 0], jnp.where(rel == 1, tp_ref[c, 1], tbl_ref[NUM_BUCKETS - 1, c]))
             for c in cols], axis=0))
    s = _nt(q, kw_ref[pl.ds(w0, width), gl].astype(MXU_DT)) * SCALE + jnp.concatenate(tiles, axis=1)
    dist = pos4 - (w0 + lax.broadcasted_iota(I32, (1, width), 1))
    p = _softmax_rows(s, (dist >= 0) & (dist < WINDOW))
    return _mm(p.astype(MXU_DT), vw_ref[pl.ds(w0, width), gl].astype(MXU_DT))


def _nsa_prompt_kernel(tbl_ref, q_ref, tail_ref, kc_ref, vc_ref, ks_ref, vs_ref, kw_ref, vw_ref,
                       tp_ref, bc_ref, o_ref, s_ref, madd_ref, *, t_len):
    qb = pl.program_id(1)
    mc = kc_ref.shape[1]
    n_cmp = (t_len - CMP_BLOCK) // CMP_STRIDE + 1
    n_slc = -(-t_len // SEL_BLOCK)
    pos = qb * QB + lax.broadcasted_iota(I32, (QB, 1), 0)
    pos4 = jnp.concatenate([pos] * A_GROUP, axis=0)
    cidx = lax.broadcasted_iota(I32, (1, mc), 1)
    gates = jax.nn.sigmoid(tail_ref[...])
    overlap = _overlap(mc, QB, n_cmp, n_slc)
    causal_add = _causal_add(A_GROUP)
    onehot = jnp.where(lax.broadcasted_iota(I32, (QB, t_len), 0)
                       == lax.broadcasted_iota(I32, (QB, t_len), 1) // SEL_BLOCK, 1.0, 0.0).astype(MXU_DT)

    o_cmp, o_win, streams = [], [], []
    for g in range(A_KV):
        cols = [g * A_GROUP + r for r in range(A_GROUP)]
        gl = slice(g * HEAD_DIM, (g + 1) * HEAD_DIM)
        q = _stack_heads(q_ref, g * A_GROUP * HEAD_DIM, A_GROUP).astype(MXU_DT)
        lc = (_nt(q, kc_ref[0, :, gl].astype(MXU_DT)) * SCALE
              + jnp.concatenate([bc_ref[c] for c in cols], axis=0))
        p_cmp = _softmax_rows(lc, (pos4 >= cidx * CMP_STRIDE + (CMP_BLOCK - 1)) & (cidx < n_cmp))
        o_cmp.append(_mm(p_cmp.astype(MXU_DT), vc_ref[0, :, gl].astype(MXU_DT)))
        p_sum = sum(p_cmp[r * QB:(r + 1) * QB] for r in range(A_GROUP))
        imp = jnp.dot(p_sum, overlap, preferred_element_type=F32, precision=lax.Precision.HIGHEST)
        sel = _select_blocks(imp, pos, n_slc).astype(MXU_DT)
        madd_ref[g] = jnp.where(_mm(sel, onehot) > 0.5, 0.0, NEG)
        o_win.append(_window_attn(q, kw_ref, vw_ref, gl, qb, pos4, tbl_ref, tp_ref, cols))
        streams.append((
            q, lambda k0, w, gl=gl: ks_ref[pl.ds(k0, w), gl].astype(MXU_DT),
            lambda k0, w, gl=gl: vs_ref[pl.ds(k0, w), gl].astype(MXU_DT), _far_bias(tbl_ref, cols),
            lambda diag, cols=cols: _near_bias(tp_ref, cols, diag) + (causal_add if diag else 0.0),
            lambda k0, w, g=g: jnp.concatenate([madd_ref[g, :, pl.ds(k0, w)]] * A_GROUP, axis=0),
            s_ref.at[g]))
    o_slc = _causal_attn(streams, HEAD_DIM, qb)

    for h in range(A_HEADS):
        g, r = divmod(h, A_GROUP)
        c = T_GA + h * N_GATES
        rows = slice(r * QB, (r + 1) * QB)
        o = (gates[:, c:c + 1] * o_cmp[g][rows] + gates[:, c + 1:c + 2] * o_slc[g][rows]
             + gates[:, c + 2:c + 3] * o_win[g][rows])
        o_ref[:, h * HEAD_DIM:(h + 1) * HEAD_DIM] = o.astype(o_ref.dtype)


def nsa_prompt(z, k_cmp, v_cmp, tp, bc, rel_bias, n, t_len):
    nb = t_len // QB
    mc = k_cmp.shape[1]
    kv = lambda c: pl.BlockSpec((t_len, 256), lambda i, j: (i, c // 256))
    cmp_spec = pl.BlockSpec((1, mc, 256), lambda i, j: (i, 0, 0))
    return pl.pallas_call(
        functools.partial(_nsa_prompt_kernel, t_len=t_len),
        grid=(n, nb),
        in_specs=[pl.BlockSpec(memory_space=pltpu.SMEM),
                  pl.BlockSpec((QB, 1024), lambda i, j: (i * nb + j, C_QA // 1024)),
                  pl.BlockSpec((QB, 128), lambda i, j: (i * nb + j, C_TAIL // 128)),
                  cmp_spec, cmp_spec, kv(C_KS), kv(C_VS), kv(C_KW), kv(C_VW),
                  pl.BlockSpec(tp.shape, lambda i, j: (0, 0, 0, 0)),
                  pl.BlockSpec((A_HEADS, QB, mc), lambda i, j: (0, j, 0))],
        out_specs=pl.BlockSpec((QB, 1024), lambda i, j: (i * nb + j, 0)),
        out_shape=jax.ShapeDtypeStruct((n * t_len, 1024), MXU_DT),
        scratch_shapes=[pltpu.VMEM((A_KV, A_GROUP * QB, t_len), F32), pltpu.VMEM((A_KV, QB, t_len), F32)],
        compiler_params=_cparams("parallel", "arbitrary"),
        name="nsa_prompt",
    )(rel_bias, z, z, k_cmp, v_cmp, z, z, z, z, tp, bc)


def _dsa_prompt_kernel(tbl_ref, q_ref, qi_ref, tailq_ref, tailk_ref, kb_ref, vb_ref, tp_ref, o_ref,
                       key_ref, madd_ref, s_ref, *, topk, nbits):
    qb = pl.program_id(1)
    nact = qb // (CHUNK // QB) + 1
    pos = qb * QB + lax.broadcasted_iota(I32, (1, QB), 1)
    wi_t = tailq_ref[...].T[T_WI:T_WI + IDX_HEADS]
    qis = [qi_ref[:, h * IDX_DIM:(h + 1) * IDX_DIM].astype(MXU_DT) for h in range(IDX_HEADS)]

    def causal(c0):
        return c0 + lax.broadcasted_iota(I32, (CHUNK, 1), 0) <= pos

    def index_chunk(c, _):
        c0 = pl.multiple_of(c * CHUNK, CHUNK)
        ki = tailk_ref[pl.ds(c0, CHUNK), T_KI:T_KI + IDX_DIM].astype(MXU_DT)
        score = sum(jnp.maximum(_nt(ki, qis[h]), 0.0) * wi_t[h:h + 1] for h in range(IDX_HEADS))
        score = score * (IDX_DIM ** -0.5 * IDX_HEADS ** -0.5)
        key_ref[pl.ds(c0, CHUNK), :] = _sort_key(jnp.where(causal(c0), score, NEG))
        return 0

    lax.fori_loop(0, nact, index_chunk, 0)
    _topk_madd_t(key_ref, madd_ref, causal, nact, CHUNK, topk, nbits)

    streams = []
    for g in range(B_KV):
        cols = [A_HEADS + g * B_GROUP + r for r in range(B_GROUP)]
        gl = slice(g * HEAD_DIM, (g + 1) * HEAD_DIM)
        streams.append((
            _stack_heads(q_ref, g * B_GROUP * HEAD_DIM, B_GROUP).astype(MXU_DT),
            lambda k0, w, gl=gl: kb_ref[pl.ds(k0, w), gl].astype(MXU_DT),
            lambda k0, w, gl=gl: vb_ref[pl.ds(k0, w), gl].astype(MXU_DT),
            _far_bias(tbl_ref, cols), lambda diag, cols=cols: _near_bias(tp_ref, cols, diag),
            lambda k0, w: jnp.concatenate([madd_ref[:, pl.ds(k0, w)]] * B_GROUP, axis=0), s_ref.at[g]))
    outs = _causal_attn(streams, HEAD_DIM, qb)
    for h in range(B_HEADS):
        g, r = divmod(h, B_GROUP)
        o_ref[:, h * HEAD_DIM:(h + 1) * HEAD_DIM] = outs[g][r * QB:(r + 1) * QB].astype(o_ref.dtype)


def dsa_prompt(z, tp, rel_bias, n, t_len):
    nb = t_len // QB
    topk = min(DSA_TOPK, t_len // 4)
    nbits = int(t_len).bit_length()
    return pl.pallas_call(
        functools.partial(_dsa_prompt_kernel, topk=topk, nbits=nbits),
        grid=(n, nb),
        in_specs=[pl.BlockSpec(memory_space=pltpu.SMEM),
                  pl.BlockSpec((QB, 1024), lambda i, j: (i * nb + j, C_QB // 1024)),
                  pl.BlockSpec((QB, 256), lambda i, j: (i * nb + j, C_QI // 256)),
                  pl.BlockSpec((QB, 128), lambda i, j: (i * nb + j, C_TAIL // 128)),
                  pl.BlockSpec((t_len, 128), lambda i, j: (i, C_TAIL // 128)),
                  pl.BlockSpec((t_len, 256), lambda i, j: (i, C_KB // 256)),
                  pl.BlockSpec((t_len, 256), lambda i, j: (i, C_VB // 256)),
                  pl.BlockSpec(tp.shape, lambda i, j: (0, 0, 0, 0))],
        out_specs=pl.BlockSpec((QB, 1024), lambda i, j: (i * nb + j, 0)),
        out_shape=jax.ShapeDtypeStruct((n * t_len, 1024), MXU_DT),
        scratch_shapes=[pltpu.VMEM((t_len, QB), I32), pltpu.VMEM((QB, t_len), F32),
                        pltpu.VMEM((B_KV, B_GROUP * QB, t_len), F32)],
        compiler_params=_cparams("parallel", "arbitrary"),
        name="dsa_prompt",
    )(rel_bias, z, z, z, z, z, z, tp)


def _diff_lambda(lam_ref):
    v = lam_ref[...]
    e1 = jnp.exp(jnp.sum(v[0:1] * v[1:2], axis=-1, keepdims=True))
    e2 = jnp.exp(jnp.sum(v[2:3] * v[3:4], axis=-1, keepdims=True))
    return e1 - e2 + LAMBDA_INIT


def _diff_finish(o, hn_ref):
    return _rms(o, hn_ref[...]) * (1.0 - LAMBDA_INIT)


def _diff_prompt_kernel(tbl_ref, q_ref, k_ref, v_ref, tp_ref, lam_ref, hn_ref, o_ref, s_ref, *, gps):
    g0 = pl.program_id(1) * gps
    qb = pl.program_id(2)
    causal_add = _causal_add(C_GROUP)
    streams = []
    for gi in range(gps):
        for m in range(2):
            cols = [m * C_HEADS + (g0 + gi) * C_GROUP + r for r in range(C_GROUP)]
            q = jnp.concatenate(
                [q_ref[:, ((gi * C_GROUP + r) * 2 + m) * HEAD_DIM:((gi * C_GROUP + r) * 2 + m + 1) * HEAD_DIM]
                 for r in range(C_GROUP)], axis=0).astype(MXU_DT)
            kl = slice((gi * 2 + m) * HEAD_DIM, (gi * 2 + m + 1) * HEAD_DIM)
            vl = slice(gi * C_VDIM, (gi + 1) * C_VDIM)
            streams.append((
                q, lambda k0, w, kl=kl: k_ref[pl.ds(k0, w), kl].astype(MXU_DT),
                lambda k0, w, vl=vl: v_ref[pl.ds(k0, w), vl].astype(MXU_DT), _far_bias(tbl_ref, cols),
                lambda diag, cols=cols: _near_bias(tp_ref, cols, diag) + (causal_add if diag else 0.0),
                lambda k0, w: None, s_ref.at[gi * 2 + m]))
    outs = _causal_attn(streams, C_VDIM, qb, pairs=True)
    lam = _diff_lambda(lam_ref)
    for gi in range(gps):
        o = _diff_finish(outs[gi * 2] - lam * outs[gi * 2 + 1], hn_ref)
        for r in range(C_GROUP):
            col = (gi * C_GROUP + r) * C_VDIM
            o_ref[:, col:col + C_VDIM] = o[r * QB:(r + 1) * QB].astype(o_ref.dtype)


def diff_prompt(z1, tp, rel_bias, lam_vecs, head_norm, n, t_len, gps=2):
    nb = t_len // QB
    qw = gps * C_GROUP * 2 * HEAD_DIM
    kw = gps * 2 * HEAD_DIM
    q_cols, k_cols = C_HEADS * 2 * HEAD_DIM, C_KV * 2 * HEAD_DIM
    return pl.pallas_call(
        functools.partial(_diff_prompt_kernel, gps=gps),
        grid=(n, C_KV // gps, nb),
        in_specs=[pl.BlockSpec(memory_space=pltpu.SMEM),
                  pl.BlockSpec((QB, qw), lambda i, g, j: (i * nb + j, g)),
                  pl.BlockSpec((t_len, kw), lambda i, g, j: (i, q_cols // kw + g)),
                  pl.BlockSpec((t_len, kw), lambda i, g, j: (i, (q_cols + k_cols) // kw + g)),
                  pl.BlockSpec(tp.shape, lambda i, g, j: (0, 0, 0, 0)),
                  pl.BlockSpec((4, HEAD_DIM), lambda i, g, j: (0, 0)),
                  pl.BlockSpec((1, C_VDIM), lambda i, g, j: (0, 0))],
        out_specs=pl.BlockSpec((QB, qw), lambda i, g, j: (i * nb + j, g)),
        out_shape=jax.ShapeDtypeStruct((n * t_len, C_HEADS * C_VDIM), MXU_DT),
        scratch_shapes=[pltpu.VMEM((2 * gps, C_GROUP * QB, t_len), F32)],
        compiler_params=_cparams("parallel", "parallel", "arbitrary"),
        name="diff_prompt",
    )(rel_bias, z1, z1, z1, tp, lam_vecs, head_norm.reshape(1, C_VDIM))


def _page_gather(pt_ref, n_pages, items, sem):
    def copy(i, pg, p, slot):
        pool_ref, buf_ref, rows = items[i]
        src = pool_ref.at[pl.ds(pl.multiple_of(pg * rows, rows), rows)]
        dst = buf_ref.at[pl.ds(pl.multiple_of((slot * n_pages + p) * rows, rows), rows)]
        return pltpu.make_async_copy(src, dst, sem.at[i, slot])

    def start(bb, slot):
        def body(p, _):
            for i in range(len(items)):
                copy(i, pt_ref[bb, p], p, slot).start()
            return 0
        lax.fori_loop(0, n_pages, body, 0)

    def wait(i, slot):
        def body(p, _):
            copy(i, 0, 0, slot).wait()
            return 0
        lax.fori_loop(0, n_pages, body, 0)

    return start, wait


def _prefetch(b, nb, start):
    slot = b % 2

    @pl.when(b == 0)
    def _():
        start(0, 0)

    @pl.when(b + 1 < nb)
    def _():
        start(b + 1, 1 - slot)

    return slot


def _pad_rows(x, rows):
    return jnp.concatenate([x, jnp.zeros((rows - x.shape[0], x.shape[1]), x.dtype)], axis=0)


def _page_rows(pool_ref, rows):
    return lambda pg: pool_ref.at[pl.ds(pl.multiple_of(pg * rows, rows), rows)]


def _interleaved(buf_ref, n, j, row0=0):
    return lambda c0, ch: buf_ref[pl.ds(row0 + c0 * n + j, ch, stride=n), :]


def _sample_scores(q, k_fn, knew, bias_fn, mask_fn, s_ref, past, ch, scale=SCALE):
    def body(c, _):
        c0 = pl.multiple_of(c * ch, ch)
        k = k_fn(c0, ch).astype(MXU_DT)
        s = _nt(q, k) * scale + bias_fn(c0, ch)
        s_ref[:, pl.ds(c0, ch)] = jnp.where(mask_fn(c0, ch, False), s, NEG)
        return 0

    lax.fori_loop(0, past // ch, body, 0, unroll=4)
    s = _nt(q, _pad_rows(knew, 128).astype(MXU_DT)) * scale + bias_fn(past, 128)
    s_ref[:, past:past + 128] = jnp.where(mask_fn(past, 128, True), s, NEG)


def _sample_softmax(s_ref):
    z = s_ref[...]
    e = jnp.where(z > 0.5 * NEG, jnp.exp(z - _row_reduce(jnp.maximum, jnp.max, z)), 0.0)
    l = _row_reduce(jnp.add, jnp.sum, e)
    return e * (1.0 / jnp.where(l > 0.0, l, 1.0))


def _sample_pv(p_ref, v_fn, vnew, past, ch):
    def body(c, acc):
        c0 = pl.multiple_of(c * ch, ch)
        return acc + _mm(p_ref[:, pl.ds(c0, ch)].astype(MXU_DT), v_fn(c0, ch).astype(MXU_DT))

    acc = lax.fori_loop(0, past // ch, body, jnp.zeros((p_ref.shape[0], vnew.shape[1]), F32), unroll=4)
    return acc + _mm(p_ref[:, past:past + 128].astype(MXU_DT), _pad_rows(vnew, 128).astype(MXU_DT))


def _new_key_mask(nq, rep):
    t = lax.broadcasted_iota(I32, (nq, 128), 0)
    j = lax.broadcasted_iota(I32, (nq, 128), 1)
    return jnp.concatenate([(j <= t) & (j < nq)] * rep, axis=0)


def _compress_sample_kernel(pt_ref, pk_ref, pv_ref, w1a, w1b, w2, pe, o_ref, buf, sem, *, n_pages):
    step = pl.program_id(0)
    cpp = PAGE // CMP_STRIDE
    rows = CMP_STRIDE * A_KV
    m = n_pages * cpp

    def copy(pool_ref, pg, p, slot):
        src = pool_ref.at[pl.ds(pl.multiple_of(pg * PAGE * A_KV, PAGE * A_KV), PAGE * A_KV)]
        dst = buf.at[pl.ds(pl.multiple_of((slot * n_pages + p) * PAGE_PITCH, 8), PAGE * A_KV)]
        return pltpu.make_async_copy(src, dst, sem.at[slot])

    def start(st, slot):
        for which, pool_ref in enumerate((pk_ref, pv_ref)):
            @pl.when(st % 2 == which)
            def _(pool_ref=pool_ref):
                def body(p, _):
                    copy(pool_ref, pt_ref[st // 2, p], p, slot).start()
                    return 0
                lax.fori_loop(0, n_pages, body, 0)

    slot = _prefetch(step, pl.num_programs(0), start)

    def wait(p, _):
        copy(pk_ref, 0, 0, slot).wait()
        return 0

    lax.fori_loop(0, n_pages, wait, 0)
    which = step % 2
    row0 = slot * n_pages * PAGE_PITCH

    def x_fn(j, g):
        return jnp.concatenate([buf[pl.ds(row0 + i * rows + 2 * j + g, n_pages, stride=PAGE_PITCH), :]
                                for i in range(cpp)], axis=0)

    def next_fn(y):
        return jnp.concatenate([y[n_pages:], pltpu.roll(y[:n_pages], n_pages - 1, 0)], axis=0)

    def store(g, tokens):
        for i in range(cpp):
            o_ref[0, 0, g, pl.ds(i, n_pages, stride=cpp), :] = tokens[i * n_pages:(i + 1) * n_pages]

    _compress(x_fn, m, w1a.at[which], w1b.at[which], w2.at[which], pe.at[which], next_fn, store)


def compress_sample(pool_k, pool_v, page_table, cw):
    bd, n_pages = page_table.shape
    m = n_pages * (PAGE // CMP_STRIDE)
    return pl.pallas_call(
        functools.partial(_compress_sample_kernel, n_pages=n_pages),
        grid_spec=pltpu.PrefetchScalarGridSpec(
            num_scalar_prefetch=1, grid=(2 * bd,),
            in_specs=[pl.BlockSpec(memory_space=pl.ANY), pl.BlockSpec(memory_space=pl.ANY)] + _cmp_weight_specs(),
            out_specs=pl.BlockSpec((1, 1, A_KV, m, HEAD_DIM), lambda s, pt: (s // 2, s % 2, 0, 0, 0)),
            scratch_shapes=[pltpu.VMEM((2 * n_pages * PAGE_PITCH, HEAD_DIM), F32), pltpu.SemaphoreType.DMA((2,))]),
        out_shape=jax.ShapeDtypeStruct((bd, 2, A_KV, m, HEAD_DIM), F32),
        compiler_params=_cparams("arbitrary"),
        name="compress_sample",
    )(page_table, pool_k, pool_v, *cw)


def _nsa_sample_kernel(pt_ref, z_ref, kc_ref, vc_ref, pks_ref, pvs_ref, wk_ref, wv_ref, bs_ref, bc_ref,
                       o_ref, kbuf, vbuf, s_ref, sw_ref, chosen_ref, sem, *, n_pages, ch):
    past = n_pages * PAGE
    nq = z_ref.shape[0]
    mc = kc_ref.shape[3]
    t_len = past + nq
    n_cmp = (t_len - CMP_BLOCK) // CMP_STRIDE + 1
    n_slc = -(-t_len // SEL_BLOCK)
    jn = 128 * (-(-n_slc // 128))
    wb = wk_ref.shape[0] // A_KV
    start, wait = _page_gather(pt_ref, n_pages, ((pks_ref, kbuf, PAGE * A_KV), (pvs_ref, vbuf, PAGE * A_KV)), sem)
    slot = _prefetch(pl.program_id(0), pl.num_programs(0), start)
    row0 = slot * past * A_KV

    pos = past + lax.broadcasted_iota(I32, (nq, 1), 0)
    pos4 = jnp.concatenate([pos] * A_GROUP, axis=0)
    cidx = lax.broadcasted_iota(I32, (1, mc), 1)
    gates = jax.nn.sigmoid(z_ref[:, C_TAIL:C_TAIL + 128])
    overlap = _overlap(mc, jn, n_cmp, n_slc)
    new_mask = _new_key_mask(nq, A_GROUP)
    first_half = lax.broadcasted_iota(I32, (nq, 2 * SEL_BLOCK), 1) < SEL_BLOCK
    waited = False

    for g in range(A_KV):
        cols = [g * A_GROUP + r for r in range(A_GROUP)]
        q = _stack_heads(z_ref, C_QA + g * A_GROUP * HEAD_DIM, A_GROUP).astype(MXU_DT)
        lc = (_nt(q, kc_ref[0, 0, g].astype(MXU_DT)) * SCALE
              + jnp.concatenate([bc_ref[c] for c in cols], axis=0))
        p_cmp = _softmax_rows(lc, (pos4 >= cidx * CMP_STRIDE + (CMP_BLOCK - 1)) & (cidx < n_cmp))
        o_cmp = _mm(p_cmp.astype(MXU_DT), vc_ref[0, 0, g].astype(MXU_DT))
        p_sum = sum(p_cmp[r * nq:(r + 1) * nq] for r in range(A_GROUP))
        imp = jnp.dot(p_sum, overlap, preferred_element_type=F32, precision=lax.Precision.HIGHEST)
        sel = _select_blocks(imp, pos, n_slc)
        def win_bias(c0, w, cols=cols):
            return jnp.concatenate([bs_ref[c, :, pl.ds(past - wb + c0, w)] for c in cols], axis=0)

        def win_mask(c0, w, is_new):
            dist = pos4 - (past - wb + c0 + lax.broadcasted_iota(I32, (1, w), 1))
            valid = (dist >= 0) & (dist < WINDOW)
            return valid & new_mask if is_new else valid

        _sample_scores(q, _interleaved(wk_ref, A_KV, g),
                       z_ref[:, C_KW + g * HEAD_DIM:C_KW + (g + 1) * HEAD_DIM],
                       win_bias, win_mask, sw_ref, wb, wb)
        sw_ref[...] = _sample_softmax(sw_ref)
        o_win = _sample_pv(sw_ref, _interleaved(wv_ref, A_KV, g),
                           z_ref[:, C_VW + g * HEAD_DIM:C_VW + (g + 1) * HEAD_DIM], wb, wb)
        if not waited:
            wait(0, slot)
            wait(1, slot)
            waited = True

        def slc_bias(c0, w, cols=cols):
            return jnp.concatenate([bs_ref[c, :, pl.ds(c0, w)] for c in cols], axis=0)

        for kk in range((past + 128) // 128):
            chosen_ref[:, kk * 128:(kk + 1) * 128] = jnp.where(
                first_half, sel[:, 2 * kk:2 * kk + 1], sel[:, 2 * kk + 1:2 * kk + 2])

        def slc_mask(c0, w, is_new):
            chosen = jnp.concatenate([chosen_ref[:, pl.ds(c0, w)] > 0.5] * A_GROUP, axis=0)
            return chosen & new_mask if is_new else chosen

        _sample_scores(q, _interleaved(kbuf, A_KV, g, row0),
                       z_ref[:, C_KS + g * HEAD_DIM:C_KS + (g + 1) * HEAD_DIM],
                       slc_bias, slc_mask, s_ref, past, ch)
        s_ref[...] = _sample_softmax(s_ref)
        o_slc = _sample_pv(s_ref, _interleaved(vbuf, A_KV, g, row0),
                           z_ref[:, C_VS + g * HEAD_DIM:C_VS + (g + 1) * HEAD_DIM], past, ch)
        for r in range(A_GROUP):
            h = g * A_GROUP + r
            c = T_GA + h * N_GATES
            rows = slice(r * nq, (r + 1) * nq)
            o_ref[:, h * HEAD_DIM:(h + 1) * HEAD_DIM] = (
                gates[:, c:c + 1] * o_cmp[rows] + gates[:, c + 1:c + 2] * o_slc[rows]
                + gates[:, c + 2:c + 3] * o_win[rows])


def nsa_sample(zs, kv_cmp, pool_ks, pool_vs, win_k, win_v, bs, bc, page_table, ch=1024):
    bd, n_pages = page_table.shape
    nq = zs.shape[0] // bd
    past = n_pages * PAGE
    mc = kv_cmp.shape[3]
    wrows = win_k.shape[0] // bd
    wb = wrows // A_KV
    win_spec = pl.BlockSpec((wrows, HEAD_DIM), lambda i, pt: (i, 0))
    buf = pltpu.VMEM((2 * past * A_KV, HEAD_DIM), F32)
    return pl.pallas_call(
        functools.partial(_nsa_sample_kernel, n_pages=n_pages, ch=ch),
        grid_spec=pltpu.PrefetchScalarGridSpec(
            num_scalar_prefetch=1, grid=(bd,),
            in_specs=[pl.BlockSpec((nq, zs.shape[1]), lambda i, pt: (i, 0)),
                      pl.BlockSpec((1, 1, A_KV, mc, HEAD_DIM), lambda i, pt: (i, 0, 0, 0, 0)),
                      pl.BlockSpec((1, 1, A_KV, mc, HEAD_DIM), lambda i, pt: (i, 1, 0, 0, 0)),
                      pl.BlockSpec(memory_space=pl.ANY), pl.BlockSpec(memory_space=pl.ANY),
                      win_spec, win_spec,
                      pl.BlockSpec((A_HEADS,) + bs.shape[1:], lambda i, pt: (0, 0, 0)),
                      pl.BlockSpec(bc.shape, lambda i, pt: (0, 0, 0))],
            out_specs=pl.BlockSpec((nq, 1024), lambda i, pt: (i, 0)),
            scratch_shapes=[buf, buf,
                            pltpu.VMEM((A_GROUP * nq, past + 128), F32),
                            pltpu.VMEM((A_GROUP * nq, wb + 128), F32),
                            pltpu.VMEM((nq, past + 128), F32),
                            pltpu.SemaphoreType.DMA((2, 2))]),
        out_shape=jax.ShapeDtypeStruct((bd * nq, 1024), F32),
        compiler_params=_cparams("arbitrary"),
        name="nsa_sample",
    )(page_table, zs, kv_cmp, kv_cmp, pool_ks, pool_vs, win_k, win_v, bs, bc)


def _dsa_sample_kernel(pt_ref, z_ref, pk_ref, pv_ref, pi_ref, bs_ref, o_ref,
                       kbuf, vbuf, ibuf, s_ref, sc_ref, key_ref, sel_ref, sem, *, n_pages, ch, topk, nbits):
    past = n_pages * PAGE
    nq = z_ref.shape[0]
    start, wait = _page_gather(pt_ref, n_pages, ((pk_ref, kbuf, PAGE * B_KV), (pv_ref, vbuf, PAGE * B_KV),
                                                 (pi_ref, ibuf, IDX_DIM)), sem)
    slot = _prefetch(pl.program_id(0), pl.num_programs(0), start)
    row0 = slot * past * B_KV
    qi = jnp.concatenate([z_ref[:, C_QI + h * IDX_DIM:C_QI + (h + 1) * IDX_DIM] for h in range(IDX_HEADS)],
                         axis=0).astype(MXU_DT)
    wi = z_ref[:, C_TAIL + T_WI:C_TAIL + T_WI + IDX_HEADS]
    wait(2, slot)

    def index_page(p, _):
        kt = ibuf[pl.ds(pl.multiple_of((slot * n_pages + p) * IDX_DIM, IDX_DIM), IDX_DIM), :]
        s_ref[:, pl.ds(pl.multiple_of(p * PAGE, PAGE), PAGE)] = _mm(qi, kt.astype(MXU_DT))
        return 0

    lax.fori_loop(0, n_pages, index_page, 0, unroll=8)
    ki_new = _pad_rows(z_ref[:, C_TAIL + T_KI:C_TAIL + T_KI + IDX_DIM], 128).astype(MXU_DT)
    s_ref[:, past:past + 128] = _nt(qi, ki_new)
    rel = jnp.maximum(s_ref[...], 0.0)
    score = sum(rel[h * nq:(h + 1) * nq] * wi[:, h:h + 1] for h in range(IDX_HEADS))
    score = score * (IDX_DIM ** -0.5 * IDX_HEADS ** -0.5)
    new_j = lax.broadcasted_iota(I32, score.shape, 1) - past
    causal = (new_j < 0) | ((new_j <= lax.broadcasted_iota(I32, score.shape, 0)) & (new_j < nq))
    key_ref[...] = _sort_key(jnp.where(causal, score, NEG))
    _topk_madd(key_ref, sel_ref, lambda c0: causal, 1, score.shape[1], topk, nbits)

    wait(0, slot)
    wait(1, slot)
    for g in range(B_KV):
        cols = [g * B_GROUP + r for r in range(B_GROUP)]
        q = _stack_heads(z_ref, C_QB + g * B_GROUP * HEAD_DIM, B_GROUP).astype(MXU_DT)

        def bias(c0, w, cols=cols):
            return jnp.concatenate([bs_ref[c, :, pl.ds(c0, w)] for c in cols], axis=0)

        def mask(c0, w, is_new):
            return jnp.concatenate([sel_ref[:, pl.ds(c0, w)] > 0.5 * NEG] * B_GROUP, axis=0)

        _sample_scores(q, _interleaved(kbuf, B_KV, g, row0),
                       z_ref[:, C_KB + g * HEAD_DIM:C_KB + (g + 1) * HEAD_DIM], bias, mask, sc_ref, past, ch)
        sc_ref[...] = _sample_softmax(sc_ref)
        o = _sample_pv(sc_ref, _interleaved(vbuf, B_KV, g, row0),
                       z_ref[:, C_VB + g * HEAD_DIM:C_VB + (g + 1) * HEAD_DIM], past, ch)
        for r in range(B_GROUP):
            h = g * B_GROUP + r
            o_ref[:, h * HEAD_DIM:(h + 1) * HEAD_DIM] = o[r * nq:(r + 1) * nq]


def dsa_sample(zs, pool_k, pool_v, pool_i, bs, page_table, ch=1024):
    bd, n_pages = page_table.shape
    nq = zs.shape[0] // bd
    past = n_pages * PAGE
    lp = past + 128
    topk = min(DSA_TOPK, (past + nq) // 4)
    return pl.pallas_call(
        functools.partial(_dsa_sample_kernel, n_pages=n_pages, ch=ch, topk=topk, nbits=int(lp).bit_length()),
        grid_spec=pltpu.PrefetchScalarGridSpec(
            num_scalar_prefetch=1, grid=(bd,),
            in_specs=[pl.BlockSpec((nq, zs.shape[1]), lambda i, pt: (i, 0)),
                      pl.BlockSpec(memory_space=pl.ANY), pl.BlockSpec(memory_space=pl.ANY),
                      pl.BlockSpec(memory_space=pl.ANY),
                      pl.BlockSpec((B_HEADS,) + bs.shape[1:], lambda i, pt: (1, 0, 0))],
            out_specs=pl.BlockSpec((nq, 1024), lambda i, pt: (i, 0)),
            scratch_shapes=[pltpu.VMEM((2 * past * B_KV, HEAD_DIM), F32), pltpu.VMEM((2 * past * B_KV, HEAD_DIM), F32),
                            pltpu.VMEM((2 * n_pages * IDX_DIM, PAGE), F32),
                            pltpu.VMEM((IDX_HEADS * nq, lp), F32), pltpu.VMEM((B_GROUP * nq, lp), F32),
                            pltpu.VMEM((nq, lp), I32), pltpu.VMEM((nq, lp), F32),
                            pltpu.SemaphoreType.DMA((3, 2))]),
        out_shape=jax.ShapeDtypeStruct((bd * nq, 1024), F32),
        compiler_params=_cparams("arbitrary"),
        name="dsa_sample",
    )(page_table, zs, pool_k, pool_v, pool_i, bs)


def _diff_sample_kernel(pt_ref, q_ref, kn_ref, vn_ref, pk_ref, pv_ref, bs_ref, lam_ref, hn_ref, o_ref,
                        kbuf, vbuf, sem, *, n_pages, cp):
    b = pl.program_id(0)
    nb = pl.num_programs(0)
    nq = q_ref.shape[0]
    pieces = C_KV * 2
    page_rows = PAGE * pieces
    slot_rows = cp * page_rows
    ch = cp * PAGE
    n_ch = n_pages // cp
    past = n_pages * PAGE
    rows = C_GROUP * nq

    def copies(bb, c, slot):
        out = []
        for i in range(cp):
            pg = pt_ref[bb, c * cp + i]
            dst = pl.ds(pl.multiple_of(slot * slot_rows + i * page_rows, page_rows), page_rows)
            out.append(pltpu.make_async_copy(_page_rows(pk_ref, page_rows)(pg), kbuf.at[dst], sem.at[0, slot]))
            out.append(pltpu.make_async_copy(_page_rows(pv_ref, page_rows)(pg), vbuf.at[dst], sem.at[1, slot]))
        return out

    @pl.when(b == 0)
    def _():
        for cpy in copies(0, 0, 0):
            cpy.start()

    qs = [jnp.concatenate([q_ref[:, ((g * C_GROUP + r) * 2 + m) * HEAD_DIM:((g * C_GROUP + r) * 2 + m + 1) * HEAD_DIM]
                           for r in range(C_GROUP)], axis=0).astype(MXU_DT)
          for g in range(C_KV) for m in range(2)]

    def update(carry, k_fn, v_fn, c0, w, mask):
        m_all, l_all, acc_all = carry
        new_m, new_l, new_acc = [], [], []
        for g in range(C_KV):
            ps, alphas = [], []
            for m in range(2):
                gm = g * 2 + m
                rs = slice(gm * rows, (gm + 1) * rows)
                bias = jnp.concatenate([bs_ref[m * C_HEADS + g * C_GROUP + r, :, pl.ds(c0, w)]
                                        for r in range(C_GROUP)], axis=0)
                s = _nt(qs[gm], k_fn(g, m).astype(MXU_DT)) * SCALE + bias
                if mask is not None:
                    s = jnp.where(mask, s, NEG)
                mn = jnp.maximum(m_all[rs], jnp.max(s, axis=-1, keepdims=True))
                p = jnp.exp(s - mn)
                if mask is not None:
                    p = jnp.where(mask, p, 0.0)
                a = jnp.exp(m_all[rs] - mn)
                new_m.append(mn)
                new_l.append(a * l_all[rs] + jnp.sum(p, axis=-1, keepdims=True))
                ps.append(p)
                alphas.append(a)
            pst = jnp.concatenate(ps, axis=0).astype(MXU_DT)
            pv = jnp.concatenate([_mm(pst, v_fn(g, h).astype(MXU_DT)) for h in range(2)], axis=1)
            for m in range(2):
                rs = slice((g * 2 + m) * rows, (g * 2 + m + 1) * rows)
                new_acc.append(alphas[m] * acc_all[rs] + pv[m * rows:(m + 1) * rows])
        return (jnp.concatenate(new_m, axis=0), jnp.concatenate(new_l, axis=0),
                jnp.concatenate(new_acc, axis=0))

    def chunk(c, carry):
        slot = c % 2
        for cpy in copies(b, c, slot):
            cpy.wait()

        @pl.when(c + 1 < n_ch)
        def _():
            for cpy in copies(b, c + 1, 1 - slot):
                cpy.start()

        @pl.when((c + 1 == n_ch) & (b + 1 < nb))
        def _():
            for cpy in copies(b + 1, 0, 1 - slot):
                cpy.start()

        base = slot * slot_rows
        return update(carry,
                      lambda g, m: kbuf[pl.ds(base + g * 2 + m, ch, stride=pieces), :],
                      lambda g, h: vbuf[pl.ds(base + h * C_KV + g, ch, stride=pieces), :],
                      pl.multiple_of(c * ch, ch), ch, None)

    n_rows = pieces * rows
    carry = (jnp.full((n_rows, 1), NEG, F32), jnp.zeros((n_rows, 1), F32), jnp.zeros((n_rows, C_VDIM), F32))
    carry = lax.fori_loop(0, n_ch, chunk, carry)
    _, l_all, acc_all = update(
        carry,
        lambda g, m: _pad_rows(kn_ref[:, (g * 2 + m) * HEAD_DIM:(g * 2 + m + 1) * HEAD_DIM], 128),
        lambda g, h: _pad_rows(vn_ref[:, g * C_VDIM + h * HEAD_DIM:g * C_VDIM + (h + 1) * HEAD_DIM], 128),
        past, 128, _new_key_mask(nq, C_GROUP))
    o_all = acc_all / l_all
    lam = _diff_lambda(lam_ref)
    for g in range(C_KV):
        r0 = g * 2 * rows
        o = _diff_finish(o_all[r0:r0 + rows] - lam * o_all[r0 + rows:r0 + 2 * rows], hn_ref)
        for r in range(C_GROUP):
            col = (g * C_GROUP + r) * C_VDIM
            o_ref[:, col:col + C_VDIM] = o[r * nq:(r + 1) * nq]


def diff_sample(z1s, pool_k, pool_v, bs, lam_vecs, head_norm, page_table, cp=16):
    bd, n_pages = page_table.shape
    nq = z1s.shape[0] // bd
    assert n_pages % (2 * cp) == 0
    slot_rows = cp * PAGE * C_KV * 2
    q_cols = C_HEADS * 2 * HEAD_DIM
    kv_cols = C_KV * C_VDIM
    return pl.pallas_call(
        functools.partial(_diff_sample_kernel, n_pages=n_pages, cp=cp),
        grid_spec=pltpu.PrefetchScalarGridSpec(
            num_scalar_prefetch=1, grid=(bd,),
            in_specs=[pl.BlockSpec((nq, q_cols), lambda i, pt: (i, 0)),
                      pl.BlockSpec((nq, kv_cols), lambda i, pt: (i, q_cols // kv_cols)),
                      pl.BlockSpec((nq, kv_cols), lambda i, pt: (i, q_cols // kv_cols + 1)),
                      pl.BlockSpec(memory_space=pl.ANY), pl.BlockSpec(memory_space=pl.ANY),
                      pl.BlockSpec(bs.shape, lambda i, pt: (0, 0, 0)),
                      pl.BlockSpec((4, HEAD_DIM), lambda i, pt: (0, 0)),
                      pl.BlockSpec((1, C_VDIM), lambda i, pt: (0, 0))],
            out_specs=pl.BlockSpec((nq, C_HEADS * C_VDIM), lambda i, pt: (i, 0)),
            scratch_shapes=[pltpu.VMEM((2 * slot_rows, HEAD_DIM), F32), pltpu.VMEM((2 * slot_rows, HEAD_DIM), F32),
                            pltpu.SemaphoreType.DMA((2, 2))]),
        out_shape=jax.ShapeDtypeStruct((bd * nq, C_HEADS * C_VDIM), F32),
        compiler_params=_cparams("arbitrary"),
        name="diff_sample",
    )(page_table, z1s, z1s, z1s, pool_k, pool_v, bs, lam_vecs, head_norm.reshape(1, C_VDIM))


def _row_tile(rows):
    tm = min(rows, ROW_TILE)
    assert rows % tm == 0
    return tm


def _reorder_l0_weight(w):
    sizes = (A_HEADS * HEAD_DIM,) + (A_KV * HEAD_DIM,) * 6 + (
        N_GATES * A_HEADS, B_HEADS * HEAD_DIM, B_KV * HEAD_DIM, B_KV * HEAD_DIM,
        IDX_HEADS * IDX_DIM, IDX_DIM, IDX_HEADS)
    offs = [0]
    for s in sizes:
        offs.append(offs[-1] + s)
    piece = lambda i, j=None: w[:, offs[i]:offs[(i if j is None else j) + 1]]
    qa, six, ga, qb, kvb, qi, ki, wi = piece(0), piece(1, 6), piece(7), piece(8), piece(9, 10), piece(11), \
        piece(12), piece(13)
    pad = jnp.zeros((w.shape[0], L0_COLS - offs[-1]), w.dtype)
    return jnp.concatenate([qa, qb, six, kvb, qi, ki, ga, wi, pad], axis=1).astype(MXU_DT)


def _compress_weights(pe, w1, w2):
    half = CMP_STRIDE * HEAD_DIM
    w1 = w1.reshape(2, half, CMP_HIDDEN).astype(MXU_DT)
    pe_rows = jnp.zeros((16, half), F32).at[0:2].set(pe.reshape(2, half))
    return w1[0], w1[1], w2.astype(MXU_DT), pe_rows


def kernel(x_prompt, x_sample, cache_l0_nsa_cmp_k, cache_l0_nsa_cmp_v, cache_l0_nsa_slc_k, cache_l0_nsa_slc_v, state_l0_nsa_win_k, state_l0_nsa_win_v, cache_l0_dsa_k, cache_l0_dsa_v, cache_l0_dsa_idx_k, cache_l1_diff_k, cache_l1_diff_v, page_table, rel_bias, attn_norm, mlp_norm, mlp_w1, mlp_w2, l0_w_in, l0_w_out, l0_cmp_pe_k, l0_cmp_w1_k, l0_cmp_w2_k, l0_cmp_pe_v, l0_cmp_w1_v, l0_cmp_w2_v, l1_w_in, l1_w_out, l1_lambda_q1, l1_lambda_k1, l1_lambda_q2, l1_lambda_k2, l1_head_norm, final_norm):
    n, t_len, d = x_prompt.shape
    bd, nq, _ = x_sample.shape
    n_pool = cache_l0_nsa_cmp_k.shape[0]
    n_pages = page_table.shape[1]
    past = n_pages * PAGE
    lp = past + 128
    kv_w = A_KV * HEAD_DIM
    assert t_len % CHUNK == 0 and t_len >= WINDOW + QB and nq <= 8
    assert state_l0_nsa_win_k.shape[1] == min(WINDOW, past)

    xp = x_prompt.reshape(n * t_len, d)
    xs = x_sample.reshape(bd * nq, d)
    tmp, tms = _row_tile(xp.shape[0]), _row_tile(xs.shape[0])
    w0 = _reorder_l0_weight(l0_w_in)
    cw = [jnp.stack(pair) for pair in zip(_compress_weights(l0_cmp_pe_k, l0_cmp_w1_k, l0_cmp_w2_k),
                                          _compress_weights(l0_cmp_pe_v, l0_cmp_w1_v, l0_cmp_w2_v))]
    lam_vecs = jnp.stack([l1_lambda_q1, l1_lambda_k1, l1_lambda_q2, l1_lambda_k2])
    bf = lambda a: a.astype(MXU_DT)

    tp, bs = bias_tiles(rel_bias, past, nq, lp)
    bc_p, bc_s = bias_cmp(rel_bias, t_len, t_len // CMP_STRIDE, past, nq, past // CMP_STRIDE)

    zp = norm_proj(xp, attn_norm[0], w0, tmp, L0_COL_TILE)
    zs = norm_proj(xs, attn_norm[0], w0, tms, L0_COL_TILE)
    cut = lambda z, c, w: z[:, c:c + w]
    p_rows = {name: cut(zp, c, kv_w) for name, c in
              (("kc", C_KC), ("vc", C_VC), ("ks", C_KS), ("vs", C_VS), ("kw", C_KW), ("vw", C_VW),
               ("kb", C_KB), ("vb", C_VB))}
    s_rows = {name: cut(zs, c, kv_w) for name, c in
              (("kc", C_KC), ("vc", C_VC), ("ks", C_KS), ("vs", C_VS), ("kw", C_KW), ("vw", C_VW),
               ("kb", C_KB), ("vb", C_VB))}
    chunk_w = CMP_STRIDE * kv_w
    kc_p, vc_p = compress_prompt(p_rows["kc"].reshape(n, t_len // CMP_STRIDE, chunk_w),
                                 p_rows["vc"].reshape(n, t_len // CMP_STRIDE, chunk_w), cw)
    lanes = lambda a: a.reshape(-1, HEAD_DIM)
    kv_cmp_s = compress_sample(lanes(cache_l0_nsa_cmp_k), lanes(cache_l0_nsa_cmp_v), page_table, cw)
    oa_p = nsa_prompt(zp, kc_p, vc_p, tp, bc_p, rel_bias, n, t_len)
    ob_p = dsa_prompt(zp, tp, rel_bias, n, t_len)
    wb = state_l0_nsa_win_k.shape[1]
    oa_s = nsa_sample(zs, kv_cmp_s, lanes(cache_l0_nsa_slc_k), lanes(cache_l0_nsa_slc_v),
                      lanes(state_l0_nsa_win_k), lanes(state_l0_nsa_win_v), bs, bc_s, page_table)
    ob_s = dsa_sample(zs, lanes(cache_l0_dsa_k), lanes(cache_l0_dsa_v),
                      jnp.swapaxes(cache_l0_dsa_idx_k, 1, 2).reshape(-1, PAGE), bs, page_table)
    w_out0 = bf(l0_w_out)
    w1_0, w2_0 = bf(mlp_w1[0]), bf(mlp_w2[0])
    xp = out_proj(xp, [oa_p, ob_p], w_out0, tmp, COL_TILE)
    xs = out_proj(xs, [oa_s, ob_s], w_out0, tms, COL_TILE)
    xp = mlp(xp, mlp_norm[0], w1_0, w2_0, final_norm, tmp, FF_TILE, False)
    xs = mlp(xs, mlp_norm[0], w1_0, w2_0, final_norm, tms, FF_TILE, False)

    w_in1 = bf(l1_w_in)
    z1p = norm_proj(xp, attn_norm[1], w_in1, tmp, COL_TILE)
    z1s = norm_proj(xs, attn_norm[1], w_in1, tms, COL_TILE)
    o1_p = diff_prompt(z1p, tp, rel_bias, lam_vecs, l1_head_norm, n, t_len)
    v_halves = cache_l1_diff_v.reshape(n_pool, PAGE, C_KV, 2, HEAD_DIM).transpose(0, 1, 3, 2, 4)
    o1_s = diff_sample(z1s, lanes(cache_l1_diff_k), lanes(v_halves), bs, lam_vecs, l1_head_norm, page_table)
    w_out1 = bf(l1_w_out)
    w1_1, w2_1 = bf(mlp_w1[1]), bf(mlp_w2[1])
    xp = out_proj(xp, [o1_p], w_out1, tmp, COL_TILE)
    xs = out_proj(xs, [o1_s], w_out1, tms, COL_TILE)
    y_prompt = mlp(xp, mlp_norm[1], w1_1, w2_1, final_norm, tmp, FF_TILE, True).reshape(n, t_len, d)
    y_sample = mlp(xs, mlp_norm[1], w1_1, w2_1, final_norm, tms, FF_TILE, True).reshape(bd, nq, d)

    row4 = lambda a, b: a.reshape(b, -1, A_KV, HEAD_DIM)
    win = min(WINDOW, t_len)
    outs = [y_prompt, y_sample]
    for name in ("kc", "vc", "ks", "vs"):
        outs += [row4(p_rows[name], n), row4(s_rows[name], bd)]
    for name, state in (("kw", state_l0_nsa_win_k), ("vw", state_l0_nsa_win_v)):
        outs += [row4(p_rows[name], n)[:, t_len - win:],
                 jnp.concatenate([state, row4(s_rows[name], bd)], axis=1)[:, -wb:]]
    for name in ("kb", "vb"):
        outs += [row4(p_rows[name], n), row4(s_rows[name], bd)]
    outs += [cut(zp, C_TAIL + T_KI, IDX_DIM).reshape(n, t_len, IDX_DIM),
             cut(zs, C_TAIL + T_KI, IDX_DIM).reshape(bd, nq, IDX_DIM)]
    k_cols, v_cols = C_KV * 2 * HEAD_DIM, C_KV * C_VDIM
    q_cols = C_HEADS * 2 * HEAD_DIM
    outs += [cut(z1p, q_cols, k_cols).reshape(n, t_len, C_KV, 2, HEAD_DIM),
             cut(z1s, q_cols, k_cols).reshape(bd, nq, C_KV, 2, HEAD_DIM),
             cut(z1p, q_cols + k_cols, v_cols).reshape(n, t_len, C_KV, C_VDIM),
             cut(z1s, q_cols + k_cols, v_cols).reshape(bd, nq, C_KV, C_VDIM)]
    return tuple(outs)
```

```python
import functools
import math

import jax
import jax.numpy as jnp
from jax import lax
from jax.experimental import pallas as pl
from jax.experimental.pallas import tpu as pltpu

F32 = jnp.float32
I32 = jnp.int32
MXU_DT = jnp.bfloat16

HEAD_DIM = 128
A_HEADS, A_KV, A_GROUP = 8, 2, 4
B_HEADS, B_KV, B_GROUP = 8, 2, 4
C_HEADS, C_KV, C_GROUP, C_VDIM = 8, 4, 2, 256
CMP_STRIDE, CMP_BLOCK, CMP_HIDDEN = 16, 32, 256
SEL_BLOCK, N_SEL_BLOCKS, WINDOW, N_GATES = 64, 16, 512, 3
IDX_HEADS, IDX_DIM, DSA_TOPK = 4, 64, 256
NUM_BUCKETS, MAX_DISTANCE = 32, 128
LAMBDA_INIT = 0.8 - 0.6 * math.exp(-0.3 * 1)
RMS_EPS = 1e-6
NEG = -1e30
SCALE = HEAD_DIM ** -0.5
QB = 128
CHUNK = 512
PAGE = 128
PAGE_PITCH = PAGE * A_KV + 8
DIFF_SLOTS = 3
assert QB >= MAX_DISTANCE and WINDOW % QB == 0 and WINDOW >= 2 * QB and 2 * SEL_BLOCK == QB

C_QA, C_QB, C_KC, C_VC, C_KS, C_VS, C_KW, C_VW, C_KB, C_VB, C_QI, C_TAIL = (
    0, 1024, 2048, 2304, 2560, 2816, 3072, 3328, 3584, 3840, 4096, 4352)
T_KI, T_GA, T_WI = 0, 64, 88
L0_COLS = 4608
VMEM_LIMIT = 56 * 1024 * 1024
ROW_TILE = 1024
L0_COL_TILE = 1536
COL_TILE = 1024
FF_TILE = 512


def _cparams(*sem):
    return pltpu.CompilerParams(dimension_semantics=sem, vmem_limit_bytes=VMEM_LIMIT)


def _nt(a, b):
    return lax.dot_general(a, b, (((1,), (1,)), ((), ())), preferred_element_type=F32)


def _mm(a, b):
    return jnp.dot(a, b, preferred_element_type=F32)


def _rms(x, g):
    return x * lax.rsqrt(jnp.mean(x * x, axis=-1, keepdims=True) + RMS_EPS) * g


def _norm_proj_kernel(x_ref, g_ref, w_ref, o_ref, xn_ref):
    @pl.when(pl.program_id(1) == 0)
    def _():
        xn_ref[...] = _rms(x_ref[...], g_ref[...]).astype(xn_ref.dtype)

    o_ref[...] = _mm(xn_ref[...], w_ref[...])


def norm_proj(x, gain, w, tm, tn):
    rows, d = x.shape
    n = w.shape[1]
    return pl.pallas_call(
        _norm_proj_kernel,
        grid=(rows // tm, n // tn),
        in_specs=[pl.BlockSpec((tm, d), lambda i, j: (i, 0)),
                  pl.BlockSpec((1, d), lambda i, j: (0, 0)),
                  pl.BlockSpec((d, tn), lambda i, j: (0, j))],
        out_specs=pl.BlockSpec((tm, tn), lambda i, j: (i, j)),
        out_shape=jax.ShapeDtypeStruct((rows, n), F32),
        scratch_shapes=[pltpu.VMEM((tm, d), MXU_DT)],
        compiler_params=_cparams("parallel", "arbitrary"),
        name="norm_proj",
    )(x, gain.reshape(1, d), w)


def _out_proj_kernel(*refs, n_in):
    x_ref, o_refs, w_refs, y_ref = refs[0], refs[1:1 + n_in], refs[1 + n_in:1 + 2 * n_in], refs[-1]
    acc = x_ref[...]
    for o_ref, w_ref in zip(o_refs, w_refs):
        acc = acc + _mm(o_ref[...].astype(MXU_DT), w_ref[...])
    y_ref[...] = acc


def out_proj(x, outs, w, tm, tn):
    rows, d = x.shape
    o_specs, w_specs, row0 = [], [], 0
    for o in outs:
        k = o.shape[1]
        o_specs.append(pl.BlockSpec((tm, k), lambda i, j: (i, 0)))
        w_specs.append(pl.BlockSpec((k, tn), lambda i, j, rb=row0 // k: (rb, j)))
        row0 += k
    return pl.pallas_call(
        functools.partial(_out_proj_kernel, n_in=len(outs)),
        grid=(rows // tm, d // tn),
        in_specs=[pl.BlockSpec((tm, tn), lambda i, j: (i, j))] + o_specs + w_specs,
        out_specs=pl.BlockSpec((tm, tn), lambda i, j: (i, j)),
        out_shape=jax.ShapeDtypeStruct((rows, d), F32),
        compiler_params=_cparams("parallel", "arbitrary"),
        name="out_proj",
    )(x, *outs, *([w] * len(outs)))


def _mlp_kernel(x_ref, g_ref, w1_ref, w2_ref, gf_ref, y_ref, xn_ref, *, final_norm):
    j = pl.program_id(1)

    @pl.when(j == 0)
    def _():
        x = x_ref[...]
        xn_ref[...] = _rms(x, g_ref[...]).astype(xn_ref.dtype)
        y_ref[...] = x

    h = jnp.square(jnp.maximum(_mm(xn_ref[...], w1_ref[...]), 0.0))
    y_ref[...] += _mm(h.astype(w2_ref.dtype), w2_ref[...])

    if final_norm:
        @pl.when(j == pl.num_programs(1) - 1)
        def _():
            y_ref[...] = _rms(y_ref[...], gf_ref[...])


def mlp(x, gain, w1, w2, final_gain, tm, tf, final_norm):
    rows, d = x.shape
    ff = w1.shape[1]
    return pl.pallas_call(
        functools.partial(_mlp_kernel, final_norm=final_norm),
        grid=(rows // tm, ff // tf),
        in_specs=[pl.BlockSpec((tm, d), lambda i, j: (i, 0)),
                  pl.BlockSpec((1, d), lambda i, j: (0, 0)),
                  pl.BlockSpec((d, tf), lambda i, j: (0, j)),
                  pl.BlockSpec((tf, d), lambda i, j: (j, 0)),
                  pl.BlockSpec((1, d), lambda i, j: (0, 0))],
        out_specs=pl.BlockSpec((tm, d), lambda i, j: (i, 0)),
        out_shape=jax.ShapeDtypeStruct((rows, d), F32),
        scratch_shapes=[pltpu.VMEM((tm, d), MXU_DT)],
        compiler_params=_cparams("parallel", "arbitrary"),
        name="mlp",
    )(x, gain.reshape(1, d), w1, w2, final_gain.reshape(1, d))


def _bucket(dist):
    n = jnp.maximum(dist, 0)
    max_exact = NUM_BUCKETS // 2
    nf = jnp.maximum(n, 1).astype(F32)
    large = max_exact + (jnp.log(nf / max_exact) / math.log(MAX_DISTANCE / max_exact)
                         * (NUM_BUCKETS - max_exact)).astype(I32)
    large = jnp.minimum(large, NUM_BUCKETS - 1)
    return jnp.where(n < max_exact, n, large)


def _lookup(tbl_ref, col, buckets):
    def body(b, accs):
        v = tbl_ref[b, col]
        return tuple(jnp.where(bk == b, v, acc) for bk, acc in zip(buckets, accs))
    return lax.fori_loop(0, NUM_BUCKETS, body, tuple(jnp.zeros(bk.shape, F32) for bk in buckets))


def _bias_tiles_kernel(tbl_ref, tp_ref, bs_ref, *, q0):
    h = pl.program_id(0)
    t = lax.broadcasted_iota(I32, (QB, QB), 0)
    k = lax.broadcasted_iota(I32, (QB, QB), 1)
    ts = lax.broadcasted_iota(I32, bs_ref.shape[1:], 0)
    ks = lax.broadcasted_iota(I32, bs_ref.shape[1:], 1)
    d0, d1, ds = _lookup(tbl_ref, h, (_bucket(t - k), _bucket(QB + t - k), _bucket(q0 + ts - ks)))
    tp_ref[0, 0] = d0
    tp_ref[0, 1] = d1
    bs_ref[0] = ds


def bias_tiles(rel_bias, q0, n_q, lp):
    nh = rel_bias.shape[1]
    return pl.pallas_call(
        functools.partial(_bias_tiles_kernel, q0=q0),
        grid=(nh,),
        in_specs=[pl.BlockSpec(memory_space=pltpu.SMEM)],
        out_specs=[pl.BlockSpec((1, 2, QB, QB), lambda h: (h, 0, 0, 0)),
                   pl.BlockSpec((1, n_q, lp), lambda h: (h, 0, 0))],
        out_shape=[jax.ShapeDtypeStruct((nh, 2, QB, QB), F32),
                   jax.ShapeDtypeStruct((nh, n_q, lp), F32)],
        compiler_params=_cparams("arbitrary"),
        name="bias_tiles",
    )(rel_bias)


def _bias_cmp_kernel(tbl_ref, bp_ref, bs_ref, *, q0):
    h = pl.program_id(0)
    tp = lax.broadcasted_iota(I32, bp_ref.shape[1:], 0)
    cp = lax.broadcasted_iota(I32, bp_ref.shape[1:], 1)
    ts = lax.broadcasted_iota(I32, bs_ref.shape[1:], 0)
    cs = lax.broadcasted_iota(I32, bs_ref.shape[1:], 1)
    end = CMP_BLOCK - 1
    bp, bs = _lookup(tbl_ref, h, (_bucket(tp - (cp * CMP_STRIDE + end)),
                                  _bucket(q0 + ts - (cs * CMP_STRIDE + end))))
    bp_ref[0] = bp
    bs_ref[0] = bs


def bias_cmp(rel_bias, t_len, mc_p, q0, n_q, mc_s):
    return pl.pallas_call(
        functools.partial(_bias_cmp_kernel, q0=q0),
        grid=(A_HEADS,),
        in_specs=[pl.BlockSpec(memory_space=pltpu.SMEM)],
        out_specs=[pl.BlockSpec((1, t_len, mc_p), lambda h: (h, 0, 0)),
                   pl.BlockSpec((1, n_q, mc_s), lambda h: (h, 0, 0))],
        out_shape=[jax.ShapeDtypeStruct((A_HEADS, t_len, mc_p), F32),
                   jax.ShapeDtypeStruct((A_HEADS, n_q, mc_s), F32)],
        compiler_params=_cparams("arbitrary"),
        name="bias_cmp",
    )(rel_bias)


def _row_reduce(fn, lane_fn, x):
    tiles = [x[:, i:i + 128] for i in range(0, x.shape[1], 128)]
    return lane_fn(_tree_reduce(fn, tiles), axis=-1, keepdims=True)


def _softmax_rows(z, mask):
    z = jnp.where(mask, z, NEG)
    e = jnp.where(mask, jnp.exp(z - _row_reduce(jnp.maximum, jnp.max, z)), 0.0)
    l = _row_reduce(jnp.add, jnp.sum, e)
    return e * (1.0 / jnp.where(l > 0.0, l, 1.0))


def _gelu_tanh(x):
    return 0.5 * x * (1.0 + jnp.tanh(math.sqrt(2.0 / math.pi) * (x + 0.044715 * (x * x * x))))


def _compress(x_fn, m, w1a_ref, w1b_ref, w2_ref, pe_ref, next_fn, out_fn):
    pe = pe_ref[...].astype(MXU_DT)
    pos = _mm(pe, w1a_ref[...])[0:1] + _mm(pe, w1b_ref[...])[1:2]
    last = lax.broadcasted_iota(I32, (m, 1), 0) == m - 1
    for g in range(A_KV):
        xg = jnp.concatenate([x_fn(j, g).astype(MXU_DT) for j in range(CMP_STRIDE)], axis=1)
        hid = _gelu_tanh(_mm(xg, w1a_ref[...]) + next_fn(_mm(xg, w1b_ref[...])) + pos)
        out_fn(g, jnp.where(last, 0.0, _mm(hid.astype(MXU_DT), w2_ref[...])))


def _overlap(mc, jn, n_cmp, n_slc):
    c = lax.broadcasted_iota(I32, (mc, jn), 0)
    j = lax.broadcasted_iota(I32, (mc, jn), 1)
    ov = ((c * CMP_STRIDE < j * SEL_BLOCK + SEL_BLOCK) & (c * CMP_STRIDE + CMP_BLOCK > j * SEL_BLOCK)
          & (c < n_cmp) & (j < n_slc))
    return jnp.where(ov, 1.0, 0.0)


def _select_blocks(imp, pos, n_slc):
    jn = imp.shape[1]
    jidx = lax.broadcasted_iota(I32, (1, jn), 1)
    cur = pos // SEL_BLOCK
    forced = (jidx == 0) | (jidx == cur) | (jidx == cur - 1)
    future = jidx * SEL_BLOCK > pos
    score = jnp.where(future, -1.0, jnp.where(forced, 1e3, imp))
    score = jnp.where(jidx < n_slc, score, -2.0)

    def body(i, rank):
        col = jnp.sum(jnp.where(jidx == i, score, 0.0), axis=-1, keepdims=True)
        beats = jnp.where(col > score, 1.0, jnp.where(col == score, jnp.where(i < jidx, 1.0, 0.0), 0.0))
        return rank + beats

    rank = lax.fori_loop(0, n_slc, body, jnp.zeros(score.shape, F32), unroll=32)
    n_sel = min(N_SEL_BLOCKS, n_slc)
    return jnp.where((rank < n_sel) & (jidx < n_slc), 1.0, 0.0)


def _tree_reduce(fn, xs):
    xs = list(xs)
    while len(xs) > 1:
        xs = [fn(xs[i], xs[i + 1]) for i in range(0, len(xs) - 1, 2)] + ([xs[-1]] if len(xs) % 2 else [])
    return xs[0]


def _tree_sum(xs):
    return _tree_reduce(jnp.add, xs)


def _sort_key(s):
    bits = lax.bitcast_convert_type(jnp.where(s == 0.0, 0.0, s), I32)
    return jnp.where(bits < 0, bits ^ jnp.int32(0x7FFFFFFF), bits)


def _topk_madd(key_ref, madd_ref, valid_fn, nch, cw, k, nbits):
    n_rows = key_ref.shape[0]
    kf = jnp.float32(k)

    def count(fn):
        def body(c, acc):
            c0 = pl.multiple_of(c * cw, cw)
            hit = jnp.where(fn(c0, key_ref[:, pl.ds(c0, cw)]), 1.0, 0.0)
            return acc + _tree_sum(hit[:, i:i + 128] for i in range(0, cw, 128))
        acc = lax.fori_loop(0, nch, body, jnp.zeros((n_rows, 128), F32))
        return jnp.sum(acc, axis=-1, keepdims=True)

    int_min = jnp.int32(-2 ** 31)
    thr0 = jnp.where(count(lambda c0, key: key >= 0) >= kf, jnp.int32(0), int_min)

    def vbody(i, thr):
        cand = thr | lax.shift_left(jnp.int32(1), 30 - i)
        return jnp.where(count(lambda c0, key: key >= cand) >= kf, cand, thr)

    thr = lax.fori_loop(0, 31, vbody, thr0)
    need = kf - count(lambda c0, key: key > thr)

    def idx(c0):
        return c0 + lax.broadcasted_iota(I32, (1, cw), 1)

    def ibody(i, cut):
        cand = cut | lax.shift_left(jnp.int32(1), nbits - 1 - i)
        return jnp.where(count(lambda c0, key: (key == thr) & (idx(c0) < cand)) <= need, cand, cut)

    tied = jnp.max(count(lambda c0, key: key == thr) - need) > 0.0
    cut = lax.cond(tied, lambda: lax.fori_loop(0, nbits, ibody, jnp.zeros((n_rows, 1), I32)),
                   lambda: jnp.full((n_rows, 1), 2 ** nbits - 1, I32))

    def write(c, _):
        c0 = pl.multiple_of(c * cw, cw)
        key = key_ref[:, pl.ds(c0, cw)]
        sel = ((key > thr) | ((key == thr) & (idx(c0) < cut))) & valid_fn(c0)
        madd_ref[:, pl.ds(c0, cw)] = jnp.where(sel, 0.0, NEG)
        return 0

    lax.fori_loop(0, nch, write, 0)


def _topk_madd_t(key_ref, madd_ref, valid_fn, nch, cw, k, nbits):
    n_rows = key_ref.shape[1]
    kf = jnp.float32(k)

    def count(fn):
        def body(c, acc):
            c0 = pl.multiple_of(c * cw, cw)
            hit = jnp.where(fn(c0, key_ref[pl.ds(c0, cw), :]), 1.0, 0.0)
            return acc + _tree_sum(hit[i:i + 8] for i in range(0, cw, 8))
        acc = lax.fori_loop(0, nch, body, jnp.zeros((8, n_rows), F32))
        return jnp.sum(acc, axis=0, keepdims=True)

    int_min = jnp.int32(-2 ** 31)
    thr0 = jnp.where(count(lambda c0, key: key >= 0) >= kf, jnp.int32(0), int_min)

    def vbody(i, thr):
        cand = thr | lax.shift_left(jnp.int32(1), 30 - i)
        return jnp.where(count(lambda c0, key: key >= cand) >= kf, cand, thr)

    thr = lax.fori_loop(0, 31, vbody, thr0)
    need = kf - count(lambda c0, key: key > thr)

    def idx(c0):
        return c0 + lax.broadcasted_iota(I32, (cw, 1), 0)

    def ibody(i, cut):
        cand = cut | lax.shift_left(jnp.int32(1), nbits - 1 - i)
        return jnp.where(count(lambda c0, key: (key == thr) & (idx(c0) < cand)) <= need, cand, cut)

    tied = jnp.max(count(lambda c0, key: key == thr) - need) > 0.0
    cut = lax.cond(tied, lambda: lax.fori_loop(0, nbits, ibody, jnp.zeros((1, n_rows), I32)),
                   lambda: jnp.full((1, n_rows), 2 ** nbits - 1, I32))

    def write(c, _):
        c0 = pl.multiple_of(c * cw, cw)
        key = key_ref[pl.ds(c0, cw), :]
        sel = ((key > thr) | ((key == thr) & (idx(c0) < cut))) & valid_fn(c0)
        madd_ref[:, pl.ds(c0, cw)] = jnp.where(sel, 0.0, NEG).T
        return 0

    lax.fori_loop(0, nch, write, 0)


def _causal_attn(streams, dv, qb, pairs=False):
    m_rows = streams[0][0].shape[0]
    per = CHUNK // QB
    nact = qb // per + 1

    def stage(st, k0, w, bias):
        q, k_fn, _, _, _, madd_fn, s_ref = st
        s = _nt(q, k_fn(k0, w)) * SCALE + bias
        madd = madd_fn(k0, w)
        s_ref[:, pl.ds(k0, w)] = s if madd is None else s + madd

    def chunk_loop(body, init):
        n2 = nact // 2 if pairs else 0
        carry = lax.fori_loop(
            0, n2, lambda i, c: body(pl.multiple_of(i * 2 * CHUNK, 2 * CHUNK), 2 * CHUNK, c), init) if pairs else init
        return lax.fori_loop(
            0, nact - 2 * n2, lambda i, c: body(pl.multiple_of((2 * n2 + i) * CHUNK, CHUNK), CHUNK, c), carry)

    def lane_tiles(x):
        return [x[:, i:i + 128] for i in range(0, x.shape[1], 128)]

    def far(k0, w, _):
        for st in streams:
            stage(st, k0, w, st[3])
        return 0

    chunk_loop(far, 0)
    for st in streams:
        stage(st, pl.multiple_of(qb * QB, QB), QB, st[4](True))

    @pl.when(qb >= 1)
    def _():
        for st in streams:
            stage(st, pl.multiple_of((qb - 1) * QB, QB), QB, st[4](False))

    for j in range(1, per):
        @pl.when(qb % per + j < per)
        def _():
            for st in streams:
                st[6][:, pl.ds(pl.multiple_of((qb + j) * QB, QB), QB)] = jnp.full((m_rows, QB), NEG, F32)

    stat_w = 128 if pairs else 1

    def lanes_or_row(fn, lane_fn, x):
        return _tree_reduce(fn, lane_tiles(x)) if pairs else lane_fn(x, axis=-1, keepdims=True)

    def row_max(k0, w, ms):
        return tuple(jnp.maximum(m, lanes_or_row(jnp.maximum, jnp.max, st[6][:, pl.ds(k0, w)]))
                     for st, m in zip(streams, ms))

    ms = chunk_loop(row_max, tuple(jnp.full((m_rows, stat_w), NEG, F32) for _ in streams))
    ms = [jnp.max(m, axis=-1, keepdims=True) for m in ms]

    def pv(k0, w, carry):
        out = []
        for st, m, (l, acc) in zip(streams, ms, carry):
            p = jnp.exp(st[6][:, pl.ds(k0, w)] - m)
            out.append((l + lanes_or_row(jnp.add, jnp.sum, p), acc + _mm(p.astype(MXU_DT), st[2](k0, w))))
        return tuple(out)

    init = tuple((jnp.zeros((m_rows, stat_w), F32), jnp.zeros((m_rows, dv), F32)) for _ in streams)
    return [acc / jnp.sum(l, axis=-1, keepdims=True) for l, acc in chunk_loop(pv, init)]


def _causal_add(rep):
    t = lax.broadcasted_iota(I32, (QB, QB), 0)
    k = lax.broadcasted_iota(I32, (QB, QB), 1)
    return jnp.concatenate([jnp.where(k <= t, 0.0, NEG)] * rep, axis=0)


def _far_bias(tbl_ref, cols):
    return jnp.concatenate([jnp.full((QB, 1), tbl_ref[NUM_BUCKETS - 1, c], F32) for c in cols], axis=0)


def _near_bias(tp_ref, cols, diag):
    return jnp.concatenate([tp_ref[c, 0 if diag else 1] for c in cols], axis=0)


def _stack_heads(ref, col0, n):
    return jnp.concatenate([ref[:, col0 + r * HEAD_DIM:col0 + (r + 1) * HEAD_DIM] for r in range(n)], axis=0)


def _compress_prompt_kernel(xk_ref, xv_ref, w1a, w1b, w2, pe, ok_ref, ov_ref):
    lanes = lambda x_ref: (lambda j, g: x_ref[0, :, (2 * j + g) * HEAD_DIM:(2 * j + g + 1) * HEAD_DIM])
    m = xk_ref.shape[1]
    for i, (x_ref, o_ref) in enumerate(((xk_ref, ok_ref), (xv_ref, ov_ref))):
        def store(g, tokens, o_ref=o_ref):
            o_ref[0, :, g * HEAD_DIM:(g + 1) * HEAD_DIM] = tokens

        _compress(lanes(x_ref), m, w1a.at[i], w1b.at[i], w2.at[i], pe.at[i],
                  lambda y: pltpu.roll(y, m - 1, 0), store)


def _cmp_weight_specs():
    full = lambda *shape: pl.BlockSpec(shape, lambda *_: (0,) * len(shape))
    half = CMP_STRIDE * HEAD_DIM
    return [full(2, half, CMP_HIDDEN), full(2, half, CMP_HIDDEN), full(2, CMP_HIDDEN, HEAD_DIM), full(2, 16, half)]


def compress_prompt(xk, xv, cw):
    n, m, w = xk.shape
    spec = pl.BlockSpec((1, m, w), lambda i: (i, 0, 0))
    ospec = pl.BlockSpec((1, m, A_KV * HEAD_DIM), lambda i: (i, 0, 0))
    osh = jax.ShapeDtypeStruct((n, m, A_KV * HEAD_DIM), F32)
    return pl.pallas_call(
        _compress_prompt_kernel,
        grid=(n,),
        in_specs=[spec, spec] + _cmp_weight_specs(),
        out_specs=[ospec, ospec],
        out_shape=[osh, osh],
        compiler_params=_cparams("parallel"),
        name="compress_prompt",
    )(xk, xv, *cw)


def _window_attn(q, kw_ref, vw_ref, gl, qb, pos4, tbl_ref, tp_ref, cols):
    n_tiles = WINDOW // QB + 1
    width = n_tiles * QB
    lo = jnp.maximum(qb - (n_tiles - 1), 0)
    w0 = pl.multiple_of(lo * QB, QB)
    tiles = []
    for j in range(n_tiles):
        rel = qb - (lo + j)
        tiles.append(jnp.concatenate(
            [jnp.where(rel == 0, tp_ref[c, 0], jnp.where(rel == 1, tp_ref[c, 1], tbl_ref[NUM_BUCKETS - 1, c]))
             for c in cols], axis=0))
    s = _nt(q, kw_ref[pl.ds(w0, width), gl].astype(MXU_DT)) * SCALE + jnp.concatenate(tiles, axis=1)
    dist = pos4 - (w0 + lax.broadcasted_iota(I32, (1, width), 1))
    p = _softmax_rows(s, (dist >= 0) & (dist < WINDOW))
    return _mm(p.astype(MXU_DT), vw_ref[pl.ds(w0, width), gl].astype(MXU_DT))


def _nsa_prompt_kernel(tbl_ref, q_ref, tail_ref, kc_ref, vc_ref, ks_ref, vs_ref, kw_ref, vw_ref,
                       tp_ref, bc_ref, o_ref, s_ref, madd_ref, *, t_len):
    qb = pl.program_id(1)
    mc = kc_ref.shape[1]
    n_cmp = (t_len - CMP_BLOCK) // CMP_STRIDE + 1
    n_slc = -(-t_len // SEL_BLOCK)
    pos = qb * QB + lax.broadcasted_iota(I32, (QB, 1), 0)
    pos4 = jnp.concatenate([pos] * A_GROUP, axis=0)
    cidx = lax.broadcasted_iota(I32, (1, mc), 1)
    gates = jax.nn.sigmoid(tail_ref[...])
    overlap = _overlap(mc, QB, n_cmp, n_slc)
    causal_add = _causal_add(A_GROUP)
    onehot = jnp.where(lax.broadcasted_iota(I32, (QB, t_len), 0)
                       == lax.broadcasted_iota(I32, (QB, t_len), 1) // SEL_BLOCK, 1.0, 0.0).astype(MXU_DT)

    o_cmp, o_win, streams = [], [], []
    for g in range(A_KV):
        cols = [g * A_GROUP + r for r in range(A_GROUP)]
        gl = slice(g * HEAD_DIM, (g + 1) * HEAD_DIM)
        q = _stack_heads(q_ref, g * A_GROUP * HEAD_DIM, A_GROUP).astype(MXU_DT)
        lc = (_nt(q, kc_ref[0, :, gl].astype(MXU_DT)) * SCALE
              + jnp.concatenate([bc_ref[c] for c in cols], axis=0))
        p_cmp = _softmax_rows(lc, (pos4 >= cidx * CMP_STRIDE + (CMP_BLOCK - 1)) & (cidx < n_cmp))
        o_cmp.append(_mm(p_cmp.astype(MXU_DT), vc_ref[0, :, gl].astype(MXU_DT)))
        p_sum = sum(p_cmp[r * QB:(r + 1) * QB] for r in range(A_GROUP))
        imp = jnp.dot(p_sum, overlap, preferred_element_type=F32, precision=lax.Precision.HIGHEST)
        sel = _select_blocks(imp, pos, n_slc).astype(MXU_DT)
        madd_ref[g] = jnp.where(_mm(sel, onehot) > 0.5, 0.0, NEG)
        o_win.append(_window_attn(q, kw_ref, vw_ref, gl, qb, pos4, tbl_ref, tp_ref, cols))
        streams.append((
            q, lambda k0, w, gl=gl: ks_ref[pl.ds(k0, w), gl].astype(MXU_DT),
            lambda k0, w, gl=gl: vs_ref[pl.ds(k0, w), gl].astype(MXU_DT), _far_bias(tbl_ref, cols),
            lambda diag, cols=cols: _near_bias(tp_ref, cols, diag) + (causal_add if diag else 0.0),
            lambda k0, w, g=g: jnp.concatenate([madd_ref[g, :, pl.ds(k0, w)]] * A_GROUP, axis=0),
            s_ref.at[g]))
    o_slc = _causal_attn(streams, HEAD_DIM, qb)

    for h in range(A_HEADS):
        g, r = divmod(h, A_GROUP)
        c = T_GA + h * N_GATES
        rows = slice(r * QB, (r + 1) * QB)
        o = (gates[:, c:c + 1] * o_cmp[g][rows] + gates[:, c + 1:c + 2] * o_slc[g][rows]
             + gates[:, c + 2:c + 3] * o_win[g][rows])
        o_ref[:, h * HEAD_DIM:(h + 1) * HEAD_DIM] = o.astype(o_ref.dtype)


def nsa_prompt(z, k_cmp, v_cmp, tp, bc, rel_bias, n, t_len):
    nb = t_len // QB
    mc = k_cmp.shape[1]
    kv = lambda c: pl.BlockSpec((t_len, 256), lambda i, j: (i, c // 256))
    cmp_spec = pl.BlockSpec((1, mc, 256), lambda i, j: (i, 0, 0))
    return pl.pallas_call(
        functools.partial(_nsa_prompt_kernel, t_len=t_len),
        grid=(n, nb),
        in_specs=[pl.BlockSpec(memory_space=pltpu.SMEM),
                  pl.BlockSpec((QB, 1024), lambda i, j: (i * nb + j, C_QA // 1024)),
                  pl.BlockSpec((QB, 128), lambda i, j: (i * nb + j, C_TAIL // 128)),
                  cmp_spec, cmp_spec, kv(C_KS), kv(C_VS), kv(C_KW), kv(C_VW),
                  pl.BlockSpec(tp.shape, lambda i, j: (0, 0, 0, 0)),
                  pl.BlockSpec((A_HEADS, QB, mc), lambda i, j: (0, j, 0))],
        out_specs=pl.BlockSpec((QB, 1024), lambda i, j: (i * nb + j, 0)),
        out_shape=jax.ShapeDtypeStruct((n * t_len, 1024), MXU_DT),
        scratch_shapes=[pltpu.VMEM((A_KV, A_GROUP * QB, t_len), F32), pltpu.VMEM((A_KV, QB, t_len), F32)],
        compiler_params=_cparams("parallel", "arbitrary"),
        name="nsa_prompt",
    )(rel_bias, z, z, k_cmp, v_cmp, z, z, z, z, tp, bc)


def _dsa_prompt_kernel(tbl_ref, q_ref, qi_ref, tailq_ref, tailk_ref, kb_ref, vb_ref, tp_ref, o_ref,
                       key_ref, madd_ref, s_ref, *, topk, nbits):
    qb = pl.program_id(1)
    nact = qb // (CHUNK // QB) + 1
    pos = qb * QB + lax.broadcasted_iota(I32, (1, QB), 1)
    wi_t = tailq_ref[...].T[T_WI:T_WI + IDX_HEADS]
    qis = [qi_ref[:, h * IDX_DIM:(h + 1) * IDX_DIM].astype(MXU_DT) for h in range(IDX_HEADS)]

    def causal(c0):
        return c0 + lax.broadcasted_iota(I32, (CHUNK, 1), 0) <= pos

    def index_chunk(c, _):
        c0 = pl.multiple_of(c * CHUNK, CHUNK)
        ki = tailk_ref[pl.ds(c0, CHUNK), T_KI:T_KI + IDX_DIM].astype(MXU_DT)
        score = sum(jnp.maximum(_nt(ki, qis[h]), 0.0) * wi_t[h:h + 1] for h in range(IDX_HEADS))
        score = score * (IDX_DIM ** -0.5 * IDX_HEADS ** -0.5)
        key_ref[pl.ds(c0, CHUNK), :] = _sort_key(jnp.where(causal(c0), score, NEG))
        return 0

    lax.fori_loop(0, nact, index_chunk, 0)
    _topk_madd_t(key_ref, madd_ref, causal, nact, CHUNK, topk, nbits)

    streams = []
    for g in range(B_KV):
        cols = [A_HEADS + g * B_GROUP + r for r in range(B_GROUP)]
        gl = slice(g * HEAD_DIM, (g + 1) * HEAD_DIM)
        streams.append((
            _stack_heads(q_ref, g * B_GROUP * HEAD_DIM, B_GROUP).astype(MXU_DT),
            lambda k0, w, gl=gl: kb_ref[pl.ds(k0, w), gl].astype(MXU_DT),
            lambda k0, w, gl=gl: vb_ref[pl.ds(k0, w), gl].astype(MXU_DT),
            _far_bias(tbl_ref, cols), lambda diag, cols=cols: _near_bias(tp_ref, cols, diag),
            lambda k0, w: jnp.concatenate([madd_ref[:, pl.ds(k0, w)]] * B_GROUP, axis=0), s_ref.at[g]))
    outs = _causal_attn(streams, HEAD_DIM, qb)
    for h in range(B_HEADS):
        g, r = divmod(h, B_GROUP)
        o_ref[:, h * HEAD_DIM:(h + 1) * HEAD_DIM] = outs[g][r * QB:(r + 1) * QB].astype(o_ref.dtype)


def dsa_prompt(z, tp, rel_bias, n, t_len):
    nb = t_len // QB
    topk = min(DSA_TOPK, t_len // 4)
    nbits = int(t_len).bit_length()
    return pl.pallas_call(
        functools.partial(_dsa_prompt_kernel, topk=topk, nbits=nbits),
        grid=(n, nb),
        in_specs=[pl.BlockSpec(memory_space=pltpu.SMEM),
                  pl.BlockSpec((QB, 1024), lambda i, j: (i * nb + j, C_QB // 1024)),
                  pl.BlockSpec((QB, 256), lambda i, j: (i * nb + j, C_QI // 256)),
                  pl.BlockSpec((QB, 128), lambda i, j: (i * nb + j, C_TAIL // 128)),
                  pl.BlockSpec((t_len, 128), lambda i, j: (i, C_TAIL // 128)),
                  pl.BlockSpec((t_len, 256), lambda i, j: (i, C_KB // 256)),
                  pl.BlockSpec((t_len, 256), lambda i, j: (i, C_VB // 256)),
                  pl.BlockSpec(tp.shape, lambda i, j: (0, 0, 0, 0))],
        out_specs=pl.BlockSpec((QB, 1024), lambda i, j: (i * nb + j, 0)),
        out_shape=jax.ShapeDtypeStruct((n * t_len, 1024), MXU_DT),
        scratch_shapes=[pltpu.VMEM((t_len, QB), I32), pltpu.VMEM((QB, t_len), F32),
                        pltpu.VMEM((B_KV, B_GROUP * QB, t_len), F32)],
        compiler_params=_cparams("parallel", "arbitrary"),
        name="dsa_prompt",
    )(rel_bias, z, z, z, z, z, z, tp)


def _diff_lambda(lam_ref):
    v = lam_ref[...]
    e1 = jnp.exp(jnp.sum(v[0:1] * v[1:2], axis=-1, keepdims=True))
    e2 = jnp.exp(jnp.sum(v[2:3] * v[3:4], axis=-1, keepdims=True))
    return e1 - e2 + LAMBDA_INIT


def _diff_finish(o, hn_ref):
    return _rms(o, hn_ref[...]) * (1.0 - LAMBDA_INIT)


def _diff_prompt_kernel(tbl_ref, q_ref, k_ref, v_ref, tp_ref, lam_ref, hn_ref, o_ref, s_ref, *, gps):
    g0 = pl.program_id(1) * gps
    qb = pl.program_id(2)
    causal_add = _causal_add(C_GROUP)
    streams = []
    for gi in range(gps):
        for m in range(2):
            cols = [m * C_HEADS + (g0 + gi) * C_GROUP + r for r in range(C_GROUP)]
            q = jnp.concatenate(
                [q_ref[:, ((gi * C_GROUP + r) * 2 + m) * HEAD_DIM:((gi * C_GROUP + r) * 2 + m + 1) * HEAD_DIM]
                 for r in range(C_GROUP)], axis=0).astype(MXU_DT)
            kl = slice((gi * 2 + m) * HEAD_DIM, (gi * 2 + m + 1) * HEAD_DIM)
            vl = slice(gi * C_VDIM, (gi + 1) * C_VDIM)
            streams.append((
                q, lambda k0, w, kl=kl: k_ref[pl.ds(k0, w), kl].astype(MXU_DT),
                lambda k0, w, vl=vl: v_ref[pl.ds(k0, w), vl].astype(MXU_DT), _far_bias(tbl_ref, cols),
                lambda diag, cols=cols: _near_bias(tp_ref, cols, diag) + (causal_add if diag else 0.0),
                lambda k0, w: None, s_ref.at[gi * 2 + m]))
    outs = _causal_attn(streams, C_VDIM, qb, pairs=True)
    lam = _diff_lambda(lam_ref)
    for gi in range(gps):
        o = _diff_finish(outs[gi * 2] - lam * outs[gi * 2 + 1], hn_ref)
        for r in range(C_GROUP):
            col = (gi * C_GROUP + r) * C_VDIM
            o_ref[:, col:col + C_VDIM] = o[r * QB:(r + 1) * QB].astype(o_ref.dtype)


def diff_prompt(z1, tp, rel_bias, lam_vecs, head_norm, n, t_len, gps=2):
    nb = t_len // QB
    qw = gps * C_GROUP * 2 * HEAD_DIM
    kw = gps * 2 * HEAD_DIM
    q_cols, k_cols = C_HEADS * 2 * HEAD_DIM, C_KV * 2 * HEAD_DIM
    return pl.pallas_call(
        functools.partial(_diff_prompt_kernel, gps=gps),
        grid=(n, C_KV // gps, nb),
        in_specs=[pl.BlockSpec(memory_space=pltpu.SMEM),
                  pl.BlockSpec((QB, qw), lambda i, g, j: (i * nb + j, g)),
                  pl.BlockSpec((t_len, kw), lambda i, g, j: (i, q_cols // kw + g)),
                  pl.BlockSpec((t_len, kw), lambda i, g, j: (i, (q_cols + k_cols) // kw + g)),
                  pl.BlockSpec(tp.shape, lambda i, g, j: (0, 0, 0, 0)),
                  pl.BlockSpec((4, HEAD_DIM), lambda i, g, j: (0, 0)),
                  pl.BlockSpec((1, C_VDIM), lambda i, g, j: (0, 0))],
        out_specs=pl.BlockSpec((QB, qw), lambda i, g, j: (i * nb + j, g)),
        out_shape=jax.ShapeDtypeStruct((n * t_len, C_HEADS * C_VDIM), MXU_DT),
        scratch_shapes=[pltpu.VMEM((2 * gps, C_GROUP * QB, t_len), F32)],
        compiler_params=_cparams("parallel", "parallel", "arbitrary"),
        name="diff_prompt",
    )(rel_bias, z1, z1, z1, tp, lam_vecs, head_norm.reshape(1, C_VDIM))


def _page_gather(pt_ref, n_pages, items, sem):
    def copy(i, pg, p, slot):
        pool_ref, buf_ref, rows = items[i]
        src = pool_ref.at[pl.ds(pl.multiple_of(pg * rows, rows), rows)]
        dst = buf_ref.at[pl.ds(pl.multiple_of((slot * n_pages + p) * rows, rows), rows)]
        return pltpu.make_async_copy(src, dst, sem.at[i, slot])

    def start(bb, slot):
        def body(p, _):
            for i in range(len(items)):
                copy(i, pt_ref[bb, p], p, slot).start()
            return 0
        lax.fori_loop(0, n_pages, body, 0)

    def wait(i, slot):
        def body(p, _):
            copy(i, 0, 0, slot).wait()
            return 0
        lax.fori_loop(0, n_pages, body, 0)

    return start, wait


def _prefetch(b, nb, start):
    slot = b % 2

    @pl.when(b == 0)
    def _():
        start(0, 0)

    @pl.when(b + 1 < nb)
    def _():
        start(b + 1, 1 - slot)

    return slot


def _pad_rows(x, rows):
    return jnp.concatenate([x, jnp.zeros((rows - x.shape[0], x.shape[1]), x.dtype)], axis=0)


def _page_rows(pool_ref, rows):
    return lambda pg: pool_ref.at[pl.ds(pl.multiple_of(pg * rows, rows), rows)]


def _interleaved(buf_ref, n, j, row0=0):
    return lambda c0, ch: buf_ref[pl.ds(row0 + c0 * n + j, ch, stride=n), :]


def _sample_scores(q, k_fn, knew, bias_fn, mask_fn, s_ref, past, ch, scale=SCALE):
    def body(c, _):
        c0 = pl.multiple_of(c * ch, ch)
        k = k_fn(c0, ch).astype(MXU_DT)
        s = _nt(q, k) * scale + bias_fn(c0, ch)
        s_ref[:, pl.ds(c0, ch)] = jnp.where(mask_fn(c0, ch, False), s, NEG)
        return 0

    lax.fori_loop(0, past // ch, body, 0, unroll=4)
    s = _nt(q, _pad_rows(knew, 128).astype(MXU_DT)) * scale + bias_fn(past, 128)
    s_ref[:, past:past + 128] = jnp.where(mask_fn(past, 128, True), s, NEG)


def _sample_softmax(s_ref):
    z = s_ref[...]
    e = jnp.where(z > 0.5 * NEG, jnp.exp(z - _row_reduce(jnp.maximum, jnp.max, z)), 0.0)
    l = _row_reduce(jnp.add, jnp.sum, e)
    return e * (1.0 / jnp.where(l > 0.0, l, 1.0))


def _sample_pv(p_ref, v_fn, vnew, past, ch):
    def body(c, acc):
        c0 = pl.multiple_of(c * ch, ch)
        return acc + _mm(p_ref[:, pl.ds(c0, ch)].astype(MXU_DT), v_fn(c0, ch).astype(MXU_DT))

    acc = lax.fori_loop(0, past // ch, body, jnp.zeros((p_ref.shape[0], vnew.shape[1]), F32), unroll=4)
    return acc + _mm(p_ref[:, past:past + 128].astype(MXU_DT), _pad_rows(vnew, 128).astype(MXU_DT))


def _new_key_mask(nq, rep):
    t = lax.broadcasted_iota(I32, (nq, 128), 0)
    j = lax.broadcasted_iota(I32, (nq, 128), 1)
    return jnp.concatenate([(j <= t) & (j < nq)] * rep, axis=0)


def _compress_sample_kernel(pt_ref, pk_ref, pv_ref, w1a, w1b, w2, pe, o_ref, buf, sem, *, n_pages):
    step = pl.program_id(0)
    cpp = PAGE // CMP_STRIDE
    rows = CMP_STRIDE * A_KV
    m = n_pages * cpp

    def copy(pool_ref, pg, p, slot):
        src = pool_ref.at[pl.ds(pl.multiple_of(pg * PAGE * A_KV, PAGE * A_KV), PAGE * A_KV)]
        dst = buf.at[pl.ds(pl.multiple_of((slot * n_pages + p) * PAGE_PITCH, 8), PAGE * A_KV)]
        return pltpu.make_async_copy(src, dst, sem.at[slot])

    def start(st, slot):
        for which, pool_ref in enumerate((pk_ref, pv_ref)):
            @pl.when(st % 2 == which)
            def _(pool_ref=pool_ref):
                def body(p, _):
                    copy(pool_ref, pt_ref[st // 2, p], p, slot).start()
                    return 0
                lax.fori_loop(0, n_pages, body, 0)

    slot = _prefetch(step, pl.num_programs(0), start)

    def wait(p, _):
        copy(pk_ref, 0, 0, slot).wait()
        return 0

    lax.fori_loop(0, n_pages, wait, 0)
    which = step % 2
    row0 = slot * n_pages * PAGE_PITCH

    def x_fn(j, g):
        return jnp.concatenate([buf[pl.ds(row0 + i * rows + 2 * j + g, n_pages, stride=PAGE_PITCH), :]
                                for i in range(cpp)], axis=0)

    def next_fn(y):
        return jnp.concatenate([y[n_pages:], pltpu.roll(y[:n_pages], n_pages - 1, 0)], axis=0)

    def store(g, tokens):
        for i in range(cpp):
            o_ref[0, 0, g, pl.ds(i, n_pages, stride=cpp), :] = tokens[i * n_pages:(i + 1) * n_pages]

    _compress(x_fn, m, w1a.at[which], w1b.at[which], w2.at[which], pe.at[which], next_fn, store)


def compress_sample(pool_k, pool_v, page_table, cw):
    bd, n_pages = page_table.shape
    m = n_pages * (PAGE // CMP_STRIDE)
    return pl.pallas_call(
        functools.partial(_compress_sample_kernel, n_pages=n_pages),
        grid_spec=pltpu.PrefetchScalarGridSpec(
            num_scalar_prefetch=1, grid=(2 * bd,),
            in_specs=[pl.BlockSpec(memory_space=pl.ANY), pl.BlockSpec(memory_space=pl.ANY)] + _cmp_weight_specs(),
            out_specs=pl.BlockSpec((1, 1, A_KV, m, HEAD_DIM), lambda s, pt: (s // 2, s % 2, 0, 0, 0)),
            scratch_shapes=[pltpu.VMEM((2 * n_pages * PAGE_PITCH, HEAD_DIM), F32), pltpu.SemaphoreType.DMA((2,))]),
        out_shape=jax.ShapeDtypeStruct((bd, 2, A_KV, m, HEAD_DIM), F32),
        compiler_params=_cparams("arbitrary"),
        name="compress_sample",
    )(page_table, pool_k, pool_v, *cw)


def _nsa_sample_kernel(pt_ref, z_ref, kc_ref, vc_ref, pks_ref, pvs_ref, wk_ref, wv_ref, bs_ref, bc_ref,
                       o_ref, kbuf, vbuf, s_ref, sw_ref, chosen_ref, sem, *, n_pages, ch):
    past = n_pages * PAGE
    nq = z_ref.shape[0]
    mc = kc_ref.shape[3]
    t_len = past + nq
    n_cmp = (t_len - CMP_BLOCK) // CMP_STRIDE + 1
    n_slc = -(-t_len // SEL_BLOCK)
    jn = 128 * (-(-n_slc // 128))
    wb = wk_ref.shape[0] // A_KV
    start, wait = _page_gather(pt_ref, n_pages, ((pks_ref, kbuf, PAGE * A_KV), (pvs_ref, vbuf, PAGE * A_KV)), sem)
    slot = _prefetch(pl.program_id(0), pl.num_programs(0), start)
    row0 = slot * past * A_KV

    pos = past + lax.broadcasted_iota(I32, (nq, 1), 0)
    pos4 = jnp.concatenate([pos] * A_GROUP, axis=0)
    cidx = lax.broadcasted_iota(I32, (1, mc), 1)
    gates = jax.nn.sigmoid(z_ref[:, C_TAIL:C_TAIL + 128])
    overlap = _overlap(mc, jn, n_cmp, n_slc)
    new_mask = _new_key_mask(nq, A_GROUP)
    first_half = lax.broadcasted_iota(I32, (nq, 2 * SEL_BLOCK), 1) < SEL_BLOCK
    waited = False

    for g in range(A_KV):
        cols = [g * A_GROUP + r for r in range(A_GROUP)]
        q = _stack_heads(z_ref, C_QA + g * A_GROUP * HEAD_DIM, A_GROUP).astype(MXU_DT)
        lc = (_nt(q, kc_ref[0, 0, g].astype(MXU_DT)) * SCALE
              + jnp.concatenate([bc_ref[c] for c in cols], axis=0))
        p_cmp = _softmax_rows(lc, (pos4 >= cidx * CMP_STRIDE + (CMP_BLOCK - 1)) & (cidx < n_cmp))
        o_cmp = _mm(p_cmp.astype(MXU_DT), vc_ref[0, 0, g].astype(MXU_DT))
        p_sum = sum(p_cmp[r * nq:(r + 1) * nq] for r in range(A_GROUP))
        imp = jnp.dot(p_sum, overlap, preferred_element_type=F32, precision=lax.Precision.HIGHEST)
        sel = _select_blocks(imp, pos, n_slc)
        def win_bias(c0, w, cols=cols):
            return jnp.concatenate([bs_ref[c, :, pl.ds(past - wb + c0, w)] for c in cols], axis=0)

        def win_mask(c0, w, is_new):
            dist = pos4 - (past - wb + c0 + lax.broadcasted_iota(I32, (1, w), 1))
            valid = (dist >= 0) & (dist < WINDOW)
            return valid & new_mask if is_new else valid

        _sample_scores(q, _interleaved(wk_ref, A_KV, g),
                       z_ref[:, C_KW + g * HEAD_DIM:C_KW + (g + 1) * HEAD_DIM],
                       win_bias, win_mask, sw_ref, wb, wb)
        sw_ref[...] = _sample_softmax(sw_ref)
        o_win = _sample_pv(sw_ref, _interleaved(wv_ref, A_KV, g),
                           z_ref[:, C_VW + g * HEAD_DIM:C_VW + (g + 1) * HEAD_DIM], wb, wb)
        if not waited:
            wait(0, slot)
            wait(1, slot)
            waited = True

        def slc_bias(c0, w, cols=cols):
            return jnp.concatenate([bs_ref[c, :, pl.ds(c0, w)] for c in cols], axis=0)

        for kk in range((past + 128) // 128):
            chosen_ref[:, kk * 128:(kk + 1) * 128] = jnp.where(
                first_half, sel[:, 2 * kk:2 * kk + 1], sel[:, 2 * kk + 1:2 * kk + 2])

        def slc_mask(c0, w, is_new):
            chosen = jnp.concatenate([chosen_ref[:, pl.ds(c0, w)] > 0.5] * A_GROUP, axis=0)
            return chosen & new_mask if is_new else chosen

        _sample_scores(q, _interleaved(kbuf, A_KV, g, row0),
                       z_ref[:, C_KS + g * HEAD_DIM:C_KS + (g + 1) * HEAD_DIM],
                       slc_bias, slc_mask, s_ref, past, ch)
        s_ref[...] = _sample_softmax(s_ref)
        o_slc = _sample_pv(s_ref, _interleaved(vbuf, A_KV, g, row0),
                           z_ref[:, C_VS + g * HEAD_DIM:C_VS + (g + 1) * HEAD_DIM], past, ch)
        for r in range(A_GROUP):
            h = g * A_GROUP + r
            c = T_GA + h * N_GATES
            rows = slice(r * nq, (r + 1) * nq)
            o_ref[:, h * HEAD_DIM:(h + 1) * HEAD_DIM] = (
                gates[:, c:c + 1] * o_cmp[rows] + gates[:, c + 1:c + 2] * o_slc[rows]
                + gates[:, c + 2:c + 3] * o_win[rows])


def nsa_sample(zs, kv_cmp, pool_ks, pool_vs, win_k, win_v, bs, bc, page_table, ch=1024):
    bd, n_pages = page_table.shape
    nq = zs.shape[0] // bd
    past = n_pages * PAGE
    mc = kv_cmp.shape[3]
    wrows = win_k.shape[0] // bd
    wb = wrows // A_KV
    win_spec = pl.BlockSpec((wrows, HEAD_DIM), lambda i, pt: (i, 0))
    buf = pltpu.VMEM((2 * past * A_KV, HEAD_DIM), F32)
    return pl.pallas_call(
        functools.partial(_nsa_sample_kernel, n_pages=n_pages, ch=ch),
        grid_spec=pltpu.PrefetchScalarGridSpec(
            num_scalar_prefetch=1, grid=(bd,),
            in_specs=[pl.BlockSpec((nq, zs.shape[1]), lambda i, pt: (i, 0)),
                      pl.BlockSpec((1, 1, A_KV, mc, HEAD_DIM), lambda i, pt: (i, 0, 0, 0, 0)),
                      pl.BlockSpec((1, 1, A_KV, mc, HEAD_DIM), lambda i, pt: (i, 1, 0, 0, 0)),
                      pl.BlockSpec(memory_space=pl.ANY), pl.BlockSpec(memory_space=pl.ANY),
                      win_spec, win_spec,
                      pl.BlockSpec((A_HEADS,) + bs.shape[1:], lambda i, pt: (0, 0, 0)),
                      pl.BlockSpec(bc.shape, lambda i, pt: (0, 0, 0))],
            out_specs=pl.BlockSpec((nq, 1024), lambda i, pt: (i, 0)),
            scratch_shapes=[buf, buf,
                            pltpu.VMEM((A_GROUP * nq, past + 128), F32),
                            pltpu.VMEM((A_GROUP * nq, wb + 128), F32),
                            pltpu.VMEM((nq, past + 128), F32),
                            pltpu.SemaphoreType.DMA((2, 2))]),
        out_shape=jax.ShapeDtypeStruct((bd * nq, 1024), F32),
        compiler_params=_cparams("arbitrary"),
        name="nsa_sample",
    )(page_table, zs, kv_cmp, kv_cmp, pool_ks, pool_vs, win_k, win_v, bs, bc)


def _dsa_sample_kernel(pt_ref, z_ref, pk_ref, pv_ref, pi_ref, bs_ref, o_ref,
                       kbuf, vbuf, ibuf, s_ref, sc_ref, key_ref, sel_ref, sem, *, n_pages, ch, topk, nbits):
    past = n_pages * PAGE
    nq = z_ref.shape[0]
    start, wait = _page_gather(pt_ref, n_pages, ((pk_ref, kbuf, PAGE * B_KV), (pv_ref, vbuf, PAGE * B_KV),
                                                 (pi_ref, ibuf, IDX_DIM)), sem)
    slot = _prefetch(pl.program_id(0), pl.num_programs(0), start)
    row0 = slot * past * B_KV
    qi = jnp.concatenate([z_ref[:, C_QI + h * IDX_DIM:C_QI + (h + 1) * IDX_DIM] for h in range(IDX_HEADS)],
                         axis=0).astype(MXU_DT)
    wi = z_ref[:, C_TAIL + T_WI:C_TAIL + T_WI + IDX_HEADS]
    wait(2, slot)

    def index_page(p, _):
        kt = ibuf[pl.ds(pl.multiple_of((slot * n_pages + p) * IDX_DIM, IDX_DIM), IDX_DIM), :]
        s_ref[:, pl.ds(pl.multiple_of(p * PAGE, PAGE), PAGE)] = _mm(qi, kt.astype(MXU_DT))
        return 0

    lax.fori_loop(0, n_pages, index_page, 0, unroll=8)
    ki_new = _pad_rows(z_ref[:, C_TAIL + T_KI:C_TAIL + T_KI + IDX_DIM], 128).astype(MXU_DT)
    s_ref[:, past:past + 128] = _nt(qi, ki_new)
    rel = jnp.maximum(s_ref[...], 0.0)
    score = sum(rel[h * nq:(h + 1) * nq] * wi[:, h:h + 1] for h in range(IDX_HEADS))
    score = score * (IDX_DIM ** -0.5 * IDX_HEADS ** -0.5)
    new_j = lax.broadcasted_iota(I32, score.shape, 1) - past
    causal = (new_j < 0) | ((new_j <= lax.broadcasted_iota(I32, score.shape, 0)) & (new_j < nq))
    key_ref[...] = _sort_key(jnp.where(causal, score, NEG))
    _topk_madd(key_ref, sel_ref, lambda c0: causal, 1, score.shape[1], topk, nbits)

    wait(0, slot)
    wait(1, slot)
    for g in range(B_KV):
        cols = [g * B_GROUP + r for r in range(B_GROUP)]
        q = _stack_heads(z_ref, C_QB + g * B_GROUP * HEAD_DIM, B_GROUP).astype(MXU_DT)

        def bias(c0, w, cols=cols):
            return jnp.concatenate([bs_ref[c, :, pl.ds(c0, w)] for c in cols], axis=0)

        def mask(c0, w, is_new):
            return jnp.concatenate([sel_ref[:, pl.ds(c0, w)] > 0.5 * NEG] * B_GROUP, axis=0)

        _sample_scores(q, _interleaved(kbuf, B_KV, g, row0),
                       z_ref[:, C_KB + g * HEAD_DIM:C_KB + (g + 1) * HEAD_DIM], bias, mask, sc_ref, past, ch)
        sc_ref[...] = _sample_softmax(sc_ref)
        o = _sample_pv(sc_ref, _interleaved(vbuf, B_KV, g, row0),
                       z_ref[:, C_VB + g * HEAD_DIM:C_VB + (g + 1) * HEAD_DIM], past, ch)
        for r in range(B_GROUP):
            h = g * B_GROUP + r
            o_ref[:, h * HEAD_DIM:(h + 1) * HEAD_DIM] = o[r * nq:(r + 1) * nq]


def dsa_sample(zs, pool_k, pool_v, pool_i, bs, page_table, ch=1024):
    bd, n_pages = page_table.shape
    nq = zs.shape[0] // bd
    past = n_pages * PAGE
    lp = past + 128
    topk = min(DSA_TOPK, (past + nq) // 4)
    return pl.pallas_call(
        functools.partial(_dsa_sample_kernel, n_pages=n_pages, ch=ch, topk=topk, nbits=int(lp).bit_length()),
        grid_spec=pltpu.PrefetchScalarGridSpec(
            num_scalar_prefetch=1, grid=(bd,),
            in_specs=[pl.BlockSpec((nq, zs.shape[1]), lambda i, pt: (i, 0)),
                      pl.BlockSpec(memory_space=pl.ANY), pl.BlockSpec(memory_space=pl.ANY),
                      pl.BlockSpec(memory_space=pl.ANY),
                      pl.BlockSpec((B_HEADS,) + bs.shape[1:], lambda i, pt: (1, 0, 0))],
            out_specs=pl.BlockSpec((nq, 1024), lambda i, pt: (i, 0)),
            scratch_shapes=[pltpu.VMEM((2 * past * B_KV, HEAD_DIM), F32), pltpu.VMEM((2 * past * B_KV, HEAD_DIM), F32),
                            pltpu.VMEM((2 * n_pages * IDX_DIM, PAGE), F32),
                            pltpu.VMEM((IDX_HEADS * nq, lp), F32), pltpu.VMEM((B_GROUP * nq, lp), F32),
                            pltpu.VMEM((nq, lp), I32), pltpu.VMEM((nq, lp), F32),
                            pltpu.SemaphoreType.DMA((3, 2))]),
        out_shape=jax.ShapeDtypeStruct((bd * nq, 1024), F32),
        compiler_params=_cparams("arbitrary"),
        name="dsa_sample",
    )(page_table, zs, pool_k, pool_v, pool_i, bs)


def _diff_sample_kernel(pt_ref, q_ref, kn_ref, vn_ref, pk_ref, pv_ref, bs_ref, lam_ref, hn_ref, o_ref,
                        kbuf, vbuf, sem, *, n_pages, cp):
    b = pl.program_id(0)
    nq = q_ref.shape[0]
    pieces = C_KV * 2
    page_rows = PAGE * pieces
    slot_rows = cp * page_rows
    ch = cp * PAGE
    n_ch = n_pages // cp
    total = pl.num_programs(0) * n_ch
    past = n_pages * PAGE
    rows = C_GROUP * nq

    def copies(idx):
        bb, c, slot = idx // n_ch, idx % n_ch, idx % DIFF_SLOTS
        out = []
        for i in range(cp):
            pg = pt_ref[bb, c * cp + i]
            dst = pl.ds(pl.multiple_of(slot * slot_rows + i * page_rows, page_rows), page_rows)
            out.append(pltpu.make_async_copy(_page_rows(pk_ref, page_rows)(pg), kbuf.at[dst], sem.at[0, slot]))
            out.append(pltpu.make_async_copy(_page_rows(pv_ref, page_rows)(pg), vbuf.at[dst], sem.at[1, slot]))
        return out

    @pl.when(b == 0)
    def _():
        for idx in range(DIFF_SLOTS - 1):
            for cpy in copies(idx):
                cpy.start()

    qs = [jnp.concatenate([q_ref[:, ((g * C_GROUP + r) * 2 + m) * HEAD_DIM:((g * C_GROUP + r) * 2 + m + 1) * HEAD_DIM]
                           for r in range(C_GROUP)], axis=0).astype(MXU_DT)
          for g in range(C_KV) for m in range(2)]

    def update(carry, k_fn, v_fn, c0, w, mask):
        m_all, l_all, acc_all = carry
        new_m, new_l, new_acc = [], [], []
        for g in range(C_KV):
            ps, alphas = [], []
            for m in range(2):
                gm = g * 2 + m
                rs = slice(gm * rows, (gm + 1) * rows)
                bias = jnp.concatenate([bs_ref[m * C_HEADS + g * C_GROUP + r, :, pl.ds(c0, w)]
                                        for r in range(C_GROUP)], axis=0)
                s = _nt(qs[gm], k_fn(g, m).astype(MXU_DT)) * SCALE + bias
                if mask is not None:
                    s = jnp.where(mask, s, NEG)
                mn = jnp.maximum(m_all[rs], jnp.max(s, axis=-1, keepdims=True))
                p = jnp.exp(s - mn)
                if mask is not None:
                    p = jnp.where(mask, p, 0.0)
                a = jnp.exp(m_all[rs] - mn)
                new_m.append(mn)
                new_l.append(a * l_all[rs] + jnp.sum(p, axis=-1, keepdims=True))
                ps.append(p)
                alphas.append(a)
            pst = jnp.concatenate(ps, axis=0).astype(MXU_DT)
            pv = jnp.concatenate([_mm(pst, v_fn(g, h).astype(MXU_DT)) for h in range(2)], axis=1)
            for m in range(2):
                rs = slice((g * 2 + m) * rows, (g * 2 + m + 1) * rows)
                new_acc.append(alphas[m] * acc_all[rs] + pv[m * rows:(m + 1) * rows])
        return (jnp.concatenate(new_m, axis=0), jnp.concatenate(new_l, axis=0),
                jnp.concatenate(new_acc, axis=0))

    def chunk(c, carry):
        idx = b * n_ch + c
        for cpy in copies(idx):
            cpy.wait()

        @pl.when(idx + DIFF_SLOTS - 1 < total)
        def _():
            for cpy in copies(idx + DIFF_SLOTS - 1):
                cpy.start()

        base = (idx % DIFF_SLOTS) * slot_rows
        return update(carry,
                      lambda g, m: kbuf[pl.ds(base + g * 2 + m, ch, stride=pieces), :],
                      lambda g, h: vbuf[pl.ds(base + h * C_KV + g, ch, stride=pieces), :],
                      pl.multiple_of(c * ch, ch), ch, None)

    n_rows = pieces * rows
    carry = (jnp.full((n_rows, 1), NEG, F32), jnp.zeros((n_rows, 1), F32), jnp.zeros((n_rows, C_VDIM), F32))
    carry = lax.fori_loop(0, n_ch, chunk, carry)
    _, l_all, acc_all = update(
        carry,
        lambda g, m: _pad_rows(kn_ref[:, (g * 2 + m) * HEAD_DIM:(g * 2 + m + 1) * HEAD_DIM], 128),
        lambda g, h: _pad_rows(vn_ref[:, g * C_VDIM + h * HEAD_DIM:g * C_VDIM + (h + 1) * HEAD_DIM], 128),
        past, 128, _new_key_mask(nq, C_GROUP))
    o_all = acc_all / l_all
    lam = _diff_lambda(lam_ref)
    for g in range(C_KV):
        r0 = g * 2 * rows
        o = _diff_finish(o_all[r0:r0 + rows] - lam * o_all[r0 + rows:r0 + 2 * rows], hn_ref)
        for r in range(C_GROUP):
            col = (g * C_GROUP + r) * C_VDIM
            o_ref[:, col:col + C_VDIM] = o[r * nq:(r + 1) * nq]


def diff_sample(z1s, pool_k, pool_v, bs, lam_vecs, head_norm, page_table, cp=8):
    bd, n_pages = page_table.shape
    nq = z1s.shape[0] // bd
    assert n_pages % cp == 0 and bd * (n_pages // cp) >= DIFF_SLOTS
    slot_rows = cp * PAGE * C_KV * 2
    buf = pltpu.VMEM((DIFF_SLOTS * slot_rows, HEAD_DIM), F32)
    q_cols = C_HEADS * 2 * HEAD_DIM
    kv_cols = C_KV * C_VDIM
    return pl.pallas_call(
        functools.partial(_diff_sample_kernel, n_pages=n_pages, cp=cp),
        grid_spec=pltpu.PrefetchScalarGridSpec(
            num_scalar_prefetch=1, grid=(bd,),
            in_specs=[pl.BlockSpec((nq, q_cols), lambda i, pt: (i, 0)),
                      pl.BlockSpec((nq, kv_cols), lambda i, pt: (i, q_cols // kv_cols)),
                      pl.BlockSpec((nq, kv_cols), lambda i, pt: (i, q_cols // kv_cols + 1)),
                      pl.BlockSpec(memory_space=pl.ANY), pl.BlockSpec(memory_space=pl.ANY),
                      pl.BlockSpec(bs.shape, lambda i, pt: (0, 0, 0)),
                      pl.BlockSpec((4, HEAD_DIM), lambda i, pt: (0, 0)),
                      pl.BlockSpec((1, C_VDIM), lambda i, pt: (0, 0))],
            out_specs=pl.BlockSpec((nq, C_HEADS * C_VDIM), lambda i, pt: (i, 0)),
            scratch_shapes=[buf, buf, pltpu.SemaphoreType.DMA((2, DIFF_SLOTS))]),
        out_shape=jax.ShapeDtypeStruct((bd * nq, C_HEADS * C_VDIM), F32),
        compiler_params=_cparams("arbitrary"),
        name="diff_sample",
    )(page_table, z1s, z1s, z1s, pool_k, pool_v, bs, lam_vecs, head_norm.reshape(1, C_VDIM))


def _row_tile(rows):
    tm = min(rows, ROW_TILE)
    assert rows % tm == 0
    return tm


def _reorder_l0_weight(w):
    sizes = (A_HEADS * HEAD_DIM,) + (A_KV * HEAD_DIM,) * 6 + (
        N_GATES * A_HEADS, B_HEADS * HEAD_DIM, B_KV * HEAD_DIM, B_KV * HEAD_DIM,
        IDX_HEADS * IDX_DIM, IDX_DIM, IDX_HEADS)
    offs = [0]
    for s in sizes:
        offs.append(offs[-1] + s)
    piece = lambda i, j=None: w[:, offs[i]:offs[(i if j is None else j) + 1]]
    qa, six, ga, qb, kvb, qi, ki, wi = piece(0), piece(1, 6), piece(7), piece(8), piece(9, 10), piece(11), \
        piece(12), piece(13)
    pad = jnp.zeros((w.shape[0], L0_COLS - offs[-1]), w.dtype)
    return jnp.concatenate([qa, qb, six, kvb, qi, ki, ga, wi, pad], axis=1).astype(MXU_DT)


def _compress_weights(pe, w1, w2):
    half = CMP_STRIDE * HEAD_DIM
    w1 = w1.reshape(2, half, CMP_HIDDEN).astype(MXU_DT)
    pe_rows = jnp.zeros((16, half), F32).at[0:2].set(pe.reshape(2, half))
    return w1[0], w1[1], w2.astype(MXU_DT), pe_rows


def kernel(x_prompt, x_sample, cache_l0_nsa_cmp_k, cache_l0_nsa_cmp_v, cache_l0_nsa_slc_k, cache_l0_nsa_slc_v, state_l0_nsa_win_k, state_l0_nsa_win_v, cache_l0_dsa_k, cache_l0_dsa_v, cache_l0_dsa_idx_k, cache_l1_diff_k, cache_l1_diff_v, page_table, rel_bias, attn_norm, mlp_norm, mlp_w1, mlp_w2, l0_w_in, l0_w_out, l0_cmp_pe_k, l0_cmp_w1_k, l0_cmp_w2_k, l0_cmp_pe_v, l0_cmp_w1_v, l0_cmp_w2_v, l1_w_in, l1_w_out, l1_lambda_q1, l1_lambda_k1, l1_lambda_q2, l1_lambda_k2, l1_head_norm, final_norm):
    n, t_len, d = x_prompt.shape
    bd, nq, _ = x_sample.shape
    n_pool = cache_l0_nsa_cmp_k.shape[0]
    n_pages = page_table.shape[1]
    past = n_pages * PAGE
    lp = past + 128
    kv_w = A_KV * HEAD_DIM
    assert t_len % CHUNK == 0 and t_len >= WINDOW + QB and nq <= 8
    assert state_l0_nsa_win_k.shape[1] == min(WINDOW, past)

    xp = x_prompt.reshape(n * t_len, d)
    xs = x_sample.reshape(bd * nq, d)
    tmp, tms = _row_tile(xp.shape[0]), _row_tile(xs.shape[0])
    w0 = _reorder_l0_weight(l0_w_in)
    cw = [jnp.stack(pair) for pair in zip(_compress_weights(l0_cmp_pe_k, l0_cmp_w1_k, l0_cmp_w2_k),
                                          _compress_weights(l0_cmp_pe_v, l0_cmp_w1_v, l0_cmp_w2_v))]
    lam_vecs = jnp.stack([l1_lambda_q1, l1_lambda_k1, l1_lambda_q2, l1_lambda_k2])
    bf = lambda a: a.astype(MXU_DT)

    tp, bs = bias_tiles(rel_bias, past, nq, lp)
    bc_p, bc_s = bias_cmp(rel_bias, t_len, t_len // CMP_STRIDE, past, nq, past // CMP_STRIDE)

    zp = norm_proj(xp, attn_norm[0], w0, tmp, L0_COL_TILE)
    zs = norm_proj(xs, attn_norm[0], w0, tms, L0_COL_TILE)
    cut = lambda z, c, w: z[:, c:c + w]
    p_rows = {name: cut(zp, c, kv_w) for name, c in
              (("kc", C_KC), ("vc", C_VC), ("ks", C_KS), ("vs", C_VS), ("kw", C_KW), ("vw", C_VW),
               ("kb", C_KB), ("vb", C_VB))}
    s_rows = {name: cut(zs, c, kv_w) for name, c in
              (("kc", C_KC), ("vc", C_VC), ("ks", C_KS), ("vs", C_VS), ("kw", C_KW), ("vw", C_VW),
               ("kb", C_KB), ("vb", C_VB))}
    chunk_w = CMP_STRIDE * kv_w
    kc_p, vc_p = compress_prompt(p_rows["kc"].reshape(n, t_len // CMP_STRIDE, chunk_w),
                                 p_rows["vc"].reshape(n, t_len // CMP_STRIDE, chunk_w), cw)
    lanes = lambda a: a.reshape(-1, HEAD_DIM)
    kv_cmp_s = compress_sample(lanes(cache_l0_nsa_cmp_k), lanes(cache_l0_nsa_cmp_v), page_table, cw)
    oa_p = nsa_prompt(zp, kc_p, vc_p, tp, bc_p, rel_bias, n, t_len)
    ob_p = dsa_prompt(zp, tp, rel_bias, n, t_len)
    wb = state_l0_nsa_win_k.shape[1]
    oa_s = nsa_sample(zs, kv_cmp_s, lanes(cache_l0_nsa_slc_k), lanes(cache_l0_nsa_slc_v),
                      lanes(state_l0_nsa_win_k), lanes(state_l0_nsa_win_v), bs, bc_s, page_table)
    ob_s = dsa_sample(zs, lanes(cache_l0_dsa_k), lanes(cache_l0_dsa_v),
                      jnp.swapaxes(cache_l0_dsa_idx_k, 1, 2).reshape(-1, PAGE), bs, page_table)
    w_out0 = bf(l0_w_out)
    w1_0, w2_0 = bf(mlp_w1[0]), bf(mlp_w2[0])
    xp = out_proj(xp, [oa_p, ob_p], w_out0, tmp, COL_TILE)
    xs = out_proj(xs, [oa_s, ob_s], w_out0, tms, COL_TILE)
    xp = mlp(xp, mlp_norm[0], w1_0, w2_0, final_norm, tmp, FF_TILE, False)
    xs = mlp(xs, mlp_norm[0], w1_0, w2_0, final_norm, tms, FF_TILE, False)

    w_in1 = bf(l1_w_in)
    z1p = norm_proj(xp, attn_norm[1], w_in1, tmp, COL_TILE)
    z1s = norm_proj(xs, attn_norm[1], w_in1, tms, COL_TILE)
    o1_p = diff_prompt(z1p, tp, rel_bias, lam_vecs, l1_head_norm, n, t_len)
    v_halves = cache_l1_diff_v.reshape(n_pool, PAGE, C_KV, 2, HEAD_DIM).transpose(0, 1, 3, 2, 4)
    o1_s = diff_sample(z1s, lanes(cache_l1_diff_k), lanes(v_halves), bs, lam_vecs, l1_head_norm, page_table)
    w_out1 = bf(l1_w_out)
    w1_1, w2_1 = bf(mlp_w1[1]), bf(mlp_w2[1])
    xp = out_proj(xp, [o1_p], w_out1, tmp, COL_TILE)
    xs = out_proj(xs, [o1_s], w_out1, tms, COL_TILE)
    y_prompt = mlp(xp, mlp_norm[1], w1_1, w2_1, final_norm, tmp, FF_TILE, True).reshape(n, t_len, d)
    y_sample = mlp(xs, mlp_norm[1], w1_1, w2_1, final_norm, tms, FF_TILE, True).reshape(bd, nq, d)

    row4 = lambda a, b: a.reshape(b, -1, A_KV, HEAD_DIM)
    win = min(WINDOW, t_len)
    outs = [y_prompt, y_sample]
    for name in ("kc", "vc", "ks", "vs"):
        outs += [row4(p_rows[name], n), row4(s_rows[name], bd)]
    for name, state in (("kw", state_l0_nsa_win_k), ("vw", state_l0_nsa_win_v)):
        outs += [row4(p_rows[name], n)[:, t_len - win:],
                 jnp.concatenate([state, row4(s_rows[name], bd)], axis=1)[:, -wb:]]
    for name in ("kb", "vb"):
        outs += [row4(p_rows[name], n), row4(s_rows[name], bd)]
    outs += [cut(zp, C_TAIL + T_KI, IDX_DIM).reshape(n, t_len, IDX_DIM),
             cut(zs, C_TAIL + T_KI, IDX_DIM).reshape(bd, nq, IDX_DIM)]
    k_cols, v_cols = C_KV * 2 * HEAD_DIM, C_KV * C_VDIM
    q_cols = C_HEADS * 2 * HEAD_DIM
    outs += [cut(z1p, q_cols, k_cols).reshape(n, t_len, C_KV, 2, HEAD_DIM),
             cut(z1s, q_cols, k_cols).reshape(bd, nq, C_KV, 2, HEAD_DIM),
             cut(z1p, q_cols + k_cols, v_cols).reshape(n, t_len, C_KV, C_VDIM),
             cut(z1s, q_cols + k_cols, v_cols).reshape(bd, nq, C_KV, C_VDIM)]
    return tuple(outs)
```

```python
import functools
import math

import jax
import jax.numpy as jnp
from jax import lax
from jax.experimental import pallas as pl
from jax.experimental.pallas import tpu as pltpu

F32 = jnp.float32
I32 = jnp.int32
MXU_DT = jnp.bfloat16

HEAD_DIM = 128
A_HEADS, A_KV, A_GROUP = 8, 2, 4
B_HEADS, B_KV, B_GROUP = 8, 2, 4
C_HEADS, C_KV, C_GROUP, C_VDIM = 8, 4, 2, 256
CMP_STRIDE, CMP_BLOCK, CMP_HIDDEN = 16, 32, 256
SEL_BLOCK, N_SEL_BLOCKS, WINDOW, N_GATES = 64, 16, 512, 3
IDX_HEADS, IDX_DIM, DSA_TOPK = 4, 64, 256
NUM_BUCKETS, MAX_DISTANCE = 32, 128
LAMBDA_INIT = 0.8 - 0.6 * math.exp(-0.3 * 1)
RMS_EPS = 1e-6
NEG = -1e30
SCALE = HEAD_DIM ** -0.5
QB = 128
CHUNK = 512
PAGE = 128
PAGE_PITCH = PAGE * A_KV + 8
DIFF_SLOTS = 3
assert QB >= MAX_DISTANCE and WINDOW % QB == 0 and WINDOW >= 2 * QB and 2 * SEL_BLOCK == QB

C_QA, C_QB, C_KC, C_VC, C_KS, C_VS, C_KW, C_VW, C_KB, C_VB, C_QI, C_TAIL = (
    0, 1024, 2048, 2304, 2560, 2816, 3072, 3328, 3584, 3840, 4096, 4352)
T_KI, T_GA, T_WI = 0, 64, 88
L0_COLS = 4608
VMEM_LIMIT = 56 * 1024 * 1024
ROW_TILE = 1024
L0_COL_TILE = 1536
COL_TILE = 1024
FF_TILE = 512


def _cparams(*sem):
    return pltpu.CompilerParams(dimension_semantics=sem, vmem_limit_bytes=VMEM_LIMIT)


def _nt(a, b):
    return lax.dot_general(a, b, (((1,), (1,)), ((), ())), preferred_element_type=F32)


def _mm(a, b):
    return jnp.dot(a, b, preferred_element_type=F32)


def _rms(x, g):
    return x * lax.rsqrt(jnp.mean(x * x, axis=-1, keepdims=True) + RMS_EPS) * g


def _norm_proj_kernel(x_ref, g_ref, w_ref, o_ref, xn_ref):
    @pl.when(pl.program_id(1) == 0)
    def _():
        xn_ref[...] = _rms(x_ref[...], g_ref[...]).astype(xn_ref.dtype)

    o_ref[...] = _mm(xn_ref[...], w_ref[...])


def norm_proj(x, gain, w, tm, tn):
    rows, d = x.shape
    n = w.shape[1]
    return pl.pallas_call(
        _norm_proj_kernel,
        grid=(rows // tm, n // tn),
        in_specs=[pl.BlockSpec((tm, d), lambda i, j: (i, 0)),
                  pl.BlockSpec((1, d), lambda i, j: (0, 0)),
                  pl.BlockSpec((d, tn), lambda i, j: (0, j))],
        out_specs=pl.BlockSpec((tm, tn), lambda i, j: (i, j)),
        out_shape=jax.ShapeDtypeStruct((rows, n), F32),
        scratch_shapes=[pltpu.VMEM((tm, d), MXU_DT)],
        compiler_params=_cparams("parallel", "arbitrary"),
        name="norm_proj",
    )(x, gain.reshape(1, d), w)


def _out_proj_kernel(*refs, n_in):
    x_ref, o_refs, w_refs, y_ref = refs[0], refs[1:1 + n_in], refs[1 + n_in:1 + 2 * n_in], refs[-1]
    acc = x_ref[...]
    for o_ref, w_ref in zip(o_refs, w_refs):
        acc = acc + _mm(o_ref[...].astype(MXU_DT), w_ref[...])
    y_ref[...] = acc


def out_proj(x, outs, w, tm, tn):
    rows, d = x.shape
    o_specs, w_specs, row0 = [], [], 0
    for o in outs:
        k = o.shape[1]
        o_specs.append(pl.BlockSpec((tm, k), lambda i, j: (i, 0)))
        w_specs.append(pl.BlockSpec((k, tn), lambda i, j, rb=row0 // k: (rb, j)))
        row0 += k
    return pl.pallas_call(
        functools.partial(_out_proj_kernel, n_in=len(outs)),
        grid=(rows // tm, d // tn),
        in_specs=[pl.BlockSpec((tm, tn), lambda i, j: (i, j))] + o_specs + w_specs,
        out_specs=pl.BlockSpec((tm, tn), lambda i, j: (i, j)),
        out_shape=jax.ShapeDtypeStruct((rows, d), F32),
        compiler_params=_cparams("parallel", "arbitrary"),
        name="out_proj",
    )(x, *outs, *([w] * len(outs)))


def _mlp_kernel(x_ref, g_ref, w1_ref, w2_ref, gf_ref, y_ref, xn_ref, *, final_norm):
    j = pl.program_id(1)

    @pl.when(j == 0)
    def _():
        x = x_ref[...]
        xn_ref[...] = _rms(x, g_ref[...]).astype(xn_ref.dtype)
        y_ref[...] = x

    h = jnp.square(jnp.maximum(_mm(xn_ref[...], w1_ref[...]), 0.0))
    y_ref[...] += _mm(h.astype(w2_ref.dtype), w2_ref[...])

    if final_norm:
        @pl.when(j == pl.num_programs(1) - 1)
        def _():
            y_ref[...] = _rms(y_ref[...], gf_ref[...])


def mlp(x, gain, w1, w2, final_gain, tm, tf, final_norm):
    rows, d = x.shape
    ff = w1.shape[1]
    return pl.pallas_call(
        functools.partial(_mlp_kernel, final_norm=final_norm),
        grid=(rows // tm, ff // tf),
        in_specs=[pl.BlockSpec((tm, d), lambda i, j: (i, 0)),
                  pl.BlockSpec((1, d), lambda i, j: (0, 0)),
                  pl.BlockSpec((d, tf), lambda i, j: (0, j)),
                  pl.BlockSpec((tf, d), lambda i, j: (j, 0)),
                  pl.BlockSpec((1, d), lambda i, j: (0, 0))],
        out_specs=pl.BlockSpec((tm, d), lambda i, j: (i, 0)),
        out_shape=jax.ShapeDtypeStruct((rows, d), F32),
        scratch_shapes=[pltpu.VMEM((tm, d), MXU_DT)],
        compiler_params=_cparams("parallel", "arbitrary"),
        name="mlp",
    )(x, gain.reshape(1, d), w1, w2, final_gain.reshape(1, d))


def _bucket(dist):
    n = jnp.maximum(dist, 0)
    max_exact = NUM_BUCKETS // 2
    nf = jnp.maximum(n, 1).astype(F32)
    large = max_exact + (jnp.log(nf / max_exact) / math.log(MAX_DISTANCE / max_exact)
                         * (NUM_BUCKETS - max_exact)).astype(I32)
    large = jnp.minimum(large, NUM_BUCKETS - 1)
    return jnp.where(n < max_exact, n, large)


def _lookup(tbl_ref, col, bucket):
    acc = jnp.zeros(bucket.shape, F32)
    for b in range(NUM_BUCKETS):
        acc = jnp.where(bucket == b, tbl_ref[b, col], acc)
    return acc


def _bias_tiles_kernel(tbl_ref, tp_ref, bs_ref, *, q0):
    h = pl.program_id(0)
    t = lax.broadcasted_iota(I32, (QB, QB), 0)
    k = lax.broadcasted_iota(I32, (QB, QB), 1)
    ts = lax.broadcasted_iota(I32, bs_ref.shape[1:], 0)
    ks = lax.broadcasted_iota(I32, bs_ref.shape[1:], 1)
    tp_ref[0, 0] = _lookup(tbl_ref, h, _bucket(t - k))
    tp_ref[0, 1] = _lookup(tbl_ref, h, _bucket(QB + t - k))
    bs_ref[0] = _lookup(tbl_ref, h, _bucket(q0 + ts - ks))


def bias_tiles(rel_bias, q0, n_q, lp):
    nh = rel_bias.shape[1]
    return pl.pallas_call(
        functools.partial(_bias_tiles_kernel, q0=q0),
        grid=(nh,),
        in_specs=[pl.BlockSpec(memory_space=pltpu.SMEM)],
        out_specs=[pl.BlockSpec((1, 2, QB, QB), lambda h: (h, 0, 0, 0)),
                   pl.BlockSpec((1, n_q, lp), lambda h: (h, 0, 0))],
        out_shape=[jax.ShapeDtypeStruct((nh, 2, QB, QB), F32),
                   jax.ShapeDtypeStruct((nh, n_q, lp), F32)],
        compiler_params=_cparams("arbitrary"),
        name="bias_tiles",
    )(rel_bias)


def _bias_cmp_kernel(tbl_ref, bp_ref, bs_ref, *, q0):
    h = pl.program_id(0)
    t_len, mc_p = bp_ref.shape[1:]
    end = CMP_BLOCK - 1

    def rows(rb, _):
        r0 = pl.multiple_of(rb * QB, QB)
        tp = r0 + lax.broadcasted_iota(I32, (QB, mc_p), 0)
        cp = lax.broadcasted_iota(I32, (QB, mc_p), 1)
        bp_ref[0, pl.ds(r0, QB), :] = _lookup(tbl_ref, h, _bucket(tp - (cp * CMP_STRIDE + end)))
        return 0

    lax.fori_loop(0, t_len // QB, rows, 0)
    ts = lax.broadcasted_iota(I32, bs_ref.shape[1:], 0)
    cs = lax.broadcasted_iota(I32, bs_ref.shape[1:], 1)
    bs_ref[0] = _lookup(tbl_ref, h, _bucket(q0 + ts - (cs * CMP_STRIDE + end)))


def bias_cmp(rel_bias, t_len, mc_p, q0, n_q, mc_s):
    return pl.pallas_call(
        functools.partial(_bias_cmp_kernel, q0=q0),
        grid=(A_HEADS,),
        in_specs=[pl.BlockSpec(memory_space=pltpu.SMEM)],
        out_specs=[pl.BlockSpec((1, t_len, mc_p), lambda h: (h, 0, 0)),
                   pl.BlockSpec((1, n_q, mc_s), lambda h: (h, 0, 0))],
        out_shape=[jax.ShapeDtypeStruct((A_HEADS, t_len, mc_p), F32),
                   jax.ShapeDtypeStruct((A_HEADS, n_q, mc_s), F32)],
        compiler_params=_cparams("arbitrary"),
        name="bias_cmp",
    )(rel_bias)


def _row_reduce(fn, lane_fn, x):
    tiles = [x[:, i:i + 128] for i in range(0, x.shape[1], 128)]
    return lane_fn(_tree_reduce(fn, tiles), axis=-1, keepdims=True)


def _softmax_rows(z, mask):
    z = jnp.where(mask, z, NEG)
    e = jnp.where(mask, jnp.exp(z - _row_reduce(jnp.maximum, jnp.max, z)), 0.0)
    l = _row_reduce(jnp.add, jnp.sum, e)
    return e * (1.0 / jnp.where(l > 0.0, l, 1.0))


def _gelu_tanh(x):
    return 0.5 * x * (1.0 + jnp.tanh(math.sqrt(2.0 / math.pi) * (x + 0.044715 * (x * x * x))))


def _compress(x_fn, m, w1a_ref, w1b_ref, w2_ref, pe_ref, next_fn, out_fn):
    pe = pe_ref[...].astype(MXU_DT)
    pos = _mm(pe, w1a_ref[...])[0:1] + _mm(pe, w1b_ref[...])[1:2]
    last = lax.broadcasted_iota(I32, (m, 1), 0) == m - 1
    for g in range(A_KV):
        xg = jnp.concatenate([x_fn(j, g).astype(MXU_DT) for j in range(CMP_STRIDE)], axis=1)
        hid = _gelu_tanh(_mm(xg, w1a_ref[...]) + next_fn(_mm(xg, w1b_ref[...])) + pos)
        out_fn(g, jnp.where(last, 0.0, _mm(hid.astype(MXU_DT), w2_ref[...])))


def _overlap(mc, jn, n_cmp, n_slc):
    c = lax.broadcasted_iota(I32, (mc, jn), 0)
    j = lax.broadcasted_iota(I32, (mc, jn), 1)
    ov = ((c * CMP_STRIDE < j * SEL_BLOCK + SEL_BLOCK) & (c * CMP_STRIDE + CMP_BLOCK > j * SEL_BLOCK)
          & (c < n_cmp) & (j < n_slc))
    return jnp.where(ov, 1.0, 0.0)


def _select_blocks(imp, pos, n_slc):
    jn = imp.shape[1]
    jidx = lax.broadcasted_iota(I32, (1, jn), 1)
    cur = pos // SEL_BLOCK
    forced = (jidx == 0) | (jidx == cur) | (jidx == cur - 1)
    future = jidx * SEL_BLOCK > pos
    score = jnp.where(future, -1.0, jnp.where(forced, 1e3, imp))
    score = jnp.where(jidx < n_slc, score, -2.0)

    def body(i, rank):
        col = jnp.sum(jnp.where(jidx == i, score, 0.0), axis=-1, keepdims=True)
        beats = jnp.where(col > score, 1.0, jnp.where(col == score, jnp.where(i < jidx, 1.0, 0.0), 0.0))
        return rank + beats

    rank = lax.fori_loop(0, n_slc, body, jnp.zeros(score.shape, F32), unroll=32)
    n_sel = min(N_SEL_BLOCKS, n_slc)
    return jnp.where((rank < n_sel) & (jidx < n_slc), 1.0, 0.0)


def _tree_reduce(fn, xs):
    xs = list(xs)
    while len(xs) > 1:
        xs = [fn(xs[i], xs[i + 1]) for i in range(0, len(xs) - 1, 2)] + ([xs[-1]] if len(xs) % 2 else [])
    return xs[0]


def _tree_sum(xs):
    return _tree_reduce(jnp.add, xs)


def _sort_key(s):
    bits = lax.bitcast_convert_type(jnp.where(s == 0.0, 0.0, s), I32)
    return jnp.where(bits < 0, bits ^ jnp.int32(0x7FFFFFFF), bits)


def _topk_madd(key_ref, madd_ref, valid_fn, nch, cw, k, nbits):
    n_rows = key_ref.shape[0]
    kf = jnp.float32(k)

    def count(fn):
        def body(c, acc):
            c0 = pl.multiple_of(c * cw, cw)
            hit = jnp.where(fn(c0, key_ref[:, pl.ds(c0, cw)]), 1.0, 0.0)
            return acc + _tree_sum(hit[:, i:i + 128] for i in range(0, cw, 128))
        acc = lax.fori_loop(0, nch, body, jnp.zeros((n_rows, 128), F32))
        return jnp.sum(acc, axis=-1, keepdims=True)

    int_min = jnp.int32(-2 ** 31)
    thr0 = jnp.where(count(lambda c0, key: key >= 0) >= kf, jnp.int32(0), int_min)

    def vbody(i, thr):
        cand = thr | lax.shift_left(jnp.int32(1), 30 - i)
        return jnp.where(count(lambda c0, key: key >= cand) >= kf, cand, thr)

    thr = lax.fori_loop(0, 31, vbody, thr0)
    need = kf - count(lambda c0, key: key > thr)

    def idx(c0):
        return c0 + lax.broadcasted_iota(I32, (1, cw), 1)

    def ibody(i, cut):
        cand = cut | lax.shift_left(jnp.int32(1), nbits - 1 - i)
        return jnp.where(count(lambda c0, key: (key == thr) & (idx(c0) < cand)) <= need, cand, cut)

    tied = jnp.max(count(lambda c0, key: key == thr) - need) > 0.0
    cut = lax.cond(tied, lambda: lax.fori_loop(0, nbits, ibody, jnp.zeros((n_rows, 1), I32)),
                   lambda: jnp.full((n_rows, 1), 2 ** nbits - 1, I32))

    def write(c, _):
        c0 = pl.multiple_of(c * cw, cw)
        key = key_ref[:, pl.ds(c0, cw)]
        sel = ((key > thr) | ((key == thr) & (idx(c0) < cut))) & valid_fn(c0)
        madd_ref[:, pl.ds(c0, cw)] = jnp.where(sel, 0.0, NEG)
        return 0

    lax.fori_loop(0, nch, write, 0)


def _topk_madd_t(key_ref, madd_ref, valid_fn, nch, cw, k, nbits):
    n_rows = key_ref.shape[1]
    kf = jnp.float32(k)

    def count(fn):
        def body(c, acc):
            c0 = pl.multiple_of(c * cw, cw)
            hit = jnp.where(fn(c0, key_ref[pl.ds(c0, cw), :]), 1.0, 0.0)
            return acc + _tree_sum(hit[i:i + 8] for i in range(0, cw, 8))
        acc = lax.fori_loop(0, nch, body, jnp.zeros((8, n_rows), F32))
        return jnp.sum(acc, axis=0, keepdims=True)

    int_min = jnp.int32(-2 ** 31)
    thr0 = jnp.where(count(lambda c0, key: key >= 0) >= kf, jnp.int32(0), int_min)

    def vbody(i, thr):
        cand = thr | lax.shift_left(jnp.int32(1), 30 - i)
        return jnp.where(count(lambda c0, key: key >= cand) >= kf, cand, thr)

    thr = lax.fori_loop(0, 31, vbody, thr0)
    need = kf - count(lambda c0, key: key > thr)

    def idx(c0):
        return c0 + lax.broadcasted_iota(I32, (cw, 1), 0)

    def ibody(i, cut):
        cand = cut | lax.shift_left(jnp.int32(1), nbits - 1 - i)
        return jnp.where(count(lambda c0, key: (key == thr) & (idx(c0) < cand)) <= need, cand, cut)

    tied = jnp.max(count(lambda c0, key: key == thr) - need) > 0.0
    cut = lax.cond(tied, lambda: lax.fori_loop(0, nbits, ibody, jnp.zeros((1, n_rows), I32)),
                   lambda: jnp.full((1, n_rows), 2 ** nbits - 1, I32))

    def write(c, _):
        c0 = pl.multiple_of(c * cw, cw)
        key = key_ref[pl.ds(c0, cw), :]
        sel = ((key > thr) | ((key == thr) & (idx(c0) < cut))) & valid_fn(c0)
        madd_ref[:, pl.ds(c0, cw)] = jnp.where(sel, 0.0, NEG).T
        return 0

    lax.fori_loop(0, nch, write, 0)


def _causal_attn(streams, dv, qb, pairs=False):
    m_rows = streams[0][0].shape[0]
    per = CHUNK // QB
    nact = qb // per + 1

    def stage(st, k0, w, bias):
        q, k_fn, _, _, _, madd_fn, s_ref = st
        s = _nt(q, k_fn(k0, w)) * SCALE + bias
        madd = madd_fn(k0, w)
        s_ref[:, pl.ds(k0, w)] = s if madd is None else s + madd

    def chunk_loop(body, init):
        n2 = nact // 2 if pairs else 0
        carry = lax.fori_loop(
            0, n2, lambda i, c: body(pl.multiple_of(i * 2 * CHUNK, 2 * CHUNK), 2 * CHUNK, c), init) if pairs else init
        return lax.fori_loop(
            0, nact - 2 * n2, lambda i, c: body(pl.multiple_of((2 * n2 + i) * CHUNK, CHUNK), CHUNK, c), carry)

    def lane_tiles(x):
        return [x[:, i:i + 128] for i in range(0, x.shape[1], 128)]

    def far(k0, w, _):
        for st in streams:
            stage(st, k0, w, st[3])
        return 0

    chunk_loop(far, 0)
    for st in streams:
        stage(st, pl.multiple_of(qb * QB, QB), QB, st[4](True))

    @pl.when(qb >= 1)
    def _():
        for st in streams:
            stage(st, pl.multiple_of((qb - 1) * QB, QB), QB, st[4](False))

    for j in range(1, per):
        @pl.when(qb % per + j < per)
        def _():
            for st in streams:
                st[6][:, pl.ds(pl.multiple_of((qb + j) * QB, QB), QB)] = jnp.full((m_rows, QB), NEG, F32)

    stat_w = 128 if pairs else 1

    def lanes_or_row(fn, lane_fn, x):
        return _tree_reduce(fn, lane_tiles(x)) if pairs else lane_fn(x, axis=-1, keepdims=True)

    def row_max(k0, w, ms):
        return tuple(jnp.maximum(m, lanes_or_row(jnp.maximum, jnp.max, st[6][:, pl.ds(k0, w)]))
                     for st, m in zip(streams, ms))

    ms = chunk_loop(row_max, tuple(jnp.full((m_rows, stat_w), NEG, F32) for _ in streams))
    ms = [jnp.max(m, axis=-1, keepdims=True) for m in ms]

    def pv(k0, w, carry):
        out = []
        for st, m, (l, acc) in zip(streams, ms, carry):
            p = jnp.exp(st[6][:, pl.ds(k0, w)] - m)
            out.append((l + lanes_or_row(jnp.add, jnp.sum, p), acc + _mm(p.astype(MXU_DT), st[2](k0, w))))
        return tuple(out)

    init = tuple((jnp.zeros((m_rows, stat_w), F32), jnp.zeros((m_rows, dv), F32)) for _ in streams)
    return [acc / jnp.sum(l, axis=-1, keepdims=True) for l, acc in chunk_loop(pv, init)]


def _causal_add(rep):
    t = lax.broadcasted_iota(I32, (QB, QB), 0)
    k = lax.broadcasted_iota(I32, (QB, QB), 1)
    return jnp.concatenate([jnp.where(k <= t, 0.0, NEG)] * rep, axis=0)


def _far_bias(tbl_ref, cols):
    return jnp.concatenate([jnp.full((QB, 1), tbl_ref[NUM_BUCKETS - 1, c], F32) for c in cols], axis=0)


def _near_bias(tp_ref, cols, diag):
    return jnp.concatenate([tp_ref[c, 0 if diag else 1] for c in cols], axis=0)


def _stack_heads(ref, col0, n):
    return jnp.concatenate([ref[:, col0 + r * HEAD_DIM:col0 + (r + 1) * HEAD_DIM] for r in range(n)], axis=0)


def _compress_prompt_kernel(xk_ref, xv_ref, w1a, w1b, w2, pe, ok_ref, ov_ref):
    lanes = lambda x_ref: (lambda j, g: x_ref[0, :, (2 * j + g) * HEAD_DIM:(2 * j + g + 1) * HEAD_DIM])
    m = xk_ref.shape[1]
    for i, (x_ref, o_ref) in enumerate(((xk_ref, ok_ref), (xv_ref, ov_ref))):
        def store(g, tokens, o_ref=o_ref):
            o_ref[0, :, g * HEAD_DIM:(g + 1) * HEAD_DIM] = tokens

        _compress(lanes(x_ref), m, w1a.at[i], w1b.at[i], w2.at[i], pe.at[i],
                  lambda y: pltpu.roll(y, m - 1, 0), store)


def _cmp_weight_specs():
    full = lambda *shape: pl.BlockSpec(shape, lambda *_: (0,) * len(shape))
    half = CMP_STRIDE * HEAD_DIM
    return [full(2, half, CMP_HIDDEN), full(2, half, CMP_HIDDEN), full(2, CMP_HIDDEN, HEAD_DIM), full(2, 16, half)]


def compress_prompt(xk, xv, cw):
    n, m, w = xk.shape
    spec = pl.BlockSpec((1, m, w), lambda i: (i, 0, 0))
    ospec = pl.BlockSpec((1, m, A_KV * HEAD_DIM), lambda i: (i, 0, 0))
    osh = jax.ShapeDtypeStruct((n, m, A_KV * HEAD_DIM), F32)
    return pl.pallas_call(
        _compress_prompt_kernel,
        grid=(n,),
        in_specs=[spec, spec] + _cmp_weight_specs(),
        out_specs=[ospec, ospec],
        out_shape=[osh, osh],
        compiler_params=_cparams("parallel"),
        name="compress_prompt",
    )(xk, xv, *cw)


def _window_attn(q, kw_ref, vw_ref, gl, qb, pos4, tbl_ref, tp_ref, cols):
    n_tiles = WINDOW // QB + 1
    width = n_tiles * QB
    lo = jnp.maximum(qb - (n_tiles - 1), 0)
    w0 = pl.multiple_of(lo * QB, QB)
    tiles = []
    for j in range(n_tiles):
        rel = qb - (lo + j)
        tiles.append(jnp.concatenate(
            [jnp.where(rel == 0, tp_ref[c, 0], jnp.where(rel == 1, tp_ref[c, 1], tbl_ref[NUM_BUCKETS - 1, c]))
             for c in cols], axis=0))
    s = _nt(q, kw_ref[pl.ds(w0, width), gl].astype(MXU_DT)) * SCALE + jnp.concatenate(tiles, axis=1)
    dist = pos4 - (w0 + lax.broadcasted_iota(I32, (1, width), 1))
    p = _softmax_rows(s, (dist >= 0) & (dist < WINDOW))
    return _mm(p.astype(MXU_DT), vw_ref[pl.ds(w0, width), gl].astype(MXU_DT))


def _nsa_prompt_kernel(tbl_ref, q_ref, tail_ref, kc_ref, vc_ref, ks_ref, vs_ref, kw_ref, vw_ref,
                       tp_ref, bc_ref, o_ref, s_ref, madd_ref, *, t_len):
    qb = pl.program_id(1)
    mc = kc_ref.shape[1]
    n_cmp = (t_len - CMP_BLOCK) // CMP_STRIDE + 1
    n_slc = -(-t_len // SEL_BLOCK)
    pos = qb * QB + lax.broadcasted_iota(I32, (QB, 1), 0)
    pos4 = jnp.concatenate([pos] * A_GROUP, axis=0)
    cidx = lax.broadcasted_iota(I32, (1, mc), 1)
    gates = jax.nn.sigmoid(tail_ref[...])
    overlap = _overlap(mc, QB, n_cmp, n_slc)
    causal_add = _causal_add(A_GROUP)
    onehot = jnp.where(lax.broadcasted_iota(I32, (QB, t_len), 0)
                       == lax.broadcasted_iota(I32, (QB, t_len), 1) // SEL_BLOCK, 1.0, 0.0).astype(MXU_DT)

    o_cmp, o_win, streams = [], [], []
    for g in range(A_KV):
        cols = [g * A_GROUP + r for r in range(A_GROUP)]
        gl = slice(g * HEAD_DIM, (g + 1) * HEAD_DIM)
        q = _stack_heads(q_ref, g * A_GROUP * HEAD_DIM, A_GROUP).astype(MXU_DT)
        lc = (_nt(q, kc_ref[0, :, gl].astype(MXU_DT)) * SCALE
              + jnp.concatenate([bc_ref[c] for c in cols], axis=0))
        p_cmp = _softmax_rows(lc, (pos4 >= cidx * CMP_STRIDE + (CMP_BLOCK - 1)) & (cidx < n_cmp))
        o_cmp.append(_mm(p_cmp.astype(MXU_DT), vc_ref[0, :, gl].astype(MXU_DT)))
        p_sum = sum(p_cmp[r * QB:(r + 1) * QB] for r in range(A_GROUP))
        imp = jnp.dot(p_sum, overlap, preferred_element_type=F32, precision=lax.Precision.HIGHEST)
        sel = _select_blocks(imp, pos, n_slc).astype(MXU_DT)
        madd_ref[g] = jnp.where(_mm(sel, onehot) > 0.5, 0.0, NEG)
        o_win.append(_window_attn(q, kw_ref, vw_ref, gl, qb, pos4, tbl_ref, tp_ref, cols))
        streams.append((
            q, lambda k0, w, gl=gl: ks_ref[pl.ds(k0, w), gl].astype(MXU_DT),
            lambda k0, w, gl=gl: vs_ref[pl.ds(k0, w), gl].astype(MXU_DT), _far_bias(tbl_ref, cols),
            lambda diag, cols=cols: _near_bias(tp_ref, cols, diag) + (causal_add if diag else 0.0),
            lambda k0, w, g=g: jnp.concatenate([madd_ref[g, :, pl.ds(k0, w)]] * A_GROUP, axis=0),
            s_ref.at[g]))
    o_slc = _causal_attn(streams, HEAD_DIM, qb)

    for h in range(A_HEADS):
        g, r = divmod(h, A_GROUP)
        c = T_GA + h * N_GATES
        rows = slice(r * QB, (r + 1) * QB)
        o = (gates[:, c:c + 1] * o_cmp[g][rows] + gates[:, c + 1:c + 2] * o_slc[g][rows]
             + gates[:, c + 2:c + 3] * o_win[g][rows])
        o_ref[:, h * HEAD_DIM:(h + 1) * HEAD_DIM] = o.astype(o_ref.dtype)


def nsa_prompt(z, k_cmp, v_cmp, tp, bc, rel_bias, n, t_len):
    nb = t_len // QB
    mc = k_cmp.shape[1]
    kv = lambda c: pl.BlockSpec((t_len, 256), lambda i, j: (i, c // 256))
    cmp_spec = pl.BlockSpec((1, mc, 256), lambda i, j: (i, 0, 0))
    return pl.pallas_call(
        functools.partial(_nsa_prompt_kernel, t_len=t_len),
        grid=(n, nb),
        in_specs=[pl.BlockSpec(memory_space=pltpu.SMEM),
                  pl.BlockSpec((QB, 1024), lambda i, j: (i * nb + j, C_QA // 1024)),
                  pl.BlockSpec((QB, 128), lambda i, j: (i * nb + j, C_TAIL // 128)),
                  cmp_spec, cmp_spec, kv(C_KS), kv(C_VS), kv(C_KW), kv(C_VW),
                  pl.BlockSpec(tp.shape, lambda i, j: (0, 0, 0, 0)),
                  pl.BlockSpec((A_HEADS, QB, mc), lambda i, j: (0, j, 0))],
        out_specs=pl.BlockSpec((QB, 1024), lambda i, j: (i * nb + j, 0)),
        out_shape=jax.ShapeDtypeStruct((n * t_len, 1024), MXU_DT),
        scratch_shapes=[pltpu.VMEM((A_KV, A_GROUP * QB, t_len), F32), pltpu.VMEM((A_KV, QB, t_len), F32)],
        compiler_params=_cparams("parallel", "arbitrary"),
        name="nsa_prompt",
    )(rel_bias, z, z, k_cmp, v_cmp, z, z, z, z, tp, bc)


def _dsa_prompt_kernel(tbl_ref, q_ref, qi_ref, tailq_ref, tailk_ref, kb_ref, vb_ref, tp_ref, o_ref,
                       key_ref, madd_ref, s_ref, *, topk, nbits):
    qb = pl.program_id(1)
    nact = qb // (CHUNK // QB) + 1
    pos = qb * QB + lax.broadcasted_iota(I32, (1, QB), 1)
    wi_t = tailq_ref[...].T[T_WI:T_WI + IDX_HEADS]
    qis = [qi_ref[:, h * IDX_DIM:(h + 1) * IDX_DIM].astype(MXU_DT) for h in range(IDX_HEADS)]

    def causal(c0):
        return c0 + lax.broadcasted_iota(I32, (CHUNK, 1), 0) <= pos

    def index_chunk(c, _):
        c0 = pl.multiple_of(c * CHUNK, CHUNK)
        ki = tailk_ref[pl.ds(c0, CHUNK), T_KI:T_KI + IDX_DIM].astype(MXU_DT)
        score = sum(jnp.maximum(_nt(ki, qis[h]), 0.0) * wi_t[h:h + 1] for h in range(IDX_HEADS))
        score = score * (IDX_DIM ** -0.5 * IDX_HEADS ** -0.5)
        key_ref[pl.ds(c0, CHUNK), :] = _sort_key(jnp.where(causal(c0), score, NEG))
        return 0

    lax.fori_loop(0, nact, index_chunk, 0)
    _topk_madd_t(key_ref, madd_ref, causal, nact, CHUNK, topk, nbits)

    streams = []
    for g in range(B_KV):
        cols = [A_HEADS + g * B_GROUP + r for r in range(B_GROUP)]
        gl = slice(g * HEAD_DIM, (g + 1) * HEAD_DIM)
        streams.append((
            _stack_heads(q_ref, g * B_GROUP * HEAD_DIM, B_GROUP).astype(MXU_DT),
            lambda k0, w, gl=gl: kb_ref[pl.ds(k0, w), gl].astype(MXU_DT),
            lambda k0, w, gl=gl: vb_ref[pl.ds(k0, w), gl].astype(MXU_DT),
            _far_bias(tbl_ref, cols), lambda diag, cols=cols: _near_bias(tp_ref, cols, diag),
            lambda k0, w: jnp.concatenate([madd_ref[:, pl.ds(k0, w)]] * B_GROUP, axis=0), s_ref.at[g]))
    outs = _causal_attn(streams, HEAD_DIM, qb)
    for h in range(B_HEADS):
        g, r = divmod(h, B_GROUP)
        o_ref[:, h * HEAD_DIM:(h + 1) * HEAD_DIM] = outs[g][r * QB:(r + 1) * QB].astype(o_ref.dtype)


def dsa_prompt(z, tp, rel_bias, n, t_len):
    nb = t_len // QB
    topk = min(DSA_TOPK, t_len // 4)
    nbits = int(t_len).bit_length()
    return pl.pallas_call(
        functools.partial(_dsa_prompt_kernel, topk=topk, nbits=nbits),
        grid=(n, nb),
        in_specs=[pl.BlockSpec(memory_space=pltpu.SMEM),
                  pl.BlockSpec((QB, 1024), lambda i, j: (i * nb + j, C_QB // 1024)),
                  pl.BlockSpec((QB, 256), lambda i, j: (i * nb + j, C_QI // 256)),
                  pl.BlockSpec((QB, 128), lambda i, j: (i * nb + j, C_TAIL // 128)),
                  pl.BlockSpec((t_len, 128), lambda i, j: (i, C_TAIL // 128)),
                  pl.BlockSpec((t_len, 256), lambda i, j: (i, C_KB // 256)),
                  pl.BlockSpec((t_len, 256), lambda i, j: (i, C_VB // 256)),
                  pl.BlockSpec(tp.shape, lambda i, j: (0, 0, 0, 0))],
        out_specs=pl.BlockSpec((QB, 1024), lambda i, j: (i * nb + j, 0)),
        out_shape=jax.ShapeDtypeStruct((n * t_len, 1024), MXU_DT),
        scratch_shapes=[pltpu.VMEM((t_len, QB), I32), pltpu.VMEM((QB, t_len), F32),
                        pltpu.VMEM((B_KV, B_GROUP * QB, t_len), F32)],
        compiler_params=_cparams("parallel", "arbitrary"),
        name="dsa_prompt",
    )(rel_bias, z, z, z, z, z, z, tp)


def _diff_lambda(lam_ref):
    v = lam_ref[...]
    e1 = jnp.exp(jnp.sum(v[0:1] * v[1:2], axis=-1, keepdims=True))
    e2 = jnp.exp(jnp.sum(v[2:3] * v[3:4], axis=-1, keepdims=True))
    return e1 - e2 + LAMBDA_INIT


def _diff_finish(o, hn_ref):
    return _rms(o, hn_ref[...]) * (1.0 - LAMBDA_INIT)


def _diff_prompt_kernel(tbl_ref, q_ref, k_ref, v_ref, tp_ref, lam_ref, hn_ref, o_ref, s_ref, *, gps):
    g0 = pl.program_id(1) * gps
    qb = pl.program_id(2)
    causal_add = _causal_add(C_GROUP)
    streams = []
    for gi in range(gps):
        for m in range(2):
            cols = [m * C_HEADS + (g0 + gi) * C_GROUP + r for r in range(C_GROUP)]
            q = jnp.concatenate(
                [q_ref[:, ((gi * C_GROUP + r) * 2 + m) * HEAD_DIM:((gi * C_GROUP + r) * 2 + m + 1) * HEAD_DIM]
                 for r in range(C_GROUP)], axis=0).astype(MXU_DT)
            kl = slice((gi * 2 + m) * HEAD_DIM, (gi * 2 + m + 1) * HEAD_DIM)
            vl = slice(gi * C_VDIM, (gi + 1) * C_VDIM)
            streams.append((
                q, lambda k0, w, kl=kl: k_ref[pl.ds(k0, w), kl].astype(MXU_DT),
                lambda k0, w, vl=vl: v_ref[pl.ds(k0, w), vl].astype(MXU_DT), _far_bias(tbl_ref, cols),
                lambda diag, cols=cols: _near_bias(tp_ref, cols, diag) + (causal_add if diag else 0.0),
                lambda k0, w: None, s_ref.at[gi * 2 + m]))
    outs = _causal_attn(streams, C_VDIM, qb, pairs=True)
    lam = _diff_lambda(lam_ref)
    for gi in range(gps):
        o = _diff_finish(outs[gi * 2] - lam * outs[gi * 2 + 1], hn_ref)
        for r in range(C_GROUP):
            col = (gi * C_GROUP + r) * C_VDIM
            o_ref[:, col:col + C_VDIM] = o[r * QB:(r + 1) * QB].astype(o_ref.dtype)


def diff_prompt(z1, tp, rel_bias, lam_vecs, head_norm, n, t_len, gps=2):
    nb = t_len // QB
    qw = gps * C_GROUP * 2 * HEAD_DIM
    kw = gps * 2 * HEAD_DIM
    q_cols, k_cols = C_HEADS * 2 * HEAD_DIM, C_KV * 2 * HEAD_DIM
    return pl.pallas_call(
        functools.partial(_diff_prompt_kernel, gps=gps),
        grid=(n, C_KV // gps, nb),
        in_specs=[pl.BlockSpec(memory_space=pltpu.SMEM),
                  pl.BlockSpec((QB, qw), lambda i, g, j: (i * nb + j, g)),
                  pl.BlockSpec((t_len, kw), lambda i, g, j: (i, q_cols // kw + g)),
                  pl.BlockSpec((t_len, kw), lambda i, g, j: (i, (q_cols + k_cols) // kw + g)),
                  pl.BlockSpec(tp.shape, lambda i, g, j: (0, 0, 0, 0)),
                  pl.BlockSpec((4, HEAD_DIM), lambda i, g, j: (0, 0)),
                  pl.BlockSpec((1, C_VDIM), lambda i, g, j: (0, 0))],
        out_specs=pl.BlockSpec((QB, qw), lambda i, g, j: (i * nb + j, g)),
        out_shape=jax.ShapeDtypeStruct((n * t_len, C_HEADS * C_VDIM), MXU_DT),
        scratch_shapes=[pltpu.VMEM((2 * gps, C_GROUP * QB, t_len), F32)],
        compiler_params=_cparams("parallel", "parallel", "arbitrary"),
        name="diff_prompt",
    )(rel_bias, z1, z1, z1, tp, lam_vecs, head_norm.reshape(1, C_VDIM))


def _page_gather(pt_ref, n_pages, items, sem):
    def copy(i, pg, p, slot):
        pool_ref, buf_ref, rows = items[i]
        src = pool_ref.at[pl.ds(pl.multiple_of(pg * rows, rows), rows)]
        dst = buf_ref.at[pl.ds(pl.multiple_of((slot * n_pages + p) * rows, rows), rows)]
        return pltpu.make_async_copy(src, dst, sem.at[i, slot])

    def start(bb, slot):
        def body(p, _):
            for i in range(len(items)):
                copy(i, pt_ref[bb, p], p, slot).start()
            return 0
        lax.fori_loop(0, n_pages, body, 0)

    def wait(i, slot):
        def body(p, _):
            copy(i, 0, 0, slot).wait()
            return 0
        lax.fori_loop(0, n_pages, body, 0)

    return start, wait


def _prefetch(b, nb, start):
    slot = b % 2

    @pl.when(b == 0)
    def _():
        start(0, 0)

    @pl.when(b + 1 < nb)
    def _():
        start(b + 1, 1 - slot)

    return slot


def _pad_rows(x, rows):
    return jnp.concatenate([x, jnp.zeros((rows - x.shape[0], x.shape[1]), x.dtype)], axis=0)


def _page_rows(pool_ref, rows):
    return lambda pg: pool_ref.at[pl.ds(pl.multiple_of(pg * rows, rows), rows)]


def _interleaved(buf_ref, n, j, row0=0):
    return lambda c0, ch: buf_ref[pl.ds(row0 + c0 * n + j, ch, stride=n), :]


def _sample_scores(q, k_fn, knew, bias_fn, mask_fn, s_ref, past, ch, scale=SCALE):
    def body(c, _):
        c0 = pl.multiple_of(c * ch, ch)
        k = k_fn(c0, ch).astype(MXU_DT)
        s = _nt(q, k) * scale + bias_fn(c0, ch)
        s_ref[:, pl.ds(c0, ch)] = jnp.where(mask_fn(c0, ch, False), s, NEG)
        return 0

    lax.fori_loop(0, past // ch, body, 0, unroll=4)
    s = _nt(q, _pad_rows(knew, 128).astype(MXU_DT)) * scale + bias_fn(past, 128)
    s_ref[:, past:past + 128] = jnp.where(mask_fn(past, 128, True), s, NEG)


def _sample_softmax(s_ref):
    z = s_ref[...]
    e = jnp.where(z > 0.5 * NEG, jnp.exp(z - _row_reduce(jnp.maximum, jnp.max, z)), 0.0)
    l = _row_reduce(jnp.add, jnp.sum, e)
    return e * (1.0 / jnp.where(l > 0.0, l, 1.0))


def _sample_pv(p_ref, v_fn, vnew, past, ch):
    def body(c, acc):
        c0 = pl.multiple_of(c * ch, ch)
        return acc + _mm(p_ref[:, pl.ds(c0, ch)].astype(MXU_DT), v_fn(c0, ch).astype(MXU_DT))

    acc = lax.fori_loop(0, past // ch, body, jnp.zeros((p_ref.shape[0], vnew.shape[1]), F32), unroll=4)
    return acc + _mm(p_ref[:, past:past + 128].astype(MXU_DT), _pad_rows(vnew, 128).astype(MXU_DT))


def _new_key_mask(nq, rep):
    t = lax.broadcasted_iota(I32, (nq, 128), 0)
    j = lax.broadcasted_iota(I32, (nq, 128), 1)
    return jnp.concatenate([(j <= t) & (j < nq)] * rep, axis=0)


def _compress_sample_kernel(pt_ref, pk_ref, pv_ref, w1a, w1b, w2, pe, o_ref, buf, sem, *, n_pages):
    step = pl.program_id(0)
    cpp = PAGE // CMP_STRIDE
    rows = CMP_STRIDE * A_KV
    m = n_pages * cpp

    def copy(pool_ref, pg, p, slot):
        src = pool_ref.at[pl.ds(pl.multiple_of(pg * PAGE * A_KV, PAGE * A_KV), PAGE * A_KV)]
        dst = buf.at[pl.ds(pl.multiple_of((slot * n_pages + p) * PAGE_PITCH, 8), PAGE * A_KV)]
        return pltpu.make_async_copy(src, dst, sem.at[slot])

    def start(st, slot):
        for which, pool_ref in enumerate((pk_ref, pv_ref)):
            @pl.when(st % 2 == which)
            def _(pool_ref=pool_ref):
                def body(p, _):
                    copy(pool_ref, pt_ref[st // 2, p], p, slot).start()
                    return 0
                lax.fori_loop(0, n_pages, body, 0)

    slot = _prefetch(step, pl.num_programs(0), start)

    def wait(p, _):
        copy(pk_ref, 0, 0, slot).wait()
        return 0

    lax.fori_loop(0, n_pages, wait, 0)
    which = step % 2
    row0 = slot * n_pages * PAGE_PITCH

    def x_fn(j, g):
        return jnp.concatenate([buf[pl.ds(row0 + i * rows + 2 * j + g, n_pages, stride=PAGE_PITCH), :]
                                for i in range(cpp)], axis=0)

    def next_fn(y):
        return jnp.concatenate([y[n_pages:], pltpu.roll(y[:n_pages], n_pages - 1, 0)], axis=0)

    def store(g, tokens):
        for i in range(cpp):
            o_ref[0, 0, g, pl.ds(i, n_pages, stride=cpp), :] = tokens[i * n_pages:(i + 1) * n_pages]

    _compress(x_fn, m, w1a.at[which], w1b.at[which], w2.at[which], pe.at[which], next_fn, store)


def compress_sample(pool_k, pool_v, page_table, cw):
    bd, n_pages = page_table.shape
    m = n_pages * (PAGE // CMP_STRIDE)
    return pl.pallas_call(
        functools.partial(_compress_sample_kernel, n_pages=n_pages),
        grid_spec=pltpu.PrefetchScalarGridSpec(
            num_scalar_prefetch=1, grid=(2 * bd,),
            in_specs=[pl.BlockSpec(memory_space=pl.ANY), pl.BlockSpec(memory_space=pl.ANY)] + _cmp_weight_specs(),
            out_specs=pl.BlockSpec((1, 1, A_KV, m, HEAD_DIM), lambda s, pt: (s // 2, s % 2, 0, 0, 0)),
            scratch_shapes=[pltpu.VMEM((2 * n_pages * PAGE_PITCH, HEAD_DIM), F32), pltpu.SemaphoreType.DMA((2,))]),
        out_shape=jax.ShapeDtypeStruct((bd, 2, A_KV, m, HEAD_DIM), F32),
        compiler_params=_cparams("arbitrary"),
        name="compress_sample",
    )(page_table, pool_k, pool_v, *cw)


def _nsa_sample_kernel(pt_ref, z_ref, kc_ref, vc_ref, pks_ref, pvs_ref, wk_ref, wv_ref, bs_ref, bc_ref,
                       o_ref, kbuf, vbuf, s_ref, sw_ref, chosen_ref, sem, *, n_pages, ch):
    past = n_pages * PAGE
    nq = z_ref.shape[0]
    mc = kc_ref.shape[3]
    t_len = past + nq
    n_cmp = (t_len - CMP_BLOCK) // CMP_STRIDE + 1
    n_slc = -(-t_len // SEL_BLOCK)
    jn = 128 * (-(-n_slc // 128))
    wb = wk_ref.shape[0] // A_KV
    start, wait = _page_gather(pt_ref, n_pages, ((pks_ref, kbuf, PAGE * A_KV), (pvs_ref, vbuf, PAGE * A_KV)), sem)
    slot = _prefetch(pl.program_id(0), pl.num_programs(0), start)
    row0 = slot * past * A_KV

    pos = past + lax.broadcasted_iota(I32, (nq, 1), 0)
    pos4 = jnp.concatenate([pos] * A_GROUP, axis=0)
    cidx = lax.broadcasted_iota(I32, (1, mc), 1)
    gates = jax.nn.sigmoid(z_ref[:, C_TAIL:C_TAIL + 128])
    overlap = _overlap(mc, jn, n_cmp, n_slc)
    new_mask = _new_key_mask(nq, A_GROUP)
    first_half = lax.broadcasted_iota(I32, (nq, 2 * SEL_BLOCK), 1) < SEL_BLOCK
    waited = False

    for g in range(A_KV):
        cols = [g * A_GROUP + r for r in range(A_GROUP)]
        q = _stack_heads(z_ref, C_QA + g * A_GROUP * HEAD_DIM, A_GROUP).astype(MXU_DT)
        lc = (_nt(q, kc_ref[0, 0, g].astype(MXU_DT)) * SCALE
              + jnp.concatenate([bc_ref[c] for c in cols], axis=0))
        p_cmp = _softmax_rows(lc, (pos4 >= cidx * CMP_STRIDE + (CMP_BLOCK - 1)) & (cidx < n_cmp))
        o_cmp = _mm(p_cmp.astype(MXU_DT), vc_ref[0, 0, g].astype(MXU_DT))
        p_sum = sum(p_cmp[r * nq:(r + 1) * nq] for r in range(A_GROUP))
        imp = jnp.dot(p_sum, overlap, preferred_element_type=F32, precision=lax.Precision.HIGHEST)
        sel = _select_blocks(imp, pos, n_slc)
        def win_bias(c0, w, cols=cols):
            return jnp.concatenate([bs_ref[c, :, pl.ds(past - wb + c0, w)] for c in cols], axis=0)

        def win_mask(c0, w, is_new):
            dist = pos4 - (past - wb + c0 + lax.broadcasted_iota(I32, (1, w), 1))
            valid = (dist >= 0) & (dist < WINDOW)
            return valid & new_mask if is_new else valid

        _sample_scores(q, _interleaved(wk_ref, A_KV, g),
                       z_ref[:, C_KW + g * HEAD_DIM:C_KW + (g + 1) * HEAD_DIM],
                       win_bias, win_mask, sw_ref, wb, wb)
        sw_ref[...] = _sample_softmax(sw_ref)
        o_win = _sample_pv(sw_ref, _interleaved(wv_ref, A_KV, g),
                           z_ref[:, C_VW + g * HEAD_DIM:C_VW + (g + 1) * HEAD_DIM], wb, wb)
        if not waited:
            wait(0, slot)
            wait(1, slot)
            waited = True

        def slc_bias(c0, w, cols=cols):
            return jnp.concatenate([bs_ref[c, :, pl.ds(c0, w)] for c in cols], axis=0)

        for kk in range((past + 128) // 128):
            chosen_ref[:, kk * 128:(kk + 1) * 128] = jnp.where(
                first_half, sel[:, 2 * kk:2 * kk + 1], sel[:, 2 * kk + 1:2 * kk + 2])

        def slc_mask(c0, w, is_new):
            chosen = jnp.concatenate([chosen_ref[:, pl.ds(c0, w)] > 0.5] * A_GROUP, axis=0)
            return chosen & new_mask if is_new else chosen

        _sample_scores(q, _interleaved(kbuf, A_KV, g, row0),
                       z_ref[:, C_KS + g * HEAD_DIM:C_KS + (g + 1) * HEAD_DIM],
                       slc_bias, slc_mask, s_ref, past, ch)
        s_ref[...] = _sample_softmax(s_ref)
        o_slc = _sample_pv(s_ref, _interleaved(vbuf, A_KV, g, row0),
                           z_ref[:, C_VS + g * HEAD_DIM:C_VS + (g + 1) * HEAD_DIM], past, ch)
        for r in range(A_GROUP):
            h = g * A_GROUP + r
            c = T_GA + h * N_GATES
            rows = slice(r * nq, (r + 1) * nq)
            o_ref[:, h * HEAD_DIM:(h + 1) * HEAD_DIM] = (
                gates[:, c:c + 1] * o_cmp[rows] + gates[:, c + 1:c + 2] * o_slc[rows]
                + gates[:, c + 2:c + 3] * o_win[rows])


def nsa_sample(zs, kv_cmp, pool_ks, pool_vs, win_k, win_v, bs, bc, page_table, ch=1024):
    bd, n_pages = page_table.shape
    nq = zs.shape[0] // bd
    past = n_pages * PAGE
    mc = kv_cmp.shape[3]
    wrows = win_k.shape[0] // bd
    wb = wrows // A_KV
    win_spec = pl.BlockSpec((wrows, HEAD_DIM), lambda i, pt: (i, 0))
    buf = pltpu.VMEM((2 * past * A_KV, HEAD_DIM), F32)
    return pl.pallas_call(
        functools.partial(_nsa_sample_kernel, n_pages=n_pages, ch=ch),
        grid_spec=pltpu.PrefetchScalarGridSpec(
            num_scalar_prefetch=1, grid=(bd,),
            in_specs=[pl.BlockSpec((nq, zs.shape[1]), lambda i, pt: (i, 0)),
                      pl.BlockSpec((1, 1, A_KV, mc, HEAD_DIM), lambda i, pt: (i, 0, 0, 0, 0)),
                      pl.BlockSpec((1, 1, A_KV, mc, HEAD_DIM), lambda i, pt: (i, 1, 0, 0, 0)),
                      pl.BlockSpec(memory_space=pl.ANY), pl.BlockSpec(memory_space=pl.ANY),
                      win_spec, win_spec,
                      pl.BlockSpec((A_HEADS,) + bs.shape[1:], lambda i, pt: (0, 0, 0)),
                      pl.BlockSpec(bc.shape, lambda i, pt: (0, 0, 0))],
            out_specs=pl.BlockSpec((nq, 1024), lambda i, pt: (i, 0)),
            scratch_shapes=[buf, buf,
                            pltpu.VMEM((A_GROUP * nq, past + 128), F32),
                            pltpu.VMEM((A_GROUP * nq, wb + 128), F32),
                            pltpu.VMEM((nq, past + 128), F32),
                            pltpu.SemaphoreType.DMA((2, 2))]),
        out_shape=jax.ShapeDtypeStruct((bd * nq, 1024), F32),
        compiler_params=_cparams("arbitrary"),
        name="nsa_sample",
    )(page_table, zs, kv_cmp, kv_cmp, pool_ks, pool_vs, win_k, win_v, bs, bc)


def _dsa_sample_kernel(pt_ref, z_ref, pk_ref, pv_ref, pi_ref, bs_ref, o_ref,
                       kbuf, vbuf, ibuf, s_ref, sc_ref, key_ref, sel_ref, sem, *, n_pages, ch, topk, nbits):
    past = n_pages * PAGE
    nq = z_ref.shape[0]
    start, wait = _page_gather(pt_ref, n_pages, ((pk_ref, kbuf, PAGE * B_KV), (pv_ref, vbuf, PAGE * B_KV),
                                                 (pi_ref, ibuf, IDX_DIM)), sem)
    slot = _prefetch(pl.program_id(0), pl.num_programs(0), start)
    row0 = slot * past * B_KV
    qi = jnp.concatenate([z_ref[:, C_QI + h * IDX_DIM:C_QI + (h + 1) * IDX_DIM] for h in range(IDX_HEADS)],
                         axis=0).astype(MXU_DT)
    wi = z_ref[:, C_TAIL + T_WI:C_TAIL + T_WI + IDX_HEADS]
    wait(2, slot)

    def index_page(p, _):
        kt = ibuf[pl.ds(pl.multiple_of((slot * n_pages + p) * IDX_DIM, IDX_DIM), IDX_DIM), :]
        s_ref[:, pl.ds(pl.multiple_of(p * PAGE, PAGE), PAGE)] = _mm(qi, kt.astype(MXU_DT))
        return 0

    lax.fori_loop(0, n_pages, index_page, 0, unroll=8)
    ki_new = _pad_rows(z_ref[:, C_TAIL + T_KI:C_TAIL + T_KI + IDX_DIM], 128).astype(MXU_DT)
    s_ref[:, past:past + 128] = _nt(qi, ki_new)
    rel = jnp.maximum(s_ref[...], 0.0)
    score = sum(rel[h * nq:(h + 1) * nq] * wi[:, h:h + 1] for h in range(IDX_HEADS))
    score = score * (IDX_DIM ** -0.5 * IDX_HEADS ** -0.5)
    new_j = lax.broadcasted_iota(I32, score.shape, 1) - past
    causal = (new_j < 0) | ((new_j <= lax.broadcasted_iota(I32, score.shape, 0)) & (new_j < nq))
    key_ref[...] = _sort_key(jnp.where(causal, score, NEG))
    _topk_madd(key_ref, sel_ref, lambda c0: causal, 1, score.shape[1], topk, nbits)

    wait(0, slot)
    wait(1, slot)
    for g in range(B_KV):
        cols = [g * B_GROUP + r for r in range(B_GROUP)]
        q = _stack_heads(z_ref, C_QB + g * B_GROUP * HEAD_DIM, B_GROUP).astype(MXU_DT)

        def bias(c0, w, cols=cols):
            return jnp.concatenate([bs_ref[c, :, pl.ds(c0, w)] for c in cols], axis=0)

        def mask(c0, w, is_new):
            return jnp.concatenate([sel_ref[:, pl.ds(c0, w)] > 0.5 * NEG] * B_GROUP, axis=0)

        _sample_scores(q, _interleaved(kbuf, B_KV, g, row0),
                       z_ref[:, C_KB + g * HEAD_DIM:C_KB + (g + 1) * HEAD_DIM], bias, mask, sc_ref, past, ch)
        sc_ref[...] = _sample_softmax(sc_ref)
        o = _sample_pv(sc_ref, _interleaved(vbuf, B_KV, g, row0),
                       z_ref[:, C_VB + g * HEAD_DIM:C_VB + (g + 1) * HEAD_DIM], past, ch)
        for r in range(B_GROUP):
            h = g * B_GROUP + r
            o_ref[:, h * HEAD_DIM:(h + 1) * HEAD_DIM] = o[r * nq:(r + 1) * nq]


def dsa_sample(zs, pool_k, pool_v, pool_i, bs, page_table, ch=1024):
    bd, n_pages = page_table.shape
    nq = zs.shape[0] // bd
    past = n_pages * PAGE
    lp = past + 128
    topk = min(DSA_TOPK, (past + nq) // 4)
    return pl.pallas_call(
        functools.partial(_dsa_sample_kernel, n_pages=n_pages, ch=ch, topk=topk, nbits=int(lp).bit_length()),
        grid_spec=pltpu.PrefetchScalarGridSpec(
            num_scalar_prefetch=1, grid=(bd,),
            in_specs=[pl.BlockSpec((nq, zs.shape[1]), lambda i, pt: (i, 0)),
                      pl.BlockSpec(memory_space=pl.ANY), pl.BlockSpec(memory_space=pl.ANY),
                      pl.BlockSpec(memory_space=pl.ANY),
                      pl.BlockSpec((B_HEADS,) + bs.shape[1:], lambda i, pt: (1, 0, 0))],
            out_specs=pl.BlockSpec((nq, 1024), lambda i, pt: (i, 0)),
            scratch_shapes=[pltpu.VMEM((2 * past * B_KV, HEAD_DIM), F32), pltpu.VMEM((2 * past * B_KV, HEAD_DIM), F32),
                            pltpu.VMEM((2 * n_pages * IDX_DIM, PAGE), F32),
                            pltpu.VMEM((IDX_HEADS * nq, lp), F32), pltpu.VMEM((B_GROUP * nq, lp), F32),
                            pltpu.VMEM((nq, lp), I32), pltpu.VMEM((nq, lp), F32),
                            pltpu.SemaphoreType.DMA((3, 2))]),
        out_shape=jax.ShapeDtypeStruct((bd * nq, 1024), F32),
        compiler_params=_cparams("arbitrary"),
        name="dsa_sample",
    )(page_table, zs, pool_k, pool_v, pool_i, bs)


def _diff_sample_kernel(pt_ref, q_ref, kn_ref, vn_ref, pk_ref, pv_ref, bs_ref, lam_ref, hn_ref, o_ref,
                        kbuf, vbuf, sem, *, n_pages, cp):
    b = pl.program_id(0)
    nq = q_ref.shape[0]
    pieces = C_KV * 2
    page_rows = PAGE * pieces
    slot_rows = cp * page_rows
    ch = cp * PAGE
    n_ch = n_pages // cp
    total = pl.num_programs(0) * n_ch
    past = n_pages * PAGE
    rows = C_GROUP * nq

    def copies(idx):
        bb, c, slot = idx // n_ch, idx % n_ch, idx % DIFF_SLOTS
        out = []
        for i in range(cp):
            pg = pt_ref[bb, c * cp + i]
            dst = pl.ds(pl.multiple_of(slot * slot_rows + i * page_rows, page_rows), page_rows)
            out.append(pltpu.make_async_copy(_page_rows(pk_ref, page_rows)(pg), kbuf.at[dst], sem.at[0, slot]))
            out.append(pltpu.make_async_copy(_page_rows(pv_ref, page_rows)(pg), vbuf.at[dst], sem.at[1, slot]))
        return out

    @pl.when(b == 0)
    def _():
        for idx in range(DIFF_SLOTS - 1):
            for cpy in copies(idx):
                cpy.start()

    qs = [jnp.concatenate([q_ref[:, ((g * C_GROUP + r) * 2 + m) * HEAD_DIM:((g * C_GROUP + r) * 2 + m + 1) * HEAD_DIM]
                           for r in range(C_GROUP)], axis=0).astype(MXU_DT)
          for g in range(C_KV) for m in range(2)]

    def update(carry, k_fn, v_fn, c0, w, mask):
        m_all, l_all, acc_all = carry
        new_m, new_l, new_acc = [], [], []
        for g in range(C_KV):
            ps, alphas = [], []
            for m in range(2):
                gm = g * 2 + m
                rs = slice(gm * rows, (gm + 1) * rows)
                bias = jnp.concatenate([bs_ref[m * C_HEADS + g * C_GROUP + r, :, pl.ds(c0, w)]
                                        for r in range(C_GROUP)], axis=0)
                s = _nt(qs[gm], k_fn(g, m).astype(MXU_DT)) * SCALE + bias
                if mask is not None:
                    s = jnp.where(mask, s, NEG)
                mn = jnp.maximum(m_all[rs], jnp.max(s, axis=-1, keepdims=True))
                p = jnp.exp(s - mn)
                if mask is not None:
                    p = jnp.where(mask, p, 0.0)
                a = jnp.exp(m_all[rs] - mn)
                new_m.append(mn)
                new_l.append(a * l_all[rs] + jnp.sum(p, axis=-1, keepdims=True))
                ps.append(p)
                alphas.append(a)
            pst = jnp.concatenate(ps, axis=0).astype(MXU_DT)
            pv = jnp.concatenate([_mm(pst, v_fn(g, h).astype(MXU_DT)) for h in range(2)], axis=1)
            for m in range(2):
                rs = slice((g * 2 + m) * rows, (g * 2 + m + 1) * rows)
                new_acc.append(alphas[m] * acc_all[rs] + pv[m * rows:(m + 1) * rows])
        return (jnp.concatenate(new_m, axis=0), jnp.concatenate(new_l, axis=0),
                jnp.concatenate(new_acc, axis=0))

    def chunk(c, carry):
        idx = b * n_ch + c
        for cpy in copies(idx):
            cpy.wait()

        @pl.when(idx + DIFF_SLOTS - 1 < total)
        def _():
            for cpy in copies(idx + DIFF_SLOTS - 1):
                cpy.start()

        base = (idx % DIFF_SLOTS) * slot_rows
        return update(carry,
                      lambda g, m: kbuf[pl.ds(base + g * 2 + m, ch, stride=pieces), :],
                      lambda g, h: vbuf[pl.ds(base + h * C_KV + g, ch, stride=pieces), :],
                      pl.multiple_of(c * ch, ch), ch, None)

    n_rows = pieces * rows
    carry = (jnp.full((n_rows, 1), NEG, F32), jnp.zeros((n_rows, 1), F32), jnp.zeros((n_rows, C_VDIM), F32))
    carry = lax.fori_loop(0, n_ch, chunk, carry)
    _, l_all, acc_all = update(
        carry,
        lambda g, m: _pad_rows(kn_ref[:, (g * 2 + m) * HEAD_DIM:(g * 2 + m + 1) * HEAD_DIM], 128),
        lambda g, h: _pad_rows(vn_ref[:, g * C_VDIM + h * HEAD_DIM:g * C_VDIM + (h + 1) * HEAD_DIM], 128),
        past, 128, _new_key_mask(nq, C_GROUP))
    o_all = acc_all / l_all
    lam = _diff_lambda(lam_ref)
    for g in range(C_KV):
        r0 = g * 2 * rows
        o = _diff_finish(o_all[r0:r0 + rows] - lam * o_all[r0 + rows:r0 + 2 * rows], hn_ref)
        for r in range(C_GROUP):
            col = (g * C_GROUP + r) * C_VDIM
            o_ref[:, col:col + C_VDIM] = o[r * nq:(r + 1) * nq]


def diff_sample(z1s, pool_k, pool_v, bs, lam_vecs, head_norm, page_table, cp=8):
    bd, n_pages = page_table.shape
    nq = z1s.shape[0] // bd
    assert n_pages % cp == 0 and bd * (n_pages // cp) >= DIFF_SLOTS
    slot_rows = cp * PAGE * C_KV * 2
    buf = pltpu.VMEM((DIFF_SLOTS * slot_rows, HEAD_DIM), F32)
    q_cols = C_HEADS * 2 * HEAD_DIM
    kv_cols = C_KV * C_VDIM
    return pl.pallas_call(
        functools.partial(_diff_sample_kernel, n_pages=n_pages, cp=cp),
        grid_spec=pltpu.PrefetchScalarGridSpec(
            num_scalar_prefetch=1, grid=(bd,),
            in_specs=[pl.BlockSpec((nq, q_cols), lambda i, pt: (i, 0)),
                      pl.BlockSpec((nq, kv_cols), lambda i, pt: (i, q_cols // kv_cols)),
                      pl.BlockSpec((nq, kv_cols), lambda i, pt: (i, q_cols // kv_cols + 1)),
                      pl.BlockSpec(memory_space=pl.ANY), pl.BlockSpec(memory_space=pl.ANY),
                      pl.BlockSpec(bs.shape, lambda i, pt: (0, 0, 0)),
                      pl.BlockSpec((4, HEAD_DIM), lambda i, pt: (0, 0)),
                      pl.BlockSpec((1, C_VDIM), lambda i, pt: (0, 0))],
            out_specs=pl.BlockSpec((nq, C_HEADS * C_VDIM), lambda i, pt: (i, 0)),
            scratch_shapes=[buf, buf, pltpu.SemaphoreType.DMA((2, DIFF_SLOTS))]),
        out_shape=jax.ShapeDtypeStruct((bd * nq, C_HEADS * C_VDIM), F32),
        compiler_params=_cparams("arbitrary"),
        name="diff_sample",
    )(page_table, z1s, z1s, z1s, pool_k, pool_v, bs, lam_vecs, head_norm.reshape(1, C_VDIM))


def _row_tile(rows):
    tm = min(rows, ROW_TILE)
    assert rows % tm == 0
    return tm


def _reorder_l0_weight(w):
    sizes = (A_HEADS * HEAD_DIM,) + (A_KV * HEAD_DIM,) * 6 + (
        N_GATES * A_HEADS, B_HEADS * HEAD_DIM, B_KV * HEAD_DIM, B_KV * HEAD_DIM,
        IDX_HEADS * IDX_DIM, IDX_DIM, IDX_HEADS)
    offs = [0]
    for s in sizes:
        offs.append(offs[-1] + s)
    piece = lambda i, j=None: w[:, offs[i]:offs[(i if j is None else j) + 1]]
    qa, six, ga, qb, kvb, qi, ki, wi = piece(0), piece(1, 6), piece(7), piece(8), piece(9, 10), piece(11), \
        piece(12), piece(13)
    pad = jnp.zeros((w.shape[0], L0_COLS - offs[-1]), w.dtype)
    return jnp.concatenate([qa, qb, six, kvb, qi, ki, ga, wi, pad], axis=1).astype(MXU_DT)


def _compress_weights(pe, w1, w2):
    half = CMP_STRIDE * HEAD_DIM
    w1 = w1.reshape(2, half, CMP_HIDDEN).astype(MXU_DT)
    pe_rows = jnp.zeros((16, half), F32).at[0:2].set(pe.reshape(2, half))
    return w1[0], w1[1], w2.astype(MXU_DT), pe_rows


def kernel(x_prompt, x_sample, cache_l0_nsa_cmp_k, cache_l0_nsa_cmp_v, cache_l0_nsa_slc_k, cache_l0_nsa_slc_v, state_l0_nsa_win_k, state_l0_nsa_win_v, cache_l0_dsa_k, cache_l0_dsa_v, cache_l0_dsa_idx_k, cache_l1_diff_k, cache_l1_diff_v, page_table, rel_bias, attn_norm, mlp_norm, mlp_w1, mlp_w2, l0_w_in, l0_w_out, l0_cmp_pe_k, l0_cmp_w1_k, l0_cmp_w2_k, l0_cmp_pe_v, l0_cmp_w1_v, l0_cmp_w2_v, l1_w_in, l1_w_out, l1_lambda_q1, l1_lambda_k1, l1_lambda_q2, l1_lambda_k2, l1_head_norm, final_norm):
    n, t_len, d = x_prompt.shape
    bd, nq, _ = x_sample.shape
    n_pool = cache_l0_nsa_cmp_k.shape[0]
    n_pages = page_table.shape[1]
    past = n_pages * PAGE
    lp = past + 128
    kv_w = A_KV * HEAD_DIM
    assert t_len % CHUNK == 0 and t_len >= WINDOW + QB and nq <= 8
    assert state_l0_nsa_win_k.shape[1] == min(WINDOW, past)

    xp = x_prompt.reshape(n * t_len, d)
    xs = x_sample.reshape(bd * nq, d)
    tmp, tms = _row_tile(xp.shape[0]), _row_tile(xs.shape[0])
    w0 = _reorder_l0_weight(l0_w_in)
    cw = [jnp.stack(pair) for pair in zip(_compress_weights(l0_cmp_pe_k, l0_cmp_w1_k, l0_cmp_w2_k),
                                          _compress_weights(l0_cmp_pe_v, l0_cmp_w1_v, l0_cmp_w2_v))]
    lam_vecs = jnp.stack([l1_lambda_q1, l1_lambda_k1, l1_lambda_q2, l1_lambda_k2])
    bf = lambda a: a.astype(MXU_DT)

    tp, bs = bias_tiles(rel_bias, past, nq, lp)
    bc_p, bc_s = bias_cmp(rel_bias, t_len, t_len // CMP_STRIDE, past, nq, past // CMP_STRIDE)

    zp = norm_proj(xp, attn_norm[0], w0, tmp, L0_COL_TILE)
    zs = norm_proj(xs, attn_norm[0], w0, tms, L0_COL_TILE)
    cut = lambda z, c, w: z[:, c:c + w]
    p_rows = {name: cut(zp, c, kv_w) for name, c in
              (("kc", C_KC), ("vc", C_VC), ("ks", C_KS), ("vs", C_VS), ("kw", C_KW), ("vw", C_VW),
               ("kb", C_KB), ("vb", C_VB))}
    s_rows = {name: cut(zs, c, kv_w) for name, c in
              (("kc", C_KC), ("vc", C_VC), ("ks", C_KS), ("vs", C_VS), ("kw", C_KW), ("vw", C_VW),
               ("kb", C_KB), ("vb", C_VB))}
    chunk_w = CMP_STRIDE * kv_w
    kc_p, vc_p = compress_prompt(p_rows["kc"].reshape(n, t_len // CMP_STRIDE, chunk_w),
                                 p_rows["vc"].reshape(n, t_len // CMP_STRIDE, chunk_w), cw)
    lanes = lambda a: a.reshape(-1, HEAD_DIM)
    kv_cmp_s = compress_sample(lanes(cache_l0_nsa_cmp_k), lanes(cache_l0_nsa_cmp_v), page_table, cw)
    oa_p = nsa_prompt(zp, kc_p, vc_p, tp, bc_p, rel_bias, n, t_len)
    ob_p = dsa_prompt(zp, tp, rel_bias, n, t_len)
    wb = state_l0_nsa_win_k.shape[1]
    oa_s = nsa_sample(zs, kv_cmp_s, lanes(cache_l0_nsa_slc_k), lanes(cache_l0_nsa_slc_v),
                      lanes(state_l0_nsa_win_k), lanes(state_l0_nsa_win_v), bs, bc_s, page_table)
    ob_s = dsa_sample(zs, lanes(cache_l0_dsa_k), lanes(cache_l0_dsa_v),
                      jnp.swapaxes(cache_l0_dsa_idx_k, 1, 2).reshape(-1, PAGE), bs, page_table)
    w_out0 = bf(l0_w_out)
    w1_0, w2_0 = bf(mlp_w1[0]), bf(mlp_w2[0])
    xp = out_proj(xp, [oa_p, ob_p], w_out0, tmp, COL_TILE)
    xs = out_proj(xs, [oa_s, ob_s], w_out0, tms, COL_TILE)
    xp = mlp(xp, mlp_norm[0], w1_0, w2_0, final_norm, tmp, FF_TILE, False)
    xs = mlp(xs, mlp_norm[0], w1_0, w2_0, final_norm, tms, FF_TILE, False)

    w_in1 = bf(l1_w_in)
    z1p = norm_proj(xp, attn_norm[1], w_in1, tmp, COL_TILE)
    z1s = norm_proj(xs, attn_norm[1], w_in1, tms, COL_TILE)
    o1_p = diff_prompt(z1p, tp, rel_bias, lam_vecs, l1_head_norm, n, t_len)
    v_halves = cache_l1_diff_v.reshape(n_pool, PAGE, C_KV, 2, HEAD_DIM).transpose(0, 1, 3, 2, 4)
    o1_s = diff_sample(z1s, lanes(cache_l1_diff_k), lanes(v_halves), bs, lam_vecs, l1_head_norm, page_table)
    w_out1 = bf(l1_w_out)
    w1_1, w2_1 = bf(mlp_w1[1]), bf(mlp_w2[1])
    xp = out_proj(xp, [o1_p], w_out1, tmp, COL_TILE)
    xs = out_proj(xs, [o1_s], w_out1, tms, COL_TILE)
    y_prompt = mlp(xp, mlp_norm[1], w1_1, w2_1, final_norm, tmp, FF_TILE, True).reshape(n, t_len, d)
    y_sample = mlp(xs, mlp_norm[1], w1_1, w2_1, final_norm, tms, FF_TILE, True).reshape(bd, nq, d)

    row4 = lambda a, b: a.reshape(b, -1, A_KV, HEAD_DIM)
    win = min(WINDOW, t_len)
    outs = [y_prompt, y_sample]
    for name in ("kc", "vc", "ks", "vs"):
        outs += [row4(p_rows[name], n), row4(s_rows[name], bd)]
    for name, state in (("kw", state_l0_nsa_win_k), ("vw", state_l0_nsa_win_v)):
        outs += [row4(p_rows[name], n)[:, t_len - win:],
                 jnp.concatenate([state, row4(s_rows[name], bd)], axis=1)[:, -wb:]]
    for name in ("kb", "vb"):
        outs += [row4(p_rows[name], n), row4(s_rows[name], bd)]
    outs += [cut(zp, C_TAIL + T_KI, IDX_DIM).reshape(n, t_len, IDX_DIM),
             cut(zs, C_TAIL + T_KI, IDX_DIM).reshape(bd, nq, IDX_DIM)]
    k_cols, v_cols = C_KV * 2 * HEAD_DIM, C_KV * C_VDIM
    q_cols = C_HEADS * 2 * HEAD_DIM
    outs += [cut(z1p, q_cols, k_cols).reshape(n, t_len, C_KV, 2, HEAD_DIM),
             cut(z1s, q_cols, k_cols).reshape(bd, nq, C_KV, 2, HEAD_DIM),
             cut(z1p, q_cols + k_cols, v_cols).reshape(n, t_len, C_KV, C_VDIM),
             cut(z1s, q_cols + k_cols, v_cols).reshape(bd, nq, C_KV, C_VDIM)]
    return tuple(outs)
```

```python
import functools
import math

import jax
import jax.numpy as jnp
from jax import lax
from jax.experimental import pallas as pl
from jax.experimental.pallas import tpu as pltpu

F32 = jnp.float32
I32 = jnp.int32
MXU_DT = jnp.bfloat16

HEAD_DIM = 128
A_HEADS, A_KV, A_GROUP = 8, 2, 4
B_HEADS, B_KV, B_GROUP = 8, 2, 4
C_HEADS, C_KV, C_GROUP, C_VDIM = 8, 4, 2, 256
CMP_STRIDE, CMP_BLOCK, CMP_HIDDEN = 16, 32, 256
SEL_BLOCK, N_SEL_BLOCKS, WINDOW, N_GATES = 64, 16, 512, 3
IDX_HEADS, IDX_DIM, DSA_TOPK = 4, 64, 256
NUM_BUCKETS, MAX_DISTANCE = 32, 128
LAMBDA_INIT = 0.8 - 0.6 * math.exp(-0.3 * 1)
RMS_EPS = 1e-6
NEG = -1e30
SCALE = HEAD_DIM ** -0.5
QB = 128
CHUNK = 512
PAGE = 128
PAGE_PITCH = PAGE * A_KV + 8
DIFF_SLOTS = 3
assert QB >= MAX_DISTANCE and WINDOW % QB == 0 and WINDOW >= 2 * QB and 2 * SEL_BLOCK == QB

C_QA, C_QB, C_KC, C_VC, C_KS, C_VS, C_KW, C_VW, C_KB, C_VB, C_QI, C_TAIL = (
    0, 1024, 2048, 2304, 2560, 2816, 3072, 3328, 3584, 3840, 4096, 4352)
T_KI, T_GA, T_WI = 0, 64, 88
L0_COLS = 4608
VMEM_LIMIT = 56 * 1024 * 1024
ROW_TILE = 1024
L0_COL_TILE = 1536
COL_TILE = 1024
FF_TILE = 512


def _cparams(*sem):
    return pltpu.CompilerParams(dimension_semantics=sem, vmem_limit_bytes=VMEM_LIMIT)


def _nt(a, b):
    return lax.dot_general(a, b, (((1,), (1,)), ((), ())), preferred_element_type=F32)


def _mm(a, b):
    return jnp.dot(a, b, preferred_element_type=F32)


def _rms(x, g):
    return x * lax.rsqrt(jnp.mean(x * x, axis=-1, keepdims=True) + RMS_EPS) * g


def _norm_proj_kernel(x_ref, g_ref, w_ref, o_ref, xn_ref):
    @pl.when(pl.program_id(1) == 0)
    def _():
        xn_ref[...] = _rms(x_ref[...], g_ref[...]).astype(xn_ref.dtype)

    o_ref[...] = _mm(xn_ref[...], w_ref[...])


def norm_proj(x, gain, w, tm, tn):
    rows, d = x.shape
    n = w.shape[1]
    return pl.pallas_call(
        _norm_proj_kernel,
        grid=(rows // tm, n // tn),
        in_specs=[pl.BlockSpec((tm, d), lambda i, j: (i, 0)),
                  pl.BlockSpec((1, d), lambda i, j: (0, 0)),
                  pl.BlockSpec((d, tn), lambda i, j: (0, j))],
        out_specs=pl.BlockSpec((tm, tn), lambda i, j: (i, j)),
        out_shape=jax.ShapeDtypeStruct((rows, n), F32),
        scratch_shapes=[pltpu.VMEM((tm, d), MXU_DT)],
        compiler_params=_cparams("parallel", "arbitrary"),
        name="norm_proj",
    )(x, gain.reshape(1, d), w)


def _out_proj_kernel(*refs, n_in):
    x_ref, o_refs, w_refs, y_ref = refs[0], refs[1:1 + n_in], refs[1 + n_in:1 + 2 * n_in], refs[-1]
    acc = x_ref[...]
    for o_ref, w_ref in zip(o_refs, w_refs):
        acc = acc + _mm(o_ref[...].astype(MXU_DT), w_ref[...])
    y_ref[...] = acc


def out_proj(x, outs, w, tm, tn):
    rows, d = x.shape
    o_specs, w_specs, row0 = [], [], 0
    for o in outs:
        k = o.shape[1]
        o_specs.append(pl.BlockSpec((tm, k), lambda i, j: (i, 0)))
        w_specs.append(pl.BlockSpec((k, tn), lambda i, j, rb=row0 // k: (rb, j)))
        row0 += k
    return pl.pallas_call(
        functools.partial(_out_proj_kernel, n_in=len(outs)),
        grid=(rows // tm, d // tn),
        in_specs=[pl.BlockSpec((tm, tn), lambda i, j: (i, j))] + o_specs + w_specs,
        out_specs=pl.BlockSpec((tm, tn), lambda i, j: (i, j)),
        out_shape=jax.ShapeDtypeStruct((rows, d), F32),
        compiler_params=_cparams("parallel", "arbitrary"),
        name="out_proj",
    )(x, *outs, *([w] * len(outs)))


def _mlp_kernel(x_ref, g_ref, w1_ref, w2_ref, gf_ref, y_ref, xn_ref, *, final_norm):
    j = pl.program_id(1)

    @pl.when(j == 0)
    def _():
        x = x_ref[...]
        xn_ref[...] = _rms(x, g_ref[...]).astype(xn_ref.dtype)
        y_ref[...] = x

    h = jnp.square(jnp.maximum(_mm(xn_ref[...], w1_ref[...]), 0.0))
    y_ref[...] += _mm(h.astype(w2_ref.dtype), w2_ref[...])

    if final_norm:
        @pl.when(j == pl.num_programs(1) - 1)
        def _():
            y_ref[...] = _rms(y_ref[...], gf_ref[...])


def mlp(x, gain, w1, w2, final_gain, tm, tf, final_norm):
    rows, d = x.shape
    ff = w1.shape[1]
    return pl.pallas_call(
        functools.partial(_mlp_kernel, final_norm=final_norm),
        grid=(rows // tm, ff // tf),
        in_specs=[pl.BlockSpec((tm, d), lambda i, j: (i, 0)),
                  pl.BlockSpec((1, d), lambda i, j: (0, 0)),
                  pl.BlockSpec((d, tf), lambda i, j: (0, j)),
                  pl.BlockSpec((tf, d), lambda i, j: (j, 0)),
                  pl.BlockSpec((1, d), lambda i, j: (0, 0))],
        out_specs=pl.BlockSpec((tm, d), lambda i, j: (i, 0)),
        out_shape=jax.ShapeDtypeStruct((rows, d), F32),
        scratch_shapes=[pltpu.VMEM((tm, d), MXU_DT)],
        compiler_params=_cparams("parallel", "arbitrary"),
        name="mlp",
    )(x, gain.reshape(1, d), w1, w2, final_gain.reshape(1, d))


def _bucket(dist):
    n = jnp.maximum(dist, 0)
    max_exact = NUM_BUCKETS // 2
    nf = jnp.maximum(n, 1).astype(F32)
    large = max_exact + (jnp.log(nf / max_exact) / math.log(MAX_DISTANCE / max_exact)
                         * (NUM_BUCKETS - max_exact)).astype(I32)
    large = jnp.minimum(large, NUM_BUCKETS - 1)
    return jnp.where(n < max_exact, n, large)


def _lookup(tbl_ref, col, bucket):
    acc = jnp.zeros(bucket.shape, F32)
    for b in range(NUM_BUCKETS):
        acc = jnp.where(bucket == b, tbl_ref[b, col], acc)
    return acc


def _bias_tiles_kernel(tbl_ref, tp_ref, bs_ref, *, q0):
    h = pl.program_id(0)
    t = lax.broadcasted_iota(I32, (QB, QB), 0)
    k = lax.broadcasted_iota(I32, (QB, QB), 1)
    ts = lax.broadcasted_iota(I32, bs_ref.shape[1:], 0)
    ks = lax.broadcasted_iota(I32, bs_ref.shape[1:], 1)
    tp_ref[0, 0] = _lookup(tbl_ref, h, _bucket(t - k))
    tp_ref[0, 1] = _lookup(tbl_ref, h, _bucket(QB + t - k))
    bs_ref[0] = _lookup(tbl_ref, h, _bucket(q0 + ts - ks))


def bias_tiles(rel_bias, q0, n_q, lp):
    nh = rel_bias.shape[1]
    return pl.pallas_call(
        functools.partial(_bias_tiles_kernel, q0=q0),
        grid=(nh,),
        in_specs=[pl.BlockSpec(memory_space=pltpu.SMEM)],
        out_specs=[pl.BlockSpec((1, 2, QB, QB), lambda h: (h, 0, 0, 0)),
                   pl.BlockSpec((1, n_q, lp), lambda h: (h, 0, 0))],
        out_shape=[jax.ShapeDtypeStruct((nh, 2, QB, QB), F32),
                   jax.ShapeDtypeStruct((nh, n_q, lp), F32)],
        compiler_params=_cparams("arbitrary"),
        name="bias_tiles",
    )(rel_bias)


def _bias_cmp_kernel(tbl_ref, bp_ref, bs_ref, *, q0):
    h = pl.program_id(0)
    t_len, mc_p = bp_ref.shape[1:]
    end = CMP_BLOCK - 1

    def rows(rb, _):
        r0 = pl.multiple_of(rb * QB, QB)
        tp = r0 + lax.broadcasted_iota(I32, (QB, mc_p), 0)
        cp = lax.broadcasted_iota(I32, (QB, mc_p), 1)
        bp_ref[0, pl.ds(r0, QB), :] = _lookup(tbl_ref, h, _bucket(tp - (cp * CMP_STRIDE + end)))
        return 0

    lax.fori_loop(0, t_len // QB, rows, 0)
    ts = lax.broadcasted_iota(I32, bs_ref.shape[1:], 0)
    cs = lax.broadcasted_iota(I32, bs_ref.shape[1:], 1)
    bs_ref[0] = _lookup(tbl_ref, h, _bucket(q0 + ts - (cs * CMP_STRIDE + end)))


def bias_cmp(rel_bias, t_len, mc_p, q0, n_q, mc_s):
    return pl.pallas_call(
        functools.partial(_bias_cmp_kernel, q0=q0),
        grid=(A_HEADS,),
        in_specs=[pl.BlockSpec(memory_space=pltpu.SMEM)],
        out_specs=[pl.BlockSpec((1, t_len, mc_p), lambda h: (h, 0, 0)),
                   pl.BlockSpec((1, n_q, mc_s), lambda h: (h, 0, 0))],
        out_shape=[jax.ShapeDtypeStruct((A_HEADS, t_len, mc_p), F32),
                   jax.ShapeDtypeStruct((A_HEADS, n_q, mc_s), F32)],
        compiler_params=_cparams("arbitrary"),
        name="bias_cmp",
    )(rel_bias)


def _row_reduce(fn, lane_fn, x):
    tiles = [x[:, i:i + 128] for i in range(0, x.shape[1], 128)]
    return lane_fn(_tree_reduce(fn, tiles), axis=-1, keepdims=True)


def _softmax_rows(z, mask):
    z = jnp.where(mask, z, NEG)
    e = jnp.where(mask, jnp.exp(z - _row_reduce(jnp.maximum, jnp.max, z)), 0.0)
    l = _row_reduce(jnp.add, jnp.sum, e)
    return e * (1.0 / jnp.where(l > 0.0, l, 1.0))


def _gelu_tanh(x):
    return 0.5 * x * (1.0 + jnp.tanh(math.sqrt(2.0 / math.pi) * (x + 0.044715 * (x * x * x))))


def _compress(x_fn, m, w1a_ref, w1b_ref, w2_ref, pe_ref, next_fn, out_fn):
    pe = pe_ref[...].astype(MXU_DT)
    pos = _mm(pe, w1a_ref[...])[0:1] + _mm(pe, w1b_ref[...])[1:2]
    last = lax.broadcasted_iota(I32, (m, 1), 0) == m - 1
    for g in range(A_KV):
        xg = jnp.concatenate([x_fn(j, g).astype(MXU_DT) for j in range(CMP_STRIDE)], axis=1)
        hid = _gelu_tanh(_mm(xg, w1a_ref[...]) + next_fn(_mm(xg, w1b_ref[...])) + pos)
        out_fn(g, jnp.where(last, 0.0, _mm(hid.astype(MXU_DT), w2_ref[...])))


def _overlap(mc, jn, n_cmp, n_slc):
    c = lax.broadcasted_iota(I32, (mc, jn), 0)
    j = lax.broadcasted_iota(I32, (mc, jn), 1)
    ov = ((c * CMP_STRIDE < j * SEL_BLOCK + SEL_BLOCK) & (c * CMP_STRIDE + CMP_BLOCK > j * SEL_BLOCK)
          & (c < n_cmp) & (j < n_slc))
    return jnp.where(ov, 1.0, 0.0)


def _select_blocks(imp, pos, n_slc):
    jn = imp.shape[1]
    jidx = lax.broadcasted_iota(I32, (1, jn), 1)
    cur = pos // SEL_BLOCK
    forced = (jidx == 0) | (jidx == cur) | (jidx == cur - 1)
    future = jidx * SEL_BLOCK > pos
    score = jnp.where(future, -1.0, jnp.where(forced, 1e3, imp))
    score = jnp.where(jidx < n_slc, score, -2.0)

    def body(i, rank):
        col = jnp.sum(jnp.where(jidx == i, score, 0.0), axis=-1, keepdims=True)
        beats = jnp.where(col > score, 1.0, jnp.where(col == score, jnp.where(i < jidx, 1.0, 0.0), 0.0))
        return rank + beats

    rank = lax.fori_loop(0, n_slc, body, jnp.zeros(score.shape, F32), unroll=32)
    n_sel = min(N_SEL_BLOCKS, n_slc)
    return jnp.where((rank < n_sel) & (jidx < n_slc), 1.0, 0.0)


def _tree_reduce(fn, xs):
    xs = list(xs)
    while len(xs) > 1:
        xs = [fn(xs[i], xs[i + 1]) for i in range(0, len(xs) - 1, 2)] + ([xs[-1]] if len(xs) % 2 else [])
    return xs[0]


def _tree_sum(xs):
    return _tree_reduce(jnp.add, xs)


def _sort_key(s):
    bits = lax.bitcast_convert_type(jnp.where(s == 0.0, 0.0, s), I32)
    return jnp.where(bits < 0, bits ^ jnp.int32(0x7FFFFFFF), bits)


def _topk_madd(key_ref, madd_ref, valid_fn, nch, cw, k, nbits):
    n_rows = key_ref.shape[0]
    kf = jnp.float32(k)

    def count(fn):
        def body(c, acc):
            c0 = pl.multiple_of(c * cw, cw)
            hit = jnp.where(fn(c0, key_ref[:, pl.ds(c0, cw)]), 1.0, 0.0)
            return acc + _tree_sum(hit[:, i:i + 128] for i in range(0, cw, 128))
        acc = lax.fori_loop(0, nch, body, jnp.zeros((n_rows, 128), F32))
        return jnp.sum(acc, axis=-1, keepdims=True)

    int_min = jnp.int32(-2 ** 31)
    thr0 = jnp.where(count(lambda c0, key: key >= 0) >= kf, jnp.int32(0), int_min)

    def enough(cand):
        return count(lambda c0, key: key >= cand) >= kf

    def vbody(i, thr):
        hi = lax.shift_left(jnp.int32(1), 30 - 2 * i)
        lo = lax.shift_left(jnp.int32(1), 29 - 2 * i)
        return jnp.where(enough(thr | hi | lo), thr | hi | lo,
                         jnp.where(enough(thr | hi), thr | hi, jnp.where(enough(thr | lo), thr | lo, thr)))

    thr = lax.fori_loop(0, 15, vbody, thr0)
    thr = jnp.where(enough(thr | 1), thr | 1, thr)
    need = kf - count(lambda c0, key: key > thr)

    def idx(c0):
        return c0 + lax.broadcasted_iota(I32, (1, cw), 1)

    def ibody(i, cut):
        cand = cut | lax.shift_left(jnp.int32(1), nbits - 1 - i)
        return jnp.where(count(lambda c0, key: (key == thr) & (idx(c0) < cand)) <= need, cand, cut)

    tied = jnp.max(count(lambda c0, key: key == thr) - need) > 0.0
    cut = lax.cond(tied, lambda: lax.fori_loop(0, nbits, ibody, jnp.zeros((n_rows, 1), I32)),
                   lambda: jnp.full((n_rows, 1), 2 ** nbits - 1, I32))

    def write(c, _):
        c0 = pl.multiple_of(c * cw, cw)
        key = key_ref[:, pl.ds(c0, cw)]
        sel = ((key > thr) | ((key == thr) & (idx(c0) < cut))) & valid_fn(c0)
        madd_ref[:, pl.ds(c0, cw)] = jnp.where(sel, 0.0, NEG)
        return 0

    lax.fori_loop(0, nch, write, 0)


def _topk_madd_t(key_ref, madd_ref, valid_fn, nch, cw, k, nbits):
    n_rows = key_ref.shape[1]
    kf = jnp.float32(k)

    def count(fn):
        def body(c, acc):
            c0 = pl.multiple_of(c * cw, cw)
            hit = jnp.where(fn(c0, key_ref[pl.ds(c0, cw), :]), 1.0, 0.0)
            return acc + _tree_sum(hit[i:i + 8] for i in range(0, cw, 8))
        acc = lax.fori_loop(0, nch, body, jnp.zeros((8, n_rows), F32))
        return jnp.sum(acc, axis=0, keepdims=True)

    int_min = jnp.int32(-2 ** 31)
    thr0 = jnp.where(count(lambda c0, key: key >= 0) >= kf, jnp.int32(0), int_min)

    def vbody(i, thr):
        cand = thr | lax.shift_left(jnp.int32(1), 30 - i)
        return jnp.where(count(lambda c0, key: key >= cand) >= kf, cand, thr)

    thr = lax.fori_loop(0, 31, vbody, thr0)
    need = kf - count(lambda c0, key: key > thr)

    def idx(c0):
        return c0 + lax.broadcasted_iota(I32, (cw, 1), 0)

    def ibody(i, cut):
        cand = cut | lax.shift_left(jnp.int32(1), nbits - 1 - i)
        return jnp.where(count(lambda c0, key: (key == thr) & (idx(c0) < cand)) <= need, cand, cut)

    tied = jnp.max(count(lambda c0, key: key == thr) - need) > 0.0
    cut = lax.cond(tied, lambda: lax.fori_loop(0, nbits, ibody, jnp.zeros((1, n_rows), I32)),
                   lambda: jnp.full((1, n_rows), 2 ** nbits - 1, I32))

    def write(c, _):
        c0 = pl.multiple_of(c * cw, cw)
        key = key_ref[pl.ds(c0, cw), :]
        sel = ((key > thr) | ((key == thr) & (idx(c0) < cut))) & valid_fn(c0)
        madd_ref[:, pl.ds(c0, cw)] = jnp.where(sel, 0.0, NEG).T
        return 0

    lax.fori_loop(0, nch, write, 0)


def _causal_attn(streams, dv, qb, pairs=False):
    m_rows = streams[0][0].shape[0]
    per = CHUNK // QB
    nact = qb // per + 1

    def stage(st, k0, w, bias):
        q, k_fn, _, _, _, madd_fn, s_ref = st
        s = _nt(q, k_fn(k0, w)) * SCALE + bias
        madd = madd_fn(k0, w)
        s_ref[:, pl.ds(k0, w)] = s if madd is None else s + madd

    def chunk_loop(body, init):
        n2 = nact // 2 if pairs else 0
        carry = lax.fori_loop(
            0, n2, lambda i, c: body(pl.multiple_of(i * 2 * CHUNK, 2 * CHUNK), 2 * CHUNK, c), init) if pairs else init
        return lax.fori_loop(
            0, nact - 2 * n2, lambda i, c: body(pl.multiple_of((2 * n2 + i) * CHUNK, CHUNK), CHUNK, c), carry)

    def lane_tiles(x):
        return [x[:, i:i + 128] for i in range(0, x.shape[1], 128)]

    def far(k0, w, _):
        for st in streams:
            stage(st, k0, w, st[3])
        return 0

    chunk_loop(far, 0)
    for st in streams:
        stage(st, pl.multiple_of(qb * QB, QB), QB, st[4](True))

    @pl.when(qb >= 1)
    def _():
        for st in streams:
            stage(st, pl.multiple_of((qb - 1) * QB, QB), QB, st[4](False))

    for j in range(1, per):
        @pl.when(qb % per + j < per)
        def _():
            for st in streams:
                st[6][:, pl.ds(pl.multiple_of((qb + j) * QB, QB), QB)] = jnp.full((m_rows, QB), NEG, F32)

    stat_w = 128 if pairs else 1

    def lanes_or_row(fn, lane_fn, x):
        return _tree_reduce(fn, lane_tiles(x)) if pairs else lane_fn(x, axis=-1, keepdims=True)

    def row_max(k0, w, ms):
        return tuple(jnp.maximum(m, lanes_or_row(jnp.maximum, jnp.max, st[6][:, pl.ds(k0, w)]))
                     for st, m in zip(streams, ms))

    ms = chunk_loop(row_max, tuple(jnp.full((m_rows, stat_w), NEG, F32) for _ in streams))
    ms = [jnp.max(m, axis=-1, keepdims=True) for m in ms]

    def pv(k0, w, carry):
        out = []
        for st, m, (l, acc) in zip(streams, ms, carry):
            p = jnp.exp(st[6][:, pl.ds(k0, w)] - m)
            out.append((l + lanes_or_row(jnp.add, jnp.sum, p), acc + _mm(p.astype(MXU_DT), st[2](k0, w))))
        return tuple(out)

    init = tuple((jnp.zeros((m_rows, stat_w), F32), jnp.zeros((m_rows, dv), F32)) for _ in streams)
    return [acc / jnp.sum(l, axis=-1, keepdims=True) for l, acc in chunk_loop(pv, init)]


def _causal_add(rep):
    t = lax.broadcasted_iota(I32, (QB, QB), 0)
    k = lax.broadcasted_iota(I32, (QB, QB), 1)
    return jnp.concatenate([jnp.where(k <= t, 0.0, NEG)] * rep, axis=0)


def _far_bias(tbl_ref, cols):
    return jnp.concatenate([jnp.full((QB, 1), tbl_ref[NUM_BUCKETS - 1, c], F32) for c in cols], axis=0)


def _near_bias(tp_ref, cols, diag):
    return jnp.concatenate([tp_ref[c, 0 if diag else 1] for c in cols], axis=0)


def _stack_heads(ref, col0, n):
    return jnp.concatenate([ref[:, col0 + r * HEAD_DIM:col0 + (r + 1) * HEAD_DIM] for r in range(n)], axis=0)


def _compress_prompt_kernel(xk_ref, xv_ref, w1a, w1b, w2, pe, ok_ref, ov_ref):
    lanes = lambda x_ref: (lambda j, g: x_ref[0, :, (2 * j + g) * HEAD_DIM:(2 * j + g + 1) * HEAD_DIM])
    m = xk_ref.shape[1]
    for i, (x_ref, o_ref) in enumerate(((xk_ref, ok_ref), (xv_ref, ov_ref))):
        def store(g, tokens, o_ref=o_ref):
            o_ref[0, :, g * HEAD_DIM:(g + 1) * HEAD_DIM] = tokens

        _compress(lanes(x_ref), m, w1a.at[i], w1b.at[i], w2.at[i], pe.at[i],
                  lambda y: pltpu.roll(y, m - 1, 0), store)


def _cmp_weight_specs():
    full = lambda *shape: pl.BlockSpec(shape, lambda *_: (0,) * len(shape))
    half = CMP_STRIDE * HEAD_DIM
    return [full(2, half, CMP_HIDDEN), full(2, half, CMP_HIDDEN), full(2, CMP_HIDDEN, HEAD_DIM), full(2, 16, half)]


def compress_prompt(xk, xv, cw):
    n, m, w = xk.shape
    spec = pl.BlockSpec((1, m, w), lambda i: (i, 0, 0))
    ospec = pl.BlockSpec((1, m, A_KV * HEAD_DIM), lambda i: (i, 0, 0))
    osh = jax.ShapeDtypeStruct((n, m, A_KV * HEAD_DIM), F32)
    return pl.pallas_call(
        _compress_prompt_kernel,
        grid=(n,),
        in_specs=[spec, spec] + _cmp_weight_specs(),
        out_specs=[ospec, ospec],
        out_shape=[osh, osh],
        compiler_params=_cparams("parallel"),
        name="compress_prompt",
    )(xk, xv, *cw)


def _window_attn(q, kw_ref, vw_ref, gl, qb, pos4, tbl_ref, tp_ref, cols):
    n_tiles = WINDOW // QB + 1
    width = n_tiles * QB
    lo = jnp.maximum(qb - (n_tiles - 1), 0)
    w0 = pl.multiple_of(lo * QB, QB)
    tiles = []
    for j in range(n_tiles):
        rel = qb - (lo + j)
        tiles.append(jnp.concatenate(
            [jnp.where(rel == 0, tp_ref[c, 0], jnp.where(rel == 1, tp_ref[c, 1], tbl_ref[NUM_BUCKETS - 1, c]))
             for c in cols], axis=0))
    s = _nt(q, kw_ref[pl.ds(w0, width), gl].astype(MXU_DT)) * SCALE + jnp.concatenate(tiles, axis=1)
    dist = pos4 - (w0 + lax.broadcasted_iota(I32, (1, width), 1))
    p = _softmax_rows(s, (dist >= 0) & (dist < WINDOW))
    return _mm(p.astype(MXU_DT), vw_ref[pl.ds(w0, width), gl].astype(MXU_DT))


def _nsa_prompt_kernel(tbl_ref, q_ref, tail_ref, kc_ref, vc_ref, ks_ref, vs_ref, kw_ref, vw_ref,
                       tp_ref, bc_ref, o_ref, s_ref, madd_ref, *, t_len):
    qb = pl.program_id(1)
    mc = kc_ref.shape[1]
    n_cmp = (t_len - CMP_BLOCK) // CMP_STRIDE + 1
    n_slc = -(-t_len // SEL_BLOCK)
    pos = qb * QB + lax.broadcasted_iota(I32, (QB, 1), 0)
    pos4 = jnp.concatenate([pos] * A_GROUP, axis=0)
    cidx = lax.broadcasted_iota(I32, (1, mc), 1)
    gates = jax.nn.sigmoid(tail_ref[...])
    overlap = _overlap(mc, QB, n_cmp, n_slc)
    causal_add = _causal_add(A_GROUP)
    onehot = jnp.where(lax.broadcasted_iota(I32, (QB, t_len), 0)
                       == lax.broadcasted_iota(I32, (QB, t_len), 1) // SEL_BLOCK, 1.0, 0.0).astype(MXU_DT)

    o_cmp, o_win, streams = [], [], []
    for g in range(A_KV):
        cols = [g * A_GROUP + r for r in range(A_GROUP)]
        gl = slice(g * HEAD_DIM, (g + 1) * HEAD_DIM)
        q = _stack_heads(q_ref, g * A_GROUP * HEAD_DIM, A_GROUP).astype(MXU_DT)
        lc = (_nt(q, kc_ref[0, :, gl].astype(MXU_DT)) * SCALE
              + jnp.concatenate([bc_ref[c] for c in cols], axis=0))
        p_cmp = _softmax_rows(lc, (pos4 >= cidx * CMP_STRIDE + (CMP_BLOCK - 1)) & (cidx < n_cmp))
        o_cmp.append(_mm(p_cmp.astype(MXU_DT), vc_ref[0, :, gl].astype(MXU_DT)))
        p_sum = sum(p_cmp[r * QB:(r + 1) * QB] for r in range(A_GROUP))
        imp = jnp.dot(p_sum, overlap, preferred_element_type=F32, precision=lax.Precision.HIGHEST)
        sel = _select_blocks(imp, pos, n_slc).astype(MXU_DT)
        madd_ref[g] = jnp.where(_mm(sel, onehot) > 0.5, 0.0, NEG)
        o_win.append(_window_attn(q, kw_ref, vw_ref, gl, qb, pos4, tbl_ref, tp_ref, cols))
        streams.append((
            q, lambda k0, w, gl=gl: ks_ref[pl.ds(k0, w), gl].astype(MXU_DT),
            lambda k0, w, gl=gl: vs_ref[pl.ds(k0, w), gl].astype(MXU_DT), _far_bias(tbl_ref, cols),
            lambda diag, cols=cols: _near_bias(tp_ref, cols, diag) + (causal_add if diag else 0.0),
            lambda k0, w, g=g: jnp.concatenate([madd_ref[g, :, pl.ds(k0, w)]] * A_GROUP, axis=0),
            s_ref.at[g]))
    o_slc = _causal_attn(streams, HEAD_DIM, qb)

    for h in range(A_HEADS):
        g, r = divmod(h, A_GROUP)
        c = T_GA + h * N_GATES
        rows = slice(r * QB, (r + 1) * QB)
        o = (gates[:, c:c + 1] * o_cmp[g][rows] + gates[:, c + 1:c + 2] * o_slc[g][rows]
             + gates[:, c + 2:c + 3] * o_win[g][rows])
        o_ref[:, h * HEAD_DIM:(h + 1) * HEAD_DIM] = o.astype(o_ref.dtype)


def nsa_prompt(z, k_cmp, v_cmp, tp, bc, rel_bias, n, t_len):
    nb = t_len // QB
    mc = k_cmp.shape[1]
    kv = lambda c: pl.BlockSpec((t_len, 256), lambda i, j: (i, c // 256))
    cmp_spec = pl.BlockSpec((1, mc, 256), lambda i, j: (i, 0, 0))
    return pl.pallas_call(
        functools.partial(_nsa_prompt_kernel, t_len=t_len),
        grid=(n, nb),
        in_specs=[pl.BlockSpec(memory_space=pltpu.SMEM),
                  pl.BlockSpec((QB, 1024), lambda i, j: (i * nb + j, C_QA // 1024)),
                  pl.BlockSpec((QB, 128), lambda i, j: (i * nb + j, C_TAIL // 128)),
                  cmp_spec, cmp_spec, kv(C_KS), kv(C_VS), kv(C_KW), kv(C_VW),
                  pl.BlockSpec(tp.shape, lambda i, j: (0, 0, 0, 0)),
                  pl.BlockSpec((A_HEADS, QB, mc), lambda i, j: (0, j, 0))],
        out_specs=pl.BlockSpec((QB, 1024), lambda i, j: (i * nb + j, 0)),
        out_shape=jax.ShapeDtypeStruct((n * t_len, 1024), MXU_DT),
        scratch_shapes=[pltpu.VMEM((A_KV, A_GROUP * QB, t_len), F32), pltpu.VMEM((A_KV, QB, t_len), F32)],
        compiler_params=_cparams("parallel", "arbitrary"),
        name="nsa_prompt",
    )(rel_bias, z, z, k_cmp, v_cmp, z, z, z, z, tp, bc)


def _dsa_prompt_kernel(tbl_ref, q_ref, qi_ref, tailq_ref, tailk_ref, kb_ref, vb_ref, tp_ref, o_ref,
                       key_ref, madd_ref, s_ref, *, topk, nbits):
    qb = pl.program_id(1)
    nact = qb // (CHUNK // QB) + 1
    pos = qb * QB + lax.broadcasted_iota(I32, (1, QB), 1)
    wi_t = tailq_ref[...].T[T_WI:T_WI + IDX_HEADS]
    qis = [qi_ref[:, h * IDX_DIM:(h + 1) * IDX_DIM].astype(MXU_DT) for h in range(IDX_HEADS)]

    def causal(c0):
        return c0 + lax.broadcasted_iota(I32, (CHUNK, 1), 0) <= pos

    def index_chunk(c, _):
        c0 = pl.multiple_of(c * CHUNK, CHUNK)
        ki = tailk_ref[pl.ds(c0, CHUNK), T_KI:T_KI + IDX_DIM].astype(MXU_DT)
        score = sum(jnp.maximum(_nt(ki, qis[h]), 0.0) * wi_t[h:h + 1] for h in range(IDX_HEADS))
        score = score * (IDX_DIM ** -0.5 * IDX_HEADS ** -0.5)
        key_ref[pl.ds(c0, CHUNK), :] = _sort_key(jnp.where(causal(c0), score, NEG))
        return 0

    lax.fori_loop(0, nact, index_chunk, 0)
    _topk_madd_t(key_ref, madd_ref, causal, nact, CHUNK, topk, nbits)

    streams = []
    for g in range(B_KV):
        cols = [A_HEADS + g * B_GROUP + r for r in range(B_GROUP)]
        gl = slice(g * HEAD_DIM, (g + 1) * HEAD_DIM)
        streams.append((
            _stack_heads(q_ref, g * B_GROUP * HEAD_DIM, B_GROUP).astype(MXU_DT),
            lambda k0, w, gl=gl: kb_ref[pl.ds(k0, w), gl].astype(MXU_DT),
            lambda k0, w, gl=gl: vb_ref[pl.ds(k0, w), gl].astype(MXU_DT),
            _far_bias(tbl_ref, cols), lambda diag, cols=cols: _near_bias(tp_ref, cols, diag),
            lambda k0, w: jnp.concatenate([madd_ref[:, pl.ds(k0, w)]] * B_GROUP, axis=0), s_ref.at[g]))
    outs = _causal_attn(streams, HEAD_DIM, qb)
    for h in range(B_HEADS):
        g, r = divmod(h, B_GROUP)
        o_ref[:, h * HEAD_DIM:(h + 1) * HEAD_DIM] = outs[g][r * QB:(r + 1) * QB].astype(o_ref.dtype)


def dsa_prompt(z, tp, rel_bias, n, t_len):
    nb = t_len // QB
    topk = min(DSA_TOPK, t_len // 4)
    nbits = int(t_len).bit_length()
    return pl.pallas_call(
        functools.partial(_dsa_prompt_kernel, topk=topk, nbits=nbits),
        grid=(n, nb),
        in_specs=[pl.BlockSpec(memory_space=pltpu.SMEM),
                  pl.BlockSpec((QB, 1024), lambda i, j: (i * nb + j, C_QB // 1024)),
                  pl.BlockSpec((QB, 256), lambda i, j: (i * nb + j, C_QI // 256)),
                  pl.BlockSpec((QB, 128), lambda i, j: (i * nb + j, C_TAIL // 128)),
                  pl.BlockSpec((t_len, 128), lambda i, j: (i, C_TAIL // 128)),
                  pl.BlockSpec((t_len, 256), lambda i, j: (i, C_KB // 256)),
                  pl.BlockSpec((t_len, 256), lambda i, j: (i, C_VB // 256)),
                  pl.BlockSpec(tp.shape, lambda i, j: (0, 0, 0, 0))],
        out_specs=pl.BlockSpec((QB, 1024), lambda i, j: (i * nb + j, 0)),
        out_shape=jax.ShapeDtypeStruct((n * t_len, 1024), MXU_DT),
        scratch_shapes=[pltpu.VMEM((t_len, QB), I32), pltpu.VMEM((QB, t_len), F32),
                        pltpu.VMEM((B_KV, B_GROUP * QB, t_len), F32)],
        compiler_params=_cparams("parallel", "arbitrary"),
        name="dsa_prompt",
    )(rel_bias, z, z, z, z, z, z, tp)


def _diff_lambda(lam_ref):
    v = lam_ref[...]
    e1 = jnp.exp(jnp.sum(v[0:1] * v[1:2], axis=-1, keepdims=True))
    e2 = jnp.exp(jnp.sum(v[2:3] * v[3:4], axis=-1, keepdims=True))
    return e1 - e2 + LAMBDA_INIT


def _diff_finish(o, hn_ref):
    return _rms(o, hn_ref[...]) * (1.0 - LAMBDA_INIT)


def _diff_prompt_kernel(tbl_ref, q_ref, k_ref, v_ref, tp_ref, lam_ref, hn_ref, o_ref, s_ref, *, gps):
    g0 = pl.program_id(1) * gps
    qb = pl.program_id(2)
    causal_add = _causal_add(C_GROUP)
    streams = []
    for gi in range(gps):
        for m in range(2):
            cols = [m * C_HEADS + (g0 + gi) * C_GROUP + r for r in range(C_GROUP)]
            q = jnp.concatenate(
                [q_ref[:, ((gi * C_GROUP + r) * 2 + m) * HEAD_DIM:((gi * C_GROUP + r) * 2 + m + 1) * HEAD_DIM]
                 for r in range(C_GROUP)], axis=0).astype(MXU_DT)
            kl = slice((gi * 2 + m) * HEAD_DIM, (gi * 2 + m + 1) * HEAD_DIM)
            vl = slice(gi * C_VDIM, (gi + 1) * C_VDIM)
            streams.append((
                q, lambda k0, w, kl=kl: k_ref[pl.ds(k0, w), kl].astype(MXU_DT),
                lambda k0, w, vl=vl: v_ref[pl.ds(k0, w), vl].astype(MXU_DT), _far_bias(tbl_ref, cols),
                lambda diag, cols=cols: _near_bias(tp_ref, cols, diag) + (causal_add if diag else 0.0),
                lambda k0, w: None, s_ref.at[gi * 2 + m]))
    outs = _causal_attn(streams, C_VDIM, qb, pairs=True)
    lam = _diff_lambda(lam_ref)
    for gi in range(gps):
        o = _diff_finish(outs[gi * 2] - lam * outs[gi * 2 + 1], hn_ref)
        for r in range(C_GROUP):
            col = (gi * C_GROUP + r) * C_VDIM
            o_ref[:, col:col + C_VDIM] = o[r * QB:(r + 1) * QB].astype(o_ref.dtype)


def diff_prompt(z1, tp, rel_bias, lam_vecs, head_norm, n, t_len, gps=2):
    nb = t_len // QB
    qw = gps * C_GROUP * 2 * HEAD_DIM
    kw = gps * 2 * HEAD_DIM
    q_cols, k_cols = C_HEADS * 2 * HEAD_DIM, C_KV * 2 * HEAD_DIM
    return pl.pallas_call(
        functools.partial(_diff_prompt_kernel, gps=gps),
        grid=(n, C_KV // gps, nb),
        in_specs=[pl.BlockSpec(memory_space=pltpu.SMEM),
                  pl.BlockSpec((QB, qw), lambda i, g, j: (i * nb + j, g)),
                  pl.BlockSpec((t_len, kw), lambda i, g, j: (i, q_cols // kw + g)),
                  pl.BlockSpec((t_len, kw), lambda i, g, j: (i, (q_cols + k_cols) // kw + g)),
                  pl.BlockSpec(tp.shape, lambda i, g, j: (0, 0, 0, 0)),
                  pl.BlockSpec((4, HEAD_DIM), lambda i, g, j: (0, 0)),
                  pl.BlockSpec((1, C_VDIM), lambda i, g, j: (0, 0))],
        out_specs=pl.BlockSpec((QB, qw), lambda i, g, j: (i * nb + j, g)),
        out_shape=jax.ShapeDtypeStruct((n * t_len, C_HEADS * C_VDIM), MXU_DT),
        scratch_shapes=[pltpu.VMEM((2 * gps, C_GROUP * QB, t_len), F32)],
        compiler_params=_cparams("parallel", "parallel", "arbitrary"),
        name="diff_prompt",
    )(rel_bias, z1, z1, z1, tp, lam_vecs, head_norm.reshape(1, C_VDIM))


def _page_gather(pt_ref, n_pages, items, sem):
    def copy(i, pg, p, slot):
        pool_ref, buf_ref, rows = items[i]
        src = pool_ref.at[pl.ds(pl.multiple_of(pg * rows, rows), rows)]
        dst = buf_ref.at[pl.ds(pl.multiple_of((slot * n_pages + p) * rows, rows), rows)]
        return pltpu.make_async_copy(src, dst, sem.at[i, slot])

    def start(bb, slot):
        def body(p, _):
            for i in range(len(items)):
                copy(i, pt_ref[bb, p], p, slot).start()
            return 0
        lax.fori_loop(0, n_pages, body, 0)

    def wait(i, slot):
        def body(p, _):
            copy(i, 0, 0, slot).wait()
            return 0
        lax.fori_loop(0, n_pages, body, 0)

    return start, wait


def _prefetch(b, nb, start):
    slot = b % 2

    @pl.when(b == 0)
    def _():
        start(0, 0)

    @pl.when(b + 1 < nb)
    def _():
        start(b + 1, 1 - slot)

    return slot


def _pad_rows(x, rows):
    return jnp.concatenate([x, jnp.zeros((rows - x.shape[0], x.shape[1]), x.dtype)], axis=0)


def _page_rows(pool_ref, rows):
    return lambda pg: pool_ref.at[pl.ds(pl.multiple_of(pg * rows, rows), rows)]


def _interleaved(buf_ref, n, j, row0=0):
    return lambda c0, ch: buf_ref[pl.ds(row0 + c0 * n + j, ch, stride=n), :]


def _sample_scores(q, k_fn, knew, bias_fn, mask_fn, s_ref, past, ch, scale=SCALE):
    def body(c, _):
        c0 = pl.multiple_of(c * ch, ch)
        k = k_fn(c0, ch).astype(MXU_DT)
        s = _nt(q, k) * scale + bias_fn(c0, ch)
        s_ref[:, pl.ds(c0, ch)] = jnp.where(mask_fn(c0, ch, False), s, NEG)
        return 0

    lax.fori_loop(0, past // ch, body, 0, unroll=4)
    s = _nt(q, _pad_rows(knew, 128).astype(MXU_DT)) * scale + bias_fn(past, 128)
    s_ref[:, past:past + 128] = jnp.where(mask_fn(past, 128, True), s, NEG)


def _sample_softmax(s_ref):
    z = s_ref[...]
    e = jnp.where(z > 0.5 * NEG, jnp.exp(z - _row_reduce(jnp.maximum, jnp.max, z)), 0.0)
    l = _row_reduce(jnp.add, jnp.sum, e)
    return e * (1.0 / jnp.where(l > 0.0, l, 1.0))


def _sample_pv(p_ref, v_fn, vnew, past, ch):
    def body(c, acc):
        c0 = pl.multiple_of(c * ch, ch)
        return acc + _mm(p_ref[:, pl.ds(c0, ch)].astype(MXU_DT), v_fn(c0, ch).astype(MXU_DT))

    acc = lax.fori_loop(0, past // ch, body, jnp.zeros((p_ref.shape[0], vnew.shape[1]), F32), unroll=4)
    return acc + _mm(p_ref[:, past:past + 128].astype(MXU_DT), _pad_rows(vnew, 128).astype(MXU_DT))


def _new_key_mask(nq, rep):
    t = lax.broadcasted_iota(I32, (nq, 128), 0)
    j = lax.broadcasted_iota(I32, (nq, 128), 1)
    return jnp.concatenate([(j <= t) & (j < nq)] * rep, axis=0)


def _compress_sample_kernel(pt_ref, pk_ref, pv_ref, w1a, w1b, w2, pe, o_ref, buf, sem, *, n_pages):
    step = pl.program_id(0)
    cpp = PAGE // CMP_STRIDE
    rows = CMP_STRIDE * A_KV
    m = n_pages * cpp

    def copy(pool_ref, pg, p, slot):
        src = pool_ref.at[pl.ds(pl.multiple_of(pg * PAGE * A_KV, PAGE * A_KV), PAGE * A_KV)]
        dst = buf.at[pl.ds(pl.multiple_of((slot * n_pages + p) * PAGE_PITCH, 8), PAGE * A_KV)]
        return pltpu.make_async_copy(src, dst, sem.at[slot])

    def start(st, slot):
        for which, pool_ref in enumerate((pk_ref, pv_ref)):
            @pl.when(st % 2 == which)
            def _(pool_ref=pool_ref):
                def body(p, _):
                    copy(pool_ref, pt_ref[st // 2, p], p, slot).start()
                    return 0
                lax.fori_loop(0, n_pages, body, 0)

    slot = _prefetch(step, pl.num_programs(0), start)

    def wait(p, _):
        copy(pk_ref, 0, 0, slot).wait()
        return 0

    lax.fori_loop(0, n_pages, wait, 0)
    which = step % 2
    row0 = slot * n_pages * PAGE_PITCH

    def x_fn(j, g):
        return jnp.concatenate([buf[pl.ds(row0 + i * rows + 2 * j + g, n_pages, stride=PAGE_PITCH), :]
                                for i in range(cpp)], axis=0)

    def next_fn(y):
        return jnp.concatenate([y[n_pages:], pltpu.roll(y[:n_pages], n_pages - 1, 0)], axis=0)

    def store(g, tokens):
        for i in range(cpp):
            o_ref[0, 0, g, pl.ds(i, n_pages, stride=cpp), :] = tokens[i * n_pages:(i + 1) * n_pages]

    _compress(x_fn, m, w1a.at[which], w1b.at[which], w2.at[which], pe.at[which], next_fn, store)


def compress_sample(pool_k, pool_v, page_table, cw):
    bd, n_pages = page_table.shape
    m = n_pages * (PAGE // CMP_STRIDE)
    return pl.pallas_call(
        functools.partial(_compress_sample_kernel, n_pages=n_pages),
        grid_spec=pltpu.PrefetchScalarGridSpec(
            num_scalar_prefetch=1, grid=(2 * bd,),
            in_specs=[pl.BlockSpec(memory_space=pl.ANY), pl.BlockSpec(memory_space=pl.ANY)] + _cmp_weight_specs(),
            out_specs=pl.BlockSpec((1, 1, A_KV, m, HEAD_DIM), lambda s, pt: (s // 2, s % 2, 0, 0, 0)),
            scratch_shapes=[pltpu.VMEM((2 * n_pages * PAGE_PITCH, HEAD_DIM), F32), pltpu.SemaphoreType.DMA((2,))]),
        out_shape=jax.ShapeDtypeStruct((bd, 2, A_KV, m, HEAD_DIM), F32),
        compiler_params=_cparams("arbitrary"),
        name="compress_sample",
    )(page_table, pool_k, pool_v, *cw)


def _nsa_sample_kernel(pt_ref, z_ref, kc_ref, vc_ref, pks_ref, pvs_ref, wk_ref, wv_ref, bs_ref, bc_ref,
                       o_ref, kbuf, vbuf, s_ref, sw_ref, chosen_ref, sem, *, n_pages, ch):
    past = n_pages * PAGE
    nq = z_ref.shape[0]
    mc = kc_ref.shape[3]
    t_len = past + nq
    n_cmp = (t_len - CMP_BLOCK) // CMP_STRIDE + 1
    n_slc = -(-t_len // SEL_BLOCK)
    jn = 128 * (-(-n_slc // 128))
    wb = wk_ref.shape[0] // A_KV
    start, wait = _page_gather(pt_ref, n_pages, ((pks_ref, kbuf, PAGE * A_KV), (pvs_ref, vbuf, PAGE * A_KV)), sem)
    slot = _prefetch(pl.program_id(0), pl.num_programs(0), start)
    row0 = slot * past * A_KV

    pos = past + lax.broadcasted_iota(I32, (nq, 1), 0)
    pos4 = jnp.concatenate([pos] * A_GROUP, axis=0)
    cidx = lax.broadcasted_iota(I32, (1, mc), 1)
    gates = jax.nn.sigmoid(z_ref[:, C_TAIL:C_TAIL + 128])
    overlap = _overlap(mc, jn, n_cmp, n_slc)
    new_mask = _new_key_mask(nq, A_GROUP)
    first_half = lax.broadcasted_iota(I32, (nq, 2 * SEL_BLOCK), 1) < SEL_BLOCK
    waited = False

    for g in range(A_KV):
        cols = [g * A_GROUP + r for r in range(A_GROUP)]
        q = _stack_heads(z_ref, C_QA + g * A_GROUP * HEAD_DIM, A_GROUP).astype(MXU_DT)
        lc = (_nt(q, kc_ref[0, 0, g].astype(MXU_DT)) * SCALE
              + jnp.concatenate([bc_ref[c] for c in cols], axis=0))
        p_cmp = _softmax_rows(lc, (pos4 >= cidx * CMP_STRIDE + (CMP_BLOCK - 1)) & (cidx < n_cmp))
        o_cmp = _mm(p_cmp.astype(MXU_DT), vc_ref[0, 0, g].astype(MXU_DT))
        p_sum = sum(p_cmp[r * nq:(r + 1) * nq] for r in range(A_GROUP))
        imp = jnp.dot(p_sum, overlap, preferred_element_type=F32, precision=lax.Precision.HIGHEST)
        sel = _select_blocks(imp, pos, n_slc)
        def win_bias(c0, w, cols=cols):
            return jnp.concatenate([bs_ref[c, :, pl.ds(past - wb + c0, w)] for c in cols], axis=0)

        def win_mask(c0, w, is_new):
            dist = pos4 - (past - wb + c0 + lax.broadcasted_iota(I32, (1, w), 1))
            valid = (dist >= 0) & (dist < WINDOW)
            return valid & new_mask if is_new else valid

        _sample_scores(q, _interleaved(wk_ref, A_KV, g),
                       z_ref[:, C_KW + g * HEAD_DIM:C_KW + (g + 1) * HEAD_DIM],
                       win_bias, win_mask, sw_ref, wb, wb)
        sw_ref[...] = _sample_softmax(sw_ref)
        o_win = _sample_pv(sw_ref, _interleaved(wv_ref, A_KV, g),
                           z_ref[:, C_VW + g * HEAD_DIM:C_VW + (g + 1) * HEAD_DIM], wb, wb)
        if not waited:
            wait(0, slot)
            wait(1, slot)
            waited = True

        def slc_bias(c0, w, cols=cols):
            return jnp.concatenate([bs_ref[c, :, pl.ds(c0, w)] for c in cols], axis=0)

        for kk in range((past + 128) // 128):
            chosen_ref[:, kk * 128:(kk + 1) * 128] = jnp.where(
                first_half, sel[:, 2 * kk:2 * kk + 1], sel[:, 2 * kk + 1:2 * kk + 2])

        def slc_mask(c0, w, is_new):
            chosen = jnp.concatenate([chosen_ref[:, pl.ds(c0, w)] > 0.5] * A_GROUP, axis=0)
            return chosen & new_mask if is_new else chosen

        _sample_scores(q, _interleaved(kbuf, A_KV, g, row0),
                       z_ref[:, C_KS + g * HEAD_DIM:C_KS + (g + 1) * HEAD_DIM],
                       slc_bias, slc_mask, s_ref, past, ch)
        s_ref[...] = _sample_softmax(s_ref)
        o_slc = _sample_pv(s_ref, _interleaved(vbuf, A_KV, g, row0),
                           z_ref[:, C_VS + g * HEAD_DIM:C_VS + (g + 1) * HEAD_DIM], past, ch)
        for r in range(A_GROUP):
            h = g * A_GROUP + r
            c = T_GA + h * N_GATES
            rows = slice(r * nq, (r + 1) * nq)
            o_ref[:, h * HEAD_DIM:(h + 1) * HEAD_DIM] = (
                gates[:, c:c + 1] * o_cmp[rows] + gates[:, c + 1:c + 2] * o_slc[rows]
                + gates[:, c + 2:c + 3] * o_win[rows])


def nsa_sample(zs, kv_cmp, pool_ks, pool_vs, win_k, win_v, bs, bc, page_table, ch=1024):
    bd, n_pages = page_table.shape
    nq = zs.shape[0] // bd
    past = n_pages * PAGE
    mc = kv_cmp.shape[3]
    wrows = win_k.shape[0] // bd
    wb = wrows // A_KV
    win_spec = pl.BlockSpec((wrows, HEAD_DIM), lambda i, pt: (i, 0))
    buf = pltpu.VMEM((2 * past * A_KV, HEAD_DIM), F32)
    return pl.pallas_call(
        functools.partial(_nsa_sample_kernel, n_pages=n_pages, ch=ch),
        grid_spec=pltpu.PrefetchScalarGridSpec(
            num_scalar_prefetch=1, grid=(bd,),
            in_specs=[pl.BlockSpec((nq, zs.shape[1]), lambda i, pt: (i, 0)),
                      pl.BlockSpec((1, 1, A_KV, mc, HEAD_DIM), lambda i, pt: (i, 0, 0, 0, 0)),
                      pl.BlockSpec((1, 1, A_KV, mc, HEAD_DIM), lambda i, pt: (i, 1, 0, 0, 0)),
                      pl.BlockSpec(memory_space=pl.ANY), pl.BlockSpec(memory_space=pl.ANY),
                      win_spec, win_spec,
                      pl.BlockSpec((A_HEADS,) + bs.shape[1:], lambda i, pt: (0, 0, 0)),
                      pl.BlockSpec(bc.shape, lambda i, pt: (0, 0, 0))],
            out_specs=pl.BlockSpec((nq, 1024), lambda i, pt: (i, 0)),
            scratch_shapes=[buf, buf,
                            pltpu.VMEM((A_GROUP * nq, past + 128), F32),
                            pltpu.VMEM((A_GROUP * nq, wb + 128), F32),
                            pltpu.VMEM((nq, past + 128), F32),
                            pltpu.SemaphoreType.DMA((2, 2))]),
        out_shape=jax.ShapeDtypeStruct((bd * nq, 1024), F32),
        compiler_params=_cparams("arbitrary"),
        name="nsa_sample",
    )(page_table, zs, kv_cmp, kv_cmp, pool_ks, pool_vs, win_k, win_v, bs, bc)


def _dsa_sample_kernel(pt_ref, z_ref, pk_ref, pv_ref, pi_ref, bs_ref, o_ref,
                       kbuf, vbuf, ibuf, s_ref, sc_ref, key_ref, sel_ref, sem, *, n_pages, ch, topk, nbits):
    past = n_pages * PAGE
    nq = z_ref.shape[0]
    start, wait = _page_gather(pt_ref, n_pages, ((pk_ref, kbuf, PAGE * B_KV), (pv_ref, vbuf, PAGE * B_KV),
                                                 (pi_ref, ibuf, IDX_DIM)), sem)
    slot = _prefetch(pl.program_id(0), pl.num_programs(0), start)
    row0 = slot * past * B_KV
    qi = jnp.concatenate([z_ref[:, C_QI + h * IDX_DIM:C_QI + (h + 1) * IDX_DIM] for h in range(IDX_HEADS)],
                         axis=0).astype(MXU_DT)
    wi = z_ref[:, C_TAIL + T_WI:C_TAIL + T_WI + IDX_HEADS]
    wait(2, slot)

    def index_page(p, _):
        kt = ibuf[pl.ds(pl.multiple_of((slot * n_pages + p) * IDX_DIM, IDX_DIM), IDX_DIM), :]
        s_ref[:, pl.ds(pl.multiple_of(p * PAGE, PAGE), PAGE)] = _mm(qi, kt.astype(MXU_DT))
        return 0

    lax.fori_loop(0, n_pages, index_page, 0, unroll=8)
    ki_new = _pad_rows(z_ref[:, C_TAIL + T_KI:C_TAIL + T_KI + IDX_DIM], 128).astype(MXU_DT)
    s_ref[:, past:past + 128] = _nt(qi, ki_new)
    rel = jnp.maximum(s_ref[...], 0.0)
    score = sum(rel[h * nq:(h + 1) * nq] * wi[:, h:h + 1] for h in range(IDX_HEADS))
    score = score * (IDX_DIM ** -0.5 * IDX_HEADS ** -0.5)
    new_j = lax.broadcasted_iota(I32, score.shape, 1) - past
    causal = (new_j < 0) | ((new_j <= lax.broadcasted_iota(I32, score.shape, 0)) & (new_j < nq))
    key_ref[...] = _sort_key(jnp.where(causal, score, NEG))
    _topk_madd(key_ref, sel_ref, lambda c0: causal, 1, score.shape[1], topk, nbits)

    wait(0, slot)
    wait(1, slot)
    for g in range(B_KV):
        cols = [g * B_GROUP + r for r in range(B_GROUP)]
        q = _stack_heads(z_ref, C_QB + g * B_GROUP * HEAD_DIM, B_GROUP).astype(MXU_DT)

        def bias(c0, w, cols=cols):
            return jnp.concatenate([bs_ref[c, :, pl.ds(c0, w)] for c in cols], axis=0)

        def mask(c0, w, is_new):
            return jnp.concatenate([sel_ref[:, pl.ds(c0, w)] > 0.5 * NEG] * B_GROUP, axis=0)

        _sample_scores(q, _interleaved(kbuf, B_KV, g, row0),
                       z_ref[:, C_KB + g * HEAD_DIM:C_KB + (g + 1) * HEAD_DIM], bias, mask, sc_ref, past, ch)
        sc_ref[...] = _sample_softmax(sc_ref)
        o = _sample_pv(sc_ref, _interleaved(vbuf, B_KV, g, row0),
                       z_ref[:, C_VB + g * HEAD_DIM:C_VB + (g + 1) * HEAD_DIM], past, ch)
        for r in range(B_GROUP):
            h = g * B_GROUP + r
            o_ref[:, h * HEAD_DIM:(h + 1) * HEAD_DIM] = o[r * nq:(r + 1) * nq]


def dsa_sample(zs, pool_k, pool_v, pool_i, bs, page_table, ch=1024):
    bd, n_pages = page_table.shape
    nq = zs.shape[0] // bd
    past = n_pages * PAGE
    lp = past + 128
    topk = min(DSA_TOPK, (past + nq) // 4)
    return pl.pallas_call(
        functools.partial(_dsa_sample_kernel, n_pages=n_pages, ch=ch, topk=topk, nbits=int(lp).bit_length()),
        grid_spec=pltpu.PrefetchScalarGridSpec(
            num_scalar_prefetch=1, grid=(bd,),
            in_specs=[pl.BlockSpec((nq, zs.shape[1]), lambda i, pt: (i, 0)),
                      pl.BlockSpec(memory_space=pl.ANY), pl.BlockSpec(memory_space=pl.ANY),
                      pl.BlockSpec(memory_space=pl.ANY),
                      pl.BlockSpec((B_HEADS,) + bs.shape[1:], lambda i, pt: (1, 0, 0))],
            out_specs=pl.BlockSpec((nq, 1024), lambda i, pt: (i, 0)),
            scratch_shapes=[pltpu.VMEM((2 * past * B_KV, HEAD_DIM), F32), pltpu.VMEM((2 * past * B_KV, HEAD_DIM), F32),
                            pltpu.VMEM((2 * n_pages * IDX_DIM, PAGE), F32),
                            pltpu.VMEM((IDX_HEADS * nq, lp), F32), pltpu.VMEM((B_GROUP * nq, lp), F32),
                            pltpu.VMEM((nq, lp), I32), pltpu.VMEM((nq, lp), F32),
                            pltpu.SemaphoreType.DMA((3, 2))]),
        out_shape=jax.ShapeDtypeStruct((bd * nq, 1024), F32),
        compiler_params=_cparams("arbitrary"),
        name="dsa_sample",
    )(page_table, zs, pool_k, pool_v, pool_i, bs)


def _diff_sample_kernel(pt_ref, q_ref, kn_ref, vn_ref, pk_ref, pv_ref, bs_ref, lam_ref, hn_ref, o_ref,
                        kbuf, vbuf, sem, *, n_pages, cp):
    b = pl.program_id(0)
    nq = q_ref.shape[0]
    pieces = C_KV * 2
    page_rows = PAGE * pieces
    slot_rows = cp * page_rows
    ch = cp * PAGE
    n_ch = n_pages // cp
    total = pl.num_programs(0) * n_ch
    past = n_pages * PAGE
    rows = C_GROUP * nq

    def copies(idx):
        bb, c, slot = idx // n_ch, idx % n_ch, idx % DIFF_SLOTS
        out = []
        for i in range(cp):
            pg = pt_ref[bb, c * cp + i]
            dst = pl.ds(pl.multiple_of(slot * slot_rows + i * page_rows, page_rows), page_rows)
            out.append(pltpu.make_async_copy(_page_rows(pk_ref, page_rows)(pg), kbuf.at[dst], sem.at[0, slot]))
            out.append(pltpu.make_async_copy(_page_rows(pv_ref, page_rows)(pg), vbuf.at[dst], sem.at[1, slot]))
        return out

    @pl.when(b == 0)
    def _():
        for idx in range(DIFF_SLOTS - 1):
            for cpy in copies(idx):
                cpy.start()

    qs = [jnp.concatenate([q_ref[:, ((g * C_GROUP + r) * 2 + m) * HEAD_DIM:((g * C_GROUP + r) * 2 + m + 1) * HEAD_DIM]
                           for r in range(C_GROUP)], axis=0).astype(MXU_DT)
          for g in range(C_KV) for m in range(2)]

    def update(carry, k_fn, v_fn, c0, w, mask):
        m_all, l_all, acc_all = carry
        new_m, new_l, new_acc = [], [], []
        for g in range(C_KV):
            ps, alphas = [], []
            for m in range(2):
                gm = g * 2 + m
                rs = slice(gm * rows, (gm + 1) * rows)
                bias = jnp.concatenate([bs_ref[m * C_HEADS + g * C_GROUP + r, :, pl.ds(c0, w)]
                                        for r in range(C_GROUP)], axis=0)
                s = _nt(qs[gm], k_fn(g, m).astype(MXU_DT)) * SCALE + bias
                if mask is not None:
                    s = jnp.where(mask, s, NEG)
                mn = jnp.maximum(m_all[rs], jnp.max(s, axis=-1, keepdims=True))
                p = jnp.exp(s - mn)
                if mask is not None:
                    p = jnp.where(mask, p, 0.0)
                a = jnp.exp(m_all[rs] - mn)
                new_m.append(mn)
                new_l.append(a * l_all[rs] + jnp.sum(p, axis=-1, keepdims=True))
                ps.append(p)
                alphas.append(a)
            pst = jnp.concatenate(ps, axis=0).astype(MXU_DT)
            pv = jnp.concatenate([_mm(pst, v_fn(g, h).astype(MXU_DT)) for h in range(2)], axis=1)
            for m in range(2):
                rs = slice((g * 2 + m) * rows, (g * 2 + m + 1) * rows)
                new_acc.append(alphas[m] * acc_all[rs] + pv[m * rows:(m + 1) * rows])
        return (jnp.concatenate(new_m, axis=0), jnp.concatenate(new_l, axis=0),
                jnp.concatenate(new_acc, axis=0))

    def chunk(c, carry):
        idx = b * n_ch + c
        for cpy in copies(idx):
            cpy.wait()

        @pl.when(idx + DIFF_SLOTS - 1 < total)
        def _():
            for cpy in copies(idx + DIFF_SLOTS - 1):
                cpy.start()

        base = (idx % DIFF_SLOTS) * slot_rows
        return update(carry,
                      lambda g, m: kbuf[pl.ds(base + g * 2 + m, ch, stride=pieces), :],
                      lambda g, h: vbuf[pl.ds(base + h * C_KV + g, ch, stride=pieces), :],
                      pl.multiple_of(c * ch, ch), ch, None)

    n_rows = pieces * rows
    carry = (jnp.full((n_rows, 1), NEG, F32), jnp.zeros((n_rows, 1), F32), jnp.zeros((n_rows, C_VDIM), F32))
    carry = lax.fori_loop(0, n_ch, chunk, carry)
    _, l_all, acc_all = update(
        carry,
        lambda g, m: _pad_rows(kn_ref[:, (g * 2 + m) * HEAD_DIM:(g * 2 + m + 1) * HEAD_DIM], 128),
        lambda g, h: _pad_rows(vn_ref[:, g * C_VDIM + h * HEAD_DIM:g * C_VDIM + (h + 1) * HEAD_DIM], 128),
        past, 128, _new_key_mask(nq, C_GROUP))
    o_all = acc_all / l_all
    lam = _diff_lambda(lam_ref)
    for g in range(C_KV):
        r0 = g * 2 * rows
        o = _diff_finish(o_all[r0:r0 + rows] - lam * o_all[r0 + rows:r0 + 2 * rows], hn_ref)
        for r in range(C_GROUP):
            col = (g * C_GROUP + r) * C_VDIM
            o_ref[:, col:col + C_VDIM] = o[r * nq:(r + 1) * nq]


def diff_sample(z1s, pool_k, pool_v, bs, lam_vecs, head_norm, page_table, cp=8):
    bd, n_pages = page_table.shape
    nq = z1s.shape[0] // bd
    assert n_pages % cp == 0 and bd * (n_pages // cp) >= DIFF_SLOTS
    slot_rows = cp * PAGE * C_KV * 2
    buf = pltpu.VMEM((DIFF_SLOTS * slot_rows, HEAD_DIM), F32)
    q_cols = C_HEADS * 2 * HEAD_DIM
    kv_cols = C_KV * C_VDIM
    return pl.pallas_call(
        functools.partial(_diff_sample_kernel, n_pages=n_pages, cp=cp),
        grid_spec=pltpu.PrefetchScalarGridSpec(
            num_scalar_prefetch=1, grid=(bd,),
            in_specs=[pl.BlockSpec((nq, q_cols), lambda i, pt: (i, 0)),
                      pl.BlockSpec((nq, kv_cols), lambda i, pt: (i, q_cols // kv_cols)),
                      pl.BlockSpec((nq, kv_cols), lambda i, pt: (i, q_cols // kv_cols + 1)),
                      pl.BlockSpec(memory_space=pl.ANY), pl.BlockSpec(memory_space=pl.ANY),
                      pl.BlockSpec(bs.shape, lambda i, pt: (0, 0, 0)),
                      pl.BlockSpec((4, HEAD_DIM), lambda i, pt: (0, 0)),
                      pl.BlockSpec((1, C_VDIM), lambda i, pt: (0, 0))],
            out_specs=pl.BlockSpec((nq, C_HEADS * C_VDIM), lambda i, pt: (i, 0)),
            scratch_shapes=[buf, buf, pltpu.SemaphoreType.DMA((2, DIFF_SLOTS))]),
        out_shape=jax.ShapeDtypeStruct((bd * nq, C_HEADS * C_VDIM), F32),
        compiler_params=_cparams("arbitrary"),
        name="diff_sample",
    )(page_table, z1s, z1s, z1s, pool_k, pool_v, bs, lam_vecs, head_norm.reshape(1, C_VDIM))


def _row_tile(rows):
    tm = min(rows, ROW_TILE)
    assert rows % tm == 0
    return tm


def _reorder_l0_weight(w):
    sizes = (A_HEADS * HEAD_DIM,) + (A_KV * HEAD_DIM,) * 6 + (
        N_GATES * A_HEADS, B_HEADS * HEAD_DIM, B_KV * HEAD_DIM, B_KV * HEAD_DIM,
        IDX_HEADS * IDX_DIM, IDX_DIM, IDX_HEADS)
    offs = [0]
    for s in sizes:
        offs.append(offs[-1] + s)
    piece = lambda i, j=None: w[:, offs[i]:offs[(i if j is None else j) + 1]]
    qa, six, ga, qb, kvb, qi, ki, wi = piece(0), piece(1, 6), piece(7), piece(8), piece(9, 10), piece(11), \
        piece(12), piece(13)
    pad = jnp.zeros((w.shape[0], L0_COLS - offs[-1]), w.dtype)
    return jnp.concatenate([qa, qb, six, kvb, qi, ki, ga, wi, pad], axis=1).astype(MXU_DT)


def _compress_weights(pe, w1, w2):
    half = CMP_STRIDE * HEAD_DIM
    w1 = w1.reshape(2, half, CMP_HIDDEN).astype(MXU_DT)
    pe_rows = jnp.zeros((16, half), F32).at[0:2].set(pe.reshape(2, half))
    return w1[0], w1[1], w2.astype(MXU_DT), pe_rows


def kernel(x_prompt, x_sample, cache_l0_nsa_cmp_k, cache_l0_nsa_cmp_v, cache_l0_nsa_slc_k, cache_l0_nsa_slc_v, state_l0_nsa_win_k, state_l0_nsa_win_v, cache_l0_dsa_k, cache_l0_dsa_v, cache_l0_dsa_idx_k, cache_l1_diff_k, cache_l1_diff_v, page_table, rel_bias, attn_norm, mlp_norm, mlp_w1, mlp_w2, l0_w_in, l0_w_out, l0_cmp_pe_k, l0_cmp_w1_k, l0_cmp_w2_k, l0_cmp_pe_v, l0_cmp_w1_v, l0_cmp_w2_v, l1_w_in, l1_w_out, l1_lambda_q1, l1_lambda_k1, l1_lambda_q2, l1_lambda_k2, l1_head_norm, final_norm):
    n, t_len, d = x_prompt.shape
    bd, nq, _ = x_sample.shape
    n_pool = cache_l0_nsa_cmp_k.shape[0]
    n_pages = page_table.shape[1]
    past = n_pages * PAGE
    lp = past + 128
    kv_w = A_KV * HEAD_DIM
    assert t_len % CHUNK == 0 and t_len >= WINDOW + QB and nq <= 8
    assert state_l0_nsa_win_k.shape[1] == min(WINDOW, past)

    xp = x_prompt.reshape(n * t_len, d)
    xs = x_sample.reshape(bd * nq, d)
    tmp, tms = _row_tile(xp.shape[0]), _row_tile(xs.shape[0])
    w0 = _reorder_l0_weight(l0_w_in)
    cw = [jnp.stack(pair) for pair in zip(_compress_weights(l0_cmp_pe_k, l0_cmp_w1_k, l0_cmp_w2_k),
                                          _compress_weights(l0_cmp_pe_v, l0_cmp_w1_v, l0_cmp_w2_v))]
    lam_vecs = jnp.stack([l1_lambda_q1, l1_lambda_k1, l1_lambda_q2, l1_lambda_k2])
    bf = lambda a: a.astype(MXU_DT)

    tp, bs = bias_tiles(rel_bias, past, nq, lp)
    bc_p, bc_s = bias_cmp(rel_bias, t_len, t_len // CMP_STRIDE, past, nq, past // CMP_STRIDE)

    zp = norm_proj(xp, attn_norm[0], w0, tmp, L0_COL_TILE)
    zs = norm_proj(xs, attn_norm[0], w0, tms, L0_COL_TILE)
    cut = lambda z, c, w: z[:, c:c + w]
    p_rows = {name: cut(zp, c, kv_w) for name, c in
              (("kc", C_KC), ("vc", C_VC), ("ks", C_KS), ("vs", C_VS), ("kw", C_KW), ("vw", C_VW),
               ("kb", C_KB), ("vb", C_VB))}
    s_rows = {name: cut(zs, c, kv_w) for name, c in
              (("kc", C_KC), ("vc", C_VC), ("ks", C_KS), ("vs", C_VS), ("kw", C_KW), ("vw", C_VW),
               ("kb", C_KB), ("vb", C_VB))}
    chunk_w = CMP_STRIDE * kv_w
    kc_p, vc_p = compress_prompt(p_rows["kc"].reshape(n, t_len // CMP_STRIDE, chunk_w),
                                 p_rows["vc"].reshape(n, t_len // CMP_STRIDE, chunk_w), cw)
    lanes = lambda a: a.reshape(-1, HEAD_DIM)
    kv_cmp_s = compress_sample(lanes(cache_l0_nsa_cmp_k), lanes(cache_l0_nsa_cmp_v), page_table, cw)
    oa_p = nsa_prompt(zp, kc_p, vc_p, tp, bc_p, rel_bias, n, t_len)
    ob_p = dsa_prompt(zp, tp, rel_bias, n, t_len)
    wb = state_l0_nsa_win_k.shape[1]
    oa_s = nsa_sample(zs, kv_cmp_s, lanes(cache_l0_nsa_slc_k), lanes(cache_l0_nsa_slc_v),
                      lanes(state_l0_nsa_win_k), lanes(state_l0_nsa_win_v), bs, bc_s, page_table)
    ob_s = dsa_sample(zs, lanes(cache_l0_dsa_k), lanes(cache_l0_dsa_v),
                      jnp.swapaxes(cache_l0_dsa_idx_k, 1, 2).reshape(-1, PAGE), bs, page_table)
    w_out0 = bf(l0_w_out)
    w1_0, w2_0 = bf(mlp_w1[0]), bf(mlp_w2[0])
    xp = out_proj(xp, [oa_p, ob_p], w_out0, tmp, COL_TILE)
    xs = out_proj(xs, [oa_s, ob_s], w_out0, tms, COL_TILE)
    xp = mlp(xp, mlp_norm[0], w1_0, w2_0, final_norm, tmp, FF_TILE, False)
    xs = mlp(xs, mlp_norm[0], w1_0, w2_0, final_norm, tms, FF_TILE, False)

    w_in1 = bf(l1_w_in)
    z1p = norm_proj(xp, attn_norm[1], w_in1, tmp, COL_TILE)
    z1s = norm_proj(xs, attn_norm[1], w_in1, tms, COL_TILE)
    o1_p = diff_prompt(z1p, tp, rel_bias, lam_vecs, l1_head_norm, n, t_len)
    v_halves = cache_l1_diff_v.reshape(n_pool, PAGE, C_KV, 2, HEAD_DIM).transpose(0, 1, 3, 2, 4)
    o1_s = diff_sample(z1s, lanes(cache_l1_diff_k), lanes(v_halves), bs, lam_vecs, l1_head_norm, page_table)
    w_out1 = bf(l1_w_out)
    w1_1, w2_1 = bf(mlp_w1[1]), bf(mlp_w2[1])
    xp = out_proj(xp, [o1_p], w_out1, tmp, COL_TILE)
    xs = out_proj(xs, [o1_s], w_out1, tms, COL_TILE)
    y_prompt = mlp(xp, mlp_norm[1], w1_1, w2_1, final_norm, tmp, FF_TILE, True).reshape(n, t_len, d)
    y_sample = mlp(xs, mlp_norm[1], w1_1, w2_1, final_norm, tms, FF_TILE, True).reshape(bd, nq, d)

    row4 = lambda a, b: a.reshape(b, -1, A_KV, HEAD_DIM)
    win = min(WINDOW, t_len)
    outs = [y_prompt, y_sample]
    for name in ("kc", "vc", "ks", "vs"):
        outs += [row4(p_rows[name], n), row4(s_rows[name], bd)]
    for name, state in (("kw", state_l0_nsa_win_k), ("vw", state_l0_nsa_win_v)):
        outs += [row4(p_rows[name], n)[:, t_len - win:],
                 jnp.concatenate([state, row4(s_rows[name], bd)], axis=1)[:, -wb:]]
    for name in ("kb", "vb"):
        outs += [row4(p_rows[name], n), row4(s_rows[name], bd)]
    outs += [cut(zp, C_TAIL + T_KI, IDX_DIM).reshape(n, t_len, IDX_DIM),
             cut(zs, C_TAIL + T_KI, IDX_DIM).reshape(bd, nq, IDX_DIM)]
    k_cols, v_cols = C_KV * 2 * HEAD_DIM, C_KV * C_VDIM
    q_cols = C_HEADS * 2 * HEAD_DIM
    outs += [cut(z1p, q_cols, k_cols).reshape(n, t_len, C_KV, 2, HEAD_DIM),
             cut(z1s, q_cols, k_cols).reshape(bd, nq, C_KV, 2, HEAD_DIM),
             cut(z1p, q_cols + k_cols, v_cols).reshape(n, t_len, C_KV, C_VDIM),
             cut(z1s, q_cols + k_cols, v_cols).reshape(bd, nq, C_KV, C_VDIM)]
    return tuple(outs)
```

```python
import functools
import math

import jax
import jax.numpy as jnp
from jax import lax
from jax.experimental import pallas as pl
from jax.experimental.pallas import tpu as pltpu

F32 = jnp.float32
I32 = jnp.int32
MXU_DT = jnp.bfloat16

HEAD_DIM = 128
A_HEADS, A_KV, A_GROUP = 8, 2, 4
B_HEADS, B_KV, B_GROUP = 8, 2, 4
C_HEADS, C_KV, C_GROUP, C_VDIM = 8, 4, 2, 256
CMP_STRIDE, CMP_BLOCK, CMP_HIDDEN = 16, 32, 256
SEL_BLOCK, N_SEL_BLOCKS, WINDOW, N_GATES = 64, 16, 512, 3
IDX_HEADS, IDX_DIM, DSA_TOPK = 4, 64, 256
NUM_BUCKETS, MAX_DISTANCE = 32, 128
LAMBDA_INIT = 0.8 - 0.6 * math.exp(-0.3 * 1)
RMS_EPS = 1e-6
NEG = -1e30
SCALE = HEAD_DIM ** -0.5
QB = 128
CHUNK = 512
PAGE = 128
PAGE_PITCH = PAGE * A_KV + 8
DIFF_SLOTS = 3
assert QB >= MAX_DISTANCE and WINDOW % QB == 0 and WINDOW >= 2 * QB and 2 * SEL_BLOCK == QB

C_QA, C_QB, C_KC, C_VC, C_KS, C_VS, C_KW, C_VW, C_KB, C_VB, C_QI, C_TAIL = (
    0, 1024, 2048, 2304, 2560, 2816, 3072, 3328, 3584, 3840, 4096, 4352)
T_KI, T_GA, T_WI = 0, 64, 88
L0_COLS = 4608
VMEM_LIMIT = 56 * 1024 * 1024
ROW_TILE = 1024
L0_COL_TILE = 1536
COL_TILE = 1024
FF_TILE = 512


def _cparams(*sem):
    return pltpu.CompilerParams(dimension_semantics=sem, vmem_limit_bytes=VMEM_LIMIT)


def _nt(a, b):
    return lax.dot_general(a, b, (((1,), (1,)), ((), ())), preferred_element_type=F32)


def _mm(a, b):
    return jnp.dot(a, b, preferred_element_type=F32)


def _rms(x, g):
    return x * lax.rsqrt(jnp.mean(x * x, axis=-1, keepdims=True) + RMS_EPS) * g


def _norm_proj_kernel(x_ref, g_ref, w_ref, o_ref, xn_ref):
    @pl.when(pl.program_id(1) == 0)
    def _():
        xn_ref[...] = _rms(x_ref[...], g_ref[...]).astype(xn_ref.dtype)

    o_ref[...] = _mm(xn_ref[...], w_ref[...])


def norm_proj(x, gain, w, tm, tn):
    rows, d = x.shape
    n = w.shape[1]
    return pl.pallas_call(
        _norm_proj_kernel,
        grid=(rows // tm, n // tn),
        in_specs=[pl.BlockSpec((tm, d), lambda i, j: (i, 0)),
                  pl.BlockSpec((1, d), lambda i, j: (0, 0)),
                  pl.BlockSpec((d, tn), lambda i, j: (0, j))],
        out_specs=pl.BlockSpec((tm, tn), lambda i, j: (i, j)),
        out_shape=jax.ShapeDtypeStruct((rows, n), F32),
        scratch_shapes=[pltpu.VMEM((tm, d), MXU_DT)],
        compiler_params=_cparams("parallel", "arbitrary"),
        name="norm_proj",
    )(x, gain.reshape(1, d), w)


def _out_proj_kernel(*refs, n_in):
    x_ref, o_refs, w_refs, y_ref = refs[0], refs[1:1 + n_in], refs[1 + n_in:1 + 2 * n_in], refs[-1]
    acc = x_ref[...]
    for o_ref, w_ref in zip(o_refs, w_refs):
        acc = acc + _mm(o_ref[...].astype(MXU_DT), w_ref[...])
    y_ref[...] = acc


def out_proj(x, outs, w, tm, tn):
    rows, d = x.shape
    o_specs, w_specs, row0 = [], [], 0
    for o in outs:
        k = o.shape[1]
        o_specs.append(pl.BlockSpec((tm, k), lambda i, j: (i, 0)))
        w_specs.append(pl.BlockSpec((k, tn), lambda i, j, rb=row0 // k: (rb, j)))
        row0 += k
    return pl.pallas_call(
        functools.partial(_out_proj_kernel, n_in=len(outs)),
        grid=(rows // tm, d // tn),
        in_specs=[pl.BlockSpec((tm, tn), lambda i, j: (i, j))] + o_specs + w_specs,
        out_specs=pl.BlockSpec((tm, tn), lambda i, j: (i, j)),
        out_shape=jax.ShapeDtypeStruct((rows, d), F32),
        compiler_params=_cparams("parallel", "arbitrary"),
        name="out_proj",
    )(x, *outs, *([w] * len(outs)))


def _mlp_kernel(x_ref, g_ref, w1_ref, w2_ref, gf_ref, y_ref, xn_ref, *, final_norm):
    j = pl.program_id(1)

    @pl.when(j == 0)
    def _():
        x = x_ref[...]
        xn_ref[...] = _rms(x, g_ref[...]).astype(xn_ref.dtype)
        y_ref[...] = x

    h = jnp.square(jnp.maximum(_mm(xn_ref[...], w1_ref[...]), 0.0))
    y_ref[...] += _mm(h.astype(w2_ref.dtype), w2_ref[...])

    if final_norm:
        @pl.when(j == pl.num_programs(1) - 1)
        def _():
            y_ref[...] = _rms(y_ref[...], gf_ref[...])


def mlp(x, gain, w1, w2, final_gain, tm, tf, final_norm):
    rows, d = x.shape
    ff = w1.shape[1]
    return pl.pallas_call(
        functools.partial(_mlp_kernel, final_norm=final_norm),
        grid=(rows // tm, ff // tf),
        in_specs=[pl.BlockSpec((tm, d), lambda i, j: (i, 0)),
                  pl.BlockSpec((1, d), lambda i, j: (0, 0)),
                  pl.BlockSpec((d, tf), lambda i, j: (0, j)),
                  pl.BlockSpec((tf, d), lambda i, j: (j, 0)),
                  pl.BlockSpec((1, d), lambda i, j: (0, 0))],
        out_specs=pl.BlockSpec((tm, d), lambda i, j: (i, 0)),
        out_shape=jax.ShapeDtypeStruct((rows, d), F32),
        scratch_shapes=[pltpu.VMEM((tm, d), MXU_DT)],
        compiler_params=_cparams("parallel", "arbitrary"),
        name="mlp",
    )(x, gain.reshape(1, d), w1, w2, final_gain.reshape(1, d))


def _bucket(dist):
    n = jnp.maximum(dist, 0)
    max_exact = NUM_BUCKETS // 2
    nf = jnp.maximum(n, 1).astype(F32)
    large = max_exact + (jnp.log(nf / max_exact) / math.log(MAX_DISTANCE / max_exact)
                         * (NUM_BUCKETS - max_exact)).astype(I32)
    large = jnp.minimum(large, NUM_BUCKETS - 1)
    return jnp.where(n < max_exact, n, large)


def _lookup(tbl_ref, col, bucket):
    acc = jnp.zeros(bucket.shape, F32)
    for b in range(NUM_BUCKETS):
        acc = jnp.where(bucket == b, tbl_ref[b, col], acc)
    return acc


def _bias_tiles_kernel(tbl_ref, tp_ref, bs_ref, *, q0):
    h = pl.program_id(0)
    t = lax.broadcasted_iota(I32, (QB, QB), 0)
    k = lax.broadcasted_iota(I32, (QB, QB), 1)
    ts = lax.broadcasted_iota(I32, bs_ref.shape[1:], 0)
    ks = lax.broadcasted_iota(I32, bs_ref.shape[1:], 1)
    tp_ref[0, 0] = _lookup(tbl_ref, h, _bucket(t - k))
    tp_ref[0, 1] = _lookup(tbl_ref, h, _bucket(QB + t - k))
    bs_ref[0] = _lookup(tbl_ref, h, _bucket(q0 + ts - ks))


def bias_tiles(rel_bias, q0, n_q, lp):
    nh = rel_bias.shape[1]
    return pl.pallas_call(
        functools.partial(_bias_tiles_kernel, q0=q0),
        grid=(nh,),
        in_specs=[pl.BlockSpec(memory_space=pltpu.SMEM)],
        out_specs=[pl.BlockSpec((1, 2, QB, QB), lambda h: (h, 0, 0, 0)),
                   pl.BlockSpec((1, n_q, lp), lambda h: (h, 0, 0))],
        out_shape=[jax.ShapeDtypeStruct((nh, 2, QB, QB), F32),
                   jax.ShapeDtypeStruct((nh, n_q, lp), F32)],
        compiler_params=_cparams("arbitrary"),
        name="bias_tiles",
    )(rel_bias)


def _bias_cmp_kernel(tbl_ref, bp_ref, bs_ref, *, q0):
    h = pl.program_id(0)
    t_len, mc_p = bp_ref.shape[1:]
    end = CMP_BLOCK - 1

    def rows(rb, _):
        r0 = pl.multiple_of(rb * QB, QB)
        tp = r0 + lax.broadcasted_iota(I32, (QB, mc_p), 0)
        cp = lax.broadcasted_iota(I32, (QB, mc_p), 1)
        bp_ref[0, pl.ds(r0, QB), :] = _lookup(tbl_ref, h, _bucket(tp - (cp * CMP_STRIDE + end)))
        return 0

    lax.fori_loop(0, t_len // QB, rows, 0)
    ts = lax.broadcasted_iota(I32, bs_ref.shape[1:], 0)
    cs = lax.broadcasted_iota(I32, bs_ref.shape[1:], 1)
    bs_ref[0] = _lookup(tbl_ref, h, _bucket(q0 + ts - (cs * CMP_STRIDE + end)))


def bias_cmp(rel_bias, t_len, mc_p, q0, n_q, mc_s):
    return pl.pallas_call(
        functools.partial(_bias_cmp_kernel, q0=q0),
        grid=(A_HEADS,),
        in_specs=[pl.BlockSpec(memory_space=pltpu.SMEM)],
        out_specs=[pl.BlockSpec((1, t_len, mc_p), lambda h: (h, 0, 0)),
                   pl.BlockSpec((1, n_q, mc_s), lambda h: (h, 0, 0))],
        out_shape=[jax.ShapeDtypeStruct((A_HEADS, t_len, mc_p), F32),
                   jax.ShapeDtypeStruct((A_HEADS, n_q, mc_s), F32)],
        compiler_params=_cparams("arbitrary"),
        name="bias_cmp",
    )(rel_bias)


def _row_reduce(fn, lane_fn, x):
    tiles = [x[:, i:i + 128] for i in range(0, x.shape[1], 128)]
    return lane_fn(_tree_reduce(fn, tiles), axis=-1, keepdims=True)


def _softmax_rows(z, mask):
    z = jnp.where(mask, z, NEG)
    e = jnp.where(mask, jnp.exp(z - _row_reduce(jnp.maximum, jnp.max, z)), 0.0)
    l = _row_reduce(jnp.add, jnp.sum, e)
    return e * (1.0 / jnp.where(l > 0.0, l, 1.0))


def _gelu_tanh(x):
    return 0.5 * x * (1.0 + jnp.tanh(math.sqrt(2.0 / math.pi) * (x + 0.044715 * (x * x * x))))


def _compress(x_fn, m, w1a_ref, w1b_ref, w2_ref, pe_ref, next_fn, out_fn):
    pe = pe_ref[...].astype(MXU_DT)
    pos = _mm(pe, w1a_ref[...])[0:1] + _mm(pe, w1b_ref[...])[1:2]
    last = lax.broadcasted_iota(I32, (m, 1), 0) == m - 1
    for g in range(A_KV):
        xg = jnp.concatenate([x_fn(j, g).astype(MXU_DT) for j in range(CMP_STRIDE)], axis=1)
        hid = _gelu_tanh(_mm(xg, w1a_ref[...]) + next_fn(_mm(xg, w1b_ref[...])) + pos)
        out_fn(g, jnp.where(last, 0.0, _mm(hid.astype(MXU_DT), w2_ref[...])))


def _overlap(mc, jn, n_cmp, n_slc):
    c = lax.broadcasted_iota(I32, (mc, jn), 0)
    j = lax.broadcasted_iota(I32, (mc, jn), 1)
    ov = ((c * CMP_STRIDE < j * SEL_BLOCK + SEL_BLOCK) & (c * CMP_STRIDE + CMP_BLOCK > j * SEL_BLOCK)
          & (c < n_cmp) & (j < n_slc))
    return jnp.where(ov, 1.0, 0.0)


def _select_blocks(imp, pos, n_slc):
    jn = imp.shape[1]
    jidx = lax.broadcasted_iota(I32, (1, jn), 1)
    cur = pos // SEL_BLOCK
    forced = (jidx == 0) | (jidx == cur) | (jidx == cur - 1)
    future = jidx * SEL_BLOCK > pos
    score = jnp.where(future, -1.0, jnp.where(forced, 1e3, imp))
    score = jnp.where(jidx < n_slc, score, -2.0)

    def body(i, rank):
        col = jnp.sum(jnp.where(jidx == i, score, 0.0), axis=-1, keepdims=True)
        beats = jnp.where(col > score, 1.0, jnp.where(col == score, jnp.where(i < jidx, 1.0, 0.0), 0.0))
        return rank + beats

    rank = lax.fori_loop(0, n_slc, body, jnp.zeros(score.shape, F32), unroll=32)
    n_sel = min(N_SEL_BLOCKS, n_slc)
    return jnp.where((rank < n_sel) & (jidx < n_slc), 1.0, 0.0)


def _tree_reduce(fn, xs):
    xs = list(xs)
    while len(xs) > 1:
        xs = [fn(xs[i], xs[i + 1]) for i in range(0, len(xs) - 1, 2)] + ([xs[-1]] if len(xs) % 2 else [])
    return xs[0]


def _tree_sum(xs):
    return _tree_reduce(jnp.add, xs)


def _sort_key(s):
    bits = lax.bitcast_convert_type(jnp.where(s == 0.0, 0.0, s), I32)
    return jnp.where(bits < 0, bits ^ jnp.int32(0x7FFFFFFF), bits)


def _topk_madd(key_ref, madd_ref, valid_fn, nch, cw, k, nbits):
    n_rows = key_ref.shape[0]
    kf = jnp.float32(k)

    def count(fn):
        def body(c, acc):
            c0 = pl.multiple_of(c * cw, cw)
            hit = jnp.where(fn(c0, key_ref[:, pl.ds(c0, cw)]), 1.0, 0.0)
            return acc + _tree_sum(hit[:, i:i + 128] for i in range(0, cw, 128))
        acc = lax.fori_loop(0, nch, body, jnp.zeros((n_rows, 128), F32))
        return jnp.sum(acc, axis=-1, keepdims=True)

    int_min = jnp.int32(-2 ** 31)
    thr0 = jnp.where(count(lambda c0, key: key >= 0) >= kf, jnp.int32(0), int_min)

    def enough(cand):
        return count(lambda c0, key: key >= cand) >= kf

    def vbody(i, thr):
        hi = lax.shift_left(jnp.int32(1), 30 - 2 * i)
        lo = lax.shift_left(jnp.int32(1), 29 - 2 * i)
        return jnp.where(enough(thr | hi | lo), thr | hi | lo,
                         jnp.where(enough(thr | hi), thr | hi, jnp.where(enough(thr | lo), thr | lo, thr)))

    thr = lax.fori_loop(0, 15, vbody, thr0)
    thr = jnp.where(enough(thr | 1), thr | 1, thr)
    need = kf - count(lambda c0, key: key > thr)

    def idx(c0):
        return c0 + lax.broadcasted_iota(I32, (1, cw), 1)

    def ibody(i, cut):
        cand = cut | lax.shift_left(jnp.int32(1), nbits - 1 - i)
        return jnp.where(count(lambda c0, key: (key == thr) & (idx(c0) < cand)) <= need, cand, cut)

    tied = jnp.max(count(lambda c0, key: key == thr) - need) > 0.0
    cut = lax.cond(tied, lambda: lax.fori_loop(0, nbits, ibody, jnp.zeros((n_rows, 1), I32)),
                   lambda: jnp.full((n_rows, 1), 2 ** nbits - 1, I32))

    def write(c, _):
        c0 = pl.multiple_of(c * cw, cw)
        key = key_ref[:, pl.ds(c0, cw)]
        sel = ((key > thr) | ((key == thr) & (idx(c0) < cut))) & valid_fn(c0)
        madd_ref[:, pl.ds(c0, cw)] = jnp.where(sel, 0.0, NEG)
        return 0

    lax.fori_loop(0, nch, write, 0)


def _topk_madd_t(key_ref, madd_ref, valid_fn, nch, cw, k, nbits):
    n_rows = key_ref.shape[1]
    kf = jnp.float32(k)

    def count(fn):
        def body(c, acc):
            c0 = pl.multiple_of(c * cw, cw)
            hit = jnp.where(fn(c0, key_ref[pl.ds(c0, cw), :]), 1.0, 0.0)
            return acc + _tree_sum(hit[i:i + 8] for i in range(0, cw, 8))
        acc = lax.fori_loop(0, nch, body, jnp.zeros((8, n_rows), F32))
        return jnp.sum(acc, axis=0, keepdims=True)

    int_min = jnp.int32(-2 ** 31)
    thr0 = jnp.where(count(lambda c0, key: key >= 0) >= kf, jnp.int32(0), int_min)

    def vbody(i, thr):
        cand = thr | lax.shift_left(jnp.int32(1), 30 - i)
        return jnp.where(count(lambda c0, key: key >= cand) >= kf, cand, thr)

    thr = lax.fori_loop(0, 31, vbody, thr0)
    need = kf - count(lambda c0, key: key > thr)

    def idx(c0):
        return c0 + lax.broadcasted_iota(I32, (cw, 1), 0)

    def ibody(i, cut):
        cand = cut | lax.shift_left(jnp.int32(1), nbits - 1 - i)
        return jnp.where(count(lambda c0, key: (key == thr) & (idx(c0) < cand)) <= need, cand, cut)

    tied = jnp.max(count(lambda c0, key: key == thr) - need) > 0.0
    cut = lax.cond(tied, lambda: lax.fori_loop(0, nbits, ibody, jnp.zeros((1, n_rows), I32)),
                   lambda: jnp.full((1, n_rows), 2 ** nbits - 1, I32))

    def write(c, _):
        c0 = pl.multiple_of(c * cw, cw)
        key = key_ref[pl.ds(c0, cw), :]
        sel = ((key > thr) | ((key == thr) & (idx(c0) < cut))) & valid_fn(c0)
        madd_ref[:, pl.ds(c0, cw)] = jnp.where(sel, 0.0, NEG).T
        return 0

    lax.fori_loop(0, nch, write, 0)


def _causal_attn(streams, dv, qb, pairs=False):
    m_rows = streams[0][0].shape[0]
    per = CHUNK // QB
    nact = qb // per + 1

    def stage(st, k0, w, bias):
        q, k_fn, _, _, _, madd_fn, s_ref = st
        s = _nt(q, k_fn(k0, w)) * SCALE + bias
        madd = madd_fn(k0, w)
        s_ref[:, pl.ds(k0, w)] = s if madd is None else s + madd

    def chunk_loop(body, init):
        n2 = nact // 2 if pairs else 0
        carry = lax.fori_loop(
            0, n2, lambda i, c: body(pl.multiple_of(i * 2 * CHUNK, 2 * CHUNK), 2 * CHUNK, c), init) if pairs else init
        return lax.fori_loop(
            0, nact - 2 * n2, lambda i, c: body(pl.multiple_of((2 * n2 + i) * CHUNK, CHUNK), CHUNK, c), carry)

    def lane_tiles(x):
        return [x[:, i:i + 128] for i in range(0, x.shape[1], 128)]

    def far(k0, w, _):
        for st in streams:
            stage(st, k0, w, st[3])
        return 0

    chunk_loop(far, 0)
    for st in streams:
        stage(st, pl.multiple_of(qb * QB, QB), QB, st[4](True))

    @pl.when(qb >= 1)
    def _():
        for st in streams:
            stage(st, pl.multiple_of((qb - 1) * QB, QB), QB, st[4](False))

    for j in range(1, per):
        @pl.when(qb % per + j < per)
        def _():
            for st in streams:
                st[6][:, pl.ds(pl.multiple_of((qb + j) * QB, QB), QB)] = jnp.full((m_rows, QB), NEG, F32)

    stat_w = 128 if pairs else 1

    def lanes_or_row(fn, lane_fn, x):
        return _tree_reduce(fn, lane_tiles(x)) if pairs else lane_fn(x, axis=-1, keepdims=True)

    def row_max(k0, w, ms):
        return tuple(jnp.maximum(m, lanes_or_row(jnp.maximum, jnp.max, st[6][:, pl.ds(k0, w)]))
                     for st, m in zip(streams, ms))

    ms = chunk_loop(row_max, tuple(jnp.full((m_rows, stat_w), NEG, F32) for _ in streams))
    ms = [jnp.max(m, axis=-1, keepdims=True) for m in ms]

    def pv(k0, w, carry):
        out = []
        for st, m, (l, acc) in zip(streams, ms, carry):
            p = jnp.exp(st[6][:, pl.ds(k0, w)] - m)
            out.append((l + lanes_or_row(jnp.add, jnp.sum, p), acc + _mm(p.astype(MXU_DT), st[2](k0, w))))
        return tuple(out)

    init = tuple((jnp.zeros((m_rows, stat_w), F32), jnp.zeros((m_rows, dv), F32)) for _ in streams)
    return [acc / jnp.sum(l, axis=-1, keepdims=True) for l, acc in chunk_loop(pv, init)]


def _causal_add(rep):
    t = lax.broadcasted_iota(I32, (QB, QB), 0)
    k = lax.broadcasted_iota(I32, (QB, QB), 1)
    return jnp.concatenate([jnp.where(k <= t, 0.0, NEG)] * rep, axis=0)


def _far_bias(tbl_ref, cols):
    return jnp.concatenate([jnp.full((QB, 1), tbl_ref[NUM_BUCKETS - 1, c], F32) for c in cols], axis=0)


def _near_bias(tp_ref, cols, diag):
    return jnp.concatenate([tp_ref[c, 0 if diag else 1] for c in cols], axis=0)


def _stack_heads(ref, col0, n):
    return jnp.concatenate([ref[:, col0 + r * HEAD_DIM:col0 + (r + 1) * HEAD_DIM] for r in range(n)], axis=0)


def _compress_prompt_kernel(xk_ref, xv_ref, w1a, w1b, w2, pe, ok_ref, ov_ref):
    lanes = lambda x_ref: (lambda j, g: x_ref[0, :, (2 * j + g) * HEAD_DIM:(2 * j + g + 1) * HEAD_DIM])
    m = xk_ref.shape[1]
    for i, (x_ref, o_ref) in enumerate(((xk_ref, ok_ref), (xv_ref, ov_ref))):
        def store(g, tokens, o_ref=o_ref):
            o_ref[0, :, g * HEAD_DIM:(g + 1) * HEAD_DIM] = tokens

        _compress(lanes(x_ref), m, w1a.at[i], w1b.at[i], w2.at[i], pe.at[i],
                  lambda y: pltpu.roll(y, m - 1, 0), store)


def _cmp_weight_specs():
    full = lambda *shape: pl.BlockSpec(shape, lambda *_: (0,) * len(shape))
    half = CMP_STRIDE * HEAD_DIM
    return [full(2, half, CMP_HIDDEN), full(2, half, CMP_HIDDEN), full(2, CMP_HIDDEN, HEAD_DIM), full(2, 16, half)]


def compress_prompt(xk, xv, cw):
    n, m, w = xk.shape
    spec = pl.BlockSpec((1, m, w), lambda i: (i, 0, 0))
    ospec = pl.BlockSpec((1, m, A_KV * HEAD_DIM), lambda i: (i, 0, 0))
    osh = jax.ShapeDtypeStruct((n, m, A_KV * HEAD_DIM), F32)
    return pl.pallas_call(
        _compress_prompt_kernel,
        grid=(n,),
        in_specs=[spec, spec] + _cmp_weight_specs(),
        out_specs=[ospec, ospec],
        out_shape=[osh, osh],
        compiler_params=_cparams("parallel"),
        name="compress_prompt",
    )(xk, xv, *cw)


def _window_attn(q, kw_ref, vw_ref, gl, qb, pos4, tbl_ref, tp_ref, cols):
    n_tiles = WINDOW // QB + 1
    width = n_tiles * QB
    lo = jnp.maximum(qb - (n_tiles - 1), 0)
    w0 = pl.multiple_of(lo * QB, QB)
    tiles = []
    for j in range(n_tiles):
        rel = qb - (lo + j)
        tiles.append(jnp.concatenate(
            [jnp.where(rel == 0, tp_ref[c, 0], jnp.where(rel == 1, tp_ref[c, 1], tbl_ref[NUM_BUCKETS - 1, c]))
             for c in cols], axis=0))
    s = _nt(q, kw_ref[pl.ds(w0, width), gl].astype(MXU_DT)) * SCALE + jnp.concatenate(tiles, axis=1)
    dist = pos4 - (w0 + lax.broadcasted_iota(I32, (1, width), 1))
    p = _softmax_rows(s, (dist >= 0) & (dist < WINDOW))
    return _mm(p.astype(MXU_DT), vw_ref[pl.ds(w0, width), gl].astype(MXU_DT))


def _nsa_prompt_kernel(tbl_ref, q_ref, tail_ref, kc_ref, vc_ref, ks_ref, vs_ref, kw_ref, vw_ref,
                       tp_ref, bc_ref, o_ref, s_ref, madd_ref, *, t_len):
    qb = pl.program_id(1)
    mc = kc_ref.shape[1]
    n_cmp = (t_len - CMP_BLOCK) // CMP_STRIDE + 1
    n_slc = -(-t_len // SEL_BLOCK)
    pos = qb * QB + lax.broadcasted_iota(I32, (QB, 1), 0)
    pos4 = jnp.concatenate([pos] * A_GROUP, axis=0)
    cidx = lax.broadcasted_iota(I32, (1, mc), 1)
    gates = jax.nn.sigmoid(tail_ref[...])
    overlap = _overlap(mc, QB, n_cmp, n_slc)
    causal_add = _causal_add(A_GROUP)
    onehot = jnp.where(lax.broadcasted_iota(I32, (QB, t_len), 0)
                       == lax.broadcasted_iota(I32, (QB, t_len), 1) // SEL_BLOCK, 1.0, 0.0).astype(MXU_DT)

    o_cmp, o_win, streams = [], [], []
    for g in range(A_KV):
        cols = [g * A_GROUP + r for r in range(A_GROUP)]
        gl = slice(g * HEAD_DIM, (g + 1) * HEAD_DIM)
        q = _stack_heads(q_ref, g * A_GROUP * HEAD_DIM, A_GROUP).astype(MXU_DT)
        lc = (_nt(q, kc_ref[0, :, gl].astype(MXU_DT)) * SCALE
              + jnp.concatenate([bc_ref[c] for c in cols], axis=0))
        p_cmp = _softmax_rows(lc, (pos4 >= cidx * CMP_STRIDE + (CMP_BLOCK - 1)) & (cidx < n_cmp))
        o_cmp.append(_mm(p_cmp.astype(MXU_DT), vc_ref[0, :, gl].astype(MXU_DT)))
        p_sum = sum(p_cmp[r * QB:(r + 1) * QB] for r in range(A_GROUP))
        imp = jnp.dot(p_sum, overlap, preferred_element_type=F32, precision=lax.Precision.HIGHEST)
        sel = _select_blocks(imp, pos, n_slc).astype(MXU_DT)
        madd_ref[g] = jnp.where(_mm(sel, onehot) > 0.5, 0.0, NEG)
        o_win.append(_window_attn(q, kw_ref, vw_ref, gl, qb, pos4, tbl_ref, tp_ref, cols))
        streams.append((
            q, lambda k0, w, gl=gl: ks_ref[pl.ds(k0, w), gl].astype(MXU_DT),
            lambda k0, w, gl=gl: vs_ref[pl.ds(k0, w), gl].astype(MXU_DT), _far_bias(tbl_ref, cols),
            lambda diag, cols=cols: _near_bias(tp_ref, cols, diag) + (causal_add if diag else 0.0),
            lambda k0, w, g=g: jnp.concatenate([madd_ref[g, :, pl.ds(k0, w)]] * A_GROUP, axis=0),
            s_ref.at[g]))
    o_slc = _causal_attn(streams, HEAD_DIM, qb)

    for h in range(A_HEADS):
        g, r = divmod(h, A_GROUP)
        c = T_GA + h * N_GATES
        rows = slice(r * QB, (r + 1) * QB)
        o = (gates[:, c:c + 1] * o_cmp[g][rows] + gates[:, c + 1:c + 2] * o_slc[g][rows]
             + gates[:, c + 2:c + 3] * o_win[g][rows])
        o_ref[:, h * HEAD_DIM:(h + 1) * HEAD_DIM] = o.astype(o_ref.dtype)


def nsa_prompt(z, k_cmp, v_cmp, tp, bc, rel_bias, n, t_len):
    nb = t_len // QB
    mc = k_cmp.shape[1]
    kv = lambda c: pl.BlockSpec((t_len, 256), lambda i, j: (i, c // 256))
    cmp_spec = pl.BlockSpec((1, mc, 256), lambda i, j: (i, 0, 0))
    return pl.pallas_call(
        functools.partial(_nsa_prompt_kernel, t_len=t_len),
        grid=(n, nb),
        in_specs=[pl.BlockSpec(memory_space=pltpu.SMEM),
                  pl.BlockSpec((QB, 1024), lambda i, j: (i * nb + j, C_QA // 1024)),
                  pl.BlockSpec((QB, 128), lambda i, j: (i * nb + j, C_TAIL // 128)),
                  cmp_spec, cmp_spec, kv(C_KS), kv(C_VS), kv(C_KW), kv(C_VW),
                  pl.BlockSpec(tp.shape, lambda i, j: (0, 0, 0, 0)),
                  pl.BlockSpec((A_HEADS, QB, mc), lambda i, j: (0, j, 0))],
        out_specs=pl.BlockSpec((QB, 1024), lambda i, j: (i * nb + j, 0)),
        out_shape=jax.ShapeDtypeStruct((n * t_len, 1024), MXU_DT),
        scratch_shapes=[pltpu.VMEM((A_KV, A_GROUP * QB, t_len), F32), pltpu.VMEM((A_KV, QB, t_len), F32)],
        compiler_params=_cparams("parallel", "arbitrary"),
        name="nsa_prompt",
    )(rel_bias, z, z, k_cmp, v_cmp, z, z, z, z, tp, bc)


def _dsa_prompt_kernel(tbl_ref, q_ref, qi_ref, tailq_ref, tailk_ref, kb_ref, vb_ref, tp_ref, o_ref,
                       key_ref, madd_ref, s_ref, *, topk, nbits):
    qb = pl.program_id(1)
    nact = qb // (CHUNK // QB) + 1
    pos = qb * QB + lax.broadcasted_iota(I32, (1, QB), 1)
    wi_t = tailq_ref[...].T[T_WI:T_WI + IDX_HEADS]
    qis = [qi_ref[:, h * IDX_DIM:(h + 1) * IDX_DIM].astype(MXU_DT) for h in range(IDX_HEADS)]

    def causal(c0):
        return c0 + lax.broadcasted_iota(I32, (CHUNK, 1), 0) <= pos

    def index_chunk(c, _):
        c0 = pl.multiple_of(c * CHUNK, CHUNK)
        ki = tailk_ref[pl.ds(c0, CHUNK), T_KI:T_KI + IDX_DIM].astype(MXU_DT)
        score = sum(jnp.maximum(_nt(ki, qis[h]), 0.0) * wi_t[h:h + 1] for h in range(IDX_HEADS))
        score = score * (IDX_DIM ** -0.5 * IDX_HEADS ** -0.5)
        key_ref[pl.ds(c0, CHUNK), :] = _sort_key(jnp.where(causal(c0), score, NEG))
        return 0

    lax.fori_loop(0, nact, index_chunk, 0)
    _topk_madd_t(key_ref, madd_ref, causal, nact, CHUNK, topk, nbits)

    streams = []
    for g in range(B_KV):
        cols = [A_HEADS + g * B_GROUP + r for r in range(B_GROUP)]
        gl = slice(g * HEAD_DIM, (g + 1) * HEAD_DIM)
        streams.append((
            _stack_heads(q_ref, g * B_GROUP * HEAD_DIM, B_GROUP).astype(MXU_DT),
            lambda k0, w, gl=gl: kb_ref[pl.ds(k0, w), gl].astype(MXU_DT),
            lambda k0, w, gl=gl: vb_ref[pl.ds(k0, w), gl].astype(MXU_DT),
            _far_bias(tbl_ref, cols), lambda diag, cols=cols: _near_bias(tp_ref, cols, diag),
            lambda k0, w: jnp.concatenate([madd_ref[:, pl.ds(k0, w)]] * B_GROUP, axis=0), s_ref.at[g]))
    outs = _causal_attn(streams, HEAD_DIM, qb)
    for h in range(B_HEADS):
        g, r = divmod(h, B_GROUP)
        o_ref[:, h * HEAD_DIM:(h + 1) * HEAD_DIM] = outs[g][r * QB:(r + 1) * QB].astype(o_ref.dtype)


def dsa_prompt(z, tp, rel_bias, n, t_len):
    nb = t_len // QB
    topk = min(DSA_TOPK, t_len // 4)
    nbits = int(t_len).bit_length()
    return pl.pallas_call(
        functools.partial(_dsa_prompt_kernel, topk=topk, nbits=nbits),
        grid=(n, nb),
        in_specs=[pl.BlockSpec(memory_space=pltpu.SMEM),
                  pl.BlockSpec((QB, 1024), lambda i, j: (i * nb + j, C_QB // 1024)),
                  pl.BlockSpec((QB, 256), lambda i, j: (i * nb + j, C_QI // 256)),
                  pl.BlockSpec((QB, 128), lambda i, j: (i * nb + j, C_TAIL // 128)),
                  pl.BlockSpec((t_len, 128), lambda i, j: (i, C_TAIL // 128)),
                  pl.BlockSpec((t_len, 256), lambda i, j: (i, C_KB // 256)),
                  pl.BlockSpec((t_len, 256), lambda i, j: (i, C_VB // 256)),
                  pl.BlockSpec(tp.shape, lambda i, j: (0, 0, 0, 0))],
        out_specs=pl.BlockSpec((QB, 1024), lambda i, j: (i * nb + j, 0)),
        out_shape=jax.ShapeDtypeStruct((n * t_len, 1024), MXU_DT),
        scratch_shapes=[pltpu.VMEM((t_len, QB), I32), pltpu.VMEM((QB, t_len), F32),
                        pltpu.VMEM((B_KV, B_GROUP * QB, t_len), F32)],
        compiler_params=_cparams("parallel", "arbitrary"),
        name="dsa_prompt",
    )(rel_bias, z, z, z, z, z, z, tp)


def _diff_lambda(lam_ref):
    v = lam_ref[...]
    e1 = jnp.exp(jnp.sum(v[0:1] * v[1:2], axis=-1, keepdims=True))
    e2 = jnp.exp(jnp.sum(v[2:3] * v[3:4], axis=-1, keepdims=True))
    return e1 - e2 + LAMBDA_INIT


def _diff_finish(o, hn_ref):
    return _rms(o, hn_ref[...]) * (1.0 - LAMBDA_INIT)


def _diff_prompt_kernel(tbl_ref, q_ref, k_ref, v_ref, tp_ref, lam_ref, hn_ref, o_ref, s_ref, *, gps):
    g0 = pl.program_id(1) * gps
    qb = pl.program_id(2)
    causal_add = _causal_add(C_GROUP)
    streams = []
    for gi in range(gps):
        for m in range(2):
            cols = [m * C_HEADS + (g0 + gi) * C_GROUP + r for r in range(C_GROUP)]
            q = jnp.concatenate(
                [q_ref[:, ((gi * C_GROUP + r) * 2 + m) * HEAD_DIM:((gi * C_GROUP + r) * 2 + m + 1) * HEAD_DIM]
                 for r in range(C_GROUP)], axis=0).astype(MXU_DT)
            kl = slice((gi * 2 + m) * HEAD_DIM, (gi * 2 + m + 1) * HEAD_DIM)
            vl = slice(gi * C_VDIM, (gi + 1) * C_VDIM)
            streams.append((
                q, lambda k0, w, kl=kl: k_ref[pl.ds(k0, w), kl].astype(MXU_DT),
                lambda k0, w, vl=vl: v_ref[pl.ds(k0, w), vl].astype(MXU_DT), _far_bias(tbl_ref, cols),
                lambda diag, cols=cols: _near_bias(tp_ref, cols, diag) + (causal_add if diag else 0.0),
                lambda k0, w: None, s_ref.at[gi * 2 + m]))
    outs = _causal_attn(streams, C_VDIM, qb, pairs=True)
    lam = _diff_lambda(lam_ref)
    for gi in range(gps):
        o = _diff_finish(outs[gi * 2] - lam * outs[gi * 2 + 1], hn_ref)
        for r in range(C_GROUP):
            col = (gi * C_GROUP + r) * C_VDIM
            o_ref[:, col:col + C_VDIM] = o[r * QB:(r + 1) * QB].astype(o_ref.dtype)


def diff_prompt(z1, tp, rel_bias, lam_vecs, head_norm, n, t_len, gps=2):
    nb = t_len // QB
    qw = gps * C_GROUP * 2 * HEAD_DIM
    kw = gps * 2 * HEAD_DIM
    q_cols, k_cols = C_HEADS * 2 * HEAD_DIM, C_KV * 2 * HEAD_DIM
    return pl.pallas_call(
        functools.partial(_diff_prompt_kernel, gps=gps),
        grid=(n, C_KV // gps, nb),
        in_specs=[pl.BlockSpec(memory_space=pltpu.SMEM),
                  pl.BlockSpec((QB, qw), lambda i, g, j: (i * nb + j, g)),
                  pl.BlockSpec((t_len, kw), lambda i, g, j: (i, q_cols // kw + g)),
                  pl.BlockSpec((t_len, kw), lambda i, g, j: (i, (q_cols + k_cols) // kw + g)),
                  pl.BlockSpec(tp.shape, lambda i, g, j: (0, 0, 0, 0)),
                  pl.BlockSpec((4, HEAD_DIM), lambda i, g, j: (0, 0)),
                  pl.BlockSpec((1, C_VDIM), lambda i, g, j: (0, 0))],
        out_specs=pl.BlockSpec((QB, qw), lambda i, g, j: (i * nb + j, g)),
        out_shape=jax.ShapeDtypeStruct((n * t_len, C_HEADS * C_VDIM), MXU_DT),
        scratch_shapes=[pltpu.VMEM((2 * gps, C_GROUP * QB, t_len), F32)],
        compiler_params=_cparams("parallel", "parallel", "arbitrary"),
        name="diff_prompt",
    )(rel_bias, z1, z1, z1, tp, lam_vecs, head_norm.reshape(1, C_VDIM))


def _page_gather(pt_ref, n_pages, items, sem):
    def copy(i, pg, p, slot):
        pool_ref, buf_ref, rows = items[i]
        src = pool_ref.at[pl.ds(pl.multiple_of(pg * rows, rows), rows)]
        dst = buf_ref.at[pl.ds(pl.multiple_of((slot * n_pages + p) * rows, rows), rows)]
        return pltpu.make_async_copy(src, dst, sem.at[i, slot])

    def start(bb, slot):
        def body(p, _):
            for i in range(len(items)):
                copy(i, pt_ref[bb, p], p, slot).start()
            return 0
        lax.fori_loop(0, n_pages, body, 0)

    def wait(i, slot):
        def body(p, _):
            copy(i, 0, 0, slot).wait()
            return 0
        lax.fori_loop(0, n_pages, body, 0)

    return start, wait


def _prefetch(b, nb, start):
    slot = b % 2

    @pl.when(b == 0)
    def _():
        start(0, 0)

    @pl.when(b + 1 < nb)
    def _():
        start(b + 1, 1 - slot)

    return slot


def _pad_rows(x, rows):
    return jnp.concatenate([x, jnp.zeros((rows - x.shape[0], x.shape[1]), x.dtype)], axis=0)


def _page_rows(pool_ref, rows):
    return lambda pg: pool_ref.at[pl.ds(pl.multiple_of(pg * rows, rows), rows)]


def _interleaved(buf_ref, n, j, row0=0):
    return lambda c0, ch: buf_ref[pl.ds(row0 + c0 * n + j, ch, stride=n), :]


def _sample_scores(q, k_fn, knew, bias_fn, mask_fn, s_ref, past, ch, scale=SCALE):
    def body(c, _):
        c0 = pl.multiple_of(c * ch, ch)
        k = k_fn(c0, ch).astype(MXU_DT)
        s = _nt(q, k) * scale + bias_fn(c0, ch)
        s_ref[:, pl.ds(c0, ch)] = jnp.where(mask_fn(c0, ch, False), s, NEG)
        return 0

    lax.fori_loop(0, past // ch, body, 0, unroll=8)
    s = _nt(q, _pad_rows(knew, 128).astype(MXU_DT)) * scale + bias_fn(past, 128)
    s_ref[:, past:past + 128] = jnp.where(mask_fn(past, 128, True), s, NEG)


def _sample_softmax(s_ref):
    z = s_ref[...]
    e = jnp.where(z > 0.5 * NEG, jnp.exp(z - _row_reduce(jnp.maximum, jnp.max, z)), 0.0)
    l = _row_reduce(jnp.add, jnp.sum, e)
    return e * (1.0 / jnp.where(l > 0.0, l, 1.0))


def _sample_pv(p_ref, v_fn, vnew, past, ch):
    def body(c, acc):
        c0 = pl.multiple_of(c * ch, ch)
        return acc + _mm(p_ref[:, pl.ds(c0, ch)].astype(MXU_DT), v_fn(c0, ch).astype(MXU_DT))

    acc = lax.fori_loop(0, past // ch, body, jnp.zeros((p_ref.shape[0], vnew.shape[1]), F32), unroll=8)
    return acc + _mm(p_ref[:, past:past + 128].astype(MXU_DT), _pad_rows(vnew, 128).astype(MXU_DT))


def _new_key_mask(nq, rep):
    t = lax.broadcasted_iota(I32, (nq, 128), 0)
    j = lax.broadcasted_iota(I32, (nq, 128), 1)
    return jnp.concatenate([(j <= t) & (j < nq)] * rep, axis=0)


def _compress_sample_kernel(pt_ref, pk_ref, pv_ref, w1a, w1b, w2, pe, o_ref, buf, sem, *, n_pages):
    step = pl.program_id(0)
    cpp = PAGE // CMP_STRIDE
    rows = CMP_STRIDE * A_KV
    m = n_pages * cpp

    def copy(pool_ref, pg, p, slot):
        src = pool_ref.at[pl.ds(pl.multiple_of(pg * PAGE * A_KV, PAGE * A_KV), PAGE * A_KV)]
        dst = buf.at[pl.ds(pl.multiple_of((slot * n_pages + p) * PAGE_PITCH, 8), PAGE * A_KV)]
        return pltpu.make_async_copy(src, dst, sem.at[slot])

    def start(st, slot):
        for which, pool_ref in enumerate((pk_ref, pv_ref)):
            @pl.when(st % 2 == which)
            def _(pool_ref=pool_ref):
                def body(p, _):
                    copy(pool_ref, pt_ref[st // 2, p], p, slot).start()
                    return 0
                lax.fori_loop(0, n_pages, body, 0)

    slot = _prefetch(step, pl.num_programs(0), start)

    def wait(p, _):
        copy(pk_ref, 0, 0, slot).wait()
        return 0

    lax.fori_loop(0, n_pages, wait, 0)
    which = step % 2
    row0 = slot * n_pages * PAGE_PITCH

    def x_fn(j, g):
        return jnp.concatenate([buf[pl.ds(row0 + i * rows + 2 * j + g, n_pages, stride=PAGE_PITCH), :]
                                for i in range(cpp)], axis=0)

    def next_fn(y):
        return jnp.concatenate([y[n_pages:], pltpu.roll(y[:n_pages], n_pages - 1, 0)], axis=0)

    def store(g, tokens):
        for i in range(cpp):
            o_ref[0, 0, g, pl.ds(i, n_pages, stride=cpp), :] = tokens[i * n_pages:(i + 1) * n_pages]

    _compress(x_fn, m, w1a.at[which], w1b.at[which], w2.at[which], pe.at[which], next_fn, store)


def compress_sample(pool_k, pool_v, page_table, cw):
    bd, n_pages = page_table.shape
    m = n_pages * (PAGE // CMP_STRIDE)
    return pl.pallas_call(
        functools.partial(_compress_sample_kernel, n_pages=n_pages),
        grid_spec=pltpu.PrefetchScalarGridSpec(
            num_scalar_prefetch=1, grid=(2 * bd,),
            in_specs=[pl.BlockSpec(memory_space=pl.ANY), pl.BlockSpec(memory_space=pl.ANY)] + _cmp_weight_specs(),
            out_specs=pl.BlockSpec((1, 1, A_KV, m, HEAD_DIM), lambda s, pt: (s // 2, s % 2, 0, 0, 0)),
            scratch_shapes=[pltpu.VMEM((2 * n_pages * PAGE_PITCH, HEAD_DIM), F32), pltpu.SemaphoreType.DMA((2,))]),
        out_shape=jax.ShapeDtypeStruct((bd, 2, A_KV, m, HEAD_DIM), F32),
        compiler_params=_cparams("arbitrary"),
        name="compress_sample",
    )(page_table, pool_k, pool_v, *cw)


def _nsa_sample_kernel(pt_ref, z_ref, kc_ref, vc_ref, pks_ref, pvs_ref, wk_ref, wv_ref, bs_ref, bc_ref,
                       o_ref, kbuf, vbuf, s_ref, sw_ref, chosen_ref, sem, *, n_pages, ch):
    past = n_pages * PAGE
    nq = z_ref.shape[0]
    mc = kc_ref.shape[3]
    t_len = past + nq
    n_cmp = (t_len - CMP_BLOCK) // CMP_STRIDE + 1
    n_slc = -(-t_len // SEL_BLOCK)
    jn = 128 * (-(-n_slc // 128))
    wb = wk_ref.shape[0] // A_KV
    start, wait = _page_gather(pt_ref, n_pages, ((pks_ref, kbuf, PAGE * A_KV), (pvs_ref, vbuf, PAGE * A_KV)), sem)
    slot = _prefetch(pl.program_id(0), pl.num_programs(0), start)
    row0 = slot * past * A_KV

    pos = past + lax.broadcasted_iota(I32, (nq, 1), 0)
    pos4 = jnp.concatenate([pos] * A_GROUP, axis=0)
    cidx = lax.broadcasted_iota(I32, (1, mc), 1)
    gates = jax.nn.sigmoid(z_ref[:, C_TAIL:C_TAIL + 128])
    overlap = _overlap(mc, jn, n_cmp, n_slc)
    new_mask = _new_key_mask(nq, A_GROUP)
    first_half = lax.broadcasted_iota(I32, (nq, 2 * SEL_BLOCK), 1) < SEL_BLOCK
    waited = False

    for g in range(A_KV):
        cols = [g * A_GROUP + r for r in range(A_GROUP)]
        q = _stack_heads(z_ref, C_QA + g * A_GROUP * HEAD_DIM, A_GROUP).astype(MXU_DT)
        lc = (_nt(q, kc_ref[0, 0, g].astype(MXU_DT)) * SCALE
              + jnp.concatenate([bc_ref[c] for c in cols], axis=0))
        p_cmp = _softmax_rows(lc, (pos4 >= cidx * CMP_STRIDE + (CMP_BLOCK - 1)) & (cidx < n_cmp))
        o_cmp = _mm(p_cmp.astype(MXU_DT), vc_ref[0, 0, g].astype(MXU_DT))
        p_sum = sum(p_cmp[r * nq:(r + 1) * nq] for r in range(A_GROUP))
        imp = jnp.dot(p_sum, overlap, preferred_element_type=F32, precision=lax.Precision.HIGHEST)
        sel = _select_blocks(imp, pos, n_slc)
        def win_bias(c0, w, cols=cols):
            return jnp.concatenate([bs_ref[c, :, pl.ds(past - wb + c0, w)] for c in cols], axis=0)

        def win_mask(c0, w, is_new):
            dist = pos4 - (past - wb + c0 + lax.broadcasted_iota(I32, (1, w), 1))
            valid = (dist >= 0) & (dist < WINDOW)
            return valid & new_mask if is_new else valid

        _sample_scores(q, _interleaved(wk_ref, A_KV, g),
                       z_ref[:, C_KW + g * HEAD_DIM:C_KW + (g + 1) * HEAD_DIM],
                       win_bias, win_mask, sw_ref, wb, wb)
        sw_ref[...] = _sample_softmax(sw_ref)
        o_win = _sample_pv(sw_ref, _interleaved(wv_ref, A_KV, g),
                           z_ref[:, C_VW + g * HEAD_DIM:C_VW + (g + 1) * HEAD_DIM], wb, wb)
        if not waited:
            wait(0, slot)
            wait(1, slot)
            waited = True

        def slc_bias(c0, w, cols=cols):
            return jnp.concatenate([bs_ref[c, :, pl.ds(c0, w)] for c in cols], axis=0)

        for kk in range((past + 128) // 128):
            chosen_ref[:, kk * 128:(kk + 1) * 128] = jnp.where(
                first_half, sel[:, 2 * kk:2 * kk + 1], sel[:, 2 * kk + 1:2 * kk + 2])

        def slc_mask(c0, w, is_new):
            chosen = jnp.concatenate([chosen_ref[:, pl.ds(c0, w)] > 0.5] * A_GROUP, axis=0)
            return chosen & new_mask if is_new else chosen

        _sample_scores(q, _interleaved(kbuf, A_KV, g, row0),
                       z_ref[:, C_KS + g * HEAD_DIM:C_KS + (g + 1) * HEAD_DIM],
                       slc_bias, slc_mask, s_ref, past, ch)
        s_ref[...] = _sample_softmax(s_ref)
        o_slc = _sample_pv(s_ref, _interleaved(vbuf, A_KV, g, row0),
                           z_ref[:, C_VS + g * HEAD_DIM:C_VS + (g + 1) * HEAD_DIM], past, ch)
        for r in range(A_GROUP):
            h = g * A_GROUP + r
            c = T_GA + h * N_GATES
            rows = slice(r * nq, (r + 1) * nq)
            o_ref[:, h * HEAD_DIM:(h + 1) * HEAD_DIM] = (
                gates[:, c:c + 1] * o_cmp[rows] + gates[:, c + 1:c + 2] * o_slc[rows]
                + gates[:, c + 2:c + 3] * o_win[rows])


def nsa_sample(zs, kv_cmp, pool_ks, pool_vs, win_k, win_v, bs, bc, page_table, ch=1024):
    bd, n_pages = page_table.shape
    nq = zs.shape[0] // bd
    past = n_pages * PAGE
    mc = kv_cmp.shape[3]
    wrows = win_k.shape[0] // bd
    wb = wrows // A_KV
    win_spec = pl.BlockSpec((wrows, HEAD_DIM), lambda i, pt: (i, 0))
    buf = pltpu.VMEM((2 * past * A_KV, HEAD_DIM), F32)
    return pl.pallas_call(
        functools.partial(_nsa_sample_kernel, n_pages=n_pages, ch=ch),
        grid_spec=pltpu.PrefetchScalarGridSpec(
            num_scalar_prefetch=1, grid=(bd,),
            in_specs=[pl.BlockSpec((nq, zs.shape[1]), lambda i, pt: (i, 0)),
                      pl.BlockSpec((1, 1, A_KV, mc, HEAD_DIM), lambda i, pt: (i, 0, 0, 0, 0)),
                      pl.BlockSpec((1, 1, A_KV, mc, HEAD_DIM), lambda i, pt: (i, 1, 0, 0, 0)),
                      pl.BlockSpec(memory_space=pl.ANY), pl.BlockSpec(memory_space=pl.ANY),
                      win_spec, win_spec,
                      pl.BlockSpec((A_HEADS,) + bs.shape[1:], lambda i, pt: (0, 0, 0)),
                      pl.BlockSpec(bc.shape, lambda i, pt: (0, 0, 0))],
            out_specs=pl.BlockSpec((nq, 1024), lambda i, pt: (i, 0)),
            scratch_shapes=[buf, buf,
                            pltpu.VMEM((A_GROUP * nq, past + 128), F32),
                            pltpu.VMEM((A_GROUP * nq, wb + 128), F32),
                            pltpu.VMEM((nq, past + 128), F32),
                            pltpu.SemaphoreType.DMA((2, 2))]),
        out_shape=jax.ShapeDtypeStruct((bd * nq, 1024), F32),
        compiler_params=_cparams("arbitrary"),
        name="nsa_sample",
    )(page_table, zs, kv_cmp, kv_cmp, pool_ks, pool_vs, win_k, win_v, bs, bc)


def _dsa_sample_kernel(pt_ref, z_ref, pk_ref, pv_ref, pi_ref, bs_ref, o_ref,
                       kbuf, vbuf, ibuf, s_ref, sc_ref, key_ref, sel_ref, sem, *, n_pages, ch, topk, nbits):
    past = n_pages * PAGE
    nq = z_ref.shape[0]
    start, wait = _page_gather(pt_ref, n_pages, ((pk_ref, kbuf, PAGE * B_KV), (pv_ref, vbuf, PAGE * B_KV),
                                                 (pi_ref, ibuf, IDX_DIM)), sem)
    slot = _prefetch(pl.program_id(0), pl.num_programs(0), start)
    row0 = slot * past * B_KV
    qi = jnp.concatenate([z_ref[:, C_QI + h * IDX_DIM:C_QI + (h + 1) * IDX_DIM] for h in range(IDX_HEADS)],
                         axis=0).astype(MXU_DT)
    wi = z_ref[:, C_TAIL + T_WI:C_TAIL + T_WI + IDX_HEADS]
    wait(2, slot)

    def index_page(p, _):
        kt = ibuf[pl.ds(pl.multiple_of((slot * n_pages + p) * IDX_DIM, IDX_DIM), IDX_DIM), :]
        s_ref[:, pl.ds(pl.multiple_of(p * PAGE, PAGE), PAGE)] = _mm(qi, kt.astype(MXU_DT))
        return 0

    lax.fori_loop(0, n_pages, index_page, 0, unroll=8)
    ki_new = _pad_rows(z_ref[:, C_TAIL + T_KI:C_TAIL + T_KI + IDX_DIM], 128).astype(MXU_DT)
    s_ref[:, past:past + 128] = _nt(qi, ki_new)
    rel = jnp.maximum(s_ref[...], 0.0)
    score = sum(rel[h * nq:(h + 1) * nq] * wi[:, h:h + 1] for h in range(IDX_HEADS))
    score = score * (IDX_DIM ** -0.5 * IDX_HEADS ** -0.5)
    new_j = lax.broadcasted_iota(I32, score.shape, 1) - past
    causal = (new_j < 0) | ((new_j <= lax.broadcasted_iota(I32, score.shape, 0)) & (new_j < nq))
    key_ref[...] = _sort_key(jnp.where(causal, score, NEG))
    _topk_madd(key_ref, sel_ref, lambda c0: causal, 1, score.shape[1], topk, nbits)

    wait(0, slot)
    wait(1, slot)
    for g in range(B_KV):
        cols = [g * B_GROUP + r for r in range(B_GROUP)]
        q = _stack_heads(z_ref, C_QB + g * B_GROUP * HEAD_DIM, B_GROUP).astype(MXU_DT)

        def bias(c0, w, cols=cols):
            return jnp.concatenate([bs_ref[c, :, pl.ds(c0, w)] for c in cols], axis=0)

        def mask(c0, w, is_new):
            return jnp.concatenate([sel_ref[:, pl.ds(c0, w)] > 0.5 * NEG] * B_GROUP, axis=0)

        _sample_scores(q, _interleaved(kbuf, B_KV, g, row0),
                       z_ref[:, C_KB + g * HEAD_DIM:C_KB + (g + 1) * HEAD_DIM], bias, mask, sc_ref, past, ch)
        sc_ref[...] = _sample_softmax(sc_ref)
        o = _sample_pv(sc_ref, _interleaved(vbuf, B_KV, g, row0),
                       z_ref[:, C_VB + g * HEAD_DIM:C_VB + (g + 1) * HEAD_DIM], past, ch)
        for r in range(B_GROUP):
            h = g * B_GROUP + r
            o_ref[:, h * HEAD_DIM:(h + 1) * HEAD_DIM] = o[r * nq:(r + 1) * nq]


def dsa_sample(zs, pool_k, pool_v, pool_i, bs, page_table, ch=1024):
    bd, n_pages = page_table.shape
    nq = zs.shape[0] // bd
    past = n_pages * PAGE
    lp = past + 128
    topk = min(DSA_TOPK, (past + nq) // 4)
    return pl.pallas_call(
        functools.partial(_dsa_sample_kernel, n_pages=n_pages, ch=ch, topk=topk, nbits=int(lp).bit_length()),
        grid_spec=pltpu.PrefetchScalarGridSpec(
            num_scalar_prefetch=1, grid=(bd,),
            in_specs=[pl.BlockSpec((nq, zs.shape[1]), lambda i, pt: (i, 0)),
                      pl.BlockSpec(memory_space=pl.ANY), pl.BlockSpec(memory_space=pl.ANY),
                      pl.BlockSpec(memory_space=pl.ANY),
                      pl.BlockSpec((B_HEADS,) + bs.shape[1:], lambda i, pt: (1, 0, 0))],
            out_specs=pl.BlockSpec((nq, 1024), lambda i, pt: (i, 0)),
            scratch_shapes=[pltpu.VMEM((2 * past * B_KV, HEAD_DIM), F32), pltpu.VMEM((2 * past * B_KV, HEAD_DIM), F32),
                            pltpu.VMEM((2 * n_pages * IDX_DIM, PAGE), F32),
                            pltpu.VMEM((IDX_HEADS * nq, lp), F32), pltpu.VMEM((B_GROUP * nq, lp), F32),
                            pltpu.VMEM((nq, lp), I32), pltpu.VMEM((nq, lp), F32),
                            pltpu.SemaphoreType.DMA((3, 2))]),
        out_shape=jax.ShapeDtypeStruct((bd * nq, 1024), F32),
        compiler_params=_cparams("arbitrary"),
        name="dsa_sample",
    )(page_table, zs, pool_k, pool_v, pool_i, bs)


def _diff_sample_kernel(pt_ref, q_ref, kn_ref, vn_ref, pk_ref, pv_ref, bs_ref, lam_ref, hn_ref, o_ref,
                        kbuf, vbuf, sem, *, n_pages, cp):
    b = pl.program_id(0)
    nq = q_ref.shape[0]
    pieces = C_KV * 2
    page_rows = PAGE * pieces
    slot_rows = cp * page_rows
    ch = cp * PAGE
    n_ch = n_pages // cp
    total = pl.num_programs(0) * n_ch
    past = n_pages * PAGE
    rows = C_GROUP * nq

    def copies(idx):
        bb, c, slot = idx // n_ch, idx % n_ch, idx % DIFF_SLOTS
        out = []
        for i in range(cp):
            pg = pt_ref[bb, c * cp + i]
            dst = pl.ds(pl.multiple_of(slot * slot_rows + i * page_rows, page_rows), page_rows)
            out.append(pltpu.make_async_copy(_page_rows(pk_ref, page_rows)(pg), kbuf.at[dst], sem.at[0, slot]))
            out.append(pltpu.make_async_copy(_page_rows(pv_ref, page_rows)(pg), vbuf.at[dst], sem.at[1, slot]))
        return out

    @pl.when(b == 0)
    def _():
        for idx in range(DIFF_SLOTS - 1):
            for cpy in copies(idx):
                cpy.start()

    qs = [jnp.concatenate([q_ref[:, ((g * C_GROUP + r) * 2 + m) * HEAD_DIM:((g * C_GROUP + r) * 2 + m + 1) * HEAD_DIM]
                           for r in range(C_GROUP)], axis=0).astype(MXU_DT)
          for g in range(C_KV) for m in range(2)]

    def update(carry, k_fn, v_fn, c0, w, mask):
        m_all, l_all, acc_all = carry
        new_m, new_l, new_acc = [], [], []
        for g in range(C_KV):
            ps, alphas = [], []
            for m in range(2):
                gm = g * 2 + m
                rs = slice(gm * rows, (gm + 1) * rows)
                bias = jnp.concatenate([bs_ref[m * C_HEADS + g * C_GROUP + r, :, pl.ds(c0, w)]
                                        for r in range(C_GROUP)], axis=0)
                s = _nt(qs[gm], k_fn(g, m).astype(MXU_DT)) * SCALE + bias
                if mask is not None:
                    s = jnp.where(mask, s, NEG)
                mn = jnp.maximum(m_all[rs], jnp.max(s, axis=-1, keepdims=True))
                p = jnp.exp(s - mn)
                if mask is not None:
                    p = jnp.where(mask, p, 0.0)
                a = jnp.exp(m_all[rs] - mn)
                new_m.append(mn)
                new_l.append(a * l_all[rs] + jnp.sum(p, axis=-1, keepdims=True))
                ps.append(p)
                alphas.append(a)
            pst = jnp.concatenate(ps, axis=0).astype(MXU_DT)
            pv = jnp.concatenate([_mm(pst, v_fn(g, h).astype(MXU_DT)) for h in range(2)], axis=1)
            for m in range(2):
                rs = slice((g * 2 + m) * rows, (g * 2 + m + 1) * rows)
                new_acc.append(alphas[m] * acc_all[rs] + pv[m * rows:(m + 1) * rows])
        return (jnp.concatenate(new_m, axis=0), jnp.concatenate(new_l, axis=0),
                jnp.concatenate(new_acc, axis=0))

    def chunk(c, carry):
        idx = b * n_ch + c
        for cpy in copies(idx):
            cpy.wait()

        @pl.when(idx + DIFF_SLOTS - 1 < total)
        def _():
            for cpy in copies(idx + DIFF_SLOTS - 1):
                cpy.start()

        base = (idx % DIFF_SLOTS) * slot_rows
        return update(carry,
                      lambda g, m: kbuf[pl.ds(base + g * 2 + m, ch, stride=pieces), :],
                      lambda g, h: vbuf[pl.ds(base + h * C_KV + g, ch, stride=pieces), :],
                      pl.multiple_of(c * ch, ch), ch, None)

    n_rows = pieces * rows
    carry = (jnp.full((n_rows, 1), NEG, F32), jnp.zeros((n_rows, 1), F32), jnp.zeros((n_rows, C_VDIM), F32))
    carry = lax.fori_loop(0, n_ch, chunk, carry)
    _, l_all, acc_all = update(
        carry,
        lambda g, m: _pad_rows(kn_ref[:, (g * 2 + m) * HEAD_DIM:(g * 2 + m + 1) * HEAD_DIM], 128),
        lambda g, h: _pad_rows(vn_ref[:, g * C_VDIM + h * HEAD_DIM:g * C_VDIM + (h + 1) * HEAD_DIM], 128),
        past, 128, _new_key_mask(nq, C_GROUP))
    o_all = acc_all / l_all
    lam = _diff_lambda(lam_ref)
    for g in range(C_KV):
        r0 = g * 2 * rows
        o = _diff_finish(o_all[r0:r0 + rows] - lam * o_all[r0 + rows:r0 + 2 * rows], hn_ref)
        for r in range(C_GROUP):
            col = (g * C_GROUP + r) * C_VDIM
            o_ref[:, col:col + C_VDIM] = o[r * nq:(r + 1) * nq]


def diff_sample(z1s, pool_k, pool_v, bs, lam_vecs, head_norm, page_table, cp=8):
    bd, n_pages = page_table.shape
    nq = z1s.shape[0] // bd
    assert n_pages % cp == 0 and bd * (n_pages // cp) >= DIFF_SLOTS
    slot_rows = cp * PAGE * C_KV * 2
    buf = pltpu.VMEM((DIFF_SLOTS * slot_rows, HEAD_DIM), F32)
    q_cols = C_HEADS * 2 * HEAD_DIM
    kv_cols = C_KV * C_VDIM
    return pl.pallas_call(
        functools.partial(_diff_sample_kernel, n_pages=n_pages, cp=cp),
        grid_spec=pltpu.PrefetchScalarGridSpec(
            num_scalar_prefetch=1, grid=(bd,),
            in_specs=[pl.BlockSpec((nq, q_cols), lambda i, pt: (i, 0)),
                      pl.BlockSpec((nq, kv_cols), lambda i, pt: (i, q_cols // kv_cols)),
                      pl.BlockSpec((nq, kv_cols), lambda i, pt: (i, q_cols // kv_cols + 1)),
                      pl.BlockSpec(memory_space=pl.ANY), pl.BlockSpec(memory_space=pl.ANY),
                      pl.BlockSpec(bs.shape, lambda i, pt: (0, 0, 0)),
                      pl.BlockSpec((4, HEAD_DIM), lambda i, pt: (0, 0)),
                      pl.BlockSpec((1, C_VDIM), lambda i, pt: (0, 0))],
            out_specs=pl.BlockSpec((nq, C_HEADS * C_VDIM), lambda i, pt: (i, 0)),
            scratch_shapes=[buf, buf, pltpu.SemaphoreType.DMA((2, DIFF_SLOTS))]),
        out_shape=jax.ShapeDtypeStruct((bd * nq, C_HEADS * C_VDIM), F32),
        compiler_params=_cparams("arbitrary"),
        name="diff_sample",
    )(page_table, z1s, z1s, z1s, pool_k, pool_v, bs, lam_vecs, head_norm.reshape(1, C_VDIM))


def _row_tile(rows):
    tm = min(rows, ROW_TILE)
    assert rows % tm == 0
    return tm


def _reorder_l0_weight(w):
    sizes = (A_HEADS * HEAD_DIM,) + (A_KV * HEAD_DIM,) * 6 + (
        N_GATES * A_HEADS, B_HEADS * HEAD_DIM, B_KV * HEAD_DIM, B_KV * HEAD_DIM,
        IDX_HEADS * IDX_DIM, IDX_DIM, IDX_HEADS)
    offs = [0]
    for s in sizes:
        offs.append(offs[-1] + s)
    piece = lambda i, j=None: w[:, offs[i]:offs[(i if j is None else j) + 1]]
    qa, six, ga, qb, kvb, qi, ki, wi = piece(0), piece(1, 6), piece(7), piece(8), piece(9, 10), piece(11), \
        piece(12), piece(13)
    pad = jnp.zeros((w.shape[0], L0_COLS - offs[-1]), w.dtype)
    return jnp.concatenate([qa, qb, six, kvb, qi, ki, ga, wi, pad], axis=1).astype(MXU_DT)


def _compress_weights(pe, w1, w2):
    half = CMP_STRIDE * HEAD_DIM
    w1 = w1.reshape(2, half, CMP_HIDDEN).astype(MXU_DT)
    pe_rows = jnp.zeros((16, half), F32).at[0:2].set(pe.reshape(2, half))
    return w1[0], w1[1], w2.astype(MXU_DT), pe_rows


def kernel(x_prompt, x_sample, cache_l0_nsa_cmp_k, cache_l0_nsa_cmp_v, cache_l0_nsa_slc_k, cache_l0_nsa_slc_v, state_l0_nsa_win_k, state_l0_nsa_win_v, cache_l0_dsa_k, cache_l0_dsa_v, cache_l0_dsa_idx_k, cache_l1_diff_k, cache_l1_diff_v, page_table, rel_bias, attn_norm, mlp_norm, mlp_w1, mlp_w2, l0_w_in, l0_w_out, l0_cmp_pe_k, l0_cmp_w1_k, l0_cmp_w2_k, l0_cmp_pe_v, l0_cmp_w1_v, l0_cmp_w2_v, l1_w_in, l1_w_out, l1_lambda_q1, l1_lambda_k1, l1_lambda_q2, l1_lambda_k2, l1_head_norm, final_norm):
    n, t_len, d = x_prompt.shape
    bd, nq, _ = x_sample.shape
    n_pool = cache_l0_nsa_cmp_k.shape[0]
    n_pages = page_table.shape[1]
    past = n_pages * PAGE
    lp = past + 128
    kv_w = A_KV * HEAD_DIM
    assert t_len % CHUNK == 0 and t_len >= WINDOW + QB and nq <= 8
    assert state_l0_nsa_win_k.shape[1] == min(WINDOW, past)

    xp = x_prompt.reshape(n * t_len, d)
    xs = x_sample.reshape(bd * nq, d)
    tmp, tms = _row_tile(xp.shape[0]), _row_tile(xs.shape[0])
    w0 = _reorder_l0_weight(l0_w_in)
    cw = [jnp.stack(pair) for pair in zip(_compress_weights(l0_cmp_pe_k, l0_cmp_w1_k, l0_cmp_w2_k),
                                          _compress_weights(l0_cmp_pe_v, l0_cmp_w1_v, l0_cmp_w2_v))]
    lam_vecs = jnp.stack([l1_lambda_q1, l1_lambda_k1, l1_lambda_q2, l1_lambda_k2])
    bf = lambda a: a.astype(MXU_DT)

    tp, bs = bias_tiles(rel_bias, past, nq, lp)
    bc_p, bc_s = bias_cmp(rel_bias, t_len, t_len // CMP_STRIDE, past, nq, past // CMP_STRIDE)

    zp = norm_proj(xp, attn_norm[0], w0, tmp, L0_COL_TILE)
    zs = norm_proj(xs, attn_norm[0], w0, tms, L0_COL_TILE)
    cut = lambda z, c, w: z[:, c:c + w]
    p_rows = {name: cut(zp, c, kv_w) for name, c in
              (("kc", C_KC), ("vc", C_VC), ("ks", C_KS), ("vs", C_VS), ("kw", C_KW), ("vw", C_VW),
               ("kb", C_KB), ("vb", C_VB))}
    s_rows = {name: cut(zs, c, kv_w) for name, c in
              (("kc", C_KC), ("vc", C_VC), ("ks", C_KS), ("vs", C_VS), ("kw", C_KW), ("vw", C_VW),
               ("kb", C_KB), ("vb", C_VB))}
    chunk_w = CMP_STRIDE * kv_w
    kc_p, vc_p = compress_prompt(p_rows["kc"].reshape(n, t_len // CMP_STRIDE, chunk_w),
                                 p_rows["vc"].reshape(n, t_len // CMP_STRIDE, chunk_w), cw)
    lanes = lambda a: a.reshape(-1, HEAD_DIM)
    kv_cmp_s = compress_sample(lanes(cache_l0_nsa_cmp_k), lanes(cache_l0_nsa_cmp_v), page_table, cw)
    oa_p = nsa_prompt(zp, kc_p, vc_p, tp, bc_p, rel_bias, n, t_len)
    ob_p = dsa_prompt(zp, tp, rel_bias, n, t_len)
    wb = state_l0_nsa_win_k.shape[1]
    oa_s = nsa_sample(zs, kv_cmp_s, lanes(cache_l0_nsa_slc_k), lanes(cache_l0_nsa_slc_v),
                      lanes(state_l0_nsa_win_k), lanes(state_l0_nsa_win_v), bs, bc_s, page_table)
    ob_s = dsa_sample(zs, lanes(cache_l0_dsa_k), lanes(cache_l0_dsa_v),
                      jnp.swapaxes(cache_l0_dsa_idx_k, 1, 2).reshape(-1, PAGE), bs, page_table)
    w_out0 = bf(l0_w_out)
    w1_0, w2_0 = bf(mlp_w1[0]), bf(mlp_w2[0])
    xp = out_proj(xp, [oa_p, ob_p], w_out0, tmp, COL_TILE)
    xs = out_proj(xs, [oa_s, ob_s], w_out0, tms, COL_TILE)
    xp = mlp(xp, mlp_norm[0], w1_0, w2_0, final_norm, tmp, FF_TILE, False)
    xs = mlp(xs, mlp_norm[0], w1_0, w2_0, final_norm, tms, FF_TILE, False)

    w_in1 = bf(l1_w_in)
    z1p = norm_proj(xp, attn_norm[1], w_in1, tmp, COL_TILE)
    z1s = norm_proj(xs, attn_norm[1], w_in1, tms, COL_TILE)
    o1_p = diff_prompt(z1p, tp, rel_bias, lam_vecs, l1_head_norm, n, t_len)
    v_halves = cache_l1_diff_v.reshape(n_pool, PAGE, C_KV, 2, HEAD_DIM).transpose(0, 1, 3, 2, 4)
    o1_s = diff_sample(z1s, lanes(cache_l1_diff_k), lanes(v_halves), bs, lam_vecs, l1_head_norm, page_table)
    w_out1 = bf(l1_w_out)
    w1_1, w2_1 = bf(mlp_w1[1]), bf(mlp_w2[1])
    xp = out_proj(xp, [o1_p], w_out1, tmp, COL_TILE)
    xs = out_proj(xs, [o1_s], w_out1, tms, COL_TILE)
    y_prompt = mlp(xp, mlp_norm[1], w1_1, w2_1, final_norm, tmp, FF_TILE, True).reshape(n, t_len, d)
    y_sample = mlp(xs, mlp_norm[1], w1_1, w2_1, final_norm, tms, FF_TILE, True).reshape(bd, nq, d)

    row4 = lambda a, b: a.reshape(b, -1, A_KV, HEAD_DIM)
    win = min(WINDOW, t_len)
    outs = [y_prompt, y_sample]
    for name in ("kc", "vc", "ks", "vs"):
        outs += [row4(p_rows[name], n), row4(s_rows[name], bd)]
    for name, state in (("kw", state_l0_nsa_win_k), ("vw", state_l0_nsa_win_v)):
        outs += [row4(p_rows[name], n)[:, t_len - win:],
                 jnp.concatenate([state, row4(s_rows[name], bd)], axis=1)[:, -wb:]]
    for name in ("kb", "vb"):
        outs += [row4(p_rows[name], n), row4(s_rows[name], bd)]
    outs += [cut(zp, C_TAIL + T_KI, IDX_DIM).reshape(n, t_len, IDX_DIM),
             cut(zs, C_TAIL + T_KI, IDX_DIM).reshape(bd, nq, IDX_DIM)]
    k_cols, v_cols = C_KV * 2 * HEAD_DIM, C_KV * C_VDIM
    q_cols = C_HEADS * 2 * HEAD_DIM
    outs += [cut(z1p, q_cols, k_cols).reshape(n, t_len, C_KV, 2, HEAD_DIM),
             cut(z1s, q_cols, k_cols).reshape(bd, nq, C_KV, 2, HEAD_DIM),
             cut(z1p, q_cols + k_cols, v_cols).reshape(n, t_len, C_KV, C_VDIM),
             cut(z1s, q_cols + k_cols, v_cols).reshape(bd, nq, C_KV, C_VDIM)]
    return tuple(outs)
```
